```python
import math
import jax, jax.numpy as jnp
from jax import lax
import numpy as np

D_MODEL = 1024
BATCH = 8
SEQ = 2048
DEPTH = 4

N_MIXERS = 3
BLOCK = 128
N_BUCKETS = 32
MAX_DISTANCE = 128
N_BIAS_CH = 16
DA_HEADS = 8
DA_QK_DIM = 64
DA_V_DIM = 2 * DA_QK_DIM
SB_HEADS = 16
SB_DIM = D_MODEL // SB_HEADS
SW_Q_HEADS = 16
SW_KV_HEADS = 4
SW_DIM = 64
SW_WINDOW = 128
D_FF = 2752
CONV_W = 3
EPS = 1e-6
NEG = -1e30
N_A = (DEPTH + 2) // 3
N_B = (DEPTH + 1) // 3
N_C = DEPTH // 3

kernel_name = 'interleaved_diff_stickbreak_swa_convffn'


def rmsnorm(x, g):
    xf = x.astype(jnp.float32)
    y = xf * lax.rsqrt(jnp.mean(xf * xf, axis=-1, keepdims=True) + EPS)
    return (y * g.astype(jnp.float32)).astype(x.dtype)


def t5_bucket(dist):
    max_exact = N_BUCKETS // 2
    d = jnp.maximum(dist, 0)
    large = max_exact + (jnp.log(jnp.maximum(d, 1).astype(jnp.float32) / max_exact)
                         / math.log(MAX_DISTANCE / max_exact) * (N_BUCKETS - max_exact)).astype(jnp.int32)
    large = jnp.minimum(large, N_BUCKETS - 1)
    return jnp.where(d < max_exact, d, large)


def lambda_init_fn(layer):
    return 0.8 - 0.6 * math.exp(-0.3 * layer)


def diff_attention(h, w_qkv, lam, subln, rel_bias, lambda_init):
    B, S, _ = h.shape
    nb = S // BLOCK
    qk_w = DA_HEADS * 2 * DA_QK_DIM
    qkv = h @ w_qkv
    q, k, v = jnp.split(qkv, [qk_w, 2 * qk_w], axis=-1)
    q = q.reshape(B, S, DA_HEADS, 2, DA_QK_DIM)
    k = k.reshape(B, S, DA_HEADS, 2, DA_QK_DIM)
    v = v.reshape(B, S, DA_HEADS, DA_V_DIM)
    lf = lam.astype(jnp.float32)
    lam_full = jnp.exp(jnp.sum(lf[0] * lf[1])) - jnp.exp(jnp.sum(lf[2] * lf[3])) + lambda_init
    bias_tab = rel_bias.astype(jnp.float32).reshape(N_BUCKETS, DA_HEADS, 2)
    k_pos = jnp.arange(S)
    scale = DA_QK_DIM ** -0.5
    q_blocks = jnp.moveaxis(q.reshape(B, nb, BLOCK, DA_HEADS, 2, DA_QK_DIM), 1, 0)

    def block(args):
        qb, start = args
        q_pos = start + jnp.arange(BLOCK)
        dist = q_pos[:, None] - k_pos[None, :]
        bias = jnp.transpose(bias_tab[t5_bucket(dist)], (2, 3, 0, 1))
        s = jnp.einsum('bqhjd,bkhjd->bhjqk', qb, k).astype(jnp.float32) * scale + bias
        s = jnp.where(dist >= 0, s, NEG)
        p = jax.nn.softmax(s, axis=-1)
        a = p[:, :, 0] - lam_full * p[:, :, 1]
        return jnp.einsum('bhqk,bkhe->bqhe', a.astype(v.dtype), v)

    o = lax.map(block, (q_blocks, jnp.arange(nb) * BLOCK))
    o = jnp.moveaxis(o, 0, 1).reshape(B, S, DA_HEADS, DA_V_DIM)
    o = rmsnorm(o, subln) * (1.0 - lambda_init)
    return o.reshape(B, S, DA_HEADS * DA_V_DIM)


def stick_breaking_attention(h, w_qkv):
    B, S, _ = h.shape
    nb = S // BLOCK
    qkv = h @ w_qkv
    q, k, v = jnp.split(qkv, 3, axis=-1)
    q = q.reshape(B, S, SB_HEADS, SB_DIM)
    k = k.reshape(B, S, SB_HEADS, SB_DIM)
    v = v.reshape(B, S, SB_HEADS, SB_DIM)
    k_pos = jnp.arange(S)
    scale = SB_DIM ** -0.5
    q_blocks = jnp.moveaxis(q.reshape(B, nb, BLOCK, SB_HEADS, SB_DIM), 1, 0)

    def block(args):
        qb, start = args
        q_pos = start + jnp.arange(BLOCK)
        strict = k_pos[None, :] < q_pos[:, None]
        z = jnp.einsum('bqhd,bkhd->bhqk', qb, k).astype(jnp.float32) * scale
        log_1m_beta = jnp.where(strict, jax.nn.log_sigmoid(-z), 0.0)
        between = lax.cumsum(log_1m_beta, axis=3, reverse=True) - log_1m_beta
        a = jnp.where(strict, jnp.exp(jax.nn.log_sigmoid(z) + between), 0.0)
        return jnp.einsum('bhqk,bkhd->bqhd', a.astype(v.dtype), v)

    o = lax.map(block, (q_blocks, jnp.arange(nb) * BLOCK))
    return jnp.moveaxis(o, 0, 1).reshape(B, S, SB_HEADS * SB_DIM)


def sliding_window_attention(h, w_qkv, sinks, rel_bias):
    B, S, _ = h.shape
    nb = S // BLOCK
    G = SW_Q_HEADS // SW_KV_HEADS
    q_w = SW_Q_HEADS * SW_DIM
    kv_w = SW_KV_HEADS * SW_DIM
    qkv = h @ w_qkv
    q, k, v = jnp.split(qkv, [q_w, q_w + kv_w], axis=-1)
    q = q.reshape(B, nb, BLOCK, SW_KV_HEADS, G, SW_DIM)
    k = k.reshape(B, nb, BLOCK, SW_KV_HEADS, SW_DIM)
    v = v.reshape(B, nb, BLOCK, SW_KV_HEADS, SW_DIM)

    def band(t):
        prev = jnp.pad(t[:, :-1], ((0, 0), (1, 0), (0, 0), (0, 0), (0, 0)))
        return jnp.concatenate([prev, t], axis=2)

    kb, vb = band(k), band(v)
    i = jnp.arange(BLOCK)[:, None]
    j = jnp.arange(2 * BLOCK)[None, :]
    dist = i + BLOCK - j
    in_window = (dist >= 0) & (dist < SW_WINDOW)
    blk = jnp.arange(nb)[:, None, None]
    valid = in_window[None] & ((blk * BLOCK - BLOCK + j[None]) >= 0)
    bias = rel_bias.astype(jnp.float32)[t5_bucket(dist)]
    bias = bias.reshape(BLOCK, 2 * BLOCK, SW_KV_HEADS, G).transpose(2, 3, 0, 1)
    s = jnp.einsum('bnqgrd,bnkgd->bngrqk', q, kb).astype(jnp.float32) * (SW_DIM ** -0.5) + bias
    s = jnp.where(valid[None, :, None, None], s, NEG)
    sink = jnp.broadcast_to(sinks.astype(jnp.float32).reshape(SW_KV_HEADS, G, 1, 1), s.shape[:-1] + (1,))
    p = jax.nn.softmax(jnp.concatenate([s, sink], axis=-1), axis=-1)[..., :-1]
    o = jnp.einsum('bngrqk,bnkgd->bnqgrd', p.astype(vb.dtype), vb)
    return o.reshape(B, S, q_w)


def conv_gated_ffn(h, w_up, conv_w, conv_b, w_down):
    u = h @ w_up
    C = u.shape[-1]
    u = lax.conv_general_dilated(u, conv_w[:, None, :].astype(u.dtype), window_strides=(1,),
                                 padding=[(CONV_W - 1, 0)], dimension_numbers=('NWC', 'WIO', 'NWC'),
                                 feature_group_count=C) + conv_b
    gate, val = jnp.split(u, 2, axis=-1)
    return (jax.nn.silu(gate) * val) @ w_down


def setup_inputs(seed: int = 0) -> dict:
    key = jax.random.key(seed)
    ks = jax.random.split(key, 20)
    D, F = D_MODEL, D_FF
    nrm = lambda k, shape, fan_in: jax.random.normal(k, shape, jnp.float32) * fan_in ** -0.5
    gain = lambda k, shape: 1.0 + 0.05 * jax.random.normal(k, shape, jnp.float32)
    return {
        'x': jax.random.normal(ks[0], (BATCH, SEQ, D), jnp.float32),
        'rel_bias': 0.5 * jax.random.normal(ks[1], (N_BUCKETS, N_BIAS_CH), jnp.float32),
        'attn_norm': gain(ks[2], (DEPTH, D)),
        'ffn_norm': gain(ks[3], (DEPTH, D)),
        'w_o': nrm(ks[4], (DEPTH, D, D), D),
        'da_w_qkv': nrm(ks[5], (N_A, D, 2 * DA_HEADS * 2 * DA_QK_DIM + DA_HEADS * DA_V_DIM), D),
        'da_lambda': 0.1 * jax.random.normal(ks[6], (N_A, 4, DA_QK_DIM), jnp.float32),
        'da_subln': gain(ks[7], (N_A, DA_V_DIM)),
        'sb_w_qkv': nrm(ks[8], (N_B, D, 3 * SB_HEADS * SB_DIM), D),
        'sw_w_qkv': nrm(ks[9], (N_C, D, (SW_Q_HEADS + 2 * SW_KV_HEADS) * SW_DIM), D),
        'sw_sinks': 0.5 * jax.random.normal(ks[10], (N_C, SW_Q_HEADS), jnp.float32),
        'ffn_w_up': nrm(ks[11], (DEPTH, D, 2 * F), D),
        'ffn_conv_w': nrm(ks[12], (DEPTH, CONV_W, 2 * F), CONV_W),
        'ffn_conv_b': 0.02 * jax.random.normal(ks[13], (DEPTH, 2 * F), jnp.float32),
        'ffn_w_down': nrm(ks[14], (DEPTH, F, D), F),
        'final_norm': gain(ks[15], (D,)),
    }


def reference(x, rel_bias, attn_norm, ffn_norm, w_o, da_w_qkv, da_lambda, da_subln, sb_w_qkv,
              sw_w_qkv, sw_sinks, ffn_w_up, ffn_conv_w, ffn_conv_b, ffn_w_down, final_norm):
    for layer in range(DEPTH):
        mixer = layer % N_MIXERS
        slot = layer // N_MIXERS
        h = rmsnorm(x, attn_norm[layer])
        if mixer == 0:
            m = diff_attention(h, da_w_qkv[slot], da_lambda[slot], da_subln[slot], rel_bias,
                               lambda_init_fn(layer))
        elif mixer == 1:
            m = stick_breaking_attention(h, sb_w_qkv[slot])
        else:
            m = sliding_window_attention(h, sw_w_qkv[slot], sw_sinks[slot], rel_bias)
        x = x + m @ w_o[layer]
        h = rmsnorm(x, ffn_norm[layer])
        x = x + conv_gated_ffn(h, ffn_w_up[layer], ffn_conv_w[layer], ffn_conv_b[layer], ffn_w_down[layer])
    return rmsnorm(x, final_norm)
```

```python
import functools
import math

import jax
import jax.numpy as jnp
from jax import lax
from jax.experimental import pallas as pl
from jax.experimental.pallas import tpu as pltpu

D_MODEL = 1024
DEPTH = 4
N_MIXERS = 3
N_BUCKETS = 32
MAX_DISTANCE = 128
DA_HEADS = 8
DA_QK_DIM = 64
DA_V_DIM = 128
SB_HEADS = 16
SB_DIM = 64
SW_Q_HEADS = 16
SW_KV_HEADS = 4
SW_DIM = 64
SW_BLOCK = 128
D_FF = 2752
EPS = 1e-6
NEG = -1e30

LANES = 128
SUBLANES = 8
VMEM_LIMIT = 52 * 1024 * 1024

ATT_T = 256
SW_TQ = 512
ROW_TILE = 512
FFN_TM = 1024
FFN_TF = 256
D_FF_PAD = 2816

BF16 = jnp.bfloat16
F32 = jnp.float32


def _params(*sem):
    return pltpu.CompilerParams(dimension_semantics=sem, vmem_limit_bytes=VMEM_LIMIT)


def _rms(xf, gain):
    return xf * lax.rsqrt(jnp.mean(xf * xf, axis=-1, keepdims=True) + EPS) * gain


def _rmsnorm_kernel(x_ref, g_ref, h_ref):
    h_ref[...] = _rms(x_ref[...], g_ref[...]).astype(h_ref.dtype)


def rmsnorm_call(x2, gain):
    M, D = x2.shape
    return pl.pallas_call(
        _rmsnorm_kernel,
        out_shape=jax.ShapeDtypeStruct((M, D), BF16),
        grid=(M // ROW_TILE,),
        in_specs=[pl.BlockSpec((ROW_TILE, D), lambda i: (i, 0)),
                  pl.BlockSpec((1, D), lambda i: (0, 0))],
        out_specs=pl.BlockSpec((ROW_TILE, D), lambda i: (i, 0)),
        compiler_params=_params("parallel"),
        name="rmsnorm",
    )(x2, gain.reshape(1, D))


def _proj_kernel(h_ref, w_ref, o_ref, *, n_chunk):
    h = h_ref[...]
    for c in range(o_ref.shape[1] // n_chunk):
        sl = slice(c * n_chunk, (c + 1) * n_chunk)
        o_ref[:, sl] = jnp.dot(h, w_ref[:, sl], preferred_element_type=F32).astype(o_ref.dtype)


def proj_call(h, w):
    M, D = h.shape
    N = w.shape[1]
    return pl.pallas_call(
        functools.partial(_proj_kernel, n_chunk=512),
        out_shape=jax.ShapeDtypeStruct((M, N), BF16),
        grid=(M // ROW_TILE,),
        in_specs=[pl.BlockSpec((ROW_TILE, D), lambda i: (i, 0)),
                  pl.BlockSpec((D, N), lambda i: (0, 0))],
        out_specs=pl.BlockSpec((ROW_TILE, N), lambda i: (i, 0)),
        compiler_params=_params("parallel"),
        name="qkv_proj",
    )(h, w)


def _wo_kernel(m_ref, w_ref, x_ref, g_ref, xo_ref, h_ref):
    xn = x_ref[...] + jnp.dot(m_ref[...], w_ref[...], preferred_element_type=F32)
    xo_ref[...] = xn
    h_ref[...] = _rms(xn, g_ref[...]).astype(h_ref.dtype)


def wo_call(m, w, x2, gain):
    M, D = x2.shape
    row = lambda i: (i, 0)
    const = lambda i: (0, 0)
    return pl.pallas_call(
        _wo_kernel,
        out_shape=(jax.ShapeDtypeStruct((M, D), F32), jax.ShapeDtypeStruct((M, D), BF16)),
        grid=(M // ROW_TILE,),
        in_specs=[pl.BlockSpec((ROW_TILE, D), row), pl.BlockSpec((D, D), const),
                  pl.BlockSpec((ROW_TILE, D), row), pl.BlockSpec((1, D), const)],
        out_specs=(pl.BlockSpec((ROW_TILE, D), row), pl.BlockSpec((ROW_TILE, D), row)),
        compiler_params=_params("parallel"),
        name="wo_residual_norm",
    )(m, w, x2, gain.reshape(1, D))


def _causal_conv3(u, cw, cb, prev):
    w0, w1, w2 = cw[0:1], cw[1:2], cw[2:3]
    y = u * w2 + pltpu.roll(u, 1, 0) * w1 + pltpu.roll(u, 2, 0) * w0 + cb
    head = u[0:SUBLANES]
    ext = jnp.concatenate([prev, head], axis=0)
    e1 = pltpu.roll(ext, 1, 0)[SUBLANES:]
    e2 = pltpu.roll(ext, 2, 0)[SUBLANES:]
    y_head = head * w2 + e1 * w1 + e2 * w0 + cb
    return jnp.concatenate([y_head, y[SUBLANES:]], axis=0)


def _ffn_kernel(h_ref, x_ref, wg_ref, wv_ref, cwg_ref, cwv_ref, cbg_ref, cbv_ref, wd_ref, g_ref,
                xo_ref, ho_ref, acc_ref, carry_ref, *, tiles_per_seq):
    i = pl.program_id(0)
    f = pl.program_id(1)
    tm = h_ref.shape[0]

    @pl.when(i % tiles_per_seq == 0)
    def _():
        carry_ref[:, pl.ds(f, 1)] = jnp.zeros((2, 1) + carry_ref.shape[2:], F32)

    h = h_ref[...]
    ug = jnp.dot(h, wg_ref[...], preferred_element_type=F32)
    uv = jnp.dot(h, wv_ref[...], preferred_element_type=F32)
    prev_g = carry_ref[0, f]
    prev_v = carry_ref[1, f]
    carry_ref[0, f] = ug[tm - SUBLANES:]
    carry_ref[1, f] = uv[tm - SUBLANES:]
    yg = _causal_conv3(ug, cwg_ref[...], cbg_ref[...], prev_g)
    yv = _causal_conv3(uv, cwv_ref[...], cbv_ref[...], prev_v)
    act = yg * (1.0 / (1.0 + jnp.exp(-yg))) * yv
    part = jnp.dot(act.astype(BF16), wd_ref[...], preferred_element_type=F32)

    @pl.when(f == 0)
    def _():
        acc_ref[...] = part

    @pl.when(f > 0)
    def _():
        acc_ref[...] += part

    @pl.when(f == pl.num_programs(1) - 1)
    def _():
        xn = x_ref[...] + acc_ref[...]
        xo_ref[...] = xn
        ho_ref[...] = _rms(xn, g_ref[...]).astype(ho_ref.dtype)


def ffn_call(h, x2, wg, wv, cwg, cwv, cbg, cbv, wd, gain, seq, out_dtype):
    M, D = x2.shape
    nf = D_FF_PAD // FFN_TF
    row = lambda i, f: (i, 0)
    colf = lambda i, f: (0, f)
    return pl.pallas_call(
        functools.partial(_ffn_kernel, tiles_per_seq=seq // FFN_TM),
        out_shape=(jax.ShapeDtypeStruct((M, D), F32), jax.ShapeDtypeStruct((M, D), out_dtype)),
        grid=(M // FFN_TM, nf),
        in_specs=[pl.BlockSpec((FFN_TM, D), row), pl.BlockSpec((FFN_TM, D), row),
                  pl.BlockSpec((D, FFN_TF), colf), pl.BlockSpec((D, FFN_TF), colf),
                  pl.BlockSpec((3, FFN_TF), colf), pl.BlockSpec((3, FFN_TF), colf),
                  pl.BlockSpec((1, FFN_TF), colf), pl.BlockSpec((1, FFN_TF), colf),
                  pl.BlockSpec((FFN_TF, D), lambda i, f: (f, 0)),
                  pl.BlockSpec((1, D), lambda i, f: (0, 0))],
        out_specs=(pl.BlockSpec((FFN_TM, D), row), pl.BlockSpec((FFN_TM, D), row)),
        scratch_shapes=[pltpu.VMEM((FFN_TM, D), F32),
                        pltpu.VMEM((2, nf, SUBLANES, FFN_TF), F32)],
        compiler_params=_params("arbitrary", "arbitrary"),
        name="conv_gated_ffn",
    )(h, x2, wg, wv, cwg, cwv, cbg, cbv, wd, gain.reshape(1, D))


def _half_masks(q):
    lane = lax.broadcasted_iota(jnp.int32, q.shape, 1)
    zero = jnp.zeros_like(q)
    scale = jnp.asarray(DA_QK_DIM ** -0.5, q.dtype)
    return (jnp.where(lane < 64, q, zero) * scale, jnp.where(lane >= 64, q, zero) * scale)


def _qk(q, k):
    return lax.dot_general(q, k, (((1,), (1,)), ((), ())), preferred_element_type=F32)


def _da_kernel(far_ref, q_ref, k_ref, v_ref, bias_ref, lam_ref, sub_ref, o_ref,
               m_ref, l_ref, acc_ref, *, lambda_init):
    T = ATT_T
    h = pl.program_id(1)
    qi = pl.program_id(2)
    qs = _half_masks(q_ref[...])
    m_ref[...] = jnp.full(m_ref.shape, NEG, F32)
    l_ref[...] = jnp.zeros(l_ref.shape, F32)
    acc_ref[...] = jnp.zeros(acc_ref.shape, F32)

    def tile(j, bias_fn, causal):
        start = pl.multiple_of(j * T, T)
        k = k_ref[pl.ds(start, T), :]
        v = v_ref[pl.ds(start, T), :]
        for c in range(2):
            s = _qk(qs[c], k) + bias_fn(c)
            if causal:
                r = lax.broadcasted_iota(jnp.int32, (T, T), 0)
                cc = lax.broadcasted_iota(jnp.int32, (T, T), 1)
                s = jnp.where(r >= cc, s, NEG)
            m_old = m_ref[c]
            m_new = jnp.maximum(m_old, jnp.max(s, axis=-1, keepdims=True))
            alpha = jnp.exp(m_old - m_new)
            p = jnp.exp(s - m_new)
            l_ref[c] = alpha * l_ref[c] + jnp.sum(p, axis=-1, keepdims=True)
            acc_ref[c] = alpha * acc_ref[c] + jnp.dot(p.astype(BF16), v, preferred_element_type=F32)
            m_ref[c] = m_new

    def far_body(j, carry):
        tile(j, lambda c: far_ref[2 * h + c], False)
        return carry

    lax.fori_loop(0, qi - 1, far_body, 0)

    @pl.when(qi > 0)
    def _():
        tile(qi - 1, lambda c: bias_ref[c, :, 0:T], False)

    tile(qi, lambda c: bias_ref[c, :, T:2 * T], True)

    lam = lam_ref[...]
    lam_full = (jnp.exp(jnp.sum(lam[0:1] * lam[1:2], keepdims=True))
                - jnp.exp(jnp.sum(lam[2:3] * lam[3:4], keepdims=True)) + lambda_init)
    o = acc_ref[0] / l_ref[0] - lam_full * (acc_ref[1] / l_ref[1])
    o = _rms(o, sub_ref[...]) * (1.0 - lambda_init)
    o_ref[...] = o.astype(o_ref.dtype)


def da_call(qkv, bias_near, bias_far, lam, subln, batch, seq, lambda_init):
    T = ATT_T
    nq = seq // T
    nh = DA_HEADS
    return pl.pallas_call(
        functools.partial(_da_kernel, lambda_init=lambda_init),
        out_shape=jax.ShapeDtypeStruct((batch * seq, nh * DA_V_DIM), BF16),
        grid=(batch, nh, nq),
        in_specs=[pl.BlockSpec(memory_space=pltpu.SMEM),
                  pl.BlockSpec((T, LANES), lambda b, h, i: (b * nq + i, h)),
                  pl.BlockSpec((seq, LANES), lambda b, h, i: (b, nh + h)),
                  pl.BlockSpec((seq, LANES), lambda b, h, i: (b, 2 * nh + h)),
                  pl.BlockSpec((2, T, 2 * T), lambda b, h, i: (h, 0, 0)),
                  pl.BlockSpec((4, DA_QK_DIM), lambda b, h, i: (0, 0)),
                  pl.BlockSpec((1, DA_V_DIM), lambda b, h, i: (0, 0))],
        out_specs=pl.BlockSpec((T, LANES), lambda b, h, i: (b * nq + i, h)),
        scratch_shapes=[pltpu.VMEM((2, T, 1), F32), pltpu.VMEM((2, T, 1), F32),
                        pltpu.VMEM((2, T, DA_V_DIM), F32)],
        compiler_params=_params("parallel", "parallel", "parallel"),
        name="diff_attention",
    )(bias_far, qkv, qkv, qkv, bias_near, lam, subln.reshape(1, DA_V_DIM))


def _split3(x):
    hi = x.astype(BF16)
    r1 = x - hi.astype(F32)
    mid = r1.astype(BF16)
    lo = (r1 - mid.astype(F32)).astype(BF16)
    return hi, mid, lo


def _sb_kernel(q_ref, k_ref, v_ref, o_ref, carry_ref, acc_ref):
    T = ATT_T
    qi = pl.program_id(2)
    qs = _half_masks(q_ref[...])
    carry_ref[...] = jnp.zeros(carry_ref.shape, F32)
    acc_ref[...] = jnp.zeros(acc_ref.shape, F32)
    kr = lax.broadcasted_iota(jnp.int32, (T, T), 0)
    kc = lax.broadcasted_iota(jnp.int32, (T, T), 1)
    suffix = jnp.where(kr > kc, 1.0, 0.0).astype(BF16)
    lane = lax.broadcasted_iota(jnp.int32, (T, LANES), 1)

    def tile(j, diag):
        start = pl.multiple_of(j * T, T)
        k = k_ref[pl.ds(start, T), :]
        v = v_ref[pl.ds(start, T), :]
        vz = jnp.zeros_like(v)
        vs = (jnp.where(lane < 64, v, vz), jnp.where(lane >= 64, v, vz))
        for c in range(2):
            z = _qk(qs[c], k)
            sp = jnp.maximum(z, 0.0) + jnp.log(1.0 + jnp.exp(-jnp.abs(z)))
            log_1m_beta = -sp
            if diag:
                strict = kc < kr
                log_1m_beta = jnp.where(strict, log_1m_beta, 0.0)
            hi, mid, lo = _split3(log_1m_beta)
            between = (jnp.dot(hi, suffix, preferred_element_type=F32)
                       + jnp.dot(mid, suffix, preferred_element_type=F32)
                       + jnp.dot(lo, suffix, preferred_element_type=F32)) + carry_ref[c]
            a = jnp.exp((z - sp) + between)
            if diag:
                a = jnp.where(strict, a, 0.0)
            acc_ref[...] += jnp.dot(a.astype(BF16), vs[c], preferred_element_type=F32)
            carry_ref[c] += jnp.sum(log_1m_beta, axis=-1, keepdims=True)

    tile(qi, True)

    def body(t, carry):
        tile(qi - 1 - t, False)
        return carry

    lax.fori_loop(0, qi, body, 0)
    o_ref[...] = acc_ref[...].astype(o_ref.dtype)


def sb_call(qkv, batch, seq):
    T = ATT_T
    nq = seq // T
    npair = SB_HEADS // 2
    return pl.pallas_call(
        _sb_kernel,
        out_shape=jax.ShapeDtypeStruct((batch * seq, SB_HEADS * SB_DIM), BF16),
        grid=(batch, npair, nq),
        in_specs=[pl.BlockSpec((T, LANES), lambda b, p, i: (b * nq + i, p)),
                  pl.BlockSpec((seq, LANES), lambda b, p, i: (b, npair + p)),
                  pl.BlockSpec((seq, LANES), lambda b, p, i: (b, 2 * npair + p))],
        out_specs=pl.BlockSpec((T, LANES), lambda b, p, i: (b * nq + i, p)),
        scratch_shapes=[pltpu.VMEM((2, T, 1), F32), pltpu.VMEM((T, LANES), F32)],
        compiler_params=_params("parallel", "parallel", "parallel"),
        name="stick_breaking_attention",
    )(qkv, qkv, qkv)


def _sw_kernel(sink_ref, q_ref, k_ref, v_ref, bias_ref, o_ref):
    W = SW_BLOCK
    p_id = pl.program_id(1)
    qi = pl.program_id(2)
    r = lax.broadcasted_iota(jnp.int32, (W, W), 0)
    cidx = lax.broadcasted_iota(jnp.int32, (W, W), 1)
    lane = lax.broadcasted_iota(jnp.int32, (W, LANES), 1)
    for sub in range(SW_TQ // W):
        n = qi * (SW_TQ // W) + sub
        qs = _half_masks(q_ref[sub * W:(sub + 1) * W, :])
        cur = pl.multiple_of(n * W, W)
        prev = pl.multiple_of(jnp.maximum(n - 1, 0) * W, W)
        k_cur, v_cur = k_ref[pl.ds(cur, W), :], v_ref[pl.ds(cur, W), :]
        k_prev, v_prev = k_ref[pl.ds(prev, W), :], v_ref[pl.ds(prev, W), :]
        prev_ok = (cidx > r) & (n > 0)
        out = jnp.zeros((W, LANES), F32)
        for c in range(2):
            s_prev = jnp.where(prev_ok, _qk(qs[c], k_prev) + bias_ref[c, :, 0:W], NEG)
            s_cur = jnp.where(cidx <= r, _qk(qs[c], k_cur) + bias_ref[c, :, W:2 * W], NEG)
            sink = sink_ref[2 * p_id + c]
            m = jnp.maximum(jnp.maximum(jnp.max(s_prev, axis=-1, keepdims=True),
                                        jnp.max(s_cur, axis=-1, keepdims=True)), sink)
            e_prev = jnp.exp(s_prev - m)
            e_cur = jnp.exp(s_cur - m)
            denom = (jnp.sum(e_prev, axis=-1, keepdims=True) + jnp.sum(e_cur, axis=-1, keepdims=True)
                     + jnp.exp(sink - m))
            pv = (jnp.dot(e_prev.astype(BF16), v_prev, preferred_element_type=F32)
                  + jnp.dot(e_cur.astype(BF16), v_cur, preferred_element_type=F32))
            half = (lane < 64) if c == 0 else (lane >= 64)
            out = jnp.where(half, pv / denom, out)
        o_ref[sub * W:(sub + 1) * W, :] = out.astype(o_ref.dtype)


def sw_call(qkv, bias_band, sinks, batch, seq):
    nq = seq // SW_TQ
    npair = SW_Q_HEADS // 2
    q_blocks = SW_Q_HEADS * SW_DIM // LANES
    return pl.pallas_call(
        _sw_kernel,
        out_shape=jax.ShapeDtypeStruct((batch * seq, SW_Q_HEADS * SW_DIM), BF16),
        grid=(batch, npair, nq),
        in_specs=[pl.BlockSpec(memory_space=pltpu.SMEM),
                  pl.BlockSpec((SW_TQ, LANES), lambda b, p, i: (b * nq + i, p)),
                  pl.BlockSpec((seq, LANES), lambda b, p, i: (b, q_blocks + p // 2)),
                  pl.BlockSpec((seq, LANES), lambda b, p, i: (b, q_blocks + SW_KV_HEADS + p // 2)),
                  pl.BlockSpec((2, SW_BLOCK, 2 * SW_BLOCK), lambda b, p, i: (p, 0, 0))],
        out_specs=pl.BlockSpec((SW_TQ, LANES), lambda b, p, i: (b * nq + i, p)),
        compiler_params=_params("parallel", "parallel", "parallel"),
        name="sliding_window_attention",
    )(sinks, qkv, qkv, qkv, bias_band)


def _t5_bucket(dist):
    max_exact = N_BUCKETS // 2
    d = jnp.maximum(dist, 0)
    large = max_exact + (jnp.log(jnp.maximum(d, 1).astype(F32) / max_exact)
                         / math.log(MAX_DISTANCE / max_exact) * (N_BUCKETS - max_exact)).astype(jnp.int32)
    large = jnp.minimum(large, N_BUCKETS - 1)
    return jnp.where(d < max_exact, d, large)


def _bias_band(rel_bias, T):
    i = jnp.arange(T)[:, None]
    c = jnp.arange(2 * T)[None, :]
    band = rel_bias.astype(F32)[_t5_bucket(T + i - c)]
    return jnp.transpose(band, (2, 0, 1))


def _lambda_init(layer):
    return 0.8 - 0.6 * math.exp(-0.3 * layer)


def _dup_kv_heads(w):
    q_w = SW_Q_HEADS * SW_DIM
    kv = w[:, q_w:].reshape(w.shape[0], 2 * SW_KV_HEADS, 1, SW_DIM)
    kv = jnp.broadcast_to(kv, (w.shape[0], 2 * SW_KV_HEADS, 2, SW_DIM)).reshape(w.shape[0], -1)
    return jnp.concatenate([w[:, :q_w], kv], axis=1)


def _pad_ff(a, axis):
    pad = [(0, 0)] * a.ndim
    pad[axis] = (0, D_FF_PAD - D_FF)
    return jnp.pad(a, pad)


def kernel(x, rel_bias, attn_norm, ffn_norm, w_o, da_w_qkv, da_lambda, da_subln, sb_w_qkv, sw_w_qkv,
           sw_sinks, ffn_w_up, ffn_conv_w, ffn_conv_b, ffn_w_down, final_norm):
    B, S, D = x.shape
    x2 = x.reshape(B * S, D)
    da_bias_near = _bias_band(rel_bias, ATT_T)
    da_bias_far = rel_bias.astype(F32)[N_BUCKETS - 1]
    sw_bias = _bias_band(rel_bias, SW_BLOCK)

    h = rmsnorm_call(x2, attn_norm[0])
    for layer in range(DEPTH):
        mixer = layer % N_MIXERS
        slot = layer // N_MIXERS
        if mixer == 0:
            qkv = proj_call(h, da_w_qkv[slot].astype(BF16))
            m = da_call(qkv, da_bias_near, da_bias_far, da_lambda[slot], da_subln[slot], B, S,
                        _lambda_init(layer))
        elif mixer == 1:
            qkv = proj_call(h, sb_w_qkv[slot].astype(BF16))
            m = sb_call(qkv, B, S)
        else:
            qkv = proj_call(h, _dup_kv_heads(sw_w_qkv[slot]).astype(BF16))
            m = sw_call(qkv, sw_bias, sw_sinks[slot], B, S)
        x2, h = wo_call(m, w_o[layer].astype(BF16), x2, ffn_norm[layer])

        w_up = ffn_w_up[layer]
        wg = _pad_ff(w_up[:, :D_FF], 1).astype(BF16)
        wv = _pad_ff(w_up[:, D_FF:], 1).astype(BF16)
        cw, cb = ffn_conv_w[layer], ffn_conv_b[layer].reshape(1, -1)
        last = layer == DEPTH - 1
        gain = final_norm if last else attn_norm[layer + 1]
        x2, h = ffn_call(h, x2, wg, wv,
                         _pad_ff(cw[:, :D_FF], 1), _pad_ff(cw[:, D_FF:], 1),
                         _pad_ff(cb[:, :D_FF], 1), _pad_ff(cb[:, D_FF:], 1),
                         _pad_ff(ffn_w_down[layer], 0).astype(BF16), gain, S,
                         F32 if last else BF16)
    return h.reshape(B, S, D)
```

```python
import functools
import math

import jax
import jax.numpy as jnp
import numpy as np
from jax import lax
from jax.experimental import pallas as pl
from jax.experimental.pallas import tpu as pltpu

D_MODEL = 1024
DEPTH = 4
N_MIXERS = 3
N_BUCKETS = 32
MAX_DISTANCE = 128
DA_HEADS = 8
DA_QK_DIM = 64
DA_V_DIM = 128
SB_HEADS = 16
SB_DIM = 64
SW_Q_HEADS = 16
SW_KV_HEADS = 4
SW_DIM = 64
SW_BLOCK = 128
D_FF = 2752
EPS = 1e-6
NEG = -1e30

LANES = 128
SUBLANES = 8
VMEM_LIMIT = 52 * 1024 * 1024

ATT_T = 256
SW_TQ = 512
ROW_TILE = 512
FFN_TM = 1024
FFN_TF = 256
D_FF_PAD = 2816

BF16 = jnp.bfloat16
F32 = jnp.float32


def _params(*sem):
    return pltpu.CompilerParams(dimension_semantics=sem, vmem_limit_bytes=VMEM_LIMIT)


def _rms(xf, gain):
    return xf * lax.rsqrt(jnp.mean(xf * xf, axis=-1, keepdims=True) + EPS) * gain


def _rmsnorm_kernel(x_ref, g_ref, h_ref):
    h_ref[...] = _rms(x_ref[...], g_ref[...]).astype(h_ref.dtype)


def rmsnorm_call(x2, gain):
    M, D = x2.shape
    return pl.pallas_call(
        _rmsnorm_kernel,
        out_shape=jax.ShapeDtypeStruct((M, D), BF16),
        grid=(M // ROW_TILE,),
        in_specs=[pl.BlockSpec((ROW_TILE, D), lambda i: (i, 0)),
                  pl.BlockSpec((1, D), lambda i: (0, 0))],
        out_specs=pl.BlockSpec((ROW_TILE, D), lambda i: (i, 0)),
        compiler_params=_params("parallel"),
        name="rmsnorm",
    )(x2, gain.reshape(1, D))


def _proj_kernel(h_ref, w_ref, o_ref, *, n_chunk):
    h = h_ref[...]
    for c in range(o_ref.shape[1] // n_chunk):
        sl = slice(c * n_chunk, (c + 1) * n_chunk)
        o_ref[:, sl] = jnp.dot(h, w_ref[:, sl], preferred_element_type=F32).astype(o_ref.dtype)


def proj_call(h, w):
    M, D = h.shape
    N = w.shape[1]
    return pl.pallas_call(
        functools.partial(_proj_kernel, n_chunk=512),
        out_shape=jax.ShapeDtypeStruct((M, N), BF16),
        grid=(M // ROW_TILE,),
        in_specs=[pl.BlockSpec((ROW_TILE, D), lambda i: (i, 0)),
                  pl.BlockSpec((D, N), lambda i: (0, 0))],
        out_specs=pl.BlockSpec((ROW_TILE, N), lambda i: (i, 0)),
        compiler_params=_params("parallel"),
        name="qkv_proj",
    )(h, w)


def _wo_kernel(m_ref, w_ref, x_ref, g_ref, xo_ref, h_ref):
    xn = x_ref[...] + jnp.dot(m_ref[...], w_ref[...], preferred_element_type=F32)
    xo_ref[...] = xn
    h_ref[...] = _rms(xn, g_ref[...]).astype(h_ref.dtype)


def wo_call(m, w, x2, gain):
    M, D = x2.shape
    row = lambda i: (i, 0)
    const = lambda i: (0, 0)
    return pl.pallas_call(
        _wo_kernel,
        out_shape=(jax.ShapeDtypeStruct((M, D), F32), jax.ShapeDtypeStruct((M, D), BF16)),
        grid=(M // ROW_TILE,),
        in_specs=[pl.BlockSpec((ROW_TILE, D), row), pl.BlockSpec((D, D), const),
                  pl.BlockSpec((ROW_TILE, D), row), pl.BlockSpec((1, D), const)],
        out_specs=(pl.BlockSpec((ROW_TILE, D), row), pl.BlockSpec((ROW_TILE, D), row)),
        compiler_params=_params("parallel"),
        name="wo_residual_norm",
    )(m, w, x2, gain.reshape(1, D))


def _causal_conv3(u, cw, cb, prev):
    w0, w1, w2 = cw[0:1], cw[1:2], cw[2:3]
    y = u * w2 + pltpu.roll(u, 1, 0) * w1 + pltpu.roll(u, 2, 0) * w0 + cb
    head = u[0:SUBLANES]
    ext = jnp.concatenate([prev, head], axis=0)
    e1 = pltpu.roll(ext, 1, 0)[SUBLANES:]
    e2 = pltpu.roll(ext, 2, 0)[SUBLANES:]
    y_head = head * w2 + e1 * w1 + e2 * w0 + cb
    return jnp.concatenate([y_head, y[SUBLANES:]], axis=0)


def _ffn_kernel(h_ref, x_ref, wg_ref, wv_ref, cwg_ref, cwv_ref, cbg_ref, cbv_ref, wd_ref, g_ref,
                xo_ref, ho_ref, acc_ref, carry_ref, *, tiles_per_seq):
    i = pl.program_id(0)
    f = pl.program_id(1)
    tm = h_ref.shape[0]

    @pl.when(i % tiles_per_seq == 0)
    def _():
        carry_ref[:, pl.ds(f, 1)] = jnp.zeros((2, 1) + carry_ref.shape[2:], F32)

    h = h_ref[...]
    ug = jnp.dot(h, wg_ref[...], preferred_element_type=F32)
    uv = jnp.dot(h, wv_ref[...], preferred_element_type=F32)
    prev_g = carry_ref[0, f]
    prev_v = carry_ref[1, f]
    carry_ref[0, f] = ug[tm - SUBLANES:]
    carry_ref[1, f] = uv[tm - SUBLANES:]
    yg = _causal_conv3(ug, cwg_ref[...], cbg_ref[...], prev_g)
    yv = _causal_conv3(uv, cwv_ref[...], cbv_ref[...], prev_v)
    act = yg * (1.0 / (1.0 + jnp.exp(-yg))) * yv
    part = jnp.dot(act.astype(BF16), wd_ref[...], preferred_element_type=F32)

    @pl.when(f == 0)
    def _():
        acc_ref[...] = part

    @pl.when(f > 0)
    def _():
        acc_ref[...] += part

    @pl.when(f == pl.num_programs(1) - 1)
    def _():
        xn = x_ref[...] + acc_ref[...]
        xo_ref[...] = xn
        ho_ref[...] = _rms(xn, g_ref[...]).astype(ho_ref.dtype)


def ffn_call(h, x2, wg, wv, cwg, cwv, cbg, cbv, wd, gain, seq, out_dtype):
    M, D = x2.shape
    nf = D_FF_PAD // FFN_TF
    row = lambda i, f: (i, 0)
    colf = lambda i, f: (0, f)
    return pl.pallas_call(
        functools.partial(_ffn_kernel, tiles_per_seq=seq // FFN_TM),
        out_shape=(jax.ShapeDtypeStruct((M, D), F32), jax.ShapeDtypeStruct((M, D), out_dtype)),
        grid=(M // FFN_TM, nf),
        in_specs=[pl.BlockSpec((FFN_TM, D), row), pl.BlockSpec((FFN_TM, D), row),
                  pl.BlockSpec((D, FFN_TF), colf), pl.BlockSpec((D, FFN_TF), colf),
                  pl.BlockSpec((3, FFN_TF), colf), pl.BlockSpec((3, FFN_TF), colf),
                  pl.BlockSpec((1, FFN_TF), colf), pl.BlockSpec((1, FFN_TF), colf),
                  pl.BlockSpec((FFN_TF, D), lambda i, f: (f, 0)),
                  pl.BlockSpec((1, D), lambda i, f: (0, 0))],
        out_specs=(pl.BlockSpec((FFN_TM, D), row), pl.BlockSpec((FFN_TM, D), row)),
        scratch_shapes=[pltpu.VMEM((FFN_TM, D), F32),
                        pltpu.VMEM((2, nf, SUBLANES, FFN_TF), F32)],
        compiler_params=_params("arbitrary", "arbitrary"),
        name="conv_gated_ffn",
    )(h, x2, wg, wv, cwg, cwv, cbg, cbv, wd, gain.reshape(1, D))


def _half_masks(q):
    lane = lax.broadcasted_iota(jnp.int32, q.shape, 1)
    zero = jnp.zeros_like(q)
    scale = jnp.asarray(DA_QK_DIM ** -0.5, q.dtype)
    return (jnp.where(lane < 64, q, zero) * scale, jnp.where(lane >= 64, q, zero) * scale)


def _qk(q, k):
    return lax.dot_general(q, k, (((1,), (1,)), ((), ())), preferred_element_type=F32)


def _da_kernel(q_ref, k_ref, v_ref, bias_ref, lam_ref, sub_ref, o_ref, s_ref, *, lambda_init):
    T = ATT_T
    nq = q_ref.shape[0] // T
    r = lax.broadcasted_iota(jnp.int32, (T, T), 0)
    cc = lax.broadcasted_iota(jnp.int32, (T, T), 1)
    causal = r >= cc
    lam = lam_ref[...]
    lam_full = (jnp.exp(jnp.sum(lam[0:1] * lam[1:2], keepdims=True))
                - jnp.exp(jnp.sum(lam[2:3] * lam[3:4], keepdims=True)) + lambda_init)
    for qi in range(nq):
        qs = _half_masks(q_ref[qi * T:(qi + 1) * T, :])
        outs = []
        for c in range(2):
            mx = None
            for j in range(qi + 1):
                s = _qk(qs[c], k_ref[j * T:(j + 1) * T, :])
                if j == qi - 1:
                    s = s + bias_ref[c, :, 0:T]
                if j == qi:
                    s = jnp.where(causal, s + bias_ref[c, :, T:2 * T], NEG)
                s_ref[c, :, j * T:(j + 1) * T] = s
                t = jnp.maximum(s[:, :LANES], s[:, LANES:])
                mx = t if mx is None else jnp.maximum(mx, t)
            m = jnp.broadcast_to(jnp.max(mx, axis=-1, keepdims=True), (T, LANES))
            m2 = jnp.concatenate([m, m], axis=1)
            lsum = None
            acc = None
            for j in range(qi + 1):
                p = jnp.exp(s_ref[c, :, j * T:(j + 1) * T] - m2)
                t = p[:, :LANES] + p[:, LANES:]
                lsum = t if lsum is None else lsum + t
                pv = jnp.dot(p.astype(BF16), v_ref[j * T:(j + 1) * T, :], preferred_element_type=F32)
                acc = pv if acc is None else acc + pv
            outs.append(acc / jnp.sum(lsum, axis=-1, keepdims=True))
        o = outs[0] - lam_full * outs[1]
        o = _rms(o, sub_ref[...]) * (1.0 - lambda_init)
        o_ref[qi * T:(qi + 1) * T, :] = o.astype(o_ref.dtype)


def da_call(qkv, bias_near, lam, subln, batch, seq, lambda_init):
    T = ATT_T
    nh = DA_HEADS
    return pl.pallas_call(
        functools.partial(_da_kernel, lambda_init=lambda_init),
        out_shape=jax.ShapeDtypeStruct((batch * seq, nh * DA_V_DIM), BF16),
        grid=(batch, nh),
        in_specs=[pl.BlockSpec((seq, LANES), lambda b, h: (b, h)),
                  pl.BlockSpec((seq, LANES), lambda b, h: (b, nh + h)),
                  pl.BlockSpec((seq, LANES), lambda b, h: (b, 2 * nh + h)),
                  pl.BlockSpec((2, T, 2 * T), lambda b, h: (h, 0, 0)),
                  pl.BlockSpec((4, DA_QK_DIM), lambda b, h: (0, 0)),
                  pl.BlockSpec((1, DA_V_DIM), lambda b, h: (0, 0))],
        out_specs=pl.BlockSpec((seq, LANES), lambda b, h: (b, h)),
        scratch_shapes=[pltpu.VMEM((2, T, seq), F32)],
        compiler_params=_params("parallel", "parallel"),
        name="diff_attention",
    )(qkv, qkv, qkv, bias_near, lam, subln.reshape(1, DA_V_DIM))


def _split2(x):
    hi = x.astype(BF16)
    mid = (x - hi.astype(F32)).astype(BF16)
    return hi, mid


def _sb_kernel(q_ref, k_ref, v_ref, o_ref, vm_ref):
    T = ATT_T
    nq = q_ref.shape[0] // T
    kr = lax.broadcasted_iota(jnp.int32, (T, T), 0)
    kc = lax.broadcasted_iota(jnp.int32, (T, T), 1)
    suffix = jnp.where(kr > kc, 1.0, 0.0).astype(BF16)
    strict = kc < kr
    lane = lax.broadcasted_iota(jnp.int32, v_ref.shape, 1)
    v_all = v_ref[...]
    vm_ref[0] = jnp.where(lane < 64, v_all, jnp.zeros_like(v_all))
    vm_ref[1] = jnp.where(lane >= 64, v_all, jnp.zeros_like(v_all))
    for qi in range(nq):
        qs = _half_masks(q_ref[qi * T:(qi + 1) * T, :])
        acc = None
        for c in range(2):
            carry = None
            for j in range(qi, -1, -1):
                z = _qk(qs[c], k_ref[j * T:(j + 1) * T, :])
                sp = jnp.maximum(z, 0.0) + jnp.log(1.0 + jnp.exp(-jnp.abs(z)))
                log_1m_beta = -sp
                if j == qi:
                    log_1m_beta = jnp.where(strict, log_1m_beta, 0.0)
                hi, mid = _split2(log_1m_beta)
                between = (jnp.dot(hi, suffix, preferred_element_type=F32)
                           + jnp.dot(mid, suffix, preferred_element_type=F32))
                if carry is not None:
                    between = between + carry
                a = jnp.exp((z - sp) + between)
                if j == qi:
                    a = jnp.where(strict, a, 0.0)
                pv = jnp.dot(a.astype(BF16), vm_ref[c, j * T:(j + 1) * T, :], preferred_element_type=F32)
                acc = pv if acc is None else acc + pv
                if j > 0:
                    rs = jnp.sum(log_1m_beta, axis=-1, keepdims=True)
                    carry = rs if carry is None else carry + rs
        o_ref[qi * T:(qi + 1) * T, :] = acc.astype(o_ref.dtype)


def sb_call(qkv, batch, seq):
    npair = SB_HEADS // 2
    return pl.pallas_call(
        _sb_kernel,
        out_shape=jax.ShapeDtypeStruct((batch * seq, SB_HEADS * SB_DIM), BF16),
        grid=(batch, npair),
        in_specs=[pl.BlockSpec((seq, LANES), lambda b, p: (b, p)),
                  pl.BlockSpec((seq, LANES), lambda b, p: (b, npair + p)),
                  pl.BlockSpec((seq, LANES), lambda b, p: (b, 2 * npair + p))],
        out_specs=pl.BlockSpec((seq, LANES), lambda b, p: (b, p)),
        scratch_shapes=[pltpu.VMEM((2, seq, LANES), BF16)],
        compiler_params=_params("parallel", "parallel"),
        name="stick_breaking_attention",
    )(qkv, qkv, qkv)


def _sw_kernel(sink_ref, q_ref, k_ref, v_ref, bias_ref, o_ref):
    W = SW_BLOCK
    p_id = pl.program_id(1)
    qi = pl.program_id(2)
    r = lax.broadcasted_iota(jnp.int32, (W, W), 0)
    cidx = lax.broadcasted_iota(jnp.int32, (W, W), 1)
    lane = lax.broadcasted_iota(jnp.int32, (W, LANES), 1)
    for sub in range(SW_TQ // W):
        n = qi * (SW_TQ // W) + sub
        qs = _half_masks(q_ref[sub * W:(sub + 1) * W, :])
        cur = pl.multiple_of(n * W, W)
        prev = pl.multiple_of(jnp.maximum(n - 1, 0) * W, W)
        k_cur, v_cur = k_ref[pl.ds(cur, W), :], v_ref[pl.ds(cur, W), :]
        k_prev, v_prev = k_ref[pl.ds(prev, W), :], v_ref[pl.ds(prev, W), :]
        prev_ok = (cidx > r) & (n > 0)
        out = jnp.zeros((W, LANES), F32)
        for c in range(2):
            s_prev = jnp.where(prev_ok, _qk(qs[c], k_prev) + bias_ref[c, :, 0:W], NEG)
            s_cur = jnp.where(cidx <= r, _qk(qs[c], k_cur) + bias_ref[c, :, W:2 * W], NEG)
            sink = sink_ref[2 * p_id + c]
            m = jnp.maximum(jnp.maximum(jnp.max(s_prev, axis=-1, keepdims=True),
                                        jnp.max(s_cur, axis=-1, keepdims=True)), sink)
            e_prev = jnp.exp(s_prev - m)
            e_cur = jnp.exp(s_cur - m)
            denom = (jnp.sum(e_prev, axis=-1, keepdims=True) + jnp.sum(e_cur, axis=-1, keepdims=True)
                     + jnp.exp(sink - m))
            pv = (jnp.dot(e_prev.astype(BF16), v_prev, preferred_element_type=F32)
                  + jnp.dot(e_cur.astype(BF16), v_cur, preferred_element_type=F32))
            half = (lane < 64) if c == 0 else (lane >= 64)
            out = jnp.where(half, pv / denom, out)
        o_ref[sub * W:(sub + 1) * W, :] = out.astype(o_ref.dtype)


def sw_call(qkv, bias_band, sinks, batch, seq):
    nq = seq // SW_TQ
    npair = SW_Q_HEADS // 2
    q_blocks = SW_Q_HEADS * SW_DIM // LANES
    return pl.pallas_call(
        _sw_kernel,
        out_shape=jax.ShapeDtypeStruct((batch * seq, SW_Q_HEADS * SW_DIM), BF16),
        grid=(batch, npair, nq),
        in_specs=[pl.BlockSpec(memory_space=pltpu.SMEM),
                  pl.BlockSpec((SW_TQ, LANES), lambda b, p, i: (b * nq + i, p)),
                  pl.BlockSpec((seq, LANES), lambda b, p, i: (b, q_blocks + p // 2)),
                  pl.BlockSpec((seq, LANES), lambda b, p, i: (b, q_blocks + SW_KV_HEADS + p // 2)),
                  pl.BlockSpec((2, SW_BLOCK, 2 * SW_BLOCK), lambda b, p, i: (p, 0, 0))],
        out_specs=pl.BlockSpec((SW_TQ, LANES), lambda b, p, i: (b * nq + i, p)),
        compiler_params=_params("parallel", "parallel", "parallel"),
        name="sliding_window_attention",
    )(sinks, qkv, qkv, qkv, bias_band)


def _t5_bucket(dist):
    max_exact = N_BUCKETS // 2
    d = jnp.maximum(dist, 0)
    large = max_exact + (jnp.log(jnp.maximum(d, 1).astype(F32) / max_exact)
                         / math.log(MAX_DISTANCE / max_exact) * (N_BUCKETS - max_exact)).astype(jnp.int32)
    large = jnp.minimum(large, N_BUCKETS - 1)
    return jnp.where(d < max_exact, d, large)


def _bias_band(rel_bias, T):
    y = np.arange(3 * T)
    dist = np.clip(np.where(y <= 2 * T, T - y, 4 * T - y), 0, 2 * T - 1)
    row = rel_bias.astype(F32)[_t5_bucket(jnp.asarray(dist, jnp.int32))].T
    flat = jnp.tile(row, (1, T))[:, :T * (3 * T - 1)]
    return flat.reshape(-1, T, 3 * T - 1)[:, :, :2 * T]


def _lambda_init(layer):
    return 0.8 - 0.6 * math.exp(-0.3 * layer)


def _dup_kv_heads(w):
    q_w = SW_Q_HEADS * SW_DIM
    kv = w[:, q_w:].reshape(w.shape[0], 2 * SW_KV_HEADS, 1, SW_DIM)
    kv = jnp.broadcast_to(kv, (w.shape[0], 2 * SW_KV_HEADS, 2, SW_DIM)).reshape(w.shape[0], -1)
    return jnp.concatenate([w[:, :q_w], kv], axis=1)


def _pad_ff(a, axis):
    pad = [(0, 0)] * a.ndim
    pad[axis] = (0, D_FF_PAD - D_FF)
    return jnp.pad(a, pad)


def kernel(x, rel_bias, attn_norm, ffn_norm, w_o, da_w_qkv, da_lambda, da_subln, sb_w_qkv, sw_w_qkv,
           sw_sinks, ffn_w_up, ffn_conv_w, ffn_conv_b, ffn_w_down, final_norm):
    B, S, D = x.shape
    x2 = x.reshape(B * S, D)
    da_bias_near = _bias_band(rel_bias, ATT_T) - rel_bias.astype(F32)[N_BUCKETS - 1][:, None, None]
    sw_bias = _bias_band(rel_bias, SW_BLOCK)

    h = rmsnorm_call(x2, attn_norm[0])
    for layer in range(DEPTH):
        mixer = layer % N_MIXERS
        slot = layer // N_MIXERS
        if mixer == 0:
            qkv = proj_call(h, da_w_qkv[slot].astype(BF16))
            m = da_call(qkv, da_bias_near, da_lambda[slot], da_subln[slot], B, S, _lambda_init(layer))
        elif mixer == 1:
            qkv = proj_call(h, sb_w_qkv[slot].astype(BF16))
            m = sb_call(qkv, B, S)
        else:
            qkv = proj_call(h, _dup_kv_heads(sw_w_qkv[slot]).astype(BF16))
            m = sw_call(qkv, sw_bias, sw_sinks[slot], B, S)
        x2, h = wo_call(m, w_o[layer].astype(BF16), x2, ffn_norm[layer])

        w_up = ffn_w_up[layer]
        wg = _pad_ff(w_up[:, :D_FF], 1).astype(BF16)
        wv = _pad_ff(w_up[:, D_FF:], 1).astype(BF16)
        cw, cb = ffn_conv_w[layer], ffn_conv_b[layer].reshape(1, -1)
        last = layer == DEPTH - 1
        gain = final_norm if last else attn_norm[layer + 1]
        x2, h = ffn_call(h, x2, wg, wv,
                         _pad_ff(cw[:, :D_FF], 1), _pad_ff(cw[:, D_FF:], 1),
                         _pad_ff(cb[:, :D_FF], 1), _pad_ff(cb[:, D_FF:], 1),
                         _pad_ff(ffn_w_down[layer], 0).astype(BF16), gain, S,
                         F32 if last else BF16)
    return h.reshape(B, S, D)
```

```python
import functools
import math

import jax
import jax.numpy as jnp
import numpy as np
from jax import lax
from jax.experimental import pallas as pl
from jax.experimental.pallas import tpu as pltpu

D_MODEL = 1024
DEPTH = 4
N_MIXERS = 3
N_BUCKETS = 32
MAX_DISTANCE = 128
DA_HEADS = 8
DA_QK_DIM = 64
DA_V_DIM = 128
SB_HEADS = 16
SB_DIM = 64
SW_Q_HEADS = 16
SW_KV_HEADS = 4
SW_DIM = 64
SW_BLOCK = 128
D_FF = 2752
EPS = 1e-6
NEG = -1e30

LANES = 128
SUBLANES = 8
VMEM_LIMIT = 52 * 1024 * 1024

ATT_T = 256
ROW_TILE = 512
FFN_TM = 1024
FFN_TF = 256
D_FF_PAD = 2816

BF16 = jnp.bfloat16
F32 = jnp.float32


def _params(*sem):
    return pltpu.CompilerParams(dimension_semantics=sem, vmem_limit_bytes=VMEM_LIMIT)


def _rms(xf, gain):
    return xf * lax.rsqrt(jnp.mean(xf * xf, axis=-1, keepdims=True) + EPS) * gain


def _rmsnorm_kernel(x_ref, g_ref, h_ref):
    h_ref[...] = _rms(x_ref[...], g_ref[...]).astype(h_ref.dtype)


def rmsnorm_call(x2, gain):
    M, D = x2.shape
    return pl.pallas_call(
        _rmsnorm_kernel,
        out_shape=jax.ShapeDtypeStruct((M, D), BF16),
        grid=(M // ROW_TILE,),
        in_specs=[pl.BlockSpec((ROW_TILE, D), lambda i: (i, 0)),
                  pl.BlockSpec((1, D), lambda i: (0, 0))],
        out_specs=pl.BlockSpec((ROW_TILE, D), lambda i: (i, 0)),
        compiler_params=_params("parallel"),
        name="rmsnorm",
    )(x2, gain.reshape(1, D))


def _proj_kernel(h_ref, w_ref, o_ref, *, n_chunk):
    h = h_ref[...]
    for c in range(o_ref.shape[1] // n_chunk):
        sl = slice(c * n_chunk, (c + 1) * n_chunk)
        o_ref[:, sl] = jnp.dot(h, w_ref[:, sl], preferred_element_type=F32).astype(o_ref.dtype)


def proj_call(h, w):
    M, D = h.shape
    N = w.shape[1]
    return pl.pallas_call(
        functools.partial(_proj_kernel, n_chunk=512),
        out_shape=jax.ShapeDtypeStruct((M, N), BF16),
        grid=(M // ROW_TILE,),
        in_specs=[pl.BlockSpec((ROW_TILE, D), lambda i: (i, 0)),
                  pl.BlockSpec((D, N), lambda i: (0, 0))],
        out_specs=pl.BlockSpec((ROW_TILE, N), lambda i: (i, 0)),
        compiler_params=_params("parallel"),
        name="qkv_proj",
    )(h, w)


def _wo_kernel(m_ref, w_ref, x_ref, g_ref, xo_ref, h_ref):
    xn = x_ref[...] + jnp.dot(m_ref[...], w_ref[...], preferred_element_type=F32)
    xo_ref[...] = xn
    h_ref[...] = _rms(xn, g_ref[...]).astype(h_ref.dtype)


def wo_call(m, w, x2, gain):
    M, D = x2.shape
    row = lambda i: (i, 0)
    const = lambda i: (0, 0)
    return pl.pallas_call(
        _wo_kernel,
        out_shape=(jax.ShapeDtypeStruct((M, D), F32), jax.ShapeDtypeStruct((M, D), BF16)),
        grid=(M // ROW_TILE,),
        in_specs=[pl.BlockSpec((ROW_TILE, D), row), pl.BlockSpec((D, D), const),
                  pl.BlockSpec((ROW_TILE, D), row), pl.BlockSpec((1, D), const)],
        out_specs=(pl.BlockSpec((ROW_TILE, D), row), pl.BlockSpec((ROW_TILE, D), row)),
        compiler_params=_params("parallel"),
        name="wo_residual_norm",
    )(m, w, x2, gain.reshape(1, D))


def _causal_conv3(u, cw, cb, prev):
    ext = jnp.concatenate([prev, u], axis=0)
    u1 = pltpu.roll(ext, 1, 0)[SUBLANES:]
    u2 = pltpu.roll(ext, 2, 0)[SUBLANES:]
    return u * cw[2:3] + u1 * cw[1:2] + u2 * cw[0:1] + cb


def _ffn_up_kernel(h_ref, wg_ref, wv_ref, cwg_ref, cwv_ref, cbg_ref, cbv_ref, g_ref, carry_ref,
                   *, tiles_per_seq):
    tm = h_ref.shape[0]

    @pl.when(pl.program_id(0) % tiles_per_seq == 0)
    def _():
        carry_ref[...] = jnp.zeros(carry_ref.shape, F32)

    h = h_ref[...]
    for c in range(g_ref.shape[1] // FFN_TF):
        sl = slice(c * FFN_TF, (c + 1) * FFN_TF)
        ug = jnp.dot(h, wg_ref[:, sl], preferred_element_type=F32)
        uv = jnp.dot(h, wv_ref[:, sl], preferred_element_type=F32)
        prev_g = carry_ref[0, :, sl]
        prev_v = carry_ref[1, :, sl]
        carry_ref[0, :, sl] = ug[tm - SUBLANES:]
        carry_ref[1, :, sl] = uv[tm - SUBLANES:]
        yg = _causal_conv3(ug, cwg_ref[:, sl], cbg_ref[:, sl], prev_g)
        yv = _causal_conv3(uv, cwv_ref[:, sl], cbv_ref[:, sl], prev_v)
        g_ref[:, sl] = (yg * (1.0 / (1.0 + jnp.exp(-yg))) * yv).astype(g_ref.dtype)


def ffn_up_call(h, wg, wv, cwg, cwv, cbg, cbv, seq):
    M, D = h.shape
    F = wg.shape[1]
    row = lambda i: (i, 0)
    const = lambda i: (0, 0)
    resident = lambda shape: pl.BlockSpec(shape, const, pipeline_mode=pl.Buffered(1))
    return pl.pallas_call(
        functools.partial(_ffn_up_kernel, tiles_per_seq=seq // FFN_TM),
        out_shape=jax.ShapeDtypeStruct((M, F), BF16),
        grid=(M // FFN_TM,),
        in_specs=[pl.BlockSpec((FFN_TM, D), row), resident((D, F)), resident((D, F)),
                  resident((3, F)), resident((3, F)), resident((1, F)), resident((1, F))],
        out_specs=pl.BlockSpec((FFN_TM, F), row),
        scratch_shapes=[pltpu.VMEM((2, SUBLANES, F), F32)],
        compiler_params=_params("arbitrary"),
        name="ffn_up_conv_gate",
    )(h, wg, wv, cwg, cwv, cbg, cbv)


def _ffn_down_kernel(a_ref, w_ref, x_ref, g_ref, xo_ref, h_ref):
    xn = x_ref[...] + jnp.dot(a_ref[...], w_ref[...], preferred_element_type=F32)
    xo_ref[...] = xn
    h_ref[...] = _rms(xn, g_ref[...]).astype(h_ref.dtype)


def ffn_down_call(a, w, x2, gain, out_dtype):
    M, D = x2.shape
    F = a.shape[1]
    row = lambda i: (i, 0)
    const = lambda i: (0, 0)
    return pl.pallas_call(
        _ffn_down_kernel,
        out_shape=(jax.ShapeDtypeStruct((M, D), F32), jax.ShapeDtypeStruct((M, D), out_dtype)),
        grid=(M // ROW_TILE,),
        in_specs=[pl.BlockSpec((ROW_TILE, F), row),
                  pl.BlockSpec((F, D), const, pipeline_mode=pl.Buffered(1)),
                  pl.BlockSpec((ROW_TILE, D), row), pl.BlockSpec((1, D), const)],
        out_specs=(pl.BlockSpec((ROW_TILE, D), row), pl.BlockSpec((ROW_TILE, D), row)),
        compiler_params=_params("parallel"),
        name="ffn_down_residual_norm",
    )(a, w, x2, gain.reshape(1, D))


def _half_masks(q):
    lane = lax.broadcasted_iota(jnp.int32, q.shape, 1)
    zero = jnp.zeros_like(q)
    scale = jnp.asarray(DA_QK_DIM ** -0.5, q.dtype)
    return (jnp.where(lane < 64, q, zero) * scale, jnp.where(lane >= 64, q, zero) * scale)


def _qk(q, k):
    return lax.dot_general(q, k, (((1,), (1,)), ((), ())), preferred_element_type=F32)


def _da_kernel(q_ref, k_ref, v_ref, bias_ref, lam_ref, sub_ref, o_ref, s_ref, *, lambda_init):
    T = ATT_T
    nq = q_ref.shape[0] // T
    r = lax.broadcasted_iota(jnp.int32, (T, T), 0)
    cc = lax.broadcasted_iota(jnp.int32, (T, T), 1)
    causal = r >= cc
    lam = lam_ref[...]
    lam_full = (jnp.exp(jnp.sum(lam[0:1] * lam[1:2], keepdims=True))
                - jnp.exp(jnp.sum(lam[2:3] * lam[3:4], keepdims=True)) + lambda_init)
    for qi in range(nq):
        q01 = jnp.concatenate(_half_masks(q_ref[qi * T:(qi + 1) * T, :]), axis=0)
        mx = [None, None]
        for j in range(qi + 1):
            ss = _qk(q01, k_ref[j * T:(j + 1) * T, :])
            for c in range(2):
                s = ss[c * T:(c + 1) * T]
                if j == qi - 1:
                    s = s + bias_ref[c, :, 0:T]
                if j == qi:
                    s = jnp.where(causal, s + bias_ref[c, :, T:2 * T], NEG)
                s_ref[c, :, j * T:(j + 1) * T] = s
                t = jnp.maximum(s[:, :LANES], s[:, LANES:])
                mx[c] = t if mx[c] is None else jnp.maximum(mx[c], t)
        m2 = []
        for c in range(2):
            m = jnp.broadcast_to(jnp.max(mx[c], axis=-1, keepdims=True), (T, LANES))
            m2.append(jnp.concatenate([m, m], axis=1))
        lsum = [None, None]
        acc = None
        for j in range(qi + 1):
            ps = []
            for c in range(2):
                p = jnp.exp(s_ref[c, :, j * T:(j + 1) * T] - m2[c])
                t = p[:, :LANES] + p[:, LANES:]
                lsum[c] = t if lsum[c] is None else lsum[c] + t
                ps.append(p.astype(BF16))
            pv = jnp.dot(jnp.concatenate(ps, axis=0), v_ref[j * T:(j + 1) * T, :],
                         preferred_element_type=F32)
            acc = pv if acc is None else acc + pv
        outs = [acc[c * T:(c + 1) * T] / jnp.sum(lsum[c], axis=-1, keepdims=True) for c in range(2)]
        o = outs[0] - lam_full * outs[1]
        o = _rms(o, sub_ref[...]) * (1.0 - lambda_init)
        o_ref[qi * T:(qi + 1) * T, :] = o.astype(o_ref.dtype)


def da_call(qkv, bias_near, lam, subln, batch, seq, lambda_init):
    T = ATT_T
    nh = DA_HEADS
    return pl.pallas_call(
        functools.partial(_da_kernel, lambda_init=lambda_init),
        out_shape=jax.ShapeDtypeStruct((batch * seq, nh * DA_V_DIM), BF16),
        grid=(batch, nh),
        in_specs=[pl.BlockSpec((seq, LANES), lambda b, h: (b, h)),
                  pl.BlockSpec((seq, LANES), lambda b, h: (b, nh + h)),
                  pl.BlockSpec((seq, LANES), lambda b, h: (b, 2 * nh + h)),
                  pl.BlockSpec((2, T, 2 * T), lambda b, h: (h, 0, 0)),
                  pl.BlockSpec((4, DA_QK_DIM), lambda b, h: (0, 0)),
                  pl.BlockSpec((1, DA_V_DIM), lambda b, h: (0, 0))],
        out_specs=pl.BlockSpec((seq, LANES), lambda b, h: (b, h)),
        scratch_shapes=[pltpu.VMEM((2, T, seq), F32)],
        compiler_params=_params("parallel", "parallel"),
        name="diff_attention",
    )(qkv, qkv, qkv, bias_near, lam, subln.reshape(1, DA_V_DIM))


def _split2(x):
    hi = x.astype(BF16)
    mid = (x - hi.astype(F32)).astype(BF16)
    return hi, mid


def _sb_kernel(q_ref, k_ref, v_ref, o_ref, lb_ref, hm_ref):
    T = ATT_T
    nq = q_ref.shape[0] // T
    kr = lax.broadcasted_iota(jnp.int32, (T, T), 0)
    kc = lax.broadcasted_iota(jnp.int32, (T, T), 1)
    suffix = jnp.where(kr > kc, 1.0, 0.0).astype(BF16)
    suffix2 = jnp.concatenate([suffix, suffix], axis=0)
    strict = kc < kr
    lane = lax.broadcasted_iota(jnp.int32, (T, LANES), 1)
    for qi in range(nq):
        n = qi + 1
        q01 = jnp.concatenate(_half_masks(q_ref[qi * T:(qi + 1) * T, :]), axis=0)
        carry = [None, None]
        for j in range(qi, -1, -1):
            zz = _qk(q01, k_ref[j * T:(j + 1) * T, :])
            for c in range(2):
                z = zz[c * T:(c + 1) * T]
                sp = jnp.maximum(z, 0.0) + jnp.log(1.0 + jnp.exp(-jnp.abs(z)))
                log_1m_beta = -sp
                if j == qi:
                    log_1m_beta = jnp.where(strict, log_1m_beta, 0.0)
                hi, mid = _split2(log_1m_beta)
                blk = (j * 2 + c) * T
                hm_ref[blk:blk + T, 0:T] = hi
                hm_ref[blk:blk + T, T:2 * T] = mid
                lb = z - sp
                if carry[c] is not None:
                    lb = lb + carry[c]
                lb_ref[blk:blk + T, :] = lb
                if j > 0:
                    rs = jnp.sum(log_1m_beta, axis=-1, keepdims=True)
                    carry[c] = rs if carry[c] is None else carry[c] + rs
        between = jnp.dot(hm_ref[0:2 * n * T, :], suffix2, preferred_element_type=F32)
        acc = None
        for j in range(n):
            a = jnp.exp(lb_ref[2 * j * T:2 * (j + 1) * T, :] + between[2 * j * T:2 * (j + 1) * T])
            if j == qi:
                a = jnp.where(jnp.concatenate([strict, strict], axis=0), a, 0.0)
            pv = jnp.dot(a.astype(BF16), v_ref[j * T:(j + 1) * T, :], preferred_element_type=F32)
            acc = pv if acc is None else acc + pv
        o_ref[qi * T:(qi + 1) * T, :] = jnp.where(lane < 64, acc[0:T], acc[T:2 * T]).astype(o_ref.dtype)


def sb_call(qkv, batch, seq):
    npair = SB_HEADS // 2
    return pl.pallas_call(
        _sb_kernel,
        out_shape=jax.ShapeDtypeStruct((batch * seq, SB_HEADS * SB_DIM), BF16),
        grid=(batch, npair),
        in_specs=[pl.BlockSpec((seq, LANES), lambda b, p: (b, p)),
                  pl.BlockSpec((seq, LANES), lambda b, p: (b, npair + p)),
                  pl.BlockSpec((seq, LANES), lambda b, p: (b, 2 * npair + p))],
        out_specs=pl.BlockSpec((seq, LANES), lambda b, p: (b, p)),
        scratch_shapes=[pltpu.VMEM((2 * seq, ATT_T), F32), pltpu.VMEM((2 * seq, 2 * ATT_T), BF16)],
        compiler_params=_params("parallel", "parallel"),
        name="stick_breaking_attention",
    )(qkv, qkv, qkv)


def _sw_kernel(sink_ref, q_ref, k_ref, v_ref, bias_ref, o_ref, s_ref):
    W = SW_BLOCK
    nb = q_ref.shape[0] // W
    p_id = pl.program_id(1)
    r = lax.broadcasted_iota(jnp.int32, (2 * W, 2 * W), 0) & (W - 1)
    cidx = lax.broadcasted_iota(jnp.int32, (2 * W, 2 * W), 1)
    valid = ((cidx < W) & (cidx > r)) | ((cidx >= W) & (cidx - W <= r))
    biasm = jnp.where(valid, bias_ref[...], NEG)
    row = lax.broadcasted_iota(jnp.int32, (2 * W, 1), 0)
    sink = jnp.where(row < W, sink_ref[2 * p_id], sink_ref[2 * p_id + 1])
    lane = lax.broadcasted_iota(jnp.int32, (W, LANES), 1)
    for n in range(nb):
        q01 = jnp.concatenate(_half_masks(q_ref[n * W:(n + 1) * W, :]), axis=0)
        if n == 0:
            s_ref[0, :, W:] = _qk(q01, k_ref[0:W, :]) + biasm[:, W:]
        else:
            s_ref[n] = _qk(q01, k_ref[(n - 1) * W:(n + 1) * W, :]) + biasm
    for n in range(nb):
        s = s_ref[0, :, W:] if n == 0 else s_ref[n]
        m = jnp.maximum(jnp.max(s, axis=-1, keepdims=True), sink)
        e = jnp.exp(s - m)
        denom = jnp.sum(e, axis=-1, keepdims=True) + jnp.exp(sink - m)
        v = v_ref[0:W, :] if n == 0 else v_ref[(n - 1) * W:(n + 1) * W, :]
        o = jnp.dot(e.astype(BF16), v, preferred_element_type=F32) / denom
        o_ref[n * W:(n + 1) * W, :] = jnp.where(lane < 64, o[0:W], o[W:]).astype(o_ref.dtype)


def sw_call(qkv, bias_band, sinks, batch, seq):
    npair = SW_Q_HEADS // 2
    q_blocks = SW_Q_HEADS * SW_DIM // LANES
    W = SW_BLOCK
    return pl.pallas_call(
        _sw_kernel,
        out_shape=jax.ShapeDtypeStruct((batch * seq, SW_Q_HEADS * SW_DIM), BF16),
        grid=(batch, npair),
        in_specs=[pl.BlockSpec(memory_space=pltpu.SMEM),
                  pl.BlockSpec((seq, LANES), lambda b, p: (b, p)),
                  pl.BlockSpec((seq, LANES), lambda b, p: (b, q_blocks + p // 2)),
                  pl.BlockSpec((seq, LANES), lambda b, p: (b, q_blocks + SW_KV_HEADS + p // 2)),
                  pl.BlockSpec((None, 2 * W, 2 * W), lambda b, p: (p, 0, 0))],
        out_specs=pl.BlockSpec((seq, LANES), lambda b, p: (b, p)),
        scratch_shapes=[pltpu.VMEM((seq // W, 2 * W, 2 * W), F32)],
        compiler_params=_params("parallel", "parallel"),
        name="sliding_window_attention",
    )(sinks, qkv, qkv, qkv, bias_band.reshape(npair, 2 * W, 2 * W))


def _t5_bucket(dist):
    max_exact = N_BUCKETS // 2
    d = jnp.maximum(dist, 0)
    large = max_exact + (jnp.log(jnp.maximum(d, 1).astype(F32) / max_exact)
                         / math.log(MAX_DISTANCE / max_exact) * (N_BUCKETS - max_exact)).astype(jnp.int32)
    large = jnp.minimum(large, N_BUCKETS - 1)
    return jnp.where(d < max_exact, d, large)


def _bias_band(rel_bias, T):
    y = np.arange(3 * T)
    dist = np.clip(np.where(y <= 2 * T, T - y, 4 * T - y), 0, 2 * T - 1)
    row = rel_bias.astype(F32)[_t5_bucket(jnp.asarray(dist, jnp.int32))].T
    flat = jnp.tile(row, (1, T))[:, :T * (3 * T - 1)]
    return flat.reshape(-1, T, 3 * T - 1)[:, :, :2 * T]


def _lambda_init(layer):
    return 0.8 - 0.6 * math.exp(-0.3 * layer)


def _dup_kv_heads(w):
    q_w = SW_Q_HEADS * SW_DIM
    kv = w[:, q_w:].reshape(w.shape[0], 2 * SW_KV_HEADS, 1, SW_DIM)
    kv = jnp.broadcast_to(kv, (w.shape[0], 2 * SW_KV_HEADS, 2, SW_DIM)).reshape(w.shape[0], -1)
    return jnp.concatenate([w[:, :q_w], kv], axis=1)


def _pad_ff(a, axis):
    pad = [(0, 0)] * a.ndim
    pad[axis] = (0, D_FF_PAD - D_FF)
    return jnp.pad(a, pad)


def kernel(x, rel_bias, attn_norm, ffn_norm, w_o, da_w_qkv, da_lambda, da_subln, sb_w_qkv, sw_w_qkv,
           sw_sinks, ffn_w_up, ffn_conv_w, ffn_conv_b, ffn_w_down, final_norm):
    B, S, D = x.shape
    x2 = x.reshape(B * S, D)
    da_bias_near = _bias_band(rel_bias, ATT_T) - rel_bias.astype(F32)[N_BUCKETS - 1][:, None, None]
    sw_bias = _bias_band(rel_bias, SW_BLOCK)

    h = rmsnorm_call(x2, attn_norm[0])
    for layer in range(DEPTH):
        mixer = layer % N_MIXERS
        slot = layer // N_MIXERS
        if mixer == 0:
            qkv = proj_call(h, da_w_qkv[slot].astype(BF16))
            m = da_call(qkv, da_bias_near, da_lambda[slot], da_subln[slot], B, S, _lambda_init(layer))
        elif mixer == 1:
            qkv = proj_call(h, sb_w_qkv[slot].astype(BF16))
            m = sb_call(qkv, B, S)
        else:
            qkv = proj_call(h, _dup_kv_heads(sw_w_qkv[slot]).astype(BF16))
            m = sw_call(qkv, sw_bias, sw_sinks[slot], B, S)
        x2, h = wo_call(m, w_o[layer].astype(BF16), x2, ffn_norm[layer])

        w_up = ffn_w_up[layer]
        wg = _pad_ff(w_up[:, :D_FF], 1).astype(BF16)
        wv = _pad_ff(w_up[:, D_FF:], 1).astype(BF16)
        cw, cb = ffn_conv_w[layer], ffn_conv_b[layer].reshape(1, -1)
        last = layer == DEPTH - 1
        gain = final_norm if last else attn_norm[layer + 1]
        act = ffn_up_call(h, wg, wv,
                          _pad_ff(cw[:, :D_FF], 1), _pad_ff(cw[:, D_FF:], 1),
                          _pad_ff(cb[:, :D_FF], 1), _pad_ff(cb[:, D_FF:], 1), S)
        x2, h = ffn_down_call(act, _pad_ff(ffn_w_down[layer], 0).astype(BF16), x2, gain,
                              F32 if last else BF16)
    return h.reshape(B, S, D)
```

```python
import functools
import math

import jax
import jax.numpy as jnp
import numpy as np
from jax import lax
from jax.experimental import pallas as pl
from jax.experimental.pallas import tpu as pltpu

D_MODEL = 1024
DEPTH = 4
N_MIXERS = 3
N_BUCKETS = 32
MAX_DISTANCE = 128
DA_HEADS = 8
DA_QK_DIM = 64
DA_V_DIM = 128
SB_HEADS = 16
SB_DIM = 64
SW_Q_HEADS = 16
SW_KV_HEADS = 4
SW_DIM = 64
SW_BLOCK = 128
D_FF = 2752
EPS = 1e-6
NEG = -1e30

LANES = 128
SUBLANES = 8
VMEM_LIMIT = 52 * 1024 * 1024

ATT_T = 256
ROW_TILE = 512
FFN_TM = 1024
FFN_TF = 256
D_FF_PAD = 2816

BF16 = jnp.bfloat16
F32 = jnp.float32


def _params(*sem):
    return pltpu.CompilerParams(dimension_semantics=sem, vmem_limit_bytes=VMEM_LIMIT)


def _rms(xf, gain):
    return xf * lax.rsqrt(jnp.mean(xf * xf, axis=-1, keepdims=True) + EPS) * gain


def _rmsnorm_kernel(x_ref, g_ref, h_ref):
    h_ref[...] = _rms(x_ref[...], g_ref[...]).astype(h_ref.dtype)


def rmsnorm_call(x2, gain):
    M, D = x2.shape
    return pl.pallas_call(
        _rmsnorm_kernel,
        out_shape=jax.ShapeDtypeStruct((M, D), BF16),
        grid=(M // ROW_TILE,),
        in_specs=[pl.BlockSpec((ROW_TILE, D), lambda i: (i, 0)),
                  pl.BlockSpec((1, D), lambda i: (0, 0))],
        out_specs=pl.BlockSpec((ROW_TILE, D), lambda i: (i, 0)),
        compiler_params=_params("parallel"),
        name="rmsnorm",
    )(x2, gain.reshape(1, D))


def _proj_kernel(h_ref, w_ref, o_ref, *, n_chunk):
    h = h_ref[...]
    for c in range(o_ref.shape[1] // n_chunk):
        sl = slice(c * n_chunk, (c + 1) * n_chunk)
        o_ref[:, sl] = jnp.dot(h, w_ref[:, sl], preferred_element_type=F32).astype(o_ref.dtype)


def proj_call(h, w):
    M, D = h.shape
    N = w.shape[1]
    return pl.pallas_call(
        functools.partial(_proj_kernel, n_chunk=512),
        out_shape=jax.ShapeDtypeStruct((M, N), BF16),
        grid=(M // ROW_TILE,),
        in_specs=[pl.BlockSpec((ROW_TILE, D), lambda i: (i, 0)),
                  pl.BlockSpec((D, N), lambda i: (0, 0))],
        out_specs=pl.BlockSpec((ROW_TILE, N), lambda i: (i, 0)),
        compiler_params=_params("parallel"),
        name="qkv_proj",
    )(h, w)


def _ffn_row_starts(n_groups):
    per = n_groups // SUBLANES
    return [SUBLANES * SUBLANES * (j % per) + j // per for j in range(n_groups)]


def _stage(ref, x):
    for c in range(ref.shape[0]):
        ref[c] = x[:, c * LANES:(c + 1) * LANES]


def _stage_interleaved(ref, x):
    for c in range(ref.shape[0]):
        for j, st in enumerate(_ffn_row_starts(x.shape[0] // SUBLANES)):
            ref[c, pl.ds(st, SUBLANES, stride=SUBLANES), :] = x[j * SUBLANES:(j + 1) * SUBLANES,
                                                                c * LANES:(c + 1) * LANES]


def _deinterleave_rows(ref, c):
    return jnp.concatenate([ref[c, pl.ds(st, SUBLANES, stride=SUBLANES), :]
                            for st in _ffn_row_starts(ref.shape[1] // SUBLANES)], axis=0)


def _wo_kernel(m_ref, w_ref, x_ref, g_ref, xo_ref, h_ref, stage_ref):
    xn = x_ref[...] + jnp.dot(m_ref[...], w_ref[...], preferred_element_type=F32)
    xo_ref[...] = xn
    _stage_interleaved(stage_ref, _rms(xn, g_ref[...]))
    for c in range(stage_ref.shape[0]):
        h_ref[:, c * LANES:(c + 1) * LANES] = stage_ref[c].astype(h_ref.dtype)


def wo_call(m, w, x2, gain):
    M, D = x2.shape
    row = lambda i: (i, 0)
    const = lambda i: (0, 0)
    return pl.pallas_call(
        _wo_kernel,
        out_shape=(jax.ShapeDtypeStruct((M, D), F32), jax.ShapeDtypeStruct((M, D), BF16)),
        grid=(M // ROW_TILE,),
        in_specs=[pl.BlockSpec((ROW_TILE, D), row), pl.BlockSpec((D, D), const),
                  pl.BlockSpec((ROW_TILE, D), row), pl.BlockSpec((1, D), const)],
        out_specs=(pl.BlockSpec((ROW_TILE, D), row), pl.BlockSpec((ROW_TILE, D), row)),
        scratch_shapes=[pltpu.VMEM((D // LANES, ROW_TILE, LANES), F32)],
        compiler_params=_params("parallel"),
        name="wo_residual_norm",
    )(m, w, x2, gain.reshape(1, D))


def _causal_conv3(u, cw, cb, prev):
    nblk, g, _, tf = u.shape
    first = lax.broadcasted_iota(jnp.int32, (SUBLANES, tf), 0) == 0
    u1, u2 = [], []
    for b in range(nblk):
        tail = prev if b == 0 else u[b - 1, g - 2:]
        wrap = [jnp.where(first, pltpu.roll(tail[e], 1, 0), pltpu.roll(u[b, g - 2 + e], 1, 0))
                for e in range(2)]
        u1.append(jnp.concatenate([wrap[1][None], u[b, :g - 1]], axis=0))
        u2.append(jnp.concatenate([wrap[0][None], wrap[1][None], u[b, :g - 2]], axis=0))
    u1 = jnp.stack(u1)
    u2 = jnp.stack(u2)
    return u * cw[2:3] + u1 * cw[1:2] + u2 * cw[0:1] + cb


def _ffn_up_kernel(h_ref, wg_ref, wv_ref, cwg_ref, cwv_ref, cbg_ref, cbv_ref, g_ref, carry_ref,
                   *, tiles_per_seq):
    tm = h_ref.shape[0]
    nblk = tm // ROW_TILE
    g = ROW_TILE // SUBLANES

    @pl.when(pl.program_id(0) % tiles_per_seq == 0)
    def _():
        carry_ref[...] = jnp.zeros(carry_ref.shape, F32)

    h = h_ref[...]
    for c in range(g_ref.shape[1] // FFN_TF):
        sl = slice(c * FFN_TF, (c + 1) * FFN_TF)
        ug = jnp.dot(h, wg_ref[:, sl], preferred_element_type=F32).reshape(nblk, g, SUBLANES, FFN_TF)
        uv = jnp.dot(h, wv_ref[:, sl], preferred_element_type=F32).reshape(nblk, g, SUBLANES, FFN_TF)
        prev_g = carry_ref[0, :, :, sl]
        prev_v = carry_ref[1, :, :, sl]
        carry_ref[0, :, :, sl] = ug[nblk - 1, g - 2:]
        carry_ref[1, :, :, sl] = uv[nblk - 1, g - 2:]
        yg = _causal_conv3(ug, cwg_ref[:, sl], cbg_ref[:, sl], prev_g)
        yv = _causal_conv3(uv, cwv_ref[:, sl], cbv_ref[:, sl], prev_v)
        act = yg * (1.0 / (1.0 + jnp.exp(-yg))) * yv
        g_ref[:, sl] = act.reshape(tm, FFN_TF).astype(g_ref.dtype)


def ffn_up_call(h, wg, wv, cwg, cwv, cbg, cbv, seq):
    M, D = h.shape
    F = wg.shape[1]
    row = lambda i: (i, 0)
    const = lambda i: (0, 0)
    resident = lambda shape: pl.BlockSpec(shape, const, pipeline_mode=pl.Buffered(1))
    return pl.pallas_call(
        functools.partial(_ffn_up_kernel, tiles_per_seq=seq // FFN_TM),
        out_shape=jax.ShapeDtypeStruct((M, F), BF16),
        grid=(M // FFN_TM,),
        in_specs=[pl.BlockSpec((FFN_TM, D), row), resident((D, F)), resident((D, F)),
                  resident((3, F)), resident((3, F)), resident((1, F)), resident((1, F))],
        out_specs=pl.BlockSpec((FFN_TM, F), row),
        scratch_shapes=[pltpu.VMEM((2, 2, SUBLANES, F), F32)],
        compiler_params=_params("arbitrary"),
        name="ffn_up_conv_gate",
    )(h, wg, wv, cwg, cwv, cbg, cbv)


def _ffn_down_kernel(a_ref, w_ref, x_ref, g_ref, xo_ref, h_ref, stage_ref):
    _stage(stage_ref, jnp.dot(a_ref[...], w_ref[...], preferred_element_type=F32))
    for c in range(stage_ref.shape[0]):
        sl = slice(c * LANES, (c + 1) * LANES)
        xo_ref[:, sl] = x_ref[:, sl] + _deinterleave_rows(stage_ref, c)
    h_ref[...] = _rms(xo_ref[...], g_ref[...]).astype(h_ref.dtype)


def ffn_down_call(a, w, x2, gain, out_dtype):
    M, D = x2.shape
    F = a.shape[1]
    row = lambda i: (i, 0)
    const = lambda i: (0, 0)
    return pl.pallas_call(
        _ffn_down_kernel,
        out_shape=(jax.ShapeDtypeStruct((M, D), F32), jax.ShapeDtypeStruct((M, D), out_dtype)),
        grid=(M // ROW_TILE,),
        in_specs=[pl.BlockSpec((ROW_TILE, F), row),
                  pl.BlockSpec((F, D), const, pipeline_mode=pl.Buffered(1)),
                  pl.BlockSpec((ROW_TILE, D), row), pl.BlockSpec((1, D), const)],
        out_specs=(pl.BlockSpec((ROW_TILE, D), row), pl.BlockSpec((ROW_TILE, D), row)),
        scratch_shapes=[pltpu.VMEM((D // LANES, ROW_TILE, LANES), F32)],
        compiler_params=_params("parallel"),
        name="ffn_down_residual_norm",
    )(a, w, x2, gain.reshape(1, D))


def _half_masks(q):
    lane = lax.broadcasted_iota(jnp.int32, q.shape, 1)
    zero = jnp.zeros_like(q)
    scale = jnp.asarray(DA_QK_DIM ** -0.5, q.dtype)
    return (jnp.where(lane < 64, q, zero) * scale, jnp.where(lane >= 64, q, zero) * scale)


def _qk(q, k):
    return lax.dot_general(q, k, (((1,), (1,)), ((), ())), preferred_element_type=F32)


def _da_kernel(q_ref, k_ref, v_ref, bias_ref, lam_ref, sub_ref, o_ref, s_ref, *, lambda_init):
    T = ATT_T
    nq = q_ref.shape[0] // T
    r = lax.broadcasted_iota(jnp.int32, (T, T), 0)
    cc = lax.broadcasted_iota(jnp.int32, (T, T), 1)
    causal = r >= cc
    lam = lam_ref[...]
    lam_full = (jnp.exp(jnp.sum(lam[0:1] * lam[1:2], keepdims=True))
                - jnp.exp(jnp.sum(lam[2:3] * lam[3:4], keepdims=True)) + lambda_init)
    for qi in range(nq):
        q01 = jnp.concatenate(_half_masks(q_ref[qi * T:(qi + 1) * T, :]), axis=0)
        mx = [None, None]
        for j in range(qi + 1):
            ss = _qk(q01, k_ref[j * T:(j + 1) * T, :])
            for c in range(2):
                s = ss[c * T:(c + 1) * T]
                if j == qi - 1:
                    s = s + bias_ref[c, :, 0:T]
                if j == qi:
                    s = jnp.where(causal, s + bias_ref[c, :, T:2 * T], NEG)
                s_ref[c, :, j * T:(j + 1) * T] = s
                t = jnp.maximum(s[:, :LANES], s[:, LANES:])
                mx[c] = t if mx[c] is None else jnp.maximum(mx[c], t)
        m2 = []
        for c in range(2):
            m = jnp.broadcast_to(jnp.max(mx[c], axis=-1, keepdims=True), (T, LANES))
            m2.append(jnp.concatenate([m, m], axis=1))
        lsum = [None, None]
        acc = None
        for j in range(qi + 1):
            ps = []
            for c in range(2):
                p = jnp.exp(s_ref[c, :, j * T:(j + 1) * T] - m2[c])
                t = p[:, :LANES] + p[:, LANES:]
                lsum[c] = t if lsum[c] is None else lsum[c] + t
                ps.append(p.astype(BF16))
            pv = jnp.dot(jnp.concatenate(ps, axis=0), v_ref[j * T:(j + 1) * T, :],
                         preferred_element_type=F32)
            acc = pv if acc is None else acc + pv
        outs = [acc[c * T:(c + 1) * T] / jnp.sum(lsum[c], axis=-1, keepdims=True) for c in range(2)]
        o = outs[0] - lam_full * outs[1]
        o = _rms(o, sub_ref[...]) * (1.0 - lambda_init)
        o_ref[qi * T:(qi + 1) * T, :] = o.astype(o_ref.dtype)


def da_call(qkv, bias_near, lam, subln, batch, seq, lambda_init):
    T = ATT_T
    nh = DA_HEADS
    return pl.pallas_call(
        functools.partial(_da_kernel, lambda_init=lambda_init),
        out_shape=jax.ShapeDtypeStruct((batch * seq, nh * DA_V_DIM), BF16),
        grid=(batch, nh),
        in_specs=[pl.BlockSpec((seq, LANES), lambda b, h: (b, h)),
                  pl.BlockSpec((seq, LANES), lambda b, h: (b, nh + h)),
                  pl.BlockSpec((seq, LANES), lambda b, h: (b, 2 * nh + h)),
                  pl.BlockSpec((2, T, 2 * T), lambda b, h: (h, 0, 0)),
                  pl.BlockSpec((4, DA_QK_DIM), lambda b, h: (0, 0)),
                  pl.BlockSpec((1, DA_V_DIM), lambda b, h: (0, 0))],
        out_specs=pl.BlockSpec((seq, LANES), lambda b, h: (b, h)),
        scratch_shapes=[pltpu.VMEM((2, T, seq), F32)],
        compiler_params=_params("parallel", "parallel"),
        name="diff_attention",
    )(qkv, qkv, qkv, bias_near, lam, subln.reshape(1, DA_V_DIM))


def _split2(x):
    hi = x.astype(BF16)
    mid = (x - hi.astype(F32)).astype(BF16)
    return hi, mid


def _sb_kernel(q_ref, k_ref, v_ref, o_ref, lb_ref, hm_ref):
    T = ATT_T
    nq = q_ref.shape[0] // T
    kr = lax.broadcasted_iota(jnp.int32, (T, T), 0)
    kc = lax.broadcasted_iota(jnp.int32, (T, T), 1)
    suffix = jnp.where(kr > kc, 1.0, 0.0).astype(BF16)
    suffix2 = jnp.concatenate([suffix, suffix], axis=0)
    strict = kc < kr
    lane = lax.broadcasted_iota(jnp.int32, (T, LANES), 1)
    for qi in range(nq):
        n = qi + 1
        q01 = jnp.concatenate(_half_masks(q_ref[qi * T:(qi + 1) * T, :]), axis=0)
        carry = [None, None]
        for j in range(qi, -1, -1):
            zz = _qk(q01, k_ref[j * T:(j + 1) * T, :])
            for c in range(2):
                z = zz[c * T:(c + 1) * T]
                sp = jnp.maximum(z, 0.0) + jnp.log(1.0 + jnp.exp(-jnp.abs(z)))
                log_1m_beta = -sp
                if j == qi:
                    log_1m_beta = jnp.where(strict, log_1m_beta, 0.0)
                hi, mid = _split2(log_1m_beta)
                blk = (j * 2 + c) * T
                hm_ref[blk:blk + T, 0:T] = hi
                hm_ref[blk:blk + T, T:2 * T] = mid
                lb = z - sp
                if carry[c] is not None:
                    lb = lb + carry[c]
                lb_ref[blk:blk + T, :] = lb
                if j > 0:
                    rs = jnp.sum(log_1m_beta, axis=-1, keepdims=True)
                    carry[c] = rs if carry[c] is None else carry[c] + rs
        between = jnp.dot(hm_ref[0:2 * n * T, :], suffix2, preferred_element_type=F32)
        acc = None
        for j in range(n):
            a = jnp.exp(lb_ref[2 * j * T:2 * (j + 1) * T, :] + between[2 * j * T:2 * (j + 1) * T])
            if j == qi:
                a = jnp.where(jnp.concatenate([strict, strict], axis=0), a, 0.0)
            pv = jnp.dot(a.astype(BF16), v_ref[j * T:(j + 1) * T, :], preferred_element_type=F32)
            acc = pv if acc is None else acc + pv
        o_ref[qi * T:(qi + 1) * T, :] = jnp.where(lane < 64, acc[0:T], acc[T:2 * T]).astype(o_ref.dtype)


def sb_call(qkv, batch, seq):
    npair = SB_HEADS // 2
    return pl.pallas_call(
        _sb_kernel,
        out_shape=jax.ShapeDtypeStruct((batch * seq, SB_HEADS * SB_DIM), BF16),
        grid=(batch, npair),
        in_specs=[pl.BlockSpec((seq, LANES), lambda b, p: (b, p)),
                  pl.BlockSpec((seq, LANES), lambda b, p: (b, npair + p)),
                  pl.BlockSpec((seq, LANES), lambda b, p: (b, 2 * npair + p))],
        out_specs=pl.BlockSpec((seq, LANES), lambda b, p: (b, p)),
        scratch_shapes=[pltpu.VMEM((2 * seq, ATT_T), F32), pltpu.VMEM((2 * seq, 2 * ATT_T), BF16)],
        compiler_params=_params("parallel", "parallel"),
        name="stick_breaking_attention",
    )(qkv, qkv, qkv)


def _sw_kernel(sink_ref, q_ref, k_ref, v_ref, bias_ref, o_ref, s_ref):
    W = SW_BLOCK
    nb = q_ref.shape[0] // W
    p_id = pl.program_id(1)
    r = lax.broadcasted_iota(jnp.int32, (2 * W, 2 * W), 0) & (W - 1)
    cidx = lax.broadcasted_iota(jnp.int32, (2 * W, 2 * W), 1)
    valid = ((cidx < W) & (cidx > r)) | ((cidx >= W) & (cidx - W <= r))
    biasm = jnp.where(valid, bias_ref[...], NEG)
    row = lax.broadcasted_iota(jnp.int32, (2 * W, 1), 0)
    sink = jnp.where(row < W, sink_ref[2 * p_id], sink_ref[2 * p_id + 1])
    lane = lax.broadcasted_iota(jnp.int32, (W, LANES), 1)
    for n in range(nb):
        q01 = jnp.concatenate(_half_masks(q_ref[n * W:(n + 1) * W, :]), axis=0)
        if n == 0:
            s_ref[0, :, W:] = _qk(q01, k_ref[0:W, :]) + biasm[:, W:]
        else:
            s_ref[n] = _qk(q01, k_ref[(n - 1) * W:(n + 1) * W, :]) + biasm
    for n in range(nb):
        s = s_ref[0, :, W:] if n == 0 else s_ref[n]
        m = jnp.maximum(jnp.max(s, axis=-1, keepdims=True), sink)
        e = jnp.exp(s - m)
        denom = jnp.sum(e, axis=-1, keepdims=True) + jnp.exp(sink - m)
        v = v_ref[0:W, :] if n == 0 else v_ref[(n - 1) * W:(n + 1) * W, :]
        o = jnp.dot(e.astype(BF16), v, preferred_element_type=F32) / denom
        o_ref[n * W:(n + 1) * W, :] = jnp.where(lane < 64, o[0:W], o[W:]).astype(o_ref.dtype)


def sw_call(qkv, bias_band, sinks, batch, seq):
    npair = SW_Q_HEADS // 2
    q_blocks = SW_Q_HEADS * SW_DIM // LANES
    W = SW_BLOCK
    return pl.pallas_call(
        _sw_kernel,
        out_shape=jax.ShapeDtypeStruct((batch * seq, SW_Q_HEADS * SW_DIM), BF16),
        grid=(batch, npair),
        in_specs=[pl.BlockSpec(memory_space=pltpu.SMEM),
                  pl.BlockSpec((seq, LANES), lambda b, p: (b, p)),
                  pl.BlockSpec((seq, LANES), lambda b, p: (b, q_blocks + p // 2)),
                  pl.BlockSpec((seq, LANES), lambda b, p: (b, q_blocks + SW_KV_HEADS + p // 2)),
                  pl.BlockSpec((None, 2 * W, 2 * W), lambda b, p: (p, 0, 0))],
        out_specs=pl.BlockSpec((seq, LANES), lambda b, p: (b, p)),
        scratch_shapes=[pltpu.VMEM((seq // W, 2 * W, 2 * W), F32)],
        compiler_params=_params("parallel", "parallel"),
        name="sliding_window_attention",
    )(sinks, qkv, qkv, qkv, bias_band.reshape(npair, 2 * W, 2 * W))


def _t5_bucket(dist):
    max_exact = N_BUCKETS // 2
    d = jnp.maximum(dist, 0)
    large = max_exact + (jnp.log(jnp.maximum(d, 1).astype(F32) / max_exact)
                         / math.log(MAX_DISTANCE / max_exact) * (N_BUCKETS - max_exact)).astype(jnp.int32)
    large = jnp.minimum(large, N_BUCKETS - 1)
    return jnp.where(d < max_exact, d, large)


def _bias_band(rel_bias, T):
    y = np.arange(3 * T)
    dist = np.clip(np.where(y <= 2 * T, T - y, 4 * T - y), 0, 2 * T - 1)
    row = rel_bias.astype(F32)[_t5_bucket(jnp.asarray(dist, jnp.int32))].T
    flat = jnp.tile(row, (1, T))[:, :T * (3 * T - 1)]
    return flat.reshape(-1, T, 3 * T - 1)[:, :, :2 * T]


def _lambda_init(layer):
    return 0.8 - 0.6 * math.exp(-0.3 * layer)


def _dup_kv_heads(w):
    q_w = SW_Q_HEADS * SW_DIM
    kv = w[:, q_w:].reshape(w.shape[0], 2 * SW_KV_HEADS, 1, SW_DIM)
    kv = jnp.broadcast_to(kv, (w.shape[0], 2 * SW_KV_HEADS, 2, SW_DIM)).reshape(w.shape[0], -1)
    return jnp.concatenate([w[:, :q_w], kv], axis=1)


def _pad_ff(a, axis):
    pad = [(0, 0)] * a.ndim
    pad[axis] = (0, D_FF_PAD - D_FF)
    return jnp.pad(a, pad)


def kernel(x, rel_bias, attn_norm, ffn_norm, w_o, da_w_qkv, da_lambda, da_subln, sb_w_qkv, sw_w_qkv,
           sw_sinks, ffn_w_up, ffn_conv_w, ffn_conv_b, ffn_w_down, final_norm):
    B, S, D = x.shape
    x2 = x.reshape(B * S, D)
    da_bias_near = _bias_band(rel_bias, ATT_T) - rel_bias.astype(F32)[N_BUCKETS - 1][:, None, None]
    sw_bias = _bias_band(rel_bias, SW_BLOCK)

    h = rmsnorm_call(x2, attn_norm[0])
    for layer in range(DEPTH):
        mixer = layer % N_MIXERS
        slot = layer // N_MIXERS
        if mixer == 0:
            qkv = proj_call(h, da_w_qkv[slot].astype(BF16))
            m = da_call(qkv, da_bias_near, da_lambda[slot], da_subln[slot], B, S, _lambda_init(layer))
        elif mixer == 1:
            qkv = proj_call(h, sb_w_qkv[slot].astype(BF16))
            m = sb_call(qkv, B, S)
        else:
            qkv = proj_call(h, _dup_kv_heads(sw_w_qkv[slot]).astype(BF16))
            m = sw_call(qkv, sw_bias, sw_sinks[slot], B, S)
        x2, h = wo_call(m, w_o[layer].astype(BF16), x2, ffn_norm[layer])

        w_up = ffn_w_up[layer]
        wg = _pad_ff(w_up[:, :D_FF], 1).astype(BF16)
        wv = _pad_ff(w_up[:, D_FF:], 1).astype(BF16)
        cw, cb = ffn_conv_w[layer], ffn_conv_b[layer].reshape(1, -1)
        last = layer == DEPTH - 1
        gain = final_norm if last else attn_norm[layer + 1]
        act = ffn_up_call(h, wg, wv,
                          _pad_ff(cw[:, :D_FF], 1), _pad_ff(cw[:, D_FF:], 1),
                          _pad_ff(cb[:, :D_FF], 1), _pad_ff(cb[:, D_FF:], 1), S)
        x2, h = ffn_down_call(act, _pad_ff(ffn_w_down[layer], 0).astype(BF16), x2, gain,
                              F32 if last else BF16)
    return h.reshape(B, S, D)
```

```python
import functools
import math

import jax
import jax.numpy as jnp
import numpy as np
from jax import lax
from jax.experimental import pallas as pl
from jax.experimental.pallas import tpu as pltpu

D_MODEL = 1024
DEPTH = 4
N_MIXERS = 3
N_BUCKETS = 32
MAX_DISTANCE = 128
DA_HEADS = 8
DA_QK_DIM = 64
DA_V_DIM = 128
SB_HEADS = 16
SB_DIM = 64
SW_Q_HEADS = 16
SW_KV_HEADS = 4
SW_DIM = 64
SW_BLOCK = 128
D_FF = 2752
EPS = 1e-6
NEG = -1e30

LANES = 128
SUBLANES = 8
VMEM_LIMIT = 52 * 1024 * 1024

ATT_T = 256
ROW_TILE = 512
FFN_TM = 1024
FFN_TF = 256
D_FF_PAD = 2816

BF16 = jnp.bfloat16
F32 = jnp.float32


def _params(*sem):
    return pltpu.CompilerParams(dimension_semantics=sem, vmem_limit_bytes=VMEM_LIMIT)


def _rms(xf, gain):
    return xf * lax.rsqrt(jnp.mean(xf * xf, axis=-1, keepdims=True) + EPS) * gain


def _rmsnorm_kernel(x_ref, g_ref, h_ref):
    h_ref[...] = _rms(x_ref[...], g_ref[...]).astype(h_ref.dtype)


def rmsnorm_call(x2, gain):
    M, D = x2.shape
    return pl.pallas_call(
        _rmsnorm_kernel,
        out_shape=jax.ShapeDtypeStruct((M, D), BF16),
        grid=(M // ROW_TILE,),
        in_specs=[pl.BlockSpec((ROW_TILE, D), lambda i: (i, 0)),
                  pl.BlockSpec((1, D), lambda i: (0, 0))],
        out_specs=pl.BlockSpec((ROW_TILE, D), lambda i: (i, 0)),
        compiler_params=_params("parallel"),
        name="rmsnorm",
    )(x2, gain.reshape(1, D))


def _proj_kernel(h_ref, w_ref, o_ref, *, n_chunk):
    h = h_ref[...]
    for c in range(o_ref.shape[1] // n_chunk):
        sl = slice(c * n_chunk, (c + 1) * n_chunk)
        o_ref[:, sl] = jnp.dot(h, w_ref[:, sl], preferred_element_type=F32).astype(o_ref.dtype)


def proj_call(h, w):
    M, D = h.shape
    N = w.shape[1]
    return pl.pallas_call(
        functools.partial(_proj_kernel, n_chunk=512),
        out_shape=jax.ShapeDtypeStruct((M, N), BF16),
        grid=(M // ROW_TILE,),
        in_specs=[pl.BlockSpec((ROW_TILE, D), lambda i: (i, 0)),
                  pl.BlockSpec((D, N), lambda i: (0, 0))],
        out_specs=pl.BlockSpec((ROW_TILE, N), lambda i: (i, 0)),
        compiler_params=_params("parallel"),
        name="qkv_proj",
    )(h, w)


def _ffn_row_starts(n_groups):
    per = n_groups // SUBLANES
    return [SUBLANES * SUBLANES * (j % per) + j // per for j in range(n_groups)]


def _stage(ref, x):
    for c in range(ref.shape[0]):
        ref[c] = x[:, c * LANES:(c + 1) * LANES]


def _stage_interleaved(ref, x):
    for c in range(ref.shape[0]):
        for j, st in enumerate(_ffn_row_starts(x.shape[0] // SUBLANES)):
            ref[c, pl.ds(st, SUBLANES, stride=SUBLANES), :] = x[j * SUBLANES:(j + 1) * SUBLANES,
                                                                c * LANES:(c + 1) * LANES]


def _deinterleave_rows(ref, c):
    return jnp.concatenate([ref[c, pl.ds(st, SUBLANES, stride=SUBLANES), :]
                            for st in _ffn_row_starts(ref.shape[1] // SUBLANES)], axis=0)


def _wo_kernel(m_ref, w_ref, x_ref, g_ref, xo_ref, h_ref, stage_ref):
    xn = x_ref[...] + jnp.dot(m_ref[...], w_ref[...], preferred_element_type=F32)
    xo_ref[...] = xn
    _stage_interleaved(stage_ref, _rms(xn, g_ref[...]))
    for c in range(stage_ref.shape[0]):
        h_ref[:, c * LANES:(c + 1) * LANES] = stage_ref[c].astype(h_ref.dtype)


def wo_call(m, w, x2, gain):
    M, D = x2.shape
    row = lambda i: (i, 0)
    const = lambda i: (0, 0)
    return pl.pallas_call(
        _wo_kernel,
        out_shape=(jax.ShapeDtypeStruct((M, D), F32), jax.ShapeDtypeStruct((M, D), BF16)),
        grid=(M // ROW_TILE,),
        in_specs=[pl.BlockSpec((ROW_TILE, D), row), pl.BlockSpec((D, D), const),
                  pl.BlockSpec((ROW_TILE, D), row), pl.BlockSpec((1, D), const)],
        out_specs=(pl.BlockSpec((ROW_TILE, D), row), pl.BlockSpec((ROW_TILE, D), row)),
        scratch_shapes=[pltpu.VMEM((D // LANES, ROW_TILE, LANES), F32)],
        compiler_params=_params("parallel"),
        name="wo_residual_norm",
    )(m, w, x2, gain.reshape(1, D))


def _causal_conv3(u, cw, cb, prev):
    nblk, g, _, tf = u.shape
    first = lax.broadcasted_iota(jnp.int32, (SUBLANES, tf), 0) == 0
    u1, u2 = [], []
    for b in range(nblk):
        tail = prev if b == 0 else u[b - 1, g - 2:]
        wrap = [jnp.where(first, pltpu.roll(tail[e], 1, 0), pltpu.roll(u[b, g - 2 + e], 1, 0))
                for e in range(2)]
        u1.append(jnp.concatenate([wrap[1][None], u[b, :g - 1]], axis=0))
        u2.append(jnp.concatenate([wrap[0][None], wrap[1][None], u[b, :g - 2]], axis=0))
    u1 = jnp.stack(u1)
    u2 = jnp.stack(u2)
    return u * cw[2:3] + u1 * cw[1:2] + u2 * cw[0:1] + cb


def _ffn_up_kernel(h_ref, wg_ref, wv_ref, cwg_ref, cwv_ref, cbg_ref, cbv_ref, g_ref, carry_ref,
                   *, tiles_per_seq):
    tm = h_ref.shape[0]
    nblk = tm // ROW_TILE
    g = ROW_TILE // SUBLANES

    @pl.when(pl.program_id(0) % tiles_per_seq == 0)
    def _():
        carry_ref[...] = jnp.zeros(carry_ref.shape, F32)

    h = h_ref[...]
    for c in range(g_ref.shape[1] // FFN_TF):
        sl = slice(c * FFN_TF, (c + 1) * FFN_TF)
        ug = jnp.dot(h, wg_ref[:, sl], preferred_element_type=F32).reshape(nblk, g, SUBLANES, FFN_TF)
        uv = jnp.dot(h, wv_ref[:, sl], preferred_element_type=F32).reshape(nblk, g, SUBLANES, FFN_TF)
        prev_g = carry_ref[0, :, :, sl]
        prev_v = carry_ref[1, :, :, sl]
        carry_ref[0, :, :, sl] = ug[nblk - 1, g - 2:]
        carry_ref[1, :, :, sl] = uv[nblk - 1, g - 2:]
        yg = _causal_conv3(ug, cwg_ref[:, sl], cbg_ref[:, sl], prev_g)
        yv = _causal_conv3(uv, cwv_ref[:, sl], cbv_ref[:, sl], prev_v)
        act = yg * (1.0 / (1.0 + jnp.exp(-yg))) * yv
        g_ref[:, sl] = act.reshape(tm, FFN_TF).astype(g_ref.dtype)


def ffn_up_call(h, wg, wv, cwg, cwv, cbg, cbv, seq):
    M, D = h.shape
    F = wg.shape[1]
    row = lambda i: (i, 0)
    const = lambda i: (0, 0)
    resident = lambda shape: pl.BlockSpec(shape, const, pipeline_mode=pl.Buffered(1))
    return pl.pallas_call(
        functools.partial(_ffn_up_kernel, tiles_per_seq=seq // FFN_TM),
        out_shape=jax.ShapeDtypeStruct((M, F), BF16),
        grid=(M // FFN_TM,),
        in_specs=[pl.BlockSpec((FFN_TM, D), row), resident((D, F)), resident((D, F)),
                  resident((3, F)), resident((3, F)), resident((1, F)), resident((1, F))],
        out_specs=pl.BlockSpec((FFN_TM, F), row),
        scratch_shapes=[pltpu.VMEM((2, 2, SUBLANES, F), F32)],
        compiler_params=_params("arbitrary"),
        name="ffn_up_conv_gate",
    )(h, wg, wv, cwg, cwv, cbg, cbv)


def _ffn_down_kernel(a_ref, w_ref, x_ref, g_ref, xo_ref, h_ref, stage_ref):
    _stage(stage_ref, jnp.dot(a_ref[...], w_ref[...], preferred_element_type=F32))
    for c in range(stage_ref.shape[0]):
        sl = slice(c * LANES, (c + 1) * LANES)
        xo_ref[:, sl] = x_ref[:, sl] + _deinterleave_rows(stage_ref, c)
    h_ref[...] = _rms(xo_ref[...], g_ref[...]).astype(h_ref.dtype)


def ffn_down_call(a, w, x2, gain, out_dtype):
    M, D = x2.shape
    F = a.shape[1]
    row = lambda i: (i, 0)
    const = lambda i: (0, 0)
    return pl.pallas_call(
        _ffn_down_kernel,
        out_shape=(jax.ShapeDtypeStruct((M, D), F32), jax.ShapeDtypeStruct((M, D), out_dtype)),
        grid=(M // ROW_TILE,),
        in_specs=[pl.BlockSpec((ROW_TILE, F), row),
                  pl.BlockSpec((F, D), const, pipeline_mode=pl.Buffered(1)),
                  pl.BlockSpec((ROW_TILE, D), row), pl.BlockSpec((1, D), const)],
        out_specs=(pl.BlockSpec((ROW_TILE, D), row), pl.BlockSpec((ROW_TILE, D), row)),
        scratch_shapes=[pltpu.VMEM((D // LANES, ROW_TILE, LANES), F32)],
        compiler_params=_params("parallel"),
        name="ffn_down_residual_norm",
    )(a, w, x2, gain.reshape(1, D))


def _half_masks(q):
    lane = lax.broadcasted_iota(jnp.int32, q.shape, 1)
    zero = jnp.zeros_like(q)
    scale = jnp.asarray(DA_QK_DIM ** -0.5, q.dtype)
    return (jnp.where(lane < 64, q, zero) * scale, jnp.where(lane >= 64, q, zero) * scale)


def _qk(q, k):
    return lax.dot_general(q, k, (((1,), (1,)), ((), ())), preferred_element_type=F32)


def _da_kernel(q_ref, k_ref, v_ref, bias_ref, lam_ref, sub_ref, o_ref, s_ref, *, lambda_init):
    T = ATT_T
    nq = q_ref.shape[0] // T
    r = lax.broadcasted_iota(jnp.int32, (T, T), 0)
    cc = lax.broadcasted_iota(jnp.int32, (T, T), 1)
    causal = r >= cc
    lam = lam_ref[...]
    lam_full = (jnp.exp(jnp.sum(lam[0:1] * lam[1:2], keepdims=True))
                - jnp.exp(jnp.sum(lam[2:3] * lam[3:4], keepdims=True)) + lambda_init)
    def score_steps(qi, st):
        def first():
            st["q01"] = jnp.concatenate(_half_masks(q_ref[qi * T:(qi + 1) * T, :]), axis=0)
            st["mx"] = [None, None]

        def tile(j):
            if j == 0:
                first()
            ss = _qk(st["q01"], k_ref[j * T:(j + 1) * T, :])
            for c in range(2):
                s = ss[c * T:(c + 1) * T]
                if j == qi - 1:
                    s = s + bias_ref[c, :, 0:T]
                if j == qi:
                    s = jnp.where(causal, s + bias_ref[c, :, T:2 * T], NEG)
                s_ref[qi % 2, c, :, j * T:(j + 1) * T] = s
                t = jnp.maximum(s[:, :LANES], s[:, LANES:])
                st["mx"][c] = t if st["mx"][c] is None else jnp.maximum(st["mx"][c], t)
        return [functools.partial(tile, j) for j in range(qi + 1)]

    def value_steps(qi, st):
        def first():
            st["m2"] = []
            for c in range(2):
                m = jnp.broadcast_to(jnp.max(st["mx"][c], axis=-1, keepdims=True), (T, LANES))
                st["m2"].append(jnp.concatenate([m, m], axis=1))
            st["lsum"] = [None, None]
            st["acc"] = None

        def tile(j):
            if j == 0:
                first()
            ps = []
            for c in range(2):
                p = jnp.exp(s_ref[qi % 2, c, :, j * T:(j + 1) * T] - st["m2"][c])
                t = p[:, :LANES] + p[:, LANES:]
                st["lsum"][c] = t if st["lsum"][c] is None else st["lsum"][c] + t
                ps.append(p.astype(BF16))
            pv = jnp.dot(jnp.concatenate(ps, axis=0), v_ref[j * T:(j + 1) * T, :],
                         preferred_element_type=F32)
            st["acc"] = pv if st["acc"] is None else st["acc"] + pv
            if j == qi:
                outs = [st["acc"][c * T:(c + 1) * T] / jnp.sum(st["lsum"][c], axis=-1, keepdims=True)
                        for c in range(2)]
                o = outs[0] - lam_full * outs[1]
                o = _rms(o, sub_ref[...]) * (1.0 - lambda_init)
                o_ref[qi * T:(qi + 1) * T, :] = o.astype(o_ref.dtype)
        return [functools.partial(tile, j) for j in range(qi + 1)]

    states = [dict() for _ in range(nq)]
    pending = []
    for qi in range(nq + 1):
        scores = score_steps(qi, states[qi]) if qi < nq else []
        for t in range(max(len(scores), len(pending))):
            if t < len(scores):
                scores[t]()
            if t < len(pending):
                pending[t]()
        pending = value_steps(qi, states[qi]) if qi < nq else []


def da_call(qkv, bias_near, lam, subln, batch, seq, lambda_init):
    T = ATT_T
    nh = DA_HEADS
    return pl.pallas_call(
        functools.partial(_da_kernel, lambda_init=lambda_init),
        out_shape=jax.ShapeDtypeStruct((batch * seq, nh * DA_V_DIM), BF16),
        grid=(batch, nh),
        in_specs=[pl.BlockSpec((seq, LANES), lambda b, h: (b, h)),
                  pl.BlockSpec((seq, LANES), lambda b, h: (b, nh + h)),
                  pl.BlockSpec((seq, LANES), lambda b, h: (b, 2 * nh + h)),
                  pl.BlockSpec((2, T, 2 * T), lambda b, h: (h, 0, 0)),
                  pl.BlockSpec((4, DA_QK_DIM), lambda b, h: (0, 0)),
                  pl.BlockSpec((1, DA_V_DIM), lambda b, h: (0, 0))],
        out_specs=pl.BlockSpec((seq, LANES), lambda b, h: (b, h)),
        scratch_shapes=[pltpu.VMEM((2, 2, T, seq), F32)],
        compiler_params=_params("parallel", "parallel"),
        name="diff_attention",
    )(qkv, qkv, qkv, bias_near, lam, subln.reshape(1, DA_V_DIM))


def _split2(x):
    hi = x.astype(BF16)
    mid = (x - hi.astype(F32)).astype(BF16)
    return hi, mid


def _sb_kernel(q_ref, k_ref, v_ref, o_ref, lb_ref, hm_ref):
    T = ATT_T
    nq = q_ref.shape[0] // T
    kr = lax.broadcasted_iota(jnp.int32, (T, T), 0)
    kc = lax.broadcasted_iota(jnp.int32, (T, T), 1)
    suffix = jnp.where(kr > kc, 1.0, 0.0).astype(BF16)
    suffix2 = jnp.concatenate([suffix, suffix], axis=0)
    strict = kc < kr
    lane = lax.broadcasted_iota(jnp.int32, (T, LANES), 1)
    strict2 = jnp.concatenate([strict, strict], axis=0)

    def score_steps(qi, st):
        def tile(j):
            if j == qi:
                st["q01"] = jnp.concatenate(_half_masks(q_ref[qi * T:(qi + 1) * T, :]), axis=0)
                st["carry"] = [None, None]
            zz = _qk(st["q01"], k_ref[j * T:(j + 1) * T, :])
            for c in range(2):
                z = zz[c * T:(c + 1) * T]
                lb = jnp.minimum(z, 0.0) - jnp.log(1.0 + jnp.exp(-jnp.abs(z)))
                log_1m_beta = lb - z
                if j == qi:
                    log_1m_beta = jnp.where(strict, log_1m_beta, 0.0)
                hi, mid = _split2(log_1m_beta)
                blk = (j * 2 + c) * T
                hm_ref[qi % 2, blk:blk + T, 0:T] = hi
                hm_ref[qi % 2, blk:blk + T, T:2 * T] = mid
                if st["carry"][c] is not None:
                    lb = lb + st["carry"][c]
                lb_ref[qi % 2, blk:blk + T, :] = lb
                if j > 0:
                    rs = jnp.sum(log_1m_beta, axis=-1, keepdims=True)
                    st["carry"][c] = rs if st["carry"][c] is None else st["carry"][c] + rs
        return [functools.partial(tile, j) for j in range(qi, -1, -1)]

    def value_steps(qi, st):
        n = qi + 1

        def tile(j):
            if j == 0:
                st["between"] = jnp.dot(hm_ref[qi % 2, 0:2 * n * T, :], suffix2,
                                        preferred_element_type=F32)
                st["acc"] = None
            rows = slice(2 * j * T, 2 * (j + 1) * T)
            a = jnp.exp(lb_ref[qi % 2, rows, :] + st["between"][rows])
            if j == qi:
                a = jnp.where(strict2, a, 0.0)
            pv = jnp.dot(a.astype(BF16), v_ref[j * T:(j + 1) * T, :], preferred_element_type=F32)
            st["acc"] = pv if st["acc"] is None else st["acc"] + pv
            if j == qi:
                acc = st["acc"]
                o_ref[qi * T:(qi + 1) * T, :] = jnp.where(lane < 64, acc[0:T], acc[T:2 * T]).astype(o_ref.dtype)
        return [functools.partial(tile, j) for j in range(n)]

    states = [dict() for _ in range(nq)]
    pending = []
    for qi in range(nq + 1):
        scores = score_steps(qi, states[qi]) if qi < nq else []
        for t in range(max(len(scores), len(pending))):
            if t < len(pending):
                pending[t]()
            if t < len(scores):
                scores[t]()
        pending = value_steps(qi, states[qi]) if qi < nq else []


def sb_call(qkv, batch, seq):
    npair = SB_HEADS // 2
    return pl.pallas_call(
        _sb_kernel,
        out_shape=jax.ShapeDtypeStruct((batch * seq, SB_HEADS * SB_DIM), BF16),
        grid=(batch, npair),
        in_specs=[pl.BlockSpec((seq, LANES), lambda b, p: (b, p)),
                  pl.BlockSpec((seq, LANES), lambda b, p: (b, npair + p)),
                  pl.BlockSpec((seq, LANES), lambda b, p: (b, 2 * npair + p))],
        out_specs=pl.BlockSpec((seq, LANES), lambda b, p: (b, p)),
        scratch_shapes=[pltpu.VMEM((2, 2 * seq, ATT_T), F32), pltpu.VMEM((2, 2 * seq, 2 * ATT_T), BF16)],
        compiler_params=_params("parallel", "parallel"),
        name="stick_breaking_attention",
    )(qkv, qkv, qkv)


def _sw_kernel(sink_ref, q_ref, k_ref, v_ref, bias_ref, o_ref, s_ref):
    W = SW_BLOCK
    nb = q_ref.shape[0] // W
    p_id = pl.program_id(1)
    r = lax.broadcasted_iota(jnp.int32, (2 * W, 2 * W), 0) & (W - 1)
    cidx = lax.broadcasted_iota(jnp.int32, (2 * W, 2 * W), 1)
    valid = ((cidx < W) & (cidx > r)) | ((cidx >= W) & (cidx - W <= r))
    biasm = jnp.where(valid, bias_ref[...], NEG)
    row = lax.broadcasted_iota(jnp.int32, (2 * W, 1), 0)
    sink = jnp.where(row < W, sink_ref[2 * p_id], sink_ref[2 * p_id + 1])
    lane = lax.broadcasted_iota(jnp.int32, (W, LANES), 1)
    for n in range(nb):
        q01 = jnp.concatenate(_half_masks(q_ref[n * W:(n + 1) * W, :]), axis=0)
        if n == 0:
            s_ref[0, :, W:] = _qk(q01, k_ref[0:W, :]) + biasm[:, W:]
        else:
            s_ref[n] = _qk(q01, k_ref[(n - 1) * W:(n + 1) * W, :]) + biasm
    for n in range(nb):
        s = s_ref[0, :, W:] if n == 0 else s_ref[n]
        m = jnp.maximum(jnp.max(s, axis=-1, keepdims=True), sink)
        e = jnp.exp(s - m)
        denom = jnp.sum(e, axis=-1, keepdims=True) + jnp.exp(sink - m)
        v = v_ref[0:W, :] if n == 0 else v_ref[(n - 1) * W:(n + 1) * W, :]
        o = jnp.dot(e.astype(BF16), v, preferred_element_type=F32) / denom
        o_ref[n * W:(n + 1) * W, :] = jnp.where(lane < 64, o[0:W], o[W:]).astype(o_ref.dtype)


def sw_call(qkv, bias_band, sinks, batch, seq):
    npair = SW_Q_HEADS // 2
    q_blocks = SW_Q_HEADS * SW_DIM // LANES
    W = SW_BLOCK
    return pl.pallas_call(
        _sw_kernel,
        out_shape=jax.ShapeDtypeStruct((batch * seq, SW_Q_HEADS * SW_DIM), BF16),
        grid=(batch, npair),
        in_specs=[pl.BlockSpec(memory_space=pltpu.SMEM),
                  pl.BlockSpec((seq, LANES), lambda b, p: (b, p)),
                  pl.BlockSpec((seq, LANES), lambda b, p: (b, q_blocks + p // 2)),
                  pl.BlockSpec((seq, LANES), lambda b, p: (b, q_blocks + SW_KV_HEADS + p // 2)),
                  pl.BlockSpec((None, 2 * W, 2 * W), lambda b, p: (p, 0, 0))],
        out_specs=pl.BlockSpec((seq, LANES), lambda b, p: (b, p)),
        scratch_shapes=[pltpu.VMEM((seq // W, 2 * W, 2 * W), F32)],
        compiler_params=_params("parallel", "parallel"),
        name="sliding_window_attention",
    )(sinks, qkv, qkv, qkv, bias_band.reshape(npair, 2 * W, 2 * W))


def _t5_bucket(dist):
    max_exact = N_BUCKETS // 2
    d = jnp.maximum(dist, 0)
    large = max_exact + (jnp.log(jnp.maximum(d, 1).astype(F32) / max_exact)
                         / math.log(MAX_DISTANCE / max_exact) * (N_BUCKETS - max_exact)).astype(jnp.int32)
    large = jnp.minimum(large, N_BUCKETS - 1)
    return jnp.where(d < max_exact, d, large)


def _bias_band(rel_bias, T):
    y = np.arange(3 * T)
    dist = np.clip(np.where(y <= 2 * T, T - y, 4 * T - y), 0, 2 * T - 1)
    row = rel_bias.astype(F32)[_t5_bucket(jnp.asarray(dist, jnp.int32))].T
    flat = jnp.tile(row, (1, T))[:, :T * (3 * T - 1)]
    return flat.reshape(-1, T, 3 * T - 1)[:, :, :2 * T]


def _lambda_init(layer):
    return 0.8 - 0.6 * math.exp(-0.3 * layer)


def _dup_kv_heads(w):
    q_w = SW_Q_HEADS * SW_DIM
    kv = w[:, q_w:].reshape(w.shape[0], 2 * SW_KV_HEADS, 1, SW_DIM)
    kv = jnp.broadcast_to(kv, (w.shape[0], 2 * SW_KV_HEADS, 2, SW_DIM)).reshape(w.shape[0], -1)
    return jnp.concatenate([w[:, :q_w], kv], axis=1)


def _pad_ff(a, axis):
    pad = [(0, 0)] * a.ndim
    pad[axis] = (0, D_FF_PAD - D_FF)
    return jnp.pad(a, pad)


def kernel(x, rel_bias, attn_norm, ffn_norm, w_o, da_w_qkv, da_lambda, da_subln, sb_w_qkv, sw_w_qkv,
           sw_sinks, ffn_w_up, ffn_conv_w, ffn_conv_b, ffn_w_down, final_norm):
    B, S, D = x.shape
    x2 = x.reshape(B * S, D)
    da_bias_near = _bias_band(rel_bias, ATT_T) - rel_bias.astype(F32)[N_BUCKETS - 1][:, None, None]
    sw_bias = _bias_band(rel_bias, SW_BLOCK)

    h = rmsnorm_call(x2, attn_norm[0])
    for layer in range(DEPTH):
        mixer = layer % N_MIXERS
        slot = layer // N_MIXERS
        if mixer == 0:
            qkv = proj_call(h, da_w_qkv[slot].astype(BF16))
            m = da_call(qkv, da_bias_near, da_lambda[slot], da_subln[slot], B, S, _lambda_init(layer))
        elif mixer == 1:
            qkv = proj_call(h, sb_w_qkv[slot].astype(BF16))
            m = sb_call(qkv, B, S)
        else:
            qkv = proj_call(h, _dup_kv_heads(sw_w_qkv[slot]).astype(BF16))
            m = sw_call(qkv, sw_bias, sw_sinks[slot], B, S)
        x2, h = wo_call(m, w_o[layer].astype(BF16), x2, ffn_norm[layer])

        w_up = ffn_w_up[layer]
        wg = _pad_ff(w_up[:, :D_FF].astype(BF16), 1)
        wv = _pad_ff(w_up[:, D_FF:].astype(BF16), 1)
        cw, cb = ffn_conv_w[layer], ffn_conv_b[layer].reshape(1, -1)
        last = layer == DEPTH - 1
        gain = final_norm if last else attn_norm[layer + 1]
        act = ffn_up_call(h, wg, wv,
                          _pad_ff(cw[:, :D_FF], 1), _pad_ff(cw[:, D_FF:], 1),
                          _pad_ff(cb[:, :D_FF], 1), _pad_ff(cb[:, D_FF:], 1), S)
        x2, h = ffn_down_call(act, _pad_ff(ffn_w_down[layer].astype(BF16), 0), x2, gain,
                              F32 if last else BF16)
    return h.reshape(B, S, D)
```

```python
import functools
import math

import jax
import jax.numpy as jnp
import numpy as np
from jax import lax
from jax.experimental import pallas as pl
from jax.experimental.pallas import tpu as pltpu

D_MODEL = 1024
DEPTH = 4
N_MIXERS = 3
N_BUCKETS = 32
MAX_DISTANCE = 128
DA_HEADS = 8
DA_QK_DIM = 64
DA_V_DIM = 128
SB_HEADS = 16
SB_DIM = 64
SW_Q_HEADS = 16
SW_KV_HEADS = 4
SW_DIM = 64
SW_BLOCK = 128
D_FF = 2752
EPS = 1e-6
NEG = -1e30

LANES = 128
SUBLANES = 8
VMEM_LIMIT = 52 * 1024 * 1024

ATT_T = 256
SB_EXP_ZERO = -104.0
ROW_TILE = 512
FFN_TM = 1024
FFN_TF = 256
D_FF_PAD = 2816

BF16 = jnp.bfloat16
F32 = jnp.float32


def _params(*sem):
    return pltpu.CompilerParams(dimension_semantics=sem, vmem_limit_bytes=VMEM_LIMIT)


def _rms(xf, gain):
    return xf * lax.rsqrt(jnp.mean(xf * xf, axis=-1, keepdims=True) + EPS) * gain


def _rmsnorm_kernel(x_ref, g_ref, h_ref):
    h_ref[...] = _rms(x_ref[...], g_ref[...]).astype(h_ref.dtype)


def rmsnorm_call(x2, gain):
    M, D = x2.shape
    return pl.pallas_call(
        _rmsnorm_kernel,
        out_shape=jax.ShapeDtypeStruct((M, D), BF16),
        grid=(M // ROW_TILE,),
        in_specs=[pl.BlockSpec((ROW_TILE, D), lambda i: (i, 0)),
                  pl.BlockSpec((1, D), lambda i: (0, 0))],
        out_specs=pl.BlockSpec((ROW_TILE, D), lambda i: (i, 0)),
        compiler_params=_params("parallel"),
        name="rmsnorm",
    )(x2, gain.reshape(1, D))


def _proj_kernel(h_ref, w_ref, o_ref, *, n_chunk):
    h = h_ref[...]
    for c in range(o_ref.shape[1] // n_chunk):
        sl = slice(c * n_chunk, (c + 1) * n_chunk)
        o_ref[:, sl] = jnp.dot(h, w_ref[:, sl], preferred_element_type=F32).astype(o_ref.dtype)


def proj_call(h, w):
    M, D = h.shape
    N = w.shape[1]
    return pl.pallas_call(
        functools.partial(_proj_kernel, n_chunk=512),
        out_shape=jax.ShapeDtypeStruct((M, N), BF16),
        grid=(M // ROW_TILE,),
        in_specs=[pl.BlockSpec((ROW_TILE, D), lambda i: (i, 0)),
                  pl.BlockSpec((D, N), lambda i: (0, 0))],
        out_specs=pl.BlockSpec((ROW_TILE, N), lambda i: (i, 0)),
        compiler_params=_params("parallel"),
        name="qkv_proj",
    )(h, w)


def _ffn_row_starts(n_groups):
    per = n_groups // SUBLANES
    return [SUBLANES * SUBLANES * (j % per) + j // per for j in range(n_groups)]


def _stage(ref, x):
    for c in range(ref.shape[0]):
        ref[c] = x[:, c * LANES:(c + 1) * LANES]


def _stage_interleaved(ref, x):
    for c in range(ref.shape[0]):
        for j, st in enumerate(_ffn_row_starts(x.shape[0] // SUBLANES)):
            ref[c, pl.ds(st, SUBLANES, stride=SUBLANES), :] = x[j * SUBLANES:(j + 1) * SUBLANES,
                                                                c * LANES:(c + 1) * LANES]


def _deinterleave_rows(ref, c):
    return jnp.concatenate([ref[c, pl.ds(st, SUBLANES, stride=SUBLANES), :]
                            for st in _ffn_row_starts(ref.shape[1] // SUBLANES)], axis=0)


def _wo_kernel(m_ref, w_ref, x_ref, g_ref, xo_ref, h_ref, stage_ref):
    xn = x_ref[...] + jnp.dot(m_ref[...], w_ref[...], preferred_element_type=F32)
    xo_ref[...] = xn
    _stage_interleaved(stage_ref, _rms(xn, g_ref[...]))
    for c in range(stage_ref.shape[0]):
        h_ref[:, c * LANES:(c + 1) * LANES] = stage_ref[c].astype(h_ref.dtype)


def wo_call(m, w, x2, gain):
    M, D = x2.shape
    row = lambda i: (i, 0)
    const = lambda i: (0, 0)
    return pl.pallas_call(
        _wo_kernel,
        out_shape=(jax.ShapeDtypeStruct((M, D), F32), jax.ShapeDtypeStruct((M, D), BF16)),
        grid=(M // ROW_TILE,),
        in_specs=[pl.BlockSpec((ROW_TILE, D), row), pl.BlockSpec((D, D), const),
                  pl.BlockSpec((ROW_TILE, D), row), pl.BlockSpec((1, D), const)],
        out_specs=(pl.BlockSpec((ROW_TILE, D), row), pl.BlockSpec((ROW_TILE, D), row)),
        scratch_shapes=[pltpu.VMEM((D // LANES, ROW_TILE, LANES), F32)],
        compiler_params=_params("parallel"),
        name="wo_residual_norm",
    )(m, w, x2, gain.reshape(1, D))


def _causal_conv3(u, cw, cb, prev):
    nblk, g, _, tf = u.shape
    first = lax.broadcasted_iota(jnp.int32, (SUBLANES, tf), 0) == 0
    u1, u2 = [], []
    for b in range(nblk):
        tail = prev if b == 0 else u[b - 1, g - 2:]
        wrap = [jnp.where(first, pltpu.roll(tail[e], 1, 0), pltpu.roll(u[b, g - 2 + e], 1, 0))
                for e in range(2)]
        u1.append(jnp.concatenate([wrap[1][None], u[b, :g - 1]], axis=0))
        u2.append(jnp.concatenate([wrap[0][None], wrap[1][None], u[b, :g - 2]], axis=0))
    u1 = jnp.stack(u1)
    u2 = jnp.stack(u2)
    return u * cw[2:3] + u1 * cw[1:2] + u2 * cw[0:1] + cb


def _ffn_up_kernel(h_ref, wg_ref, wv_ref, cwg_ref, cwv_ref, cbg_ref, cbv_ref, g_ref, carry_ref,
                   *, tiles_per_seq):
    tm = h_ref.shape[0]
    nblk = tm // ROW_TILE
    g = ROW_TILE // SUBLANES

    @pl.when(pl.program_id(0) % tiles_per_seq == 0)
    def _():
        carry_ref[...] = jnp.zeros(carry_ref.shape, F32)

    h = h_ref[...]
    for c in range(g_ref.shape[1] // FFN_TF):
        sl = slice(c * FFN_TF, (c + 1) * FFN_TF)
        ug = jnp.dot(h, wg_ref[:, sl], preferred_element_type=F32).reshape(nblk, g, SUBLANES, FFN_TF)
        uv = jnp.dot(h, wv_ref[:, sl], preferred_element_type=F32).reshape(nblk, g, SUBLANES, FFN_TF)
        prev_g = carry_ref[0, :, :, sl]
        prev_v = carry_ref[1, :, :, sl]
        carry_ref[0, :, :, sl] = ug[nblk - 1, g - 2:]
        carry_ref[1, :, :, sl] = uv[nblk - 1, g - 2:]
        yg = _causal_conv3(ug, cwg_ref[:, sl], cbg_ref[:, sl], prev_g)
        yv = _causal_conv3(uv, cwv_ref[:, sl], cbv_ref[:, sl], prev_v)
        act = yg * (1.0 / (1.0 + jnp.exp(-yg))) * yv
        g_ref[:, sl] = act.reshape(tm, FFN_TF).astype(g_ref.dtype)


def ffn_up_call(h, wg, wv, cwg, cwv, cbg, cbv, seq):
    M, D = h.shape
    F = wg.shape[1]
    row = lambda i: (i, 0)
    const = lambda i: (0, 0)
    resident = lambda shape: pl.BlockSpec(shape, const, pipeline_mode=pl.Buffered(1))
    return pl.pallas_call(
        functools.partial(_ffn_up_kernel, tiles_per_seq=seq // FFN_TM),
        out_shape=jax.ShapeDtypeStruct((M, F), BF16),
        grid=(M // FFN_TM,),
        in_specs=[pl.BlockSpec((FFN_TM, D), row), resident((D, F)), resident((D, F)),
                  resident((3, F)), resident((3, F)), resident((1, F)), resident((1, F))],
        out_specs=pl.BlockSpec((FFN_TM, F), row),
        scratch_shapes=[pltpu.VMEM((2, 2, SUBLANES, F), F32)],
        compiler_params=_params("arbitrary"),
        name="ffn_up_conv_gate",
    )(h, wg, wv, cwg, cwv, cbg, cbv)


def _ffn_down_kernel(a_ref, w_ref, x_ref, g_ref, xo_ref, h_ref, stage_ref):
    _stage(stage_ref, jnp.dot(a_ref[...], w_ref[...], preferred_element_type=F32))
    for c in range(stage_ref.shape[0]):
        sl = slice(c * LANES, (c + 1) * LANES)
        xo_ref[:, sl] = x_ref[:, sl] + _deinterleave_rows(stage_ref, c)
    h_ref[...] = _rms(xo_ref[...], g_ref[...]).astype(h_ref.dtype)


def ffn_down_call(a, w, x2, gain, out_dtype):
    M, D = x2.shape
    F = a.shape[1]
    row = lambda i: (i, 0)
    const = lambda i: (0, 0)
    return pl.pallas_call(
        _ffn_down_kernel,
        out_shape=(jax.ShapeDtypeStruct((M, D), F32), jax.ShapeDtypeStruct((M, D), out_dtype)),
        grid=(M // ROW_TILE,),
        in_specs=[pl.BlockSpec((ROW_TILE, F), row),
                  pl.BlockSpec((F, D), const, pipeline_mode=pl.Buffered(1)),
                  pl.BlockSpec((ROW_TILE, D), row), pl.BlockSpec((1, D), const)],
        out_specs=(pl.BlockSpec((ROW_TILE, D), row), pl.BlockSpec((ROW_TILE, D), row)),
        scratch_shapes=[pltpu.VMEM((D // LANES, ROW_TILE, LANES), F32)],
        compiler_params=_params("parallel"),
        name="ffn_down_residual_norm",
    )(a, w, x2, gain.reshape(1, D))


def _half_masks(q):
    lane = lax.broadcasted_iota(jnp.int32, q.shape, 1)
    zero = jnp.zeros_like(q)
    scale = jnp.asarray(DA_QK_DIM ** -0.5, q.dtype)
    return (jnp.where(lane < 64, q, zero) * scale, jnp.where(lane >= 64, q, zero) * scale)


def _qk(q, k):
    return lax.dot_general(q, k, (((1,), (1,)), ((), ())), preferred_element_type=F32)


def _da_kernel(q_ref, k_ref, v_ref, bias_ref, lam_ref, sub_ref, o_ref, s_ref, *, lambda_init):
    T = ATT_T
    nq = q_ref.shape[0] // T
    r = lax.broadcasted_iota(jnp.int32, (T, T), 0)
    cc = lax.broadcasted_iota(jnp.int32, (T, T), 1)
    causal = r >= cc
    lam = lam_ref[...]
    lam_full = (jnp.exp(jnp.sum(lam[0:1] * lam[1:2], keepdims=True))
                - jnp.exp(jnp.sum(lam[2:3] * lam[3:4], keepdims=True)) + lambda_init)
    def score_steps(qi, st):
        def first():
            st["q01"] = jnp.concatenate(_half_masks(q_ref[qi * T:(qi + 1) * T, :]), axis=0)
            st["mx"] = [None, None]

        def tile(j):
            if j == 0:
                first()
            ss = _qk(st["q01"], k_ref[j * T:(j + 1) * T, :])
            for c in range(2):
                s = ss[c * T:(c + 1) * T]
                if j == qi - 1:
                    s = s + bias_ref[c, :, 0:T]
                if j == qi:
                    s = jnp.where(causal, s + bias_ref[c, :, T:2 * T], NEG)
                s_ref[qi % 2, c, :, j * T:(j + 1) * T] = s
                t = jnp.maximum(s[:, :LANES], s[:, LANES:])
                st["mx"][c] = t if st["mx"][c] is None else jnp.maximum(st["mx"][c], t)
        return [functools.partial(tile, j) for j in range(qi + 1)]

    def value_steps(qi, st):
        def first():
            st["m2"] = []
            for c in range(2):
                m = jnp.broadcast_to(jnp.max(st["mx"][c], axis=-1, keepdims=True), (T, LANES))
                st["m2"].append(jnp.concatenate([m, m], axis=1))
            st["lsum"] = [None, None]
            st["acc"] = None

        def tile(j):
            if j == 0:
                first()
            ps = []
            for c in range(2):
                p = jnp.exp(s_ref[qi % 2, c, :, j * T:(j + 1) * T] - st["m2"][c])
                t = p[:, :LANES] + p[:, LANES:]
                st["lsum"][c] = t if st["lsum"][c] is None else st["lsum"][c] + t
                ps.append(p.astype(BF16))
            pv = jnp.dot(jnp.concatenate(ps, axis=0), v_ref[j * T:(j + 1) * T, :],
                         preferred_element_type=F32)
            st["acc"] = pv if st["acc"] is None else st["acc"] + pv
            if j == qi:
                outs = [st["acc"][c * T:(c + 1) * T] / jnp.sum(st["lsum"][c], axis=-1, keepdims=True)
                        for c in range(2)]
                o = outs[0] - lam_full * outs[1]
                o = _rms(o, sub_ref[...]) * (1.0 - lambda_init)
                o_ref[qi * T:(qi + 1) * T, :] = o.astype(o_ref.dtype)
        return [functools.partial(tile, j) for j in range(qi + 1)]

    states = [dict() for _ in range(nq)]
    pending = []
    for qi in range(nq + 1):
        scores = score_steps(qi, states[qi]) if qi < nq else []
        for t in range(max(len(scores), len(pending))):
            if t < len(scores):
                scores[t]()
            if t < len(pending):
                pending[t]()
        pending = value_steps(qi, states[qi]) if qi < nq else []


def da_call(qkv, bias_near, lam, subln, batch, seq, lambda_init):
    T = ATT_T
    nh = DA_HEADS
    return pl.pallas_call(
        functools.partial(_da_kernel, lambda_init=lambda_init),
        out_shape=jax.ShapeDtypeStruct((batch * seq, nh * DA_V_DIM), BF16),
        grid=(batch, nh),
        in_specs=[pl.BlockSpec((seq, LANES), lambda b, h: (b, h)),
                  pl.BlockSpec((seq, LANES), lambda b, h: (b, nh + h)),
                  pl.BlockSpec((seq, LANES), lambda b, h: (b, 2 * nh + h)),
                  pl.BlockSpec((2, T, 2 * T), lambda b, h: (h, 0, 0)),
                  pl.BlockSpec((4, DA_QK_DIM), lambda b, h: (0, 0)),
                  pl.BlockSpec((1, DA_V_DIM), lambda b, h: (0, 0))],
        out_specs=pl.BlockSpec((seq, LANES), lambda b, h: (b, h)),
        scratch_shapes=[pltpu.VMEM((2, 2, T, seq), F32)],
        compiler_params=_params("parallel", "parallel"),
        name="diff_attention",
    )(qkv, qkv, qkv, bias_near, lam, subln.reshape(1, DA_V_DIM))


def _split2(x):
    hi = x.astype(BF16)
    mid = (x - hi.astype(F32)).astype(BF16)
    return hi, mid


def _sb_kernel(q_ref, k_ref, v_ref, o_ref, lb_ref, hm_ref, acc_ref):
    T = ATT_T
    nq = q_ref.shape[0] // T
    kr = lax.broadcasted_iota(jnp.int32, (T, T), 0)
    kc = lax.broadcasted_iota(jnp.int32, (T, T), 1)
    suffix = jnp.where(kr > kc, 1.0, 0.0).astype(BF16)
    suffix2 = jnp.concatenate([suffix, suffix], axis=0)
    strict = kc < kr
    lane = lax.broadcasted_iota(jnp.int32, (T, LANES), 1)
    strict2 = jnp.concatenate([strict, strict], axis=0)

    def tile_group(qi, q01, tiles, carry, base):
        carry = list(carry)
        for pos, j in enumerate(tiles):
            zz = _qk(q01, k_ref[j * T:(j + 1) * T, :])
            for c in range(2):
                z = zz[c * T:(c + 1) * T]
                lb = jnp.minimum(z, 0.0) - jnp.log(1.0 + jnp.exp(-jnp.abs(z)))
                log_1m_beta = lb - z
                if j == qi:
                    log_1m_beta = jnp.where(strict, log_1m_beta, 0.0)
                hi, mid = _split2(log_1m_beta)
                blk = base + (pos * 2 + c) * T
                hm_ref[blk:blk + T, 0:T] = hi
                hm_ref[blk:blk + T, T:2 * T] = mid
                if carry[c] is not None:
                    lb = lb + carry[c]
                lb_ref[blk:blk + T, :] = lb
                if j > 0:
                    rs = jnp.sum(log_1m_beta, axis=-1, keepdims=True)
                    carry[c] = rs if carry[c] is None else carry[c] + rs
        rows = 2 * len(tiles) * T
        between = jnp.dot(hm_ref[base:base + rows, :], suffix2, preferred_element_type=F32)
        acc = None
        for pos, j in enumerate(tiles):
            sl = slice(2 * pos * T, 2 * (pos + 1) * T)
            a = jnp.exp(lb_ref[base + 2 * pos * T:base + 2 * (pos + 1) * T, :] + between[sl])
            if j == qi:
                a = jnp.where(strict2, a, 0.0)
            pv = jnp.dot(a.astype(BF16), v_ref[j * T:(j + 1) * T, :], preferred_element_type=F32)
            acc = pv if acc is None else acc + pv
        return acc, carry

    for qi in range(nq):
        q01 = jnp.concatenate(_half_masks(q_ref[qi * T:(qi + 1) * T, :]), axis=0)
        near = [j for j in (qi, qi - 1) if j >= 0]
        far = list(range(qi - 2, -1, -1))
        acc, carry = tile_group(qi, q01, near, [None, None], 0)
        if far:
            acc_ref[...] = jnp.zeros(acc_ref.shape, F32)
            nearest_sum = jnp.max(jnp.maximum(carry[0], carry[1]))

            @pl.when(nearest_sum > SB_EXP_ZERO)
            def _():
                acc_far, _ = tile_group(qi, q01, far, carry, 2 * len(near) * T)
                acc_ref[...] = acc_far

            acc = acc + acc_ref[...]
        o_ref[qi * T:(qi + 1) * T, :] = jnp.where(lane < 64, acc[0:T], acc[T:2 * T]).astype(o_ref.dtype)


def sb_call(qkv, batch, seq):
    npair = SB_HEADS // 2
    return pl.pallas_call(
        _sb_kernel,
        out_shape=jax.ShapeDtypeStruct((batch * seq, SB_HEADS * SB_DIM), BF16),
        grid=(batch, npair),
        in_specs=[pl.BlockSpec((seq, LANES), lambda b, p: (b, p)),
                  pl.BlockSpec((seq, LANES), lambda b, p: (b, npair + p)),
                  pl.BlockSpec((seq, LANES), lambda b, p: (b, 2 * npair + p))],
        out_specs=pl.BlockSpec((seq, LANES), lambda b, p: (b, p)),
        scratch_shapes=[pltpu.VMEM((2 * seq, ATT_T), F32), pltpu.VMEM((2 * seq, 2 * ATT_T), BF16),
                        pltpu.VMEM((2 * ATT_T, LANES), F32)],
        compiler_params=_params("parallel", "parallel"),
        name="stick_breaking_attention",
    )(qkv, qkv, qkv)


def _sw_kernel(sink_ref, q_ref, k_ref, v_ref, bias_ref, o_ref, s_ref):
    W = SW_BLOCK
    nb = q_ref.shape[0] // W
    p_id = pl.program_id(1)
    r = lax.broadcasted_iota(jnp.int32, (2 * W, 2 * W), 0) & (W - 1)
    cidx = lax.broadcasted_iota(jnp.int32, (2 * W, 2 * W), 1)
    valid = ((cidx < W) & (cidx > r)) | ((cidx >= W) & (cidx - W <= r))
    biasm = jnp.where(valid, bias_ref[...], NEG)
    row = lax.broadcasted_iota(jnp.int32, (2 * W, 1), 0)
    sink = jnp.where(row < W, sink_ref[2 * p_id], sink_ref[2 * p_id + 1])
    lane = lax.broadcasted_iota(jnp.int32, (W, LANES), 1)
    for n in range(nb):
        q01 = jnp.concatenate(_half_masks(q_ref[n * W:(n + 1) * W, :]), axis=0)
        if n == 0:
            s_ref[0, :, W:] = _qk(q01, k_ref[0:W, :]) + biasm[:, W:]
        else:
            s_ref[n] = _qk(q01, k_ref[(n - 1) * W:(n + 1) * W, :]) + biasm
    for n in range(nb):
        s = s_ref[0, :, W:] if n == 0 else s_ref[n]
        m = jnp.maximum(jnp.max(s, axis=-1, keepdims=True), sink)
        e = jnp.exp(s - m)
        denom = jnp.sum(e, axis=-1, keepdims=True) + jnp.exp(sink - m)
        v = v_ref[0:W, :] if n == 0 else v_ref[(n - 1) * W:(n + 1) * W, :]
        o = jnp.dot(e.astype(BF16), v, preferred_element_type=F32) / denom
        o_ref[n * W:(n + 1) * W, :] = jnp.where(lane < 64, o[0:W], o[W:]).astype(o_ref.dtype)


def sw_call(qkv, bias_band, sinks, batch, seq):
    npair = SW_Q_HEADS // 2
    q_blocks = SW_Q_HEADS * SW_DIM // LANES
    W = SW_BLOCK
    return pl.pallas_call(
        _sw_kernel,
        out_shape=jax.ShapeDtypeStruct((batch * seq, SW_Q_HEADS * SW_DIM), BF16),
        grid=(batch, npair),
        in_specs=[pl.BlockSpec(memory_space=pltpu.SMEM),
                  pl.BlockSpec((seq, LANES), lambda b, p: (b, p)),
                  pl.BlockSpec((seq, LANES), lambda b, p: (b, q_blocks + p // 2)),
                  pl.BlockSpec((seq, LANES), lambda b, p: (b, q_blocks + SW_KV_HEADS + p // 2)),
                  pl.BlockSpec((None, 2 * W, 2 * W), lambda b, p: (p, 0, 0))],
        out_specs=pl.BlockSpec((seq, LANES), lambda b, p: (b, p)),
        scratch_shapes=[pltpu.VMEM((seq // W, 2 * W, 2 * W), F32)],
        compiler_params=_params("parallel", "parallel"),
        name="sliding_window_attention",
    )(sinks, qkv, qkv, qkv, bias_band.reshape(npair, 2 * W, 2 * W))


def _t5_bucket(dist):
    max_exact = N_BUCKETS // 2
    d = jnp.maximum(dist, 0)
    large = max_exact + (jnp.log(jnp.maximum(d, 1).astype(F32) / max_exact)
                         / math.log(MAX_DISTANCE / max_exact) * (N_BUCKETS - max_exact)).astype(jnp.int32)
    large = jnp.minimum(large, N_BUCKETS - 1)
    return jnp.where(d < max_exact, d, large)


def _bias_band(rel_bias, T):
    y = np.arange(3 * T)
    dist = np.clip(np.where(y <= 2 * T, T - y, 4 * T - y), 0, 2 * T - 1)
    row = rel_bias.astype(F32)[_t5_bucket(jnp.asarray(dist, jnp.int32))].T
    flat = jnp.tile(row, (1, T))[:, :T * (3 * T - 1)]
    return flat.reshape(-1, T, 3 * T - 1)[:, :, :2 * T]


def _lambda_init(layer):
    return 0.8 - 0.6 * math.exp(-0.3 * layer)


def _dup_kv_heads(w):
    q_w = SW_Q_HEADS * SW_DIM
    kv = w[:, q_w:].reshape(w.shape[0], 2 * SW_KV_HEADS, 1, SW_DIM)
    kv = jnp.broadcast_to(kv, (w.shape[0], 2 * SW_KV_HEADS, 2, SW_DIM)).reshape(w.shape[0], -1)
    return jnp.concatenate([w[:, :q_w], kv], axis=1)


def _pad_ff(a, axis):
    pad = [(0, 0)] * a.ndim
    pad[axis] = (0, D_FF_PAD - D_FF)
    return jnp.pad(a, pad)


def kernel(x, rel_bias, attn_norm, ffn_norm, w_o, da_w_qkv, da_lambda, da_subln, sb_w_qkv, sw_w_qkv,
           sw_sinks, ffn_w_up, ffn_conv_w, ffn_conv_b, ffn_w_down, final_norm):
    B, S, D = x.shape
    x2 = x.reshape(B * S, D)
    da_bias_near = _bias_band(rel_bias, ATT_T) - rel_bias.astype(F32)[N_BUCKETS - 1][:, None, None]
    sw_bias = _bias_band(rel_bias, SW_BLOCK)

    h = rmsnorm_call(x2, attn_norm[0])
    for layer in range(DEPTH):
        mixer = layer % N_MIXERS
        slot = layer // N_MIXERS
        if mixer == 0:
            qkv = proj_call(h, da_w_qkv[slot].astype(BF16))
            m = da_call(qkv, da_bias_near, da_lambda[slot], da_subln[slot], B, S, _lambda_init(layer))
        elif mixer == 1:
            qkv = proj_call(h, sb_w_qkv[slot].astype(BF16))
            m = sb_call(qkv, B, S)
        else:
            qkv = proj_call(h, _dup_kv_heads(sw_w_qkv[slot]).astype(BF16))
            m = sw_call(qkv, sw_bias, sw_sinks[slot], B, S)
        x2, h = wo_call(m, w_o[layer].astype(BF16), x2, ffn_norm[layer])

        w_up = ffn_w_up[layer]
        wg = _pad_ff(w_up[:, :D_FF].astype(BF16), 1)
        wv = _pad_ff(w_up[:, D_FF:].astype(BF16), 1)
        cw, cb = ffn_conv_w[layer], ffn_conv_b[layer].reshape(1, -1)
        last = layer == DEPTH - 1
        gain = final_norm if last else attn_norm[layer + 1]
        act = ffn_up_call(h, wg, wv,
                          _pad_ff(cw[:, :D_FF], 1), _pad_ff(cw[:, D_FF:], 1),
                          _pad_ff(cb[:, :D_FF], 1), _pad_ff(cb[:, D_FF:], 1), S)
        x2, h = ffn_down_call(act, _pad_ff(ffn_w_down[layer].astype(BF16), 0), x2, gain,
                              F32 if last else BF16)
    return h.reshape(B, S, D)
```

```python
import functools
import math

import jax
import jax.numpy as jnp
import numpy as np
from jax import lax
from jax.experimental import pallas as pl
from jax.experimental.pallas import tpu as pltpu

D_MODEL = 1024
DEPTH = 4
N_MIXERS = 3
N_BUCKETS = 32
MAX_DISTANCE = 128
DA_HEADS = 8
DA_QK_DIM = 64
DA_V_DIM = 128
SB_HEADS = 16
SB_DIM = 64
SW_Q_HEADS = 16
SW_KV_HEADS = 4
SW_DIM = 64
SW_BLOCK = 128
D_FF = 2752
EPS = 1e-6
NEG = -1e30

LANES = 128
SUBLANES = 8
VMEM_LIMIT = 52 * 1024 * 1024

ATT_T = 256
SB_EXP_ZERO = -104.0
ROW_TILE = 512
FFN_TM = 1024
FFN_TF = 256
D_FF_PAD = 2816

BF16 = jnp.bfloat16
F32 = jnp.float32


def _params(*sem):
    return pltpu.CompilerParams(dimension_semantics=sem, vmem_limit_bytes=VMEM_LIMIT)


def _rms(xf, gain):
    return xf * lax.rsqrt(jnp.mean(xf * xf, axis=-1, keepdims=True) + EPS) * gain


def _rmsnorm_kernel(x_ref, g_ref, h_ref):
    h_ref[...] = _rms(x_ref[...], g_ref[...]).astype(h_ref.dtype)


def rmsnorm_call(x2, gain):
    M, D = x2.shape
    return pl.pallas_call(
        _rmsnorm_kernel,
        out_shape=jax.ShapeDtypeStruct((M, D), BF16),
        grid=(M // ROW_TILE,),
        in_specs=[pl.BlockSpec((ROW_TILE, D), lambda i: (i, 0)),
                  pl.BlockSpec((1, D), lambda i: (0, 0))],
        out_specs=pl.BlockSpec((ROW_TILE, D), lambda i: (i, 0)),
        compiler_params=_params("parallel"),
        name="rmsnorm",
    )(x2, gain.reshape(1, D))


def _proj_kernel(h_ref, w_ref, o_ref, *, n_chunk):
    h = h_ref[...]
    for c in range(o_ref.shape[1] // n_chunk):
        sl = slice(c * n_chunk, (c + 1) * n_chunk)
        o_ref[:, sl] = jnp.dot(h, w_ref[:, sl], preferred_element_type=F32).astype(o_ref.dtype)


def proj_call(h, w):
    M, D = h.shape
    N = w.shape[1]
    return pl.pallas_call(
        functools.partial(_proj_kernel, n_chunk=512),
        out_shape=jax.ShapeDtypeStruct((M, N), BF16),
        grid=(M // ROW_TILE,),
        in_specs=[pl.BlockSpec((ROW_TILE, D), lambda i: (i, 0)),
                  pl.BlockSpec((D, N), lambda i: (0, 0))],
        out_specs=pl.BlockSpec((ROW_TILE, N), lambda i: (i, 0)),
        compiler_params=_params("parallel"),
        name="qkv_proj",
    )(h, w)


def _ffn_row_starts(n_groups):
    per = n_groups // SUBLANES
    return [SUBLANES * SUBLANES * (j % per) + j // per for j in range(n_groups)]


def _stage(ref, x):
    for c in range(ref.shape[0]):
        ref[c] = x[:, c * LANES:(c + 1) * LANES]


def _stage_interleaved(ref, x):
    for c in range(ref.shape[0]):
        for j, st in enumerate(_ffn_row_starts(x.shape[0] // SUBLANES)):
            ref[c, pl.ds(st, SUBLANES, stride=SUBLANES), :] = x[j * SUBLANES:(j + 1) * SUBLANES,
                                                                c * LANES:(c + 1) * LANES]


def _deinterleave_rows(ref, c):
    return jnp.concatenate([ref[c, pl.ds(st, SUBLANES, stride=SUBLANES), :]
                            for st in _ffn_row_starts(ref.shape[1] // SUBLANES)], axis=0)


def _wo_kernel(m_ref, w_ref, x_ref, g_ref, xo_ref, h_ref, stage_ref):
    xn = x_ref[...] + jnp.dot(m_ref[...], w_ref[...], preferred_element_type=F32)
    xo_ref[...] = xn
    _stage_interleaved(stage_ref, _rms(xn, g_ref[...]))
    for c in range(stage_ref.shape[0]):
        h_ref[:, c * LANES:(c + 1) * LANES] = stage_ref[c].astype(h_ref.dtype)


def wo_call(m, w, x2, gain):
    M, D = x2.shape
    row = lambda i: (i, 0)
    const = lambda i: (0, 0)
    return pl.pallas_call(
        _wo_kernel,
        out_shape=(jax.ShapeDtypeStruct((M, D), F32), jax.ShapeDtypeStruct((M, D), BF16)),
        grid=(M // ROW_TILE,),
        in_specs=[pl.BlockSpec((ROW_TILE, D), row), pl.BlockSpec((D, D), const),
                  pl.BlockSpec((ROW_TILE, D), row), pl.BlockSpec((1, D), const)],
        out_specs=(pl.BlockSpec((ROW_TILE, D), row), pl.BlockSpec((ROW_TILE, D), row)),
        scratch_shapes=[pltpu.VMEM((D // LANES, ROW_TILE, LANES), F32)],
        compiler_params=_params("parallel"),
        name="wo_residual_norm",
    )(m, w, x2, gain.reshape(1, D))


def _causal_conv3(u, cw, cb, prev):
    nblk, g, _, tf = u.shape
    first = lax.broadcasted_iota(jnp.int32, (SUBLANES, tf), 0) == 0
    u1, u2 = [], []
    for b in range(nblk):
        tail = prev if b == 0 else u[b - 1, g - 2:]
        wrap = [jnp.where(first, pltpu.roll(tail[e], 1, 0), pltpu.roll(u[b, g - 2 + e], 1, 0))
                for e in range(2)]
        u1.append(jnp.concatenate([wrap[1][None], u[b, :g - 1]], axis=0))
        u2.append(jnp.concatenate([wrap[0][None], wrap[1][None], u[b, :g - 2]], axis=0))
    u1 = jnp.stack(u1)
    u2 = jnp.stack(u2)
    return u * cw[2:3] + u1 * cw[1:2] + u2 * cw[0:1] + cb


def _ffn_up_kernel(h_ref, wg_ref, wv_ref, cwg_ref, cwv_ref, cbg_ref, cbv_ref, g_ref, carry_ref,
                   *, tiles_per_seq):
    tm = h_ref.shape[0]
    nblk = tm // ROW_TILE
    g = ROW_TILE // SUBLANES

    @pl.when(pl.program_id(0) % tiles_per_seq == 0)
    def _():
        carry_ref[...] = jnp.zeros(carry_ref.shape, F32)

    h = h_ref[...]
    for c in range(g_ref.shape[1] // FFN_TF):
        sl = slice(c * FFN_TF, (c + 1) * FFN_TF)
        ug = jnp.dot(h, wg_ref[:, sl], preferred_element_type=F32).reshape(nblk, g, SUBLANES, FFN_TF)
        uv = jnp.dot(h, wv_ref[:, sl], preferred_element_type=F32).reshape(nblk, g, SUBLANES, FFN_TF)
        prev_g = carry_ref[0, :, :, sl]
        prev_v = carry_ref[1, :, :, sl]
        carry_ref[0, :, :, sl] = ug[nblk - 1, g - 2:]
        carry_ref[1, :, :, sl] = uv[nblk - 1, g - 2:]
        yg = _causal_conv3(ug, cwg_ref[:, sl], cbg_ref[:, sl], prev_g)
        yv = _causal_conv3(uv, cwv_ref[:, sl], cbv_ref[:, sl], prev_v)
        act = yg * (1.0 / (1.0 + jnp.exp(-yg))) * yv
        g_ref[:, sl] = act.reshape(tm, FFN_TF).astype(g_ref.dtype)


def ffn_up_call(h, wg, wv, cwg, cwv, cbg, cbv, seq):
    M, D = h.shape
    F = wg.shape[1]
    row = lambda i: (i, 0)
    const = lambda i: (0, 0)
    resident = lambda shape: pl.BlockSpec(shape, const, pipeline_mode=pl.Buffered(1))
    return pl.pallas_call(
        functools.partial(_ffn_up_kernel, tiles_per_seq=seq // FFN_TM),
        out_shape=jax.ShapeDtypeStruct((M, F), BF16),
        grid=(M // FFN_TM,),
        in_specs=[pl.BlockSpec((FFN_TM, D), row), resident((D, F)), resident((D, F)),
                  resident((3, F)), resident((3, F)), resident((1, F)), resident((1, F))],
        out_specs=pl.BlockSpec((FFN_TM, F), row),
        scratch_shapes=[pltpu.VMEM((2, 2, SUBLANES, F), F32)],
        compiler_params=_params("arbitrary"),
        name="ffn_up_conv_gate",
    )(h, wg, wv, cwg, cwv, cbg, cbv)


def _ffn_down_kernel(a_ref, w_ref, x_ref, g_ref, xo_ref, h_ref, stage_ref):
    _stage(stage_ref, jnp.dot(a_ref[...], w_ref[...], preferred_element_type=F32))
    for c in range(stage_ref.shape[0]):
        sl = slice(c * LANES, (c + 1) * LANES)
        xo_ref[:, sl] = x_ref[:, sl] + _deinterleave_rows(stage_ref, c)
    h_ref[...] = _rms(xo_ref[...], g_ref[...]).astype(h_ref.dtype)


def ffn_down_call(a, w, x2, gain, out_dtype):
    M, D = x2.shape
    F = a.shape[1]
    row = lambda i: (i, 0)
    const = lambda i: (0, 0)
    return pl.pallas_call(
        _ffn_down_kernel,
        out_shape=(jax.ShapeDtypeStruct((M, D), F32), jax.ShapeDtypeStruct((M, D), out_dtype)),
        grid=(M // ROW_TILE,),
        in_specs=[pl.BlockSpec((ROW_TILE, F), row),
                  pl.BlockSpec((F, D), const, pipeline_mode=pl.Buffered(1)),
                  pl.BlockSpec((ROW_TILE, D), row), pl.BlockSpec((1, D), const)],
        out_specs=(pl.BlockSpec((ROW_TILE, D), row), pl.BlockSpec((ROW_TILE, D), row)),
        scratch_shapes=[pltpu.VMEM((D // LANES, ROW_TILE, LANES), F32)],
        compiler_params=_params("parallel"),
        name="ffn_down_residual_norm",
    )(a, w, x2, gain.reshape(1, D))


def _half_masks(q):
    lane = lax.broadcasted_iota(jnp.int32, q.shape, 1)
    zero = jnp.zeros_like(q)
    scale = jnp.asarray(DA_QK_DIM ** -0.5, q.dtype)
    return (jnp.where(lane < 64, q, zero) * scale, jnp.where(lane >= 64, q, zero) * scale)


def _qk(q, k):
    return lax.dot_general(q, k, (((1,), (1,)), ((), ())), preferred_element_type=F32)


def _da_kernel(q_ref, k_ref, v_ref, bias_ref, lam_ref, sub_ref, o_ref, s_ref, *, lambda_init):
    T = ATT_T
    nq = q_ref.shape[0] // T
    r = lax.broadcasted_iota(jnp.int32, (T, T), 0)
    cc = lax.broadcasted_iota(jnp.int32, (T, T), 1)
    causal = r >= cc
    lam = lam_ref[...]
    lam_full = (jnp.exp(jnp.sum(lam[0:1] * lam[1:2], keepdims=True))
                - jnp.exp(jnp.sum(lam[2:3] * lam[3:4], keepdims=True)) + lambda_init)
    def score_steps(qi, st):
        def first():
            st["q01"] = jnp.concatenate(_half_masks(q_ref[qi * T:(qi + 1) * T, :]), axis=0)
            st["mx"] = [None, None]

        def tile(j):
            if j == 0:
                first()
            ss = _qk(st["q01"], k_ref[j * T:(j + 1) * T, :])
            for c in range(2):
                s = ss[c * T:(c + 1) * T]
                if j == qi - 1:
                    s = s + bias_ref[c, :, 0:T]
                if j == qi:
                    s = jnp.where(causal, s + bias_ref[c, :, T:2 * T], NEG)
                s_ref[qi % 2, c, :, j * T:(j + 1) * T] = s
                t = jnp.maximum(s[:, :LANES], s[:, LANES:])
                st["mx"][c] = t if st["mx"][c] is None else jnp.maximum(st["mx"][c], t)
        return [functools.partial(tile, j) for j in range(qi + 1)]

    def value_steps(qi, st):
        def first():
            st["m2"] = []
            for c in range(2):
                m = jnp.broadcast_to(jnp.max(st["mx"][c], axis=-1, keepdims=True), (T, LANES))
                st["m2"].append(jnp.concatenate([m, m], axis=1))
            st["lsum"] = [None, None]
            st["acc"] = None

        def tile(j):
            if j == 0:
                first()
            ps = []
            for c in range(2):
                p = jnp.exp(s_ref[qi % 2, c, :, j * T:(j + 1) * T] - st["m2"][c])
                t = p[:, :LANES] + p[:, LANES:]
                st["lsum"][c] = t if st["lsum"][c] is None else st["lsum"][c] + t
                ps.append(p.astype(BF16))
            pv = jnp.dot(jnp.concatenate(ps, axis=0), v_ref[j * T:(j + 1) * T, :],
                         preferred_element_type=F32)
            st["acc"] = pv if st["acc"] is None else st["acc"] + pv
            if j == qi:
                outs = [st["acc"][c * T:(c + 1) * T] / jnp.sum(st["lsum"][c], axis=-1, keepdims=True)
                        for c in range(2)]
                o = outs[0] - lam_full * outs[1]
                o = _rms(o, sub_ref[...]) * (1.0 - lambda_init)
                o_ref[qi * T:(qi + 1) * T, :] = o.astype(o_ref.dtype)
        return [functools.partial(tile, j) for j in range(qi + 1)]

    states = [dict() for _ in range(nq)]
    pending = []
    for qi in range(nq + 1):
        scores = score_steps(qi, states[qi]) if qi < nq else []
        for t in range(max(len(scores), len(pending))):
            if t < len(scores):
                scores[t]()
            if t < len(pending):
                pending[t]()
        pending = value_steps(qi, states[qi]) if qi < nq else []


def da_call(qkv, bias_near, lam, subln, batch, seq, lambda_init):
    T = ATT_T
    nh = DA_HEADS
    return pl.pallas_call(
        functools.partial(_da_kernel, lambda_init=lambda_init),
        out_shape=jax.ShapeDtypeStruct((batch * seq, nh * DA_V_DIM), BF16),
        grid=(batch, nh),
        in_specs=[pl.BlockSpec((seq, LANES), lambda b, h: (b, h)),
                  pl.BlockSpec((seq, LANES), lambda b, h: (b, nh + h)),
                  pl.BlockSpec((seq, LANES), lambda b, h: (b, 2 * nh + h)),
                  pl.BlockSpec((2, T, 2 * T), lambda b, h: (h, 0, 0)),
                  pl.BlockSpec((4, DA_QK_DIM), lambda b, h: (0, 0)),
                  pl.BlockSpec((1, DA_V_DIM), lambda b, h: (0, 0))],
        out_specs=pl.BlockSpec((seq, LANES), lambda b, h: (b, h)),
        scratch_shapes=[pltpu.VMEM((2, 2, T, seq), F32)],
        compiler_params=_params("parallel", "parallel"),
        name="diff_attention",
    )(qkv, qkv, qkv, bias_near, lam, subln.reshape(1, DA_V_DIM))


def _split2(x):
    hi = x.astype(BF16)
    mid = (x - hi.astype(F32)).astype(BF16)
    return hi, mid


def _sb_kernel(q_ref, k_ref, v_ref, o_ref, lbn_ref, hmn_ref, lbf_ref, hmf_ref, acc_ref):
    T = ATT_T
    nq = q_ref.shape[0] // T
    kr = lax.broadcasted_iota(jnp.int32, (T, T), 0)
    kc = lax.broadcasted_iota(jnp.int32, (T, T), 1)
    suffix = jnp.where(kr > kc, 1.0, 0.0).astype(BF16)
    suffix2 = jnp.concatenate([suffix, suffix], axis=0)
    strict = kc < kr
    lane = lax.broadcasted_iota(jnp.int32, (T, LANES), 1)
    strict2 = jnp.concatenate([strict, strict], axis=0)

    def score_tile(qi, q01, j, blk, carry, lb_view, hm_view):
        zz = _qk(q01, k_ref[j * T:(j + 1) * T, :])
        for c in range(2):
            z = zz[c * T:(c + 1) * T]
            lb = jnp.minimum(z, 0.0) - jnp.log(1.0 + jnp.exp(-jnp.abs(z)))
            log_1m_beta = lb - z
            if j == qi:
                log_1m_beta = jnp.where(strict, log_1m_beta, 0.0)
            hi, mid = _split2(log_1m_beta)
            rows = slice(blk + c * T, blk + (c + 1) * T)
            hm_view[rows, 0:T] = hi
            hm_view[rows, T:2 * T] = mid
            if carry[c] is not None:
                lb = lb + carry[c]
            lb_view[rows, :] = lb
            if j > 0:
                rs = jnp.sum(log_1m_beta, axis=-1, keepdims=True)
                carry[c] = rs if carry[c] is None else carry[c] + rs

    def value_tiles(qi, tiles, lb_view, hm_view):
        between = jnp.dot(hm_view[0:2 * len(tiles) * T, :], suffix2, preferred_element_type=F32)
        acc = None
        for pos, j in enumerate(tiles):
            rows = slice(2 * pos * T, 2 * (pos + 1) * T)
            a = jnp.exp(lb_view[rows, :] + between[rows])
            if j == qi:
                a = jnp.where(strict2, a, 0.0)
            pv = jnp.dot(a.astype(BF16), v_ref[j * T:(j + 1) * T, :], preferred_element_type=F32)
            acc = pv if acc is None else acc + pv
        return acc

    def stacked_q(qi):
        return jnp.concatenate(_half_masks(q_ref[qi * T:(qi + 1) * T, :]), axis=0)

    near = [[j for j in (qi, qi - 1) if j >= 0] for qi in range(nq)]
    carries = [[None, None] for _ in range(nq)]
    for qi in range(nq + 1):
        if qi < nq:
            q01 = stacked_q(qi)
            for pos, j in enumerate(near[qi]):
                score_tile(qi, q01, j, 2 * pos * T, carries[qi], lbn_ref.at[qi % 2], hmn_ref.at[qi % 2])
        if qi > 0:
            acc_ref[qi - 1] = value_tiles(qi - 1, near[qi - 1], lbn_ref.at[(qi - 1) % 2], hmn_ref.at[(qi - 1) % 2])

    for qi in range(2, nq):
        far = list(range(qi - 2, -1, -1))
        carry = carries[qi]
        nearest_sum = jnp.max(jnp.maximum(carry[0], carry[1]))

        @pl.when(nearest_sum > SB_EXP_ZERO)
        def _():
            q01 = stacked_q(qi)
            far_carry = list(carry)
            for pos, j in enumerate(far):
                score_tile(qi, q01, j, 2 * pos * T, far_carry, lbf_ref, hmf_ref)
            acc_ref[qi] += value_tiles(qi, far, lbf_ref, hmf_ref)

    for qi in range(nq):
        acc = acc_ref[qi]
        o_ref[qi * T:(qi + 1) * T, :] = jnp.where(lane < 64, acc[0:T], acc[T:2 * T]).astype(o_ref.dtype)


def sb_call(qkv, batch, seq):
    npair = SB_HEADS // 2
    return pl.pallas_call(
        _sb_kernel,
        out_shape=jax.ShapeDtypeStruct((batch * seq, SB_HEADS * SB_DIM), BF16),
        grid=(batch, npair),
        in_specs=[pl.BlockSpec((seq, LANES), lambda b, p: (b, p)),
                  pl.BlockSpec((seq, LANES), lambda b, p: (b, npair + p)),
                  pl.BlockSpec((seq, LANES), lambda b, p: (b, 2 * npair + p))],
        out_specs=pl.BlockSpec((seq, LANES), lambda b, p: (b, p)),
        scratch_shapes=[pltpu.VMEM((2, 4 * ATT_T, ATT_T), F32), pltpu.VMEM((2, 4 * ATT_T, 2 * ATT_T), BF16),
                        pltpu.VMEM((2 * seq - 4 * ATT_T, ATT_T), F32),
                        pltpu.VMEM((2 * seq - 4 * ATT_T, 2 * ATT_T), BF16),
                        pltpu.VMEM((seq // ATT_T, 2 * ATT_T, LANES), F32)],
        compiler_params=_params("parallel", "parallel"),
        name="stick_breaking_attention",
    )(qkv, qkv, qkv)


def _sw_kernel(sink_ref, q_ref, k_ref, v_ref, bias_ref, o_ref, s_ref):
    W = SW_BLOCK
    nb = q_ref.shape[0] // W
    p_id = pl.program_id(1)
    r = lax.broadcasted_iota(jnp.int32, (2 * W, 2 * W), 0) & (W - 1)
    cidx = lax.broadcasted_iota(jnp.int32, (2 * W, 2 * W), 1)
    valid = ((cidx < W) & (cidx > r)) | ((cidx >= W) & (cidx - W <= r))
    biasm = jnp.where(valid, bias_ref[...], NEG)
    row = lax.broadcasted_iota(jnp.int32, (2 * W, 1), 0)
    sink = jnp.where(row < W, sink_ref[2 * p_id], sink_ref[2 * p_id + 1])
    lane = lax.broadcasted_iota(jnp.int32, (W, LANES), 1)
    for n in range(nb):
        q01 = jnp.concatenate(_half_masks(q_ref[n * W:(n + 1) * W, :]), axis=0)
        if n == 0:
            s_ref[0, :, W:] = _qk(q01, k_ref[0:W, :]) + biasm[:, W:]
        else:
            s_ref[n] = _qk(q01, k_ref[(n - 1) * W:(n + 1) * W, :]) + biasm
    for n in range(nb):
        s = s_ref[0, :, W:] if n == 0 else s_ref[n]
        m = jnp.maximum(jnp.max(s, axis=-1, keepdims=True), sink)
        e = jnp.exp(s - m)
        denom = jnp.sum(e, axis=-1, keepdims=True) + jnp.exp(sink - m)
        v = v_ref[0:W, :] if n == 0 else v_ref[(n - 1) * W:(n + 1) * W, :]
        o = jnp.dot(e.astype(BF16), v, preferred_element_type=F32) / denom
        o_ref[n * W:(n + 1) * W, :] = jnp.where(lane < 64, o[0:W], o[W:]).astype(o_ref.dtype)


def sw_call(qkv, bias_band, sinks, batch, seq):
    npair = SW_Q_HEADS // 2
    q_blocks = SW_Q_HEADS * SW_DIM // LANES
    W = SW_BLOCK
    return pl.pallas_call(
        _sw_kernel,
        out_shape=jax.ShapeDtypeStruct((batch * seq, SW_Q_HEADS * SW_DIM), BF16),
        grid=(batch, npair),
        in_specs=[pl.BlockSpec(memory_space=pltpu.SMEM),
                  pl.BlockSpec((seq, LANES), lambda b, p: (b, p)),
                  pl.BlockSpec((seq, LANES), lambda b, p: (b, q_blocks + p // 2)),
                  pl.BlockSpec((seq, LANES), lambda b, p: (b, q_blocks + SW_KV_HEADS + p // 2)),
                  pl.BlockSpec((None, 2 * W, 2 * W), lambda b, p: (p, 0, 0))],
        out_specs=pl.BlockSpec((seq, LANES), lambda b, p: (b, p)),
        scratch_shapes=[pltpu.VMEM((seq // W, 2 * W, 2 * W), F32)],
        compiler_params=_params("parallel", "parallel"),
        name="sliding_window_attention",
    )(sinks, qkv, qkv, qkv, bias_band.reshape(npair, 2 * W, 2 * W))


def _t5_bucket(dist):
    max_exact = N_BUCKETS // 2
    d = jnp.maximum(dist, 0)
    large = max_exact + (jnp.log(jnp.maximum(d, 1).astype(F32) / max_exact)
                         / math.log(MAX_DISTANCE / max_exact) * (N_BUCKETS - max_exact)).astype(jnp.int32)
    large = jnp.minimum(large, N_BUCKETS - 1)
    return jnp.where(d < max_exact, d, large)


def _bias_band(rel_bias, T):
    y = np.arange(3 * T)
    dist = np.clip(np.where(y <= 2 * T, T - y, 4 * T - y), 0, 2 * T - 1)
    row = rel_bias.astype(F32)[_t5_bucket(jnp.asarray(dist, jnp.int32))].T
    flat = jnp.tile(row, (1, T))[:, :T * (3 * T - 1)]
    return flat.reshape(-1, T, 3 * T - 1)[:, :, :2 * T]


def _lambda_init(layer):
    return 0.8 - 0.6 * math.exp(-0.3 * layer)


def _dup_kv_heads(w):
    q_w = SW_Q_HEADS * SW_DIM
    kv = w[:, q_w:].reshape(w.shape[0], 2 * SW_KV_HEADS, 1, SW_DIM)
    kv = jnp.broadcast_to(kv, (w.shape[0], 2 * SW_KV_HEADS, 2, SW_DIM)).reshape(w.shape[0], -1)
    return jnp.concatenate([w[:, :q_w], kv], axis=1)


def _pad_ff(a, axis):
    pad = [(0, 0)] * a.ndim
    pad[axis] = (0, D_FF_PAD - D_FF)
    return jnp.pad(a, pad)


def kernel(x, rel_bias, attn_norm, ffn_norm, w_o, da_w_qkv, da_lambda, da_subln, sb_w_qkv, sw_w_qkv,
           sw_sinks, ffn_w_up, ffn_conv_w, ffn_conv_b, ffn_w_down, final_norm):
    B, S, D = x.shape
    x2 = x.reshape(B * S, D)
    da_bias_near = _bias_band(rel_bias, ATT_T) - rel_bias.astype(F32)[N_BUCKETS - 1][:, None, None]
    sw_bias = _bias_band(rel_bias, SW_BLOCK)

    h = rmsnorm_call(x2, attn_norm[0])
    for layer in range(DEPTH):
        mixer = layer % N_MIXERS
        slot = layer // N_MIXERS
        if mixer == 0:
            qkv = proj_call(h, da_w_qkv[slot].astype(BF16))
            m = da_call(qkv, da_bias_near, da_lambda[slot], da_subln[slot], B, S, _lambda_init(layer))
        elif mixer == 1:
            qkv = proj_call(h, sb_w_qkv[slot].astype(BF16))
            m = sb_call(qkv, B, S)
        else:
            qkv = proj_call(h, _dup_kv_heads(sw_w_qkv[slot]).astype(BF16))
            m = sw_call(qkv, sw_bias, sw_sinks[slot], B, S)
        x2, h = wo_call(m, w_o[layer].astype(BF16), x2, ffn_norm[layer])

        w_up = ffn_w_up[layer]
        wg = _pad_ff(w_up[:, :D_FF].astype(BF16), 1)
        wv = _pad_ff(w_up[:, D_FF:].astype(BF16), 1)
        cw, cb = ffn_conv_w[layer], ffn_conv_b[layer].reshape(1, -1)
        last = layer == DEPTH - 1
        gain = final_norm if last else attn_norm[layer + 1]
        act = ffn_up_call(h, wg, wv,
                          _pad_ff(cw[:, :D_FF], 1), _pad_ff(cw[:, D_FF:], 1),
                          _pad_ff(cb[:, :D_FF], 1), _pad_ff(cb[:, D_FF:], 1), S)
        x2, h = ffn_down_call(act, _pad_ff(ffn_w_down[layer].astype(BF16), 0), x2, gain,
                              F32 if last else BF16)
    return h.reshape(B, S, D)
```

```python
import functools
import math

import jax
import jax.numpy as jnp
import numpy as np
from jax import lax
from jax.experimental import pallas as pl
from jax.experimental.pallas import tpu as pltpu

D_MODEL = 1024
DEPTH = 4
N_MIXERS = 3
N_BUCKETS = 32
MAX_DISTANCE = 128
DA_HEADS = 8
DA_QK_DIM = 64
DA_V_DIM = 128
SB_HEADS = 16
SB_DIM = 64
SW_Q_HEADS = 16
SW_KV_HEADS = 4
SW_DIM = 64
SW_BLOCK = 128
D_FF = 2752
EPS = 1e-6
NEG = -1e30

LANES = 128
SUBLANES = 8
VMEM_LIMIT = 52 * 1024 * 1024

ATT_T = 256
SB_EXP_ZERO = -104.0
ROW_TILE = 512
FFN_TM = 1024
FFN_TF = 256
D_FF_PAD = 2816

BF16 = jnp.bfloat16
F32 = jnp.float32


def _params(*sem):
    return pltpu.CompilerParams(dimension_semantics=sem, vmem_limit_bytes=VMEM_LIMIT)


def _layer_spec(arr, layer, single_buffer=False):
    shape = (None,) + arr.shape[1:]
    index_map = lambda *_: (layer,) + (0,) * (arr.ndim - 1)
    if single_buffer:
        return pl.BlockSpec(shape, index_map, pipeline_mode=pl.Buffered(1))
    return pl.BlockSpec(shape, index_map)


def _rms(xf, gain):
    return xf * lax.rsqrt(jnp.mean(xf * xf, axis=-1, keepdims=True) + EPS) * gain


def _rmsnorm_kernel(x_ref, g_ref, h_ref):
    h_ref[...] = _rms(x_ref[...], g_ref[...]).astype(h_ref.dtype)


def rmsnorm_call(x2, gain):
    M, D = x2.shape
    return pl.pallas_call(
        _rmsnorm_kernel,
        out_shape=jax.ShapeDtypeStruct((M, D), BF16),
        grid=(M // ROW_TILE,),
        in_specs=[pl.BlockSpec((ROW_TILE, D), lambda i: (i, 0)),
                  pl.BlockSpec((1, D), lambda i: (0, 0))],
        out_specs=pl.BlockSpec((ROW_TILE, D), lambda i: (i, 0)),
        compiler_params=_params("parallel"),
        name="rmsnorm",
    )(x2, gain.reshape(1, D))


def _proj_kernel(h_ref, w_ref, o_ref, *, n_chunk):
    h = h_ref[...]
    for c in range(o_ref.shape[1] // n_chunk):
        sl = slice(c * n_chunk, (c + 1) * n_chunk)
        o_ref[:, sl] = jnp.dot(h, w_ref[:, sl], preferred_element_type=F32).astype(o_ref.dtype)


def proj_call(h, w, layer):
    M, D = h.shape
    N = w.shape[2]
    return pl.pallas_call(
        functools.partial(_proj_kernel, n_chunk=512),
        out_shape=jax.ShapeDtypeStruct((M, N), BF16),
        grid=(M // ROW_TILE,),
        in_specs=[pl.BlockSpec((ROW_TILE, D), lambda i: (i, 0)),
                  _layer_spec(w, layer)],
        out_specs=pl.BlockSpec((ROW_TILE, N), lambda i: (i, 0)),
        compiler_params=_params("parallel"),
        name="qkv_proj",
    )(h, w)


def _ffn_row_starts(n_groups):
    per = n_groups // SUBLANES
    return [SUBLANES * SUBLANES * (j % per) + j // per for j in range(n_groups)]


def _stage(ref, x):
    for c in range(ref.shape[0]):
        ref[c] = x[:, c * LANES:(c + 1) * LANES]


def _stage_interleaved(ref, x):
    for c in range(ref.shape[0]):
        for j, st in enumerate(_ffn_row_starts(x.shape[0] // SUBLANES)):
            ref[c, pl.ds(st, SUBLANES, stride=SUBLANES), :] = x[j * SUBLANES:(j + 1) * SUBLANES,
                                                                c * LANES:(c + 1) * LANES]


def _deinterleave_rows(ref, c):
    return jnp.concatenate([ref[c, pl.ds(st, SUBLANES, stride=SUBLANES), :]
                            for st in _ffn_row_starts(ref.shape[1] // SUBLANES)], axis=0)


def _wo_kernel(m_ref, w_ref, x_ref, g_ref, xo_ref, h_ref, stage_ref):
    xn = x_ref[...] + jnp.dot(m_ref[...], w_ref[...], preferred_element_type=F32)
    xo_ref[...] = xn
    _stage_interleaved(stage_ref, _rms(xn, g_ref[...]))
    for c in range(stage_ref.shape[0]):
        h_ref[:, c * LANES:(c + 1) * LANES] = stage_ref[c].astype(h_ref.dtype)


def wo_call(m, w, x2, gain, layer):
    M, D = x2.shape
    row = lambda i: (i, 0)
    return pl.pallas_call(
        _wo_kernel,
        out_shape=(jax.ShapeDtypeStruct((M, D), F32), jax.ShapeDtypeStruct((M, D), BF16)),
        grid=(M // ROW_TILE,),
        in_specs=[pl.BlockSpec((ROW_TILE, D), row), _layer_spec(w, layer),
                  pl.BlockSpec((ROW_TILE, D), row), _layer_spec(gain, layer)],
        out_specs=(pl.BlockSpec((ROW_TILE, D), row), pl.BlockSpec((ROW_TILE, D), row)),
        scratch_shapes=[pltpu.VMEM((D // LANES, ROW_TILE, LANES), F32)],
        compiler_params=_params("parallel"),
        name="wo_residual_norm",
    )(m, w, x2, gain)


def _causal_conv3(u, cw, cb, prev):
    nblk, g, _, tf = u.shape
    first = lax.broadcasted_iota(jnp.int32, (SUBLANES, tf), 0) == 0
    u1, u2 = [], []
    for b in range(nblk):
        tail = prev if b == 0 else u[b - 1, g - 2:]
        wrap = [jnp.where(first, pltpu.roll(tail[e], 1, 0), pltpu.roll(u[b, g - 2 + e], 1, 0))
                for e in range(2)]
        u1.append(jnp.concatenate([wrap[1][None], u[b, :g - 1]], axis=0))
        u2.append(jnp.concatenate([wrap[0][None], wrap[1][None], u[b, :g - 2]], axis=0))
    u1 = jnp.stack(u1)
    u2 = jnp.stack(u2)
    return u * cw[2:3] + u1 * cw[1:2] + u2 * cw[0:1] + cb


def _ffn_up_kernel(h_ref, wg_ref, wv_ref, cwg_ref, cwv_ref, cbg_ref, cbv_ref, g_ref, carry_ref,
                   *, tiles_per_seq):
    tm = h_ref.shape[0]
    nblk = tm // ROW_TILE
    g = ROW_TILE // SUBLANES

    @pl.when(pl.program_id(0) % tiles_per_seq == 0)
    def _():
        carry_ref[...] = jnp.zeros(carry_ref.shape, F32)

    h = h_ref[...]
    for c in range(g_ref.shape[1] // FFN_TF):
        sl = slice(c * FFN_TF, (c + 1) * FFN_TF)
        ug = jnp.dot(h, wg_ref[:, sl], preferred_element_type=F32).reshape(nblk, g, SUBLANES, FFN_TF)
        uv = jnp.dot(h, wv_ref[:, sl], preferred_element_type=F32).reshape(nblk, g, SUBLANES, FFN_TF)
        prev_g = carry_ref[0, :, :, sl]
        prev_v = carry_ref[1, :, :, sl]
        carry_ref[0, :, :, sl] = ug[nblk - 1, g - 2:]
        carry_ref[1, :, :, sl] = uv[nblk - 1, g - 2:]
        yg = _causal_conv3(ug, cwg_ref[:, sl], cbg_ref[:, sl], prev_g)
        yv = _causal_conv3(uv, cwv_ref[:, sl], cbv_ref[:, sl], prev_v)
        act = yg * (1.0 / (1.0 + jnp.exp(-yg))) * yv
        g_ref[:, sl] = act.reshape(tm, FFN_TF).astype(g_ref.dtype)


def ffn_up_call(h, wg, wv, cwg, cwv, cbg, cbv, layer, seq):
    M, D = h.shape
    F = wg.shape[2]
    row = lambda i: (i, 0)
    resident = lambda arr: _layer_spec(arr, layer, single_buffer=True)
    return pl.pallas_call(
        functools.partial(_ffn_up_kernel, tiles_per_seq=seq // FFN_TM),
        out_shape=jax.ShapeDtypeStruct((M, F), BF16),
        grid=(M // FFN_TM,),
        in_specs=[pl.BlockSpec((FFN_TM, D), row), resident(wg), resident(wv),
                  resident(cwg), resident(cwv), resident(cbg), resident(cbv)],
        out_specs=pl.BlockSpec((FFN_TM, F), row),
        scratch_shapes=[pltpu.VMEM((2, 2, SUBLANES, F), F32)],
        compiler_params=_params("arbitrary"),
        name="ffn_up_conv_gate",
    )(h, wg, wv, cwg, cwv, cbg, cbv)


def _ffn_down_kernel(a_ref, w_ref, x_ref, g_ref, xo_ref, h_ref, stage_ref):
    _stage(stage_ref, jnp.dot(a_ref[...], w_ref[...], preferred_element_type=F32))
    for c in range(stage_ref.shape[0]):
        sl = slice(c * LANES, (c + 1) * LANES)
        xo_ref[:, sl] = x_ref[:, sl] + _deinterleave_rows(stage_ref, c)
    h_ref[...] = _rms(xo_ref[...], g_ref[...]).astype(h_ref.dtype)


def ffn_down_call(a, w, layer, x2, gain, gain_layer, out_dtype):
    M, D = x2.shape
    F = a.shape[1]
    row = lambda i: (i, 0)
    return pl.pallas_call(
        _ffn_down_kernel,
        out_shape=(jax.ShapeDtypeStruct((M, D), F32), jax.ShapeDtypeStruct((M, D), out_dtype)),
        grid=(M // ROW_TILE,),
        in_specs=[pl.BlockSpec((ROW_TILE, F), row),
                  _layer_spec(w, layer, single_buffer=True),
                  pl.BlockSpec((ROW_TILE, D), row), _layer_spec(gain, gain_layer)],
        out_specs=(pl.BlockSpec((ROW_TILE, D), row), pl.BlockSpec((ROW_TILE, D), row)),
        scratch_shapes=[pltpu.VMEM((D // LANES, ROW_TILE, LANES), F32)],
        compiler_params=_params("parallel"),
        name="ffn_down_residual_norm",
    )(a, w, x2, gain)


def _half_masks(q):
    lane = lax.broadcasted_iota(jnp.int32, q.shape, 1)
    zero = jnp.zeros_like(q)
    scale = jnp.asarray(DA_QK_DIM ** -0.5, q.dtype)
    return (jnp.where(lane < 64, q, zero) * scale, jnp.where(lane >= 64, q, zero) * scale)


def _qk(q, k):
    return lax.dot_general(q, k, (((1,), (1,)), ((), ())), preferred_element_type=F32)


def _da_kernel(q_ref, k_ref, v_ref, bias_ref, lam_ref, sub_ref, o_ref, s_ref, *, lambda_init):
    T = ATT_T
    nq = q_ref.shape[0] // T
    r = lax.broadcasted_iota(jnp.int32, (T, T), 0)
    cc = lax.broadcasted_iota(jnp.int32, (T, T), 1)
    causal = r >= cc
    lam = lam_ref[...]
    lam_full = (jnp.exp(jnp.sum(lam[0:1] * lam[1:2], keepdims=True))
                - jnp.exp(jnp.sum(lam[2:3] * lam[3:4], keepdims=True)) + lambda_init)
    def score_steps(qi, st):
        def first():
            st["q01"] = jnp.concatenate(_half_masks(q_ref[qi * T:(qi + 1) * T, :]), axis=0)
            st["mx"] = [None, None]

        def tile(j):
            if j == 0:
                first()
            ss = _qk(st["q01"], k_ref[j * T:(j + 1) * T, :])
            for c in range(2):
                s = ss[c * T:(c + 1) * T]
                if j == qi - 1:
                    s = s + bias_ref[c, :, 0:T]
                if j == qi:
                    s = jnp.where(causal, s + bias_ref[c, :, T:2 * T], NEG)
                s_ref[qi % 2, c, :, j * T:(j + 1) * T] = s
                t = jnp.maximum(s[:, :LANES], s[:, LANES:])
                st["mx"][c] = t if st["mx"][c] is None else jnp.maximum(st["mx"][c], t)
        return [functools.partial(tile, j) for j in range(qi + 1)]

    def value_steps(qi, st):
        def first():
            st["m2"] = []
            for c in range(2):
                m = jnp.broadcast_to(jnp.max(st["mx"][c], axis=-1, keepdims=True), (T, LANES))
                st["m2"].append(jnp.concatenate([m, m], axis=1))
            st["lsum"] = [None, None]
            st["acc"] = None

        def tile(j):
            if j == 0:
                first()
            ps = []
            for c in range(2):
                p = jnp.exp(s_ref[qi % 2, c, :, j * T:(j + 1) * T] - st["m2"][c])
                t = p[:, :LANES] + p[:, LANES:]
                st["lsum"][c] = t if st["lsum"][c] is None else st["lsum"][c] + t
                ps.append(p.astype(BF16))
            pv = jnp.dot(jnp.concatenate(ps, axis=0), v_ref[j * T:(j + 1) * T, :],
                         preferred_element_type=F32)
            st["acc"] = pv if st["acc"] is None else st["acc"] + pv
            if j == qi:
                outs = [st["acc"][c * T:(c + 1) * T] / jnp.sum(st["lsum"][c], axis=-1, keepdims=True)
                        for c in range(2)]
                o = outs[0] - lam_full * outs[1]
                o = _rms(o, sub_ref[...]) * (1.0 - lambda_init)
                o_ref[qi * T:(qi + 1) * T, :] = o.astype(o_ref.dtype)
        return [functools.partial(tile, j) for j in range(qi + 1)]

    states = [dict() for _ in range(nq)]
    pending = []
    for qi in range(nq + 1):
        scores = score_steps(qi, states[qi]) if qi < nq else []
        for t in range(max(len(scores), len(pending))):
            if t < len(scores):
                scores[t]()
            if t < len(pending):
                pending[t]()
        pending = value_steps(qi, states[qi]) if qi < nq else []


def da_call(qkv, bias_near, lam, subln, batch, seq, lambda_init):
    T = ATT_T
    nh = DA_HEADS
    return pl.pallas_call(
        functools.partial(_da_kernel, lambda_init=lambda_init),
        out_shape=jax.ShapeDtypeStruct((batch * seq, nh * DA_V_DIM), BF16),
        grid=(batch, nh),
        in_specs=[pl.BlockSpec((seq, LANES), lambda b, h: (b, h)),
                  pl.BlockSpec((seq, LANES), lambda b, h: (b, nh + h)),
                  pl.BlockSpec((seq, LANES), lambda b, h: (b, 2 * nh + h)),
                  pl.BlockSpec((2, T, 2 * T), lambda b, h: (h, 0, 0)),
                  pl.BlockSpec((4, DA_QK_DIM), lambda b, h: (0, 0)),
                  pl.BlockSpec((1, DA_V_DIM), lambda b, h: (0, 0))],
        out_specs=pl.BlockSpec((seq, LANES), lambda b, h: (b, h)),
        scratch_shapes=[pltpu.VMEM((2, 2, T, seq), F32)],
        compiler_params=_params("parallel", "parallel"),
        name="diff_attention",
    )(qkv, qkv, qkv, bias_near, lam, subln.reshape(1, DA_V_DIM))


def _split2(x):
    hi = x.astype(BF16)
    mid = (x - hi.astype(F32)).astype(BF16)
    return hi, mid


def _sb_kernel(q_ref, k_ref, v_ref, o_ref, lbn_ref, hmn_ref, lbf_ref, hmf_ref, acc_ref):
    T = ATT_T
    nq = q_ref.shape[0] // T
    kr = lax.broadcasted_iota(jnp.int32, (T, T), 0)
    kc = lax.broadcasted_iota(jnp.int32, (T, T), 1)
    suffix = jnp.where(kr > kc, 1.0, 0.0).astype(BF16)
    suffix2 = jnp.concatenate([suffix, suffix], axis=0)
    strict = kc < kr
    lane = lax.broadcasted_iota(jnp.int32, (T, LANES), 1)
    strict2 = jnp.concatenate([strict, strict], axis=0)

    def score_tile(qi, q01, j, blk, carry, lb_view, hm_view):
        zz = _qk(q01, k_ref[j * T:(j + 1) * T, :])
        for c in range(2):
            z = zz[c * T:(c + 1) * T]
            lb = jnp.minimum(z, 0.0) - jnp.log(1.0 + jnp.exp(-jnp.abs(z)))
            log_1m_beta = lb - z
            if j == qi:
                log_1m_beta = jnp.where(strict, log_1m_beta, 0.0)
            hi, mid = _split2(log_1m_beta)
            rows = slice(blk + c * T, blk + (c + 1) * T)
            hm_view[rows, 0:T] = hi
            hm_view[rows, T:2 * T] = mid
            if carry[c] is not None:
                lb = lb + carry[c]
            lb_view[rows, :] = lb
            if j > 0:
                rs = jnp.sum(log_1m_beta, axis=-1, keepdims=True)
                carry[c] = rs if carry[c] is None else carry[c] + rs

    def value_tiles(qi, tiles, lb_view, hm_view):
        between = jnp.dot(hm_view[0:2 * len(tiles) * T, :], suffix2, preferred_element_type=F32)
        acc = None
        for pos, j in enumerate(tiles):
            rows = slice(2 * pos * T, 2 * (pos + 1) * T)
            a = jnp.exp(lb_view[rows, :] + between[rows])
            if j == qi:
                a = jnp.where(strict2, a, 0.0)
            pv = jnp.dot(a.astype(BF16), v_ref[j * T:(j + 1) * T, :], preferred_element_type=F32)
            acc = pv if acc is None else acc + pv
        return acc

    def stacked_q(qi):
        return jnp.concatenate(_half_masks(q_ref[qi * T:(qi + 1) * T, :]), axis=0)

    near = [[j for j in (qi, qi - 1) if j >= 0] for qi in range(nq)]
    carries = [[None, None] for _ in range(nq)]
    for qi in range(nq + 1):
        if qi < nq:
            q01 = stacked_q(qi)
            for pos, j in enumerate(near[qi]):
                score_tile(qi, q01, j, 2 * pos * T, carries[qi], lbn_ref.at[qi % 2], hmn_ref.at[qi % 2])
        if qi > 0:
            acc_ref[qi - 1] = value_tiles(qi - 1, near[qi - 1], lbn_ref.at[(qi - 1) % 2], hmn_ref.at[(qi - 1) % 2])

    for qi in range(2, nq):
        far = list(range(qi - 2, -1, -1))
        carry = carries[qi]
        nearest_sum = jnp.max(jnp.maximum(carry[0], carry[1]))

        @pl.when(nearest_sum > SB_EXP_ZERO)
        def _():
            q01 = stacked_q(qi)
            far_carry = list(carry)
            for pos, j in enumerate(far):
                score_tile(qi, q01, j, 2 * pos * T, far_carry, lbf_ref, hmf_ref)
            acc_ref[qi] += value_tiles(qi, far, lbf_ref, hmf_ref)

    for qi in range(nq):
        acc = acc_ref[qi]
        o_ref[qi * T:(qi + 1) * T, :] = jnp.where(lane < 64, acc[0:T], acc[T:2 * T]).astype(o_ref.dtype)


def sb_call(qkv, batch, seq):
    npair = SB_HEADS // 2
    return pl.pallas_call(
        _sb_kernel,
        out_shape=jax.ShapeDtypeStruct((batch * seq, SB_HEADS * SB_DIM), BF16),
        grid=(batch, npair),
        in_specs=[pl.BlockSpec((seq, LANES), lambda b, p: (b, p)),
                  pl.BlockSpec((seq, LANES), lambda b, p: (b, npair + p)),
                  pl.BlockSpec((seq, LANES), lambda b, p: (b, 2 * npair + p))],
        out_specs=pl.BlockSpec((seq, LANES), lambda b, p: (b, p)),
        scratch_shapes=[pltpu.VMEM((2, 4 * ATT_T, ATT_T), F32), pltpu.VMEM((2, 4 * ATT_T, 2 * ATT_T), BF16),
                        pltpu.VMEM((2 * seq - 4 * ATT_T, ATT_T), F32),
                        pltpu.VMEM((2 * seq - 4 * ATT_T, 2 * ATT_T), BF16),
                        pltpu.VMEM((seq // ATT_T, 2 * ATT_T, LANES), F32)],
        compiler_params=_params("parallel", "parallel"),
        name="stick_breaking_attention",
    )(qkv, qkv, qkv)


def _sw_kernel(sink_ref, q_ref, k_ref, v_ref, bias_ref, o_ref, s_ref):
    W = SW_BLOCK
    nb = q_ref.shape[0] // W
    p_id = pl.program_id(1)
    r = lax.broadcasted_iota(jnp.int32, (2 * W, 2 * W), 0) & (W - 1)
    cidx = lax.broadcasted_iota(jnp.int32, (2 * W, 2 * W), 1)
    valid = ((cidx < W) & (cidx > r)) | ((cidx >= W) & (cidx - W <= r))
    biasm = jnp.where(valid, bias_ref[...], NEG)
    row = lax.broadcasted_iota(jnp.int32, (2 * W, 1), 0)
    sink = jnp.where(row < W, sink_ref[2 * p_id], sink_ref[2 * p_id + 1])
    lane = lax.broadcasted_iota(jnp.int32, (W, LANES), 1)
    for n in range(nb):
        q01 = jnp.concatenate(_half_masks(q_ref[n * W:(n + 1) * W, :]), axis=0)
        if n == 0:
            s_ref[0, :, W:] = _qk(q01, k_ref[0:W, :]) + biasm[:, W:]
        else:
            s_ref[n] = _qk(q01, k_ref[(n - 1) * W:(n + 1) * W, :]) + biasm
    for n in range(nb):
        s = s_ref[0, :, W:] if n == 0 else s_ref[n]
        m = jnp.maximum(jnp.max(s, axis=-1, keepdims=True), sink)
        e = jnp.exp(s - m)
        denom = jnp.sum(e, axis=-1, keepdims=True) + jnp.exp(sink - m)
        v = v_ref[0:W, :] if n == 0 else v_ref[(n - 1) * W:(n + 1) * W, :]
        o = jnp.dot(e.astype(BF16), v, preferred_element_type=F32) / denom
        o_ref[n * W:(n + 1) * W, :] = jnp.where(lane < 64, o[0:W], o[W:]).astype(o_ref.dtype)


def sw_call(qkv, bias_band, sinks, batch, seq):
    npair = SW_Q_HEADS // 2
    q_blocks = SW_Q_HEADS * SW_DIM // LANES
    W = SW_BLOCK
    return pl.pallas_call(
        _sw_kernel,
        out_shape=jax.ShapeDtypeStruct((batch * seq, SW_Q_HEADS * SW_DIM), BF16),
        grid=(batch, npair),
        in_specs=[pl.BlockSpec(memory_space=pltpu.SMEM),
                  pl.BlockSpec((seq, LANES), lambda b, p: (b, p)),
                  pl.BlockSpec((seq, LANES), lambda b, p: (b, q_blocks + p // 2)),
                  pl.BlockSpec((seq, LANES), lambda b, p: (b, q_blocks + SW_KV_HEADS + p // 2)),
                  pl.BlockSpec((None, 2 * W, 2 * W), lambda b, p: (p, 0, 0))],
        out_specs=pl.BlockSpec((seq, LANES), lambda b, p: (b, p)),
        scratch_shapes=[pltpu.VMEM((seq // W, 2 * W, 2 * W), F32)],
        compiler_params=_params("parallel", "parallel"),
        name="sliding_window_attention",
    )(sinks, qkv, qkv, qkv, bias_band.reshape(npair, 2 * W, 2 * W))


def _t5_bucket(dist):
    max_exact = N_BUCKETS // 2
    d = jnp.maximum(dist, 0)
    large = max_exact + (jnp.log(jnp.maximum(d, 1).astype(F32) / max_exact)
                         / math.log(MAX_DISTANCE / max_exact) * (N_BUCKETS - max_exact)).astype(jnp.int32)
    large = jnp.minimum(large, N_BUCKETS - 1)
    return jnp.where(d < max_exact, d, large)


def _bias_band(rel_bias, T):
    y = np.arange(3 * T)
    dist = np.clip(np.where(y <= 2 * T, T - y, 4 * T - y), 0, 2 * T - 1)
    row = rel_bias.astype(F32)[_t5_bucket(jnp.asarray(dist, jnp.int32))].T
    flat = jnp.tile(row, (1, T))[:, :T * (3 * T - 1)]
    return flat.reshape(-1, T, 3 * T - 1)[:, :, :2 * T]


def _lambda_init(layer):
    return 0.8 - 0.6 * math.exp(-0.3 * layer)


def _dup_kv_heads(w):
    q_w = SW_Q_HEADS * SW_DIM
    kv = w[:, q_w:].reshape(w.shape[0], 2 * SW_KV_HEADS, 1, SW_DIM)
    kv = jnp.broadcast_to(kv, (w.shape[0], 2 * SW_KV_HEADS, 2, SW_DIM)).reshape(w.shape[0], -1)
    return jnp.concatenate([w[:, :q_w], kv], axis=1)


def _pad_ff(a, axis):
    pad = [(0, 0)] * a.ndim
    pad[axis] = (0, D_FF_PAD - D_FF)
    return jnp.pad(a, pad)


def kernel(x, rel_bias, attn_norm, ffn_norm, w_o, da_w_qkv, da_lambda, da_subln, sb_w_qkv, sw_w_qkv,
           sw_sinks, ffn_w_up, ffn_conv_w, ffn_conv_b, ffn_w_down, final_norm):
    B, S, D = x.shape
    x2 = x.reshape(B * S, D)
    da_bias_near = _bias_band(rel_bias, ATT_T) - rel_bias.astype(F32)[N_BUCKETS - 1][:, None, None]
    sw_bias = _bias_band(rel_bias, SW_BLOCK)
    da_w = da_w_qkv.astype(BF16)
    sb_w = sb_w_qkv.astype(BF16)
    sw_w = jax.vmap(_dup_kv_heads)(sw_w_qkv).astype(BF16)
    wo_w = w_o.astype(BF16)
    wg = _pad_ff(ffn_w_up[:, :, :D_FF].astype(BF16), 2)
    wv = _pad_ff(ffn_w_up[:, :, D_FF:].astype(BF16), 2)
    cwg = _pad_ff(ffn_conv_w[:, :, :D_FF], 2)
    cwv = _pad_ff(ffn_conv_w[:, :, D_FF:], 2)
    cbg = _pad_ff(ffn_conv_b[:, None, :D_FF], 2)
    cbv = _pad_ff(ffn_conv_b[:, None, D_FF:], 2)
    wd = _pad_ff(ffn_w_down.astype(BF16), 1)
    attn_gain = attn_norm[:, None, :]
    ffn_gain = ffn_norm[:, None, :]
    final_gain = final_norm[None, None, :]

    h = rmsnorm_call(x2, attn_norm[0])
    for layer in range(DEPTH):
        mixer = layer % N_MIXERS
        slot = layer // N_MIXERS
        if mixer == 0:
            qkv = proj_call(h, da_w, slot)
            m = da_call(qkv, da_bias_near, da_lambda[slot], da_subln[slot], B, S, _lambda_init(layer))
        elif mixer == 1:
            qkv = proj_call(h, sb_w, slot)
            m = sb_call(qkv, B, S)
        else:
            qkv = proj_call(h, sw_w, slot)
            m = sw_call(qkv, sw_bias, sw_sinks[slot], B, S)
        x2, h = wo_call(m, wo_w, x2, ffn_gain, layer)
        act = ffn_up_call(h, wg, wv, cwg, cwv, cbg, cbv, layer, S)
        last = layer == DEPTH - 1
        x2, h = ffn_down_call(act, wd, layer, x2, final_gain if last else attn_gain, 0 if last else layer + 1,
                              F32 if last else BF16)
    return h.reshape(B, S, D)
```

```python
import functools
import math

import jax
import jax.numpy as jnp
import numpy as np
from jax import lax
from jax.experimental import pallas as pl
from jax.experimental.pallas import tpu as pltpu

D_MODEL = 1024
DEPTH = 4
N_MIXERS = 3
N_BUCKETS = 32
MAX_DISTANCE = 128
DA_HEADS = 8
DA_QK_DIM = 64
DA_V_DIM = 128
SB_HEADS = 16
SB_DIM = 64
SW_Q_HEADS = 16
SW_KV_HEADS = 4
SW_DIM = 64
SW_BLOCK = 128
D_FF = 2752
EPS = 1e-6
NEG = -1e30

LANES = 128
SUBLANES = 8
VMEM_LIMIT = 52 * 1024 * 1024

ATT_T = 256
SB_EXP_ZERO = -104.0
ROW_TILE = 512
FFN_TM = 1024
FFN_TF = 256
D_FF_PAD = 2816

BF16 = jnp.bfloat16
F32 = jnp.float32


def _params(*sem):
    return pltpu.CompilerParams(dimension_semantics=sem, vmem_limit_bytes=VMEM_LIMIT)


def _layer_spec(arr, layer, single_buffer=False):
    shape = (None,) + arr.shape[1:]
    index_map = lambda *_: (layer,) + (0,) * (arr.ndim - 1)
    if single_buffer:
        return pl.BlockSpec(shape, index_map, pipeline_mode=pl.Buffered(1))
    return pl.BlockSpec(shape, index_map)


def _rms(xf, gain):
    return xf * lax.rsqrt(jnp.mean(xf * xf, axis=-1, keepdims=True) + EPS) * gain


def _rmsnorm_kernel(x_ref, g_ref, h_ref):
    h_ref[...] = _rms(x_ref[...], g_ref[...]).astype(h_ref.dtype)


def rmsnorm_call(x2, gain):
    M, D = x2.shape
    return pl.pallas_call(
        _rmsnorm_kernel,
        out_shape=jax.ShapeDtypeStruct((M, D), BF16),
        grid=(M // ROW_TILE,),
        in_specs=[pl.BlockSpec((ROW_TILE, D), lambda i: (i, 0)),
                  pl.BlockSpec((1, D), lambda i: (0, 0))],
        out_specs=pl.BlockSpec((ROW_TILE, D), lambda i: (i, 0)),
        compiler_params=_params("parallel"),
        name="rmsnorm",
    )(x2, gain.reshape(1, D))


def _proj_kernel(h_ref, w_ref, o_ref, *, n_chunk):
    h = h_ref[...]
    for c in range(o_ref.shape[1] // n_chunk):
        sl = slice(c * n_chunk, (c + 1) * n_chunk)
        o_ref[:, sl] = jnp.dot(h, w_ref[:, sl], preferred_element_type=F32).astype(o_ref.dtype)


def proj_call(h, w, layer):
    M, D = h.shape
    N = w.shape[2]
    return pl.pallas_call(
        functools.partial(_proj_kernel, n_chunk=512),
        out_shape=jax.ShapeDtypeStruct((M, N), BF16),
        grid=(M // ROW_TILE,),
        in_specs=[pl.BlockSpec((ROW_TILE, D), lambda i: (i, 0)),
                  _layer_spec(w, layer)],
        out_specs=pl.BlockSpec((ROW_TILE, N), lambda i: (i, 0)),
        compiler_params=_params("parallel"),
        name="qkv_proj",
    )(h, w)


def _ffn_row_starts(n_groups):
    per = n_groups // SUBLANES
    return [SUBLANES * SUBLANES * (j % per) + j // per for j in range(n_groups)]


def _stage(ref, x):
    for c in range(ref.shape[0]):
        ref[c] = x[:, c * LANES:(c + 1) * LANES]


def _stage_interleaved(ref, x):
    for c in range(ref.shape[0]):
        for j, st in enumerate(_ffn_row_starts(x.shape[0] // SUBLANES)):
            ref[c, pl.ds(st, SUBLANES, stride=SUBLANES), :] = x[j * SUBLANES:(j + 1) * SUBLANES,
                                                                c * LANES:(c + 1) * LANES]


def _deinterleave_rows(ref, c):
    return jnp.concatenate([ref[c, pl.ds(st, SUBLANES, stride=SUBLANES), :]
                            for st in _ffn_row_starts(ref.shape[1] // SUBLANES)], axis=0)


def _wo_kernel(m_ref, w_ref, x_ref, g_ref, xo_ref, h_ref, stage_ref):
    xn = x_ref[...] + jnp.dot(m_ref[...], w_ref[...], preferred_element_type=F32)
    xo_ref[...] = xn
    _stage_interleaved(stage_ref, _rms(xn, g_ref[...]))
    for c in range(stage_ref.shape[0]):
        h_ref[:, c * LANES:(c + 1) * LANES] = stage_ref[c].astype(h_ref.dtype)


def wo_call(m, w, x2, gain, layer):
    M, D = x2.shape
    row = lambda i: (i, 0)
    return pl.pallas_call(
        _wo_kernel,
        out_shape=(jax.ShapeDtypeStruct((M, D), F32), jax.ShapeDtypeStruct((M, D), BF16)),
        grid=(M // ROW_TILE,),
        in_specs=[pl.BlockSpec((ROW_TILE, D), row), _layer_spec(w, layer),
                  pl.BlockSpec((ROW_TILE, D), row), _layer_spec(gain, layer)],
        out_specs=(pl.BlockSpec((ROW_TILE, D), row), pl.BlockSpec((ROW_TILE, D), row)),
        scratch_shapes=[pltpu.VMEM((D // LANES, ROW_TILE, LANES), F32)],
        compiler_params=_params("parallel"),
        name="wo_residual_norm",
    )(m, w, x2, gain)


def _causal_conv3(u, cw, cb, prev):
    nblk, g, _, tf = u.shape
    first = lax.broadcasted_iota(jnp.int32, (SUBLANES, tf), 0) == 0
    u1, u2 = [], []
    for b in range(nblk):
        tail = prev if b == 0 else u[b - 1, g - 2:]
        wrap = [jnp.where(first, pltpu.roll(tail[e], 1, 0), pltpu.roll(u[b, g - 2 + e], 1, 0))
                for e in range(2)]
        u1.append(jnp.concatenate([wrap[1][None], u[b, :g - 1]], axis=0))
        u2.append(jnp.concatenate([wrap[0][None], wrap[1][None], u[b, :g - 2]], axis=0))
    u1 = jnp.stack(u1)
    u2 = jnp.stack(u2)
    return u * cw[2:3] + u1 * cw[1:2] + u2 * cw[0:1] + cb


def _ffn_up_kernel(h_ref, wg_ref, wv_ref, cwg_ref, cwv_ref, cbg_ref, cbv_ref, g_ref, carry_ref,
                   *, tiles_per_seq):
    tm = h_ref.shape[0]
    nblk = tm // ROW_TILE
    g = ROW_TILE // SUBLANES

    @pl.when(pl.program_id(0) % tiles_per_seq == 0)
    def _():
        carry_ref[...] = jnp.zeros(carry_ref.shape, F32)

    h = h_ref[...]
    for c in range(g_ref.shape[1] // FFN_TF):
        sl = slice(c * FFN_TF, (c + 1) * FFN_TF)
        ug = jnp.dot(h, wg_ref[:, sl], preferred_element_type=F32).reshape(nblk, g, SUBLANES, FFN_TF)
        uv = jnp.dot(h, wv_ref[:, sl], preferred_element_type=F32).reshape(nblk, g, SUBLANES, FFN_TF)
        prev_g = carry_ref[0, :, :, sl]
        prev_v = carry_ref[1, :, :, sl]
        carry_ref[0, :, :, sl] = ug[nblk - 1, g - 2:]
        carry_ref[1, :, :, sl] = uv[nblk - 1, g - 2:]
        yg = _causal_conv3(ug, cwg_ref[:, sl], cbg_ref[:, sl], prev_g)
        yv = _causal_conv3(uv, cwv_ref[:, sl], cbv_ref[:, sl], prev_v)
        act = yg * (1.0 / (1.0 + jnp.exp(-yg))) * yv
        g_ref[:, sl] = act.reshape(tm, FFN_TF).astype(g_ref.dtype)


def ffn_up_call(h, w, cw, cb, layer, seq):
    M, D = h.shape
    F = w.shape[2]
    row = lambda i: (i, 0)
    gate = lambda arr: _layer_spec(arr, 2 * layer, single_buffer=True)
    value = lambda arr: _layer_spec(arr, 2 * layer + 1, single_buffer=True)
    return pl.pallas_call(
        functools.partial(_ffn_up_kernel, tiles_per_seq=seq // FFN_TM),
        out_shape=jax.ShapeDtypeStruct((M, F), BF16),
        grid=(M // FFN_TM,),
        in_specs=[pl.BlockSpec((FFN_TM, D), row), gate(w), value(w), gate(cw), value(cw), gate(cb), value(cb)],
        out_specs=pl.BlockSpec((FFN_TM, F), row),
        scratch_shapes=[pltpu.VMEM((2, 2, SUBLANES, F), F32)],
        compiler_params=_params("arbitrary"),
        name="ffn_up_conv_gate",
    )(h, w, w, cw, cw, cb, cb)


def _ffn_down_kernel(a_ref, w_ref, x_ref, g_ref, xo_ref, h_ref, stage_ref):
    _stage(stage_ref, jnp.dot(a_ref[...], w_ref[...], preferred_element_type=F32))
    for c in range(stage_ref.shape[0]):
        sl = slice(c * LANES, (c + 1) * LANES)
        xo_ref[:, sl] = x_ref[:, sl] + _deinterleave_rows(stage_ref, c)
    h_ref[...] = _rms(xo_ref[...], g_ref[...]).astype(h_ref.dtype)


def ffn_down_call(a, w, layer, x2, gain, gain_layer, out_dtype):
    M, D = x2.shape
    F = a.shape[1]
    row = lambda i: (i, 0)
    return pl.pallas_call(
        _ffn_down_kernel,
        out_shape=(jax.ShapeDtypeStruct((M, D), F32), jax.ShapeDtypeStruct((M, D), out_dtype)),
        grid=(M // ROW_TILE,),
        in_specs=[pl.BlockSpec((ROW_TILE, F), row),
                  _layer_spec(w, layer, single_buffer=True),
                  pl.BlockSpec((ROW_TILE, D), row), _layer_spec(gain, gain_layer)],
        out_specs=(pl.BlockSpec((ROW_TILE, D), row), pl.BlockSpec((ROW_TILE, D), row)),
        scratch_shapes=[pltpu.VMEM((D // LANES, ROW_TILE, LANES), F32)],
        compiler_params=_params("parallel"),
        name="ffn_down_residual_norm",
    )(a, w, x2, gain)


def _half_masks(q):
    lane = lax.broadcasted_iota(jnp.int32, q.shape, 1)
    zero = jnp.zeros_like(q)
    scale = jnp.asarray(DA_QK_DIM ** -0.5, q.dtype)
    return (jnp.where(lane < 64, q, zero) * scale, jnp.where(lane >= 64, q, zero) * scale)


def _qk(q, k):
    return lax.dot_general(q, k, (((1,), (1,)), ((), ())), preferred_element_type=F32)


def _da_kernel(q_ref, k_ref, v_ref, bias_ref, lam_ref, sub_ref, o_ref, s_ref, *, lambda_init):
    T = ATT_T
    nq = q_ref.shape[0] // T
    r = lax.broadcasted_iota(jnp.int32, (T, T), 0)
    cc = lax.broadcasted_iota(jnp.int32, (T, T), 1)
    causal = r >= cc
    lam = lam_ref[...]
    lam_full = (jnp.exp(jnp.sum(lam[0:1] * lam[1:2], keepdims=True))
                - jnp.exp(jnp.sum(lam[2:3] * lam[3:4], keepdims=True)) + lambda_init)
    def score_steps(qi, st):
        def first():
            st["q01"] = jnp.concatenate(_half_masks(q_ref[qi * T:(qi + 1) * T, :]), axis=0)
            st["mx"] = [None, None]

        def tile(j):
            if j == 0:
                first()
            ss = _qk(st["q01"], k_ref[j * T:(j + 1) * T, :])
            for c in range(2):
                s = ss[c * T:(c + 1) * T]
                if j == qi - 1:
                    s = s + bias_ref[c, :, 0:T]
                if j == qi:
                    s = jnp.where(causal, s + bias_ref[c, :, T:2 * T], NEG)
                s_ref[qi % 2, c, :, j * T:(j + 1) * T] = s
                t = jnp.maximum(s[:, :LANES], s[:, LANES:])
                st["mx"][c] = t if st["mx"][c] is None else jnp.maximum(st["mx"][c], t)
        return [functools.partial(tile, j) for j in range(qi + 1)]

    def value_steps(qi, st):
        def first():
            st["m2"] = []
            for c in range(2):
                m = jnp.broadcast_to(jnp.max(st["mx"][c], axis=-1, keepdims=True), (T, LANES))
                st["m2"].append(jnp.concatenate([m, m], axis=1))
            st["lsum"] = [None, None]
            st["acc"] = None

        def tile(j):
            if j == 0:
                first()
            ps = []
            for c in range(2):
                p = jnp.exp(s_ref[qi % 2, c, :, j * T:(j + 1) * T] - st["m2"][c])
                t = p[:, :LANES] + p[:, LANES:]
                st["lsum"][c] = t if st["lsum"][c] is None else st["lsum"][c] + t
                ps.append(p.astype(BF16))
            pv = jnp.dot(jnp.concatenate(ps, axis=0), v_ref[j * T:(j + 1) * T, :],
                         preferred_element_type=F32)
            st["acc"] = pv if st["acc"] is None else st["acc"] + pv
            if j == qi:
                outs = [st["acc"][c * T:(c + 1) * T] / jnp.sum(st["lsum"][c], axis=-1, keepdims=True)
                        for c in range(2)]
                o = outs[0] - lam_full * outs[1]
                o = _rms(o, sub_ref[...]) * (1.0 - lambda_init)
                o_ref[qi * T:(qi + 1) * T, :] = o.astype(o_ref.dtype)
        return [functools.partial(tile, j) for j in range(qi + 1)]

    states = [dict() for _ in range(nq)]
    pending = []
    for qi in range(nq + 1):
        scores = score_steps(qi, states[qi]) if qi < nq else []
        for t in range(max(len(scores), len(pending))):
            if t < len(scores):
                scores[t]()
            if t < len(pending):
                pending[t]()
        pending = value_steps(qi, states[qi]) if qi < nq else []


def da_call(qkv, bias_near, lam, subln, batch, seq, lambda_init):
    T = ATT_T
    nh = DA_HEADS
    return pl.pallas_call(
        functools.partial(_da_kernel, lambda_init=lambda_init),
        out_shape=jax.ShapeDtypeStruct((batch * seq, nh * DA_V_DIM), BF16),
        grid=(batch, nh),
        in_specs=[pl.BlockSpec((seq, LANES), lambda b, h: (b, h)),
                  pl.BlockSpec((seq, LANES), lambda b, h: (b, nh + h)),
                  pl.BlockSpec((seq, LANES), lambda b, h: (b, 2 * nh + h)),
                  pl.BlockSpec((2, T, 2 * T), lambda b, h: (h, 0, 0)),
                  pl.BlockSpec((4, DA_QK_DIM), lambda b, h: (0, 0)),
                  pl.BlockSpec((1, DA_V_DIM), lambda b, h: (0, 0))],
        out_specs=pl.BlockSpec((seq, LANES), lambda b, h: (b, h)),
        scratch_shapes=[pltpu.VMEM((2, 2, T, seq), F32)],
        compiler_params=_params("parallel", "parallel"),
        name="diff_attention",
    )(qkv, qkv, qkv, bias_near, lam, subln.reshape(1, DA_V_DIM))


def _split2(x):
    hi = x.astype(BF16)
    mid = (x - hi.astype(F32)).astype(BF16)
    return hi, mid


def _sb_kernel(q_ref, k_ref, v_ref, o_ref, lbn_ref, hmn_ref, lbf_ref, hmf_ref, acc_ref):
    T = ATT_T
    nq = q_ref.shape[0] // T
    kr = lax.broadcasted_iota(jnp.int32, (T, T), 0)
    kc = lax.broadcasted_iota(jnp.int32, (T, T), 1)
    suffix = jnp.where(kr > kc, 1.0, 0.0).astype(BF16)
    suffix2 = jnp.concatenate([suffix, suffix], axis=0)
    strict = kc < kr
    lane = lax.broadcasted_iota(jnp.int32, (T, LANES), 1)
    strict2 = jnp.concatenate([strict, strict], axis=0)

    def score_tile(qi, q01, j, blk, carry, lb_view, hm_view):
        zz = _qk(q01, k_ref[j * T:(j + 1) * T, :])
        for c in range(2):
            z = zz[c * T:(c + 1) * T]
            lb = jnp.minimum(z, 0.0) - jnp.log(1.0 + jnp.exp(-jnp.abs(z)))
            log_1m_beta = lb - z
            if j == qi:
                log_1m_beta = jnp.where(strict, log_1m_beta, 0.0)
            hi, mid = _split2(log_1m_beta)
            rows = slice(blk + c * T, blk + (c + 1) * T)
            hm_view[rows, 0:T] = hi
            hm_view[rows, T:2 * T] = mid
            if carry[c] is not None:
                lb = lb + carry[c]
            lb_view[rows, :] = lb
            if j > 0:
                rs = jnp.sum(log_1m_beta, axis=-1, keepdims=True)
                carry[c] = rs if carry[c] is None else carry[c] + rs

    def value_tiles(qi, tiles, lb_view, hm_view):
        between = jnp.dot(hm_view[0:2 * len(tiles) * T, :], suffix2, preferred_element_type=F32)
        acc = None
        for pos, j in enumerate(tiles):
            rows = slice(2 * pos * T, 2 * (pos + 1) * T)
            a = jnp.exp(lb_view[rows, :] + between[rows])
            if j == qi:
                a = jnp.where(strict2, a, 0.0)
            pv = jnp.dot(a.astype(BF16), v_ref[j * T:(j + 1) * T, :], preferred_element_type=F32)
            acc = pv if acc is None else acc + pv
        return acc

    def stacked_q(qi):
        return jnp.concatenate(_half_masks(q_ref[qi * T:(qi + 1) * T, :]), axis=0)

    near = [[j for j in (qi, qi - 1) if j >= 0] for qi in range(nq)]
    carries = [[None, None] for _ in range(nq)]
    for qi in range(nq + 1):
        if qi < nq:
            q01 = stacked_q(qi)
            for pos, j in enumerate(near[qi]):
                score_tile(qi, q01, j, 2 * pos * T, carries[qi], lbn_ref.at[qi % 2], hmn_ref.at[qi % 2])
        if qi > 0:
            acc_ref[qi - 1] = value_tiles(qi - 1, near[qi - 1], lbn_ref.at[(qi - 1) % 2], hmn_ref.at[(qi - 1) % 2])

    for qi in range(2, nq):
        far = list(range(qi - 2, -1, -1))
        carry = carries[qi]
        nearest_sum = jnp.max(jnp.maximum(carry[0], carry[1]))

        @pl.when(nearest_sum > SB_EXP_ZERO)
        def _():
            q01 = stacked_q(qi)
            far_carry = list(carry)
            for pos, j in enumerate(far):
                score_tile(qi, q01, j, 2 * pos * T, far_carry, lbf_ref, hmf_ref)
            acc_ref[qi] += value_tiles(qi, far, lbf_ref, hmf_ref)

    for qi in range(nq):
        acc = acc_ref[qi]
        o_ref[qi * T:(qi + 1) * T, :] = jnp.where(lane < 64, acc[0:T], acc[T:2 * T]).astype(o_ref.dtype)


def sb_call(qkv, batch, seq):
    npair = SB_HEADS // 2
    return pl.pallas_call(
        _sb_kernel,
        out_shape=jax.ShapeDtypeStruct((batch * seq, SB_HEADS * SB_DIM), BF16),
        grid=(batch, npair),
        in_specs=[pl.BlockSpec((seq, LANES), lambda b, p: (b, p)),
                  pl.BlockSpec((seq, LANES), lambda b, p: (b, npair + p)),
                  pl.BlockSpec((seq, LANES), lambda b, p: (b, 2 * npair + p))],
        out_specs=pl.BlockSpec((seq, LANES), lambda b, p: (b, p)),
        scratch_shapes=[pltpu.VMEM((2, 4 * ATT_T, ATT_T), F32), pltpu.VMEM((2, 4 * ATT_T, 2 * ATT_T), BF16),
                        pltpu.VMEM((2 * seq - 4 * ATT_T, ATT_T), F32),
                        pltpu.VMEM((2 * seq - 4 * ATT_T, 2 * ATT_T), BF16),
                        pltpu.VMEM((seq // ATT_T, 2 * ATT_T, LANES), F32)],
        compiler_params=_params("parallel", "parallel"),
        name="stick_breaking_attention",
    )(qkv, qkv, qkv)


def _sw_kernel(sink_ref, q_ref, k_ref, v_ref, bias_ref, o_ref, s_ref):
    W = SW_BLOCK
    nb = q_ref.shape[0] // W
    p_id = pl.program_id(1)
    r = lax.broadcasted_iota(jnp.int32, (2 * W, 2 * W), 0) & (W - 1)
    cidx = lax.broadcasted_iota(jnp.int32, (2 * W, 2 * W), 1)
    valid = ((cidx < W) & (cidx > r)) | ((cidx >= W) & (cidx - W <= r))
    biasm = jnp.where(valid, bias_ref[...], NEG)
    row = lax.broadcasted_iota(jnp.int32, (2 * W, 1), 0)
    sink = jnp.where(row < W, sink_ref[2 * p_id], sink_ref[2 * p_id + 1])
    lane = lax.broadcasted_iota(jnp.int32, (W, LANES), 1)
    for n in range(nb):
        q01 = jnp.concatenate(_half_masks(q_ref[n * W:(n + 1) * W, :]), axis=0)
        if n == 0:
            s_ref[0, :, W:] = _qk(q01, k_ref[0:W, :]) + biasm[:, W:]
        else:
            s_ref[n] = _qk(q01, k_ref[(n - 1) * W:(n + 1) * W, :]) + biasm
    for n in range(nb):
        s = s_ref[0, :, W:] if n == 0 else s_ref[n]
        m = jnp.maximum(jnp.max(s, axis=-1, keepdims=True), sink)
        e = jnp.exp(s - m)
        denom = jnp.sum(e, axis=-1, keepdims=True) + jnp.exp(sink - m)
        v = v_ref[0:W, :] if n == 0 else v_ref[(n - 1) * W:(n + 1) * W, :]
        o = jnp.dot(e.astype(BF16), v, preferred_element_type=F32) / denom
        o_ref[n * W:(n + 1) * W, :] = jnp.where(lane < 64, o[0:W], o[W:]).astype(o_ref.dtype)


def sw_call(qkv, bias_band, sinks, batch, seq):
    npair = SW_Q_HEADS // 2
    q_blocks = SW_Q_HEADS * SW_DIM // LANES
    W = SW_BLOCK
    return pl.pallas_call(
        _sw_kernel,
        out_shape=jax.ShapeDtypeStruct((batch * seq, SW_Q_HEADS * SW_DIM), BF16),
        grid=(batch, npair),
        in_specs=[pl.BlockSpec(memory_space=pltpu.SMEM),
                  pl.BlockSpec((seq, LANES), lambda b, p: (b, p)),
                  pl.BlockSpec((seq, LANES), lambda b, p: (b, q_blocks + p // 2)),
                  pl.BlockSpec((seq, LANES), lambda b, p: (b, q_blocks + SW_KV_HEADS + p // 2)),
                  pl.BlockSpec((None, 2 * W, 2 * W), lambda b, p: (p, 0, 0))],
        out_specs=pl.BlockSpec((seq, LANES), lambda b, p: (b, p)),
        scratch_shapes=[pltpu.VMEM((seq // W, 2 * W, 2 * W), F32)],
        compiler_params=_params("parallel", "parallel"),
        name="sliding_window_attention",
    )(sinks, qkv, qkv, qkv, bias_band.reshape(npair, 2 * W, 2 * W))


def _t5_bucket(dist):
    max_exact = N_BUCKETS // 2
    d = jnp.maximum(dist, 0)
    large = max_exact + (jnp.log(jnp.maximum(d, 1).astype(F32) / max_exact)
                         / math.log(MAX_DISTANCE / max_exact) * (N_BUCKETS - max_exact)).astype(jnp.int32)
    large = jnp.minimum(large, N_BUCKETS - 1)
    return jnp.where(d < max_exact, d, large)


def _band_kernel(row_ref, o_ref):
    rows = jnp.broadcast_to(row_ref[...], (o_ref.shape[0], row_ref.shape[1]))
    o_ref[...] = pltpu.roll(rows, 0, 1, stride=1, stride_axis=0)[:, :o_ref.shape[1]]


def _bias_band(table, T):
    y = np.arange(3 * T)
    dist = np.clip(np.where(y <= 2 * T, T - y, 4 * T - y), 0, 2 * T - 1)
    row = table.astype(F32)[_t5_bucket(jnp.asarray(dist, jnp.int32))].T
    ch = row.shape[0]
    return pl.pallas_call(
        _band_kernel,
        out_shape=jax.ShapeDtypeStruct((ch, T, 2 * T), F32),
        grid=(ch,),
        in_specs=[pl.BlockSpec((None, 1, 3 * T), lambda c: (c, 0, 0))],
        out_specs=pl.BlockSpec((None, T, 2 * T), lambda c: (c, 0, 0)),
        compiler_params=_params("parallel"),
        name="bias_band",
    )(row[:, None, :])


def _lambda_init(layer):
    return 0.8 - 0.6 * math.exp(-0.3 * layer)


def _dup_kv_heads(w):
    q_w = SW_Q_HEADS * SW_DIM
    kv = w[:, q_w:].reshape(w.shape[0], 2 * SW_KV_HEADS, 1, SW_DIM)
    kv = jnp.broadcast_to(kv, (w.shape[0], 2 * SW_KV_HEADS, 2, SW_DIM)).reshape(w.shape[0], -1)
    return jnp.concatenate([w[:, :q_w], kv], axis=1)


def _split_gate_value(a):
    L, r, _ = a.shape
    halves = jnp.moveaxis(a.reshape(L, r, 2, D_FF), 2, 1)
    return _pad_ff(halves, 3).reshape(2 * L, r, D_FF_PAD)


def _pad_ff(a, axis):
    pad = [(0, 0)] * a.ndim
    pad[axis] = (0, D_FF_PAD - D_FF)
    return jnp.pad(a, pad)


def kernel(x, rel_bias, attn_norm, ffn_norm, w_o, da_w_qkv, da_lambda, da_subln, sb_w_qkv, sw_w_qkv,
           sw_sinks, ffn_w_up, ffn_conv_w, ffn_conv_b, ffn_w_down, final_norm):
    B, S, D = x.shape
    x2 = x.reshape(B * S, D)
    da_bias_near = _bias_band(rel_bias - rel_bias[N_BUCKETS - 1], ATT_T)
    sw_bias = _bias_band(rel_bias, SW_BLOCK)
    da_w = da_w_qkv.astype(BF16)
    sb_w = sb_w_qkv.astype(BF16)
    sw_w = jax.vmap(_dup_kv_heads)(sw_w_qkv).astype(BF16)
    wo_w = w_o.astype(BF16)
    w_up = _split_gate_value(ffn_w_up.astype(BF16))
    conv_w = _split_gate_value(ffn_conv_w)
    conv_b = _split_gate_value(ffn_conv_b[:, None, :])
    wd = _pad_ff(ffn_w_down.astype(BF16), 1)
    attn_gain = attn_norm[:, None, :]
    ffn_gain = ffn_norm[:, None, :]
    final_gain = final_norm[None, None, :]

    h = rmsnorm_call(x2, attn_norm[0])
    for layer in range(DEPTH):
        mixer = layer % N_MIXERS
        slot = layer // N_MIXERS
        if mixer == 0:
            qkv = proj_call(h, da_w, slot)
            m = da_call(qkv, da_bias_near, da_lambda[slot], da_subln[slot], B, S, _lambda_init(layer))
        elif mixer == 1:
            qkv = proj_call(h, sb_w, slot)
            m = sb_call(qkv, B, S)
        else:
            qkv = proj_call(h, sw_w, slot)
            m = sw_call(qkv, sw_bias, sw_sinks[slot], B, S)
        x2, h = wo_call(m, wo_w, x2, ffn_gain, layer)
        act = ffn_up_call(h, w_up, conv_w, conv_b, layer, S)
        last = layer == DEPTH - 1
        x2, h = ffn_down_call(act, wd, layer, x2, final_gain if last else attn_gain, 0 if last else layer + 1,
                              F32 if last else BF16)
    return h.reshape(B, S, D)
```

```python
import functools
import math

import jax
import jax.numpy as jnp
import numpy as np
from jax import lax
from jax.experimental import pallas as pl
from jax.experimental.pallas import tpu as pltpu

D_MODEL = 1024
DEPTH = 4
N_MIXERS = 3
N_BUCKETS = 32
MAX_DISTANCE = 128
DA_HEADS = 8
DA_QK_DIM = 64
DA_V_DIM = 128
SB_HEADS = 16
SB_DIM = 64
SW_Q_HEADS = 16
SW_KV_HEADS = 4
SW_DIM = 64
SW_BLOCK = 128
D_FF = 2752
EPS = 1e-6
NEG = -1e30

LANES = 128
SUBLANES = 8
VMEM_LIMIT = 52 * 1024 * 1024
WO_FFN_VMEM_LIMIT = 57 * 1024 * 1024

ATT_T = 256
SB_EXP_ZERO = -104.0
ROW_TILE = 512
FFN_TM = 1024
FFN_TF = 256
D_FF_PAD = 2816

BF16 = jnp.bfloat16
F32 = jnp.float32


def _params(*sem):
    return pltpu.CompilerParams(dimension_semantics=sem, vmem_limit_bytes=VMEM_LIMIT)


def _layer_spec(arr, layer, single_buffer=False):
    shape = (None,) + arr.shape[1:]
    index_map = lambda *_: (layer,) + (0,) * (arr.ndim - 1)
    if single_buffer:
        return pl.BlockSpec(shape, index_map, pipeline_mode=pl.Buffered(1))
    return pl.BlockSpec(shape, index_map)


def _rms(xf, gain):
    return xf * lax.rsqrt(jnp.mean(xf * xf, axis=-1, keepdims=True) + EPS) * gain


def _rmsnorm_kernel(x_ref, g_ref, h_ref):
    h_ref[...] = _rms(x_ref[...], g_ref[...]).astype(h_ref.dtype)


def rmsnorm_call(x2, gain):
    M, D = x2.shape
    return pl.pallas_call(
        _rmsnorm_kernel,
        out_shape=jax.ShapeDtypeStruct((M, D), BF16),
        grid=(M // ROW_TILE,),
        in_specs=[pl.BlockSpec((ROW_TILE, D), lambda i: (i, 0)),
                  pl.BlockSpec((1, D), lambda i: (0, 0))],
        out_specs=pl.BlockSpec((ROW_TILE, D), lambda i: (i, 0)),
        compiler_params=_params("parallel"),
        name="rmsnorm",
    )(x2, gain.reshape(1, D))


def _proj_kernel(h_ref, w_ref, o_ref, *, n_chunk):
    h = h_ref[...]
    for c in range(o_ref.shape[1] // n_chunk):
        sl = slice(c * n_chunk, (c + 1) * n_chunk)
        o_ref[:, sl] = jnp.dot(h, w_ref[:, sl], preferred_element_type=F32).astype(o_ref.dtype)


def proj_call(h, w, layer):
    M, D = h.shape
    N = w.shape[2]
    return pl.pallas_call(
        functools.partial(_proj_kernel, n_chunk=512),
        out_shape=jax.ShapeDtypeStruct((M, N), BF16),
        grid=(M // ROW_TILE,),
        in_specs=[pl.BlockSpec((ROW_TILE, D), lambda i: (i, 0)),
                  _layer_spec(w, layer)],
        out_specs=pl.BlockSpec((ROW_TILE, N), lambda i: (i, 0)),
        compiler_params=_params("parallel"),
        name="qkv_proj",
    )(h, w)


def _ffn_row_starts(n_groups):
    per = n_groups // SUBLANES
    return [SUBLANES * SUBLANES * (j % per) + j // per for j in range(n_groups)]


def _stage(ref, x):
    for c in range(ref.shape[0]):
        ref[c] = x[:, c * LANES:(c + 1) * LANES]


def _stage_interleaved(ref, x):
    for c in range(ref.shape[0]):
        for j, st in enumerate(_ffn_row_starts(x.shape[0] // SUBLANES)):
            ref[c, pl.ds(st, SUBLANES, stride=SUBLANES), :] = x[j * SUBLANES:(j + 1) * SUBLANES,
                                                                c * LANES:(c + 1) * LANES]


def _deinterleave_rows(ref, c):
    return jnp.concatenate([ref[c, pl.ds(st, SUBLANES, stride=SUBLANES), :]
                            for st in _ffn_row_starts(ref.shape[1] // SUBLANES)], axis=0)


def _causal_conv3(u, cw, cb, prev):
    nblk, g, _, tf = u.shape
    first = lax.broadcasted_iota(jnp.int32, (SUBLANES, tf), 0) == 0
    u1, u2 = [], []
    for b in range(nblk):
        tail = prev if b == 0 else u[b - 1, g - 2:]
        wrap = [jnp.where(first, pltpu.roll(tail[e], 1, 0), pltpu.roll(u[b, g - 2 + e], 1, 0))
                for e in range(2)]
        u1.append(jnp.concatenate([wrap[1][None], u[b, :g - 1]], axis=0))
        u2.append(jnp.concatenate([wrap[0][None], wrap[1][None], u[b, :g - 2]], axis=0))
    u1 = jnp.stack(u1)
    u2 = jnp.stack(u2)
    return u * cw[2:3] + u1 * cw[1:2] + u2 * cw[0:1] + cb


def _wo_ffn_up_kernel(m_ref, wo_ref, x_ref, gain_ref, wg_ref, wv_ref, cwg_ref, cwv_ref, cbg_ref, cbv_ref,
                      xo_ref, g_ref, carry_ref, stage_ref, h_ref, *, tiles_per_seq):
    tm = m_ref.shape[0]
    nblk = tm // ROW_TILE
    g = ROW_TILE // SUBLANES

    @pl.when(pl.program_id(0) % tiles_per_seq == 0)
    def _():
        carry_ref[...] = jnp.zeros(carry_ref.shape, F32)

    for b in range(nblk):
        rows = slice(b * ROW_TILE, (b + 1) * ROW_TILE)
        xn = x_ref[rows, :] + jnp.dot(m_ref[rows, :], wo_ref[...], preferred_element_type=F32)
        xo_ref[rows, :] = xn
        _stage_interleaved(stage_ref, _rms(xn, gain_ref[...]))
        for c in range(stage_ref.shape[0]):
            h_ref[rows, c * LANES:(c + 1) * LANES] = stage_ref[c].astype(h_ref.dtype)

    h = h_ref[...]
    for c in range(g_ref.shape[1] // FFN_TF):
        sl = slice(c * FFN_TF, (c + 1) * FFN_TF)
        ug = jnp.dot(h, wg_ref[:, sl], preferred_element_type=F32).reshape(nblk, g, SUBLANES, FFN_TF)
        uv = jnp.dot(h, wv_ref[:, sl], preferred_element_type=F32).reshape(nblk, g, SUBLANES, FFN_TF)
        prev_g = carry_ref[0, :, :, sl]
        prev_v = carry_ref[1, :, :, sl]
        carry_ref[0, :, :, sl] = ug[nblk - 1, g - 2:]
        carry_ref[1, :, :, sl] = uv[nblk - 1, g - 2:]
        yg = _causal_conv3(ug, cwg_ref[:, sl], cbg_ref[:, sl], prev_g)
        yv = _causal_conv3(uv, cwv_ref[:, sl], cbv_ref[:, sl], prev_v)
        act = yg * (1.0 / (1.0 + jnp.exp(-yg))) * yv
        g_ref[:, sl] = act.reshape(tm, FFN_TF).astype(g_ref.dtype)


def wo_ffn_up_call(m, wo, x2, gain, wg, wv, cwg, cwv, cbg, cbv, layer, seq):
    M, D = x2.shape
    F = wg.shape[2]
    row = lambda i: (i, 0)
    resident = lambda arr: _layer_spec(arr, layer, single_buffer=True)
    return pl.pallas_call(
        functools.partial(_wo_ffn_up_kernel, tiles_per_seq=seq // FFN_TM),
        out_shape=(jax.ShapeDtypeStruct((M, D), F32), jax.ShapeDtypeStruct((M, F), BF16)),
        grid=(M // FFN_TM,),
        in_specs=[pl.BlockSpec((FFN_TM, D), row), resident(wo), pl.BlockSpec((FFN_TM, D), row), resident(gain),
                  resident(wg), resident(wv), resident(cwg), resident(cwv), resident(cbg), resident(cbv)],
        out_specs=(pl.BlockSpec((FFN_TM, D), row), pl.BlockSpec((FFN_TM, F), row)),
        scratch_shapes=[pltpu.VMEM((2, 2, SUBLANES, F), F32),
                        pltpu.VMEM((D // LANES, ROW_TILE, LANES), F32),
                        pltpu.VMEM((FFN_TM, D), BF16)],
        compiler_params=pltpu.CompilerParams(dimension_semantics=("arbitrary",),
                                             vmem_limit_bytes=WO_FFN_VMEM_LIMIT),
        name="wo_ffn_up_conv_gate",
    )(m, wo, x2, gain, wg, wv, cwg, cwv, cbg, cbv)


def _ffn_down_kernel(a_ref, w_ref, x_ref, g_ref, xo_ref, h_ref, stage_ref):
    _stage(stage_ref, jnp.dot(a_ref[...], w_ref[...], preferred_element_type=F32))
    for c in range(stage_ref.shape[0]):
        sl = slice(c * LANES, (c + 1) * LANES)
        xo_ref[:, sl] = x_ref[:, sl] + _deinterleave_rows(stage_ref, c)
    h_ref[...] = _rms(xo_ref[...], g_ref[...]).astype(h_ref.dtype)


def ffn_down_call(a, w, layer, x2, gain, gain_layer, out_dtype):
    M, D = x2.shape
    F = a.shape[1]
    row = lambda i: (i, 0)
    return pl.pallas_call(
        _ffn_down_kernel,
        out_shape=(jax.ShapeDtypeStruct((M, D), F32), jax.ShapeDtypeStruct((M, D), out_dtype)),
        grid=(M // ROW_TILE,),
        in_specs=[pl.BlockSpec((ROW_TILE, F), row),
                  _layer_spec(w, layer, single_buffer=True),
                  pl.BlockSpec((ROW_TILE, D), row), _layer_spec(gain, gain_layer)],
        out_specs=(pl.BlockSpec((ROW_TILE, D), row), pl.BlockSpec((ROW_TILE, D), row)),
        scratch_shapes=[pltpu.VMEM((D // LANES, ROW_TILE, LANES), F32)],
        compiler_params=_params("parallel"),
        name="ffn_down_residual_norm",
    )(a, w, x2, gain)


def _half_masks(q):
    lane = lax.broadcasted_iota(jnp.int32, q.shape, 1)
    zero = jnp.zeros_like(q)
    scale = jnp.asarray(DA_QK_DIM ** -0.5, q.dtype)
    return (jnp.where(lane < 64, q, zero) * scale, jnp.where(lane >= 64, q, zero) * scale)


def _qk(q, k):
    return lax.dot_general(q, k, (((1,), (1,)), ((), ())), preferred_element_type=F32)


def _da_kernel(q_ref, k_ref, v_ref, bias_ref, lam_ref, sub_ref, o_ref, s_ref, *, lambda_init):
    T = ATT_T
    nq = q_ref.shape[0] // T
    r = lax.broadcasted_iota(jnp.int32, (T, T), 0)
    cc = lax.broadcasted_iota(jnp.int32, (T, T), 1)
    causal = r >= cc
    lam = lam_ref[...]
    lam_full = (jnp.exp(jnp.sum(lam[0:1] * lam[1:2], keepdims=True))
                - jnp.exp(jnp.sum(lam[2:3] * lam[3:4], keepdims=True)) + lambda_init)
    def score_steps(qi, st):
        def first():
            st["q01"] = jnp.concatenate(_half_masks(q_ref[qi * T:(qi + 1) * T, :]), axis=0)
            st["mx"] = [None, None]

        def tile(j):
            if j == 0:
                first()
            ss = _qk(st["q01"], k_ref[j * T:(j + 1) * T, :])
            for c in range(2):
                s = ss[c * T:(c + 1) * T]
                if j == qi - 1:
                    s = s + bias_ref[c, :, 0:T]
                if j == qi:
                    s = jnp.where(causal, s + bias_ref[c, :, T:2 * T], NEG)
                s_ref[qi % 2, c, :, j * T:(j + 1) * T] = s
                t = jnp.maximum(s[:, :LANES], s[:, LANES:])
                st["mx"][c] = t if st["mx"][c] is None else jnp.maximum(st["mx"][c], t)
        return [functools.partial(tile, j) for j in range(qi + 1)]

    def value_steps(qi, st):
        def first():
            st["m2"] = []
            for c in range(2):
                m = jnp.broadcast_to(jnp.max(st["mx"][c], axis=-1, keepdims=True), (T, LANES))
                st["m2"].append(jnp.concatenate([m, m], axis=1))
            st["lsum"] = [None, None]
            st["acc"] = None

        def tile(j):
            if j == 0:
                first()
            ps = []
            for c in range(2):
                p = jnp.exp(s_ref[qi % 2, c, :, j * T:(j + 1) * T] - st["m2"][c])
                t = p[:, :LANES] + p[:, LANES:]
                st["lsum"][c] = t if st["lsum"][c] is None else st["lsum"][c] + t
                ps.append(p.astype(BF16))
            pv = jnp.dot(jnp.concatenate(ps, axis=0), v_ref[j * T:(j + 1) * T, :],
                         preferred_element_type=F32)
            st["acc"] = pv if st["acc"] is None else st["acc"] + pv
            if j == qi:
                outs = [st["acc"][c * T:(c + 1) * T] / jnp.sum(st["lsum"][c], axis=-1, keepdims=True)
                        for c in range(2)]
                o = outs[0] - lam_full * outs[1]
                o = _rms(o, sub_ref[...]) * (1.0 - lambda_init)
                o_ref[qi * T:(qi + 1) * T, :] = o.astype(o_ref.dtype)
        return [functools.partial(tile, j) for j in range(qi + 1)]

    states = [dict() for _ in range(nq)]
    pending = []
    for qi in range(nq + 1):
        scores = score_steps(qi, states[qi]) if qi < nq else []
        for t in range(max(len(scores), len(pending))):
            if t < len(scores):
                scores[t]()
            if t < len(pending):
                pending[t]()
        pending = value_steps(qi, states[qi]) if qi < nq else []


def da_call(qkv, bias_near, lam, subln, batch, seq, lambda_init):
    T = ATT_T
    nh = DA_HEADS
    return pl.pallas_call(
        functools.partial(_da_kernel, lambda_init=lambda_init),
        out_shape=jax.ShapeDtypeStruct((batch * seq, nh * DA_V_DIM), BF16),
        grid=(batch, nh),
        in_specs=[pl.BlockSpec((seq, LANES), lambda b, h: (b, h)),
                  pl.BlockSpec((seq, LANES), lambda b, h: (b, nh + h)),
                  pl.BlockSpec((seq, LANES), lambda b, h: (b, 2 * nh + h)),
                  pl.BlockSpec((2, T, 2 * T), lambda b, h: (h, 0, 0)),
                  pl.BlockSpec((4, DA_QK_DIM), lambda b, h: (0, 0)),
                  pl.BlockSpec((1, DA_V_DIM), lambda b, h: (0, 0))],
        out_specs=pl.BlockSpec((seq, LANES), lambda b, h: (b, h)),
        scratch_shapes=[pltpu.VMEM((2, 2, T, seq), F32)],
        compiler_params=_params("parallel", "parallel"),
        name="diff_attention",
    )(qkv, qkv, qkv, bias_near, lam, subln.reshape(1, DA_V_DIM))


def _split2(x):
    hi = x.astype(BF16)
    mid = (x - hi.astype(F32)).astype(BF16)
    return hi, mid


def _sb_kernel(q_ref, k_ref, v_ref, o_ref, lbn_ref, hmn_ref, lbf_ref, hmf_ref, acc_ref):
    T = ATT_T
    nq = q_ref.shape[0] // T
    kr = lax.broadcasted_iota(jnp.int32, (T, T), 0)
    kc = lax.broadcasted_iota(jnp.int32, (T, T), 1)
    suffix = jnp.where(kr > kc, 1.0, 0.0).astype(BF16)
    suffix2 = jnp.concatenate([suffix, suffix], axis=0)
    strict = kc < kr
    lane = lax.broadcasted_iota(jnp.int32, (T, LANES), 1)
    strict2 = jnp.concatenate([strict, strict], axis=0)

    def score_tile(qi, q01, j, blk, carry, lb_view, hm_view):
        zz = _qk(q01, k_ref[j * T:(j + 1) * T, :])
        for c in range(2):
            z = zz[c * T:(c + 1) * T]
            lb = jnp.minimum(z, 0.0) - jnp.log(1.0 + jnp.exp(-jnp.abs(z)))
            log_1m_beta = lb - z
            if j == qi:
                log_1m_beta = jnp.where(strict, log_1m_beta, 0.0)
            hi, mid = _split2(log_1m_beta)
            rows = slice(blk + c * T, blk + (c + 1) * T)
            hm_view[rows, 0:T] = hi
            hm_view[rows, T:2 * T] = mid
            if carry[c] is not None:
                lb = lb + carry[c]
            lb_view[rows, :] = lb
            if j > 0:
                rs = jnp.sum(log_1m_beta, axis=-1, keepdims=True)
                carry[c] = rs if carry[c] is None else carry[c] + rs

    def value_tiles(qi, tiles, lb_view, hm_view):
        between = jnp.dot(hm_view[0:2 * len(tiles) * T, :], suffix2, preferred_element_type=F32)
        acc = None
        for pos, j in enumerate(tiles):
            rows = slice(2 * pos * T, 2 * (pos + 1) * T)
            a = jnp.exp(lb_view[rows, :] + between[rows])
            if j == qi:
                a = jnp.where(strict2, a, 0.0)
            pv = jnp.dot(a.astype(BF16), v_ref[j * T:(j + 1) * T, :], preferred_element_type=F32)
            acc = pv if acc is None else acc + pv
        return acc

    def stacked_q(qi):
        return jnp.concatenate(_half_masks(q_ref[qi * T:(qi + 1) * T, :]), axis=0)

    near = [[j for j in (qi, qi - 1) if j >= 0] for qi in range(nq)]
    carries = [[None, None] for _ in range(nq)]
    for qi in range(nq + 1):
        if qi < nq:
            q01 = stacked_q(qi)
            for pos, j in enumerate(near[qi]):
                score_tile(qi, q01, j, 2 * pos * T, carries[qi], lbn_ref.at[qi % 2], hmn_ref.at[qi % 2])
        if qi > 0:
            acc_ref[qi - 1] = value_tiles(qi - 1, near[qi - 1], lbn_ref.at[(qi - 1) % 2], hmn_ref.at[(qi - 1) % 2])

    for qi in range(2, nq):
        far = list(range(qi - 2, -1, -1))
        carry = carries[qi]
        nearest_sum = jnp.max(jnp.maximum(carry[0], carry[1]))

        @pl.when(nearest_sum > SB_EXP_ZERO)
        def _():
            q01 = stacked_q(qi)
            far_carry = list(carry)
            for pos, j in enumerate(far):
                score_tile(qi, q01, j, 2 * pos * T, far_carry, lbf_ref, hmf_ref)
            acc_ref[qi] += value_tiles(qi, far, lbf_ref, hmf_ref)

    for qi in range(nq):
        acc = acc_ref[qi]
        o_ref[qi * T:(qi + 1) * T, :] = jnp.where(lane < 64, acc[0:T], acc[T:2 * T]).astype(o_ref.dtype)


def sb_call(qkv, batch, seq):
    npair = SB_HEADS // 2
    return pl.pallas_call(
        _sb_kernel,
        out_shape=jax.ShapeDtypeStruct((batch * seq, SB_HEADS * SB_DIM), BF16),
        grid=(batch, npair),
        in_specs=[pl.BlockSpec((seq, LANES), lambda b, p: (b, p)),
                  pl.BlockSpec((seq, LANES), lambda b, p: (b, npair + p)),
                  pl.BlockSpec((seq, LANES), lambda b, p: (b, 2 * npair + p))],
        out_specs=pl.BlockSpec((seq, LANES), lambda b, p: (b, p)),
        scratch_shapes=[pltpu.VMEM((2, 4 * ATT_T, ATT_T), F32), pltpu.VMEM((2, 4 * ATT_T, 2 * ATT_T), BF16),
                        pltpu.VMEM((2 * seq - 4 * ATT_T, ATT_T), F32),
                        pltpu.VMEM((2 * seq - 4 * ATT_T, 2 * ATT_T), BF16),
                        pltpu.VMEM((seq // ATT_T, 2 * ATT_T, LANES), F32)],
        compiler_params=_params("parallel", "parallel"),
        name="stick_breaking_attention",
    )(qkv, qkv, qkv)


def _sw_kernel(sink_ref, q_ref, k_ref, v_ref, bias_ref, o_ref, s_ref):
    W = SW_BLOCK
    nb = q_ref.shape[0] // W
    p_id = pl.program_id(1)
    r = lax.broadcasted_iota(jnp.int32, (2 * W, 2 * W), 0) & (W - 1)
    cidx = lax.broadcasted_iota(jnp.int32, (2 * W, 2 * W), 1)
    valid = ((cidx < W) & (cidx > r)) | ((cidx >= W) & (cidx - W <= r))
    biasm = jnp.where(valid, bias_ref[...], NEG)
    row = lax.broadcasted_iota(jnp.int32, (2 * W, 1), 0)
    sink = jnp.where(row < W, sink_ref[2 * p_id], sink_ref[2 * p_id + 1])
    lane = lax.broadcasted_iota(jnp.int32, (W, LANES), 1)
    for n in range(nb):
        q01 = jnp.concatenate(_half_masks(q_ref[n * W:(n + 1) * W, :]), axis=0)
        if n == 0:
            s_ref[0, :, W:] = _qk(q01, k_ref[0:W, :]) + biasm[:, W:]
        else:
            s_ref[n] = _qk(q01, k_ref[(n - 1) * W:(n + 1) * W, :]) + biasm
    for n in range(nb):
        s = s_ref[0, :, W:] if n == 0 else s_ref[n]
        m = jnp.maximum(jnp.max(s, axis=-1, keepdims=True), sink)
        e = jnp.exp(s - m)
        denom = jnp.sum(e, axis=-1, keepdims=True) + jnp.exp(sink - m)
        v = v_ref[0:W, :] if n == 0 else v_ref[(n - 1) * W:(n + 1) * W, :]
        o = jnp.dot(e.astype(BF16), v, preferred_element_type=F32) / denom
        o_ref[n * W:(n + 1) * W, :] = jnp.where(lane < 64, o[0:W], o[W:]).astype(o_ref.dtype)


def sw_call(qkv, bias_band, sinks, batch, seq):
    npair = SW_Q_HEADS // 2
    q_blocks = SW_Q_HEADS * SW_DIM // LANES
    W = SW_BLOCK
    return pl.pallas_call(
        _sw_kernel,
        out_shape=jax.ShapeDtypeStruct((batch * seq, SW_Q_HEADS * SW_DIM), BF16),
        grid=(batch, npair),
        in_specs=[pl.BlockSpec(memory_space=pltpu.SMEM),
                  pl.BlockSpec((seq, LANES), lambda b, p: (b, p)),
                  pl.BlockSpec((seq, LANES), lambda b, p: (b, q_blocks + p // 2)),
                  pl.BlockSpec((seq, LANES), lambda b, p: (b, q_blocks + SW_KV_HEADS + p // 2)),
                  pl.BlockSpec((None, 2 * W, 2 * W), lambda b, p: (p, 0, 0))],
        out_specs=pl.BlockSpec((seq, LANES), lambda b, p: (b, p)),
        scratch_shapes=[pltpu.VMEM((seq // W, 2 * W, 2 * W), F32)],
        compiler_params=_params("parallel", "parallel"),
        name="sliding_window_attention",
    )(sinks, qkv, qkv, qkv, bias_band.reshape(npair, 2 * W, 2 * W))


def _t5_bucket(dist):
    max_exact = N_BUCKETS // 2
    d = jnp.maximum(dist, 0)
    large = max_exact + (jnp.log(jnp.maximum(d, 1).astype(F32) / max_exact)
                         / math.log(MAX_DISTANCE / max_exact) * (N_BUCKETS - max_exact)).astype(jnp.int32)
    large = jnp.minimum(large, N_BUCKETS - 1)
    return jnp.where(d < max_exact, d, large)


def _band_kernel(row_ref, o_ref):
    rows = jnp.broadcast_to(row_ref[...], (o_ref.shape[0], row_ref.shape[1]))
    o_ref[...] = pltpu.roll(rows, 0, 1, stride=1, stride_axis=0)[:, :o_ref.shape[1]]


def _bias_band(table, T):
    y = np.arange(3 * T)
    dist = np.clip(np.where(y <= 2 * T, T - y, 4 * T - y), 0, 2 * T - 1)
    row = table.astype(F32)[_t5_bucket(jnp.asarray(dist, jnp.int32))].T
    ch = row.shape[0]
    return pl.pallas_call(
        _band_kernel,
        out_shape=jax.ShapeDtypeStruct((ch, T, 2 * T), F32),
        grid=(ch,),
        in_specs=[pl.BlockSpec((None, 1, 3 * T), lambda c: (c, 0, 0))],
        out_specs=pl.BlockSpec((None, T, 2 * T), lambda c: (c, 0, 0)),
        compiler_params=_params("parallel"),
        name="bias_band",
    )(row[:, None, :])


def _lambda_init(layer):
    return 0.8 - 0.6 * math.exp(-0.3 * layer)


def _dup_kv_heads(w):
    q_w = SW_Q_HEADS * SW_DIM
    kv = w[:, q_w:].reshape(w.shape[0], 2 * SW_KV_HEADS, 1, SW_DIM)
    kv = jnp.broadcast_to(kv, (w.shape[0], 2 * SW_KV_HEADS, 2, SW_DIM)).reshape(w.shape[0], -1)
    return jnp.concatenate([w[:, :q_w], kv], axis=1)


def _pad_ff(a, axis):
    pad = [(0, 0)] * a.ndim
    pad[axis] = (0, D_FF_PAD - D_FF)
    return jnp.pad(a, pad)


def kernel(x, rel_bias, attn_norm, ffn_norm, w_o, da_w_qkv, da_lambda, da_subln, sb_w_qkv, sw_w_qkv,
           sw_sinks, ffn_w_up, ffn_conv_w, ffn_conv_b, ffn_w_down, final_norm):
    B, S, D = x.shape
    x2 = x.reshape(B * S, D)
    da_bias_near = _bias_band(rel_bias - rel_bias[N_BUCKETS - 1], ATT_T)
    sw_bias = _bias_band(rel_bias, SW_BLOCK)
    da_w = da_w_qkv.astype(BF16)
    sb_w = sb_w_qkv.astype(BF16)
    sw_w = jax.vmap(_dup_kv_heads)(sw_w_qkv).astype(BF16)
    wo_w = w_o.astype(BF16)
    wg = _pad_ff(ffn_w_up[:, :, :D_FF].astype(BF16), 2)
    wv = _pad_ff(ffn_w_up[:, :, D_FF:].astype(BF16), 2)
    cwg = _pad_ff(ffn_conv_w[:, :, :D_FF], 2)
    cwv = _pad_ff(ffn_conv_w[:, :, D_FF:], 2)
    cbg = _pad_ff(ffn_conv_b[:, None, :D_FF], 2)
    cbv = _pad_ff(ffn_conv_b[:, None, D_FF:], 2)
    wd = _pad_ff(ffn_w_down.astype(BF16), 1)
    attn_gain = attn_norm[:, None, :]
    ffn_gain = ffn_norm[:, None, :]
    final_gain = final_norm[None, None, :]

    h = rmsnorm_call(x2, attn_norm[0])
    for layer in range(DEPTH):
        mixer = layer % N_MIXERS
        slot = layer // N_MIXERS
        if mixer == 0:
            qkv = proj_call(h, da_w, slot)
            m = da_call(qkv, da_bias_near, da_lambda[slot], da_subln[slot], B, S, _lambda_init(layer))
        elif mixer == 1:
            qkv = proj_call(h, sb_w, slot)
            m = sb_call(qkv, B, S)
        else:
            qkv = proj_call(h, sw_w, slot)
            m = sw_call(qkv, sw_bias, sw_sinks[slot], B, S)
        x2, act = wo_ffn_up_call(m, wo_w, x2, ffn_gain, wg, wv, cwg, cwv, cbg, cbv, layer, S)
        last = layer == DEPTH - 1
        x2, h = ffn_down_call(act, wd, layer, x2, final_gain if last else attn_gain, 0 if last else layer + 1,
                              F32 if last else BF16)
    return h.reshape(B, S, D)
```

```python
import functools
import math

import jax
import jax.numpy as jnp
import numpy as np
from jax import lax
from jax.experimental import pallas as pl
from jax.experimental.pallas import tpu as pltpu

D_MODEL = 1024
DEPTH = 4
N_MIXERS = 3
N_BUCKETS = 32
MAX_DISTANCE = 128
DA_HEADS = 8
DA_QK_DIM = 64
DA_V_DIM = 128
SB_HEADS = 16
SB_DIM = 64
SW_Q_HEADS = 16
SW_KV_HEADS = 4
SW_DIM = 64
SW_BLOCK = 128
D_FF = 2752
EPS = 1e-6
NEG = -1e30

LANES = 128
SUBLANES = 8
VMEM_LIMIT = 52 * 1024 * 1024
WO_FFN_VMEM_LIMIT = 57 * 1024 * 1024

ATT_T = 256
SB_EXP_ZERO = -104.0
ROW_TILE = 512
PROJ_TM = 1024
PROJ_CHUNK = 512
FFN_TM = 1024
FFN_TF = 256
D_FF_PAD = 2816

BF16 = jnp.bfloat16
F32 = jnp.float32


def _params(*sem):
    return pltpu.CompilerParams(dimension_semantics=sem, vmem_limit_bytes=VMEM_LIMIT)


def _layer_spec(arr, layer, single_buffer=False):
    shape = (None,) + arr.shape[1:]
    index_map = lambda *_: (layer,) + (0,) * (arr.ndim - 1)
    if single_buffer:
        return pl.BlockSpec(shape, index_map, pipeline_mode=pl.Buffered(1))
    return pl.BlockSpec(shape, index_map)


def _rms(xf, gain):
    return xf * lax.rsqrt(jnp.mean(xf * xf, axis=-1, keepdims=True) + EPS) * gain


def _project(h, w_ref, o_ref):
    for c in range(o_ref.shape[1] // PROJ_CHUNK):
        sl = slice(c * PROJ_CHUNK, (c + 1) * PROJ_CHUNK)
        o_ref[:, sl] = jnp.dot(h, w_ref[:, sl], preferred_element_type=F32).astype(o_ref.dtype)


def _proj_kernel(h_ref, w_ref, o_ref):
    _project(h_ref[...], w_ref, o_ref)


def _norm_proj_kernel(x_ref, g_ref, w_ref, o_ref):
    _project(_rms(x_ref[...], g_ref[...]).astype(BF16), w_ref, o_ref)


def proj_call(h, w, layer, gain=None, gain_layer=0):
    M, D = h.shape
    N = w.shape[2]
    row = lambda i: (i, 0)
    in_specs = [pl.BlockSpec((PROJ_TM, D), row)]
    args = [h]
    if gain is not None:
        in_specs.append(_layer_spec(gain, gain_layer))
        args.append(gain)
    return pl.pallas_call(
        _proj_kernel if gain is None else _norm_proj_kernel,
        out_shape=jax.ShapeDtypeStruct((M, N), BF16),
        grid=(M // PROJ_TM,),
        in_specs=in_specs + [_layer_spec(w, layer, single_buffer=True)],
        out_specs=pl.BlockSpec((PROJ_TM, N), row),
        compiler_params=_params("parallel"),
        name="qkv_proj",
    )(*args, w)


def _ffn_row_starts(n_groups):
    per = n_groups // SUBLANES
    return [SUBLANES * SUBLANES * (j % per) + j // per for j in range(n_groups)]


def _stage(ref, x):
    for c in range(ref.shape[0]):
        ref[c] = x[:, c * LANES:(c + 1) * LANES]


def _stage_interleaved(ref, x):
    for c in range(ref.shape[0]):
        for j, st in enumerate(_ffn_row_starts(x.shape[0] // SUBLANES)):
            ref[c, pl.ds(st, SUBLANES, stride=SUBLANES), :] = x[j * SUBLANES:(j + 1) * SUBLANES,
                                                                c * LANES:(c + 1) * LANES]


def _deinterleave_rows(ref, c):
    return jnp.concatenate([ref[c, pl.ds(st, SUBLANES, stride=SUBLANES), :]
                            for st in _ffn_row_starts(ref.shape[1] // SUBLANES)], axis=0)


def _causal_conv3(u, cw, cb, prev):
    nblk, g, _, tf = u.shape
    first = lax.broadcasted_iota(jnp.int32, (SUBLANES, tf), 0) == 0
    u1, u2 = [], []
    for b in range(nblk):
        tail = prev if b == 0 else u[b - 1, g - 2:]
        wrap = [jnp.where(first, pltpu.roll(tail[e], 1, 0), pltpu.roll(u[b, g - 2 + e], 1, 0))
                for e in range(2)]
        u1.append(jnp.concatenate([wrap[1][None], u[b, :g - 1]], axis=0))
        u2.append(jnp.concatenate([wrap[0][None], wrap[1][None], u[b, :g - 2]], axis=0))
    u1 = jnp.stack(u1)
    u2 = jnp.stack(u2)
    return u * cw[2:3] + u1 * cw[1:2] + u2 * cw[0:1] + cb


def _wo_ffn_up_kernel(m_ref, wo_ref, x_ref, gain_ref, wg_ref, wv_ref, cwg_ref, cwv_ref, cbg_ref, cbv_ref,
                      xo_ref, g_ref, carry_ref, stage_ref, h_ref, *, tiles_per_seq):
    tm = m_ref.shape[0]
    nblk = tm // ROW_TILE
    g = ROW_TILE // SUBLANES

    @pl.when(pl.program_id(0) % tiles_per_seq == 0)
    def _():
        carry_ref[...] = jnp.zeros(carry_ref.shape, F32)

    for b in range(nblk):
        rows = slice(b * ROW_TILE, (b + 1) * ROW_TILE)
        xn = x_ref[rows, :] + jnp.dot(m_ref[rows, :], wo_ref[...], preferred_element_type=F32)
        xo_ref[rows, :] = xn
        _stage_interleaved(stage_ref, _rms(xn, gain_ref[...]))
        for c in range(stage_ref.shape[0]):
            h_ref[rows, c * LANES:(c + 1) * LANES] = stage_ref[c].astype(h_ref.dtype)

    h = h_ref[...]
    for c in range(g_ref.shape[1] // FFN_TF):
        sl = slice(c * FFN_TF, (c + 1) * FFN_TF)
        ug = jnp.dot(h, wg_ref[:, sl], preferred_element_type=F32).reshape(nblk, g, SUBLANES, FFN_TF)
        uv = jnp.dot(h, wv_ref[:, sl], preferred_element_type=F32).reshape(nblk, g, SUBLANES, FFN_TF)
        prev_g = carry_ref[0, :, :, sl]
        prev_v = carry_ref[1, :, :, sl]
        carry_ref[0, :, :, sl] = ug[nblk - 1, g - 2:]
        carry_ref[1, :, :, sl] = uv[nblk - 1, g - 2:]
        yg = _causal_conv3(ug, cwg_ref[:, sl], cbg_ref[:, sl], prev_g)
        yv = _causal_conv3(uv, cwv_ref[:, sl], cbv_ref[:, sl], prev_v)
        act = yg * (1.0 / (1.0 + jnp.exp(-yg))) * yv
        g_ref[:, sl] = act.reshape(tm, FFN_TF).astype(g_ref.dtype)


def wo_ffn_up_call(m, wo, x2, gain, wg, wv, cwg, cwv, cbg, cbv, layer, seq):
    M, D = x2.shape
    F = wg.shape[2]
    row = lambda i: (i, 0)
    resident = lambda arr: _layer_spec(arr, layer, single_buffer=True)
    return pl.pallas_call(
        functools.partial(_wo_ffn_up_kernel, tiles_per_seq=seq // FFN_TM),
        out_shape=(jax.ShapeDtypeStruct((M, D), F32), jax.ShapeDtypeStruct((M, F), BF16)),
        grid=(M // FFN_TM,),
        in_specs=[pl.BlockSpec((FFN_TM, D), row), resident(wo), pl.BlockSpec((FFN_TM, D), row), resident(gain),
                  resident(wg), resident(wv), resident(cwg), resident(cwv), resident(cbg), resident(cbv)],
        out_specs=(pl.BlockSpec((FFN_TM, D), row), pl.BlockSpec((FFN_TM, F), row)),
        scratch_shapes=[pltpu.VMEM((2, 2, SUBLANES, F), F32),
                        pltpu.VMEM((D // LANES, ROW_TILE, LANES), F32),
                        pltpu.VMEM((FFN_TM, D), BF16)],
        compiler_params=pltpu.CompilerParams(dimension_semantics=("arbitrary",),
                                             vmem_limit_bytes=WO_FFN_VMEM_LIMIT),
        name="wo_ffn_up_conv_gate",
    )(m, wo, x2, gain, wg, wv, cwg, cwv, cbg, cbv)


def _ffn_down_kernel(a_ref, w_ref, x_ref, g_ref, *refs, emit_x):
    xo_ref, h_ref, stage_ref = refs if emit_x else (refs[2], refs[0], refs[1])
    _stage(stage_ref, jnp.dot(a_ref[...], w_ref[...], preferred_element_type=F32))
    for c in range(stage_ref.shape[0]):
        sl = slice(c * LANES, (c + 1) * LANES)
        xo_ref[:, sl] = x_ref[:, sl] + _deinterleave_rows(stage_ref, c)
    h_ref[...] = _rms(xo_ref[...], g_ref[...]).astype(h_ref.dtype)


def ffn_down_call(a, w, layer, x2, gain, gain_layer, out_dtype, emit_x):
    M, D = x2.shape
    F = a.shape[1]
    row = lambda i: (i, 0)
    x_shape, x_spec = jax.ShapeDtypeStruct((M, D), F32), pl.BlockSpec((ROW_TILE, D), row)
    h_shape, h_spec = jax.ShapeDtypeStruct((M, D), out_dtype), pl.BlockSpec((ROW_TILE, D), row)
    stage = pltpu.VMEM((D // LANES, ROW_TILE, LANES), F32)
    return pl.pallas_call(
        functools.partial(_ffn_down_kernel, emit_x=emit_x),
        out_shape=(x_shape, h_shape) if emit_x else h_shape,
        grid=(M // ROW_TILE,),
        in_specs=[pl.BlockSpec((ROW_TILE, F), row),
                  _layer_spec(w, layer, single_buffer=True),
                  pl.BlockSpec((ROW_TILE, D), row), _layer_spec(gain, gain_layer)],
        out_specs=(x_spec, h_spec) if emit_x else h_spec,
        scratch_shapes=[stage] if emit_x else [stage, pltpu.VMEM((ROW_TILE, D), F32)],
        compiler_params=_params("parallel"),
        name="ffn_down_residual_norm",
    )(a, w, x2, gain)


def _half_masks(q):
    lane = lax.broadcasted_iota(jnp.int32, q.shape, 1)
    zero = jnp.zeros_like(q)
    scale = jnp.asarray(DA_QK_DIM ** -0.5, q.dtype)
    return (jnp.where(lane < 64, q, zero) * scale, jnp.where(lane >= 64, q, zero) * scale)


def _qk(q, k):
    return lax.dot_general(q, k, (((1,), (1,)), ((), ())), preferred_element_type=F32)


def _da_kernel(q_ref, k_ref, v_ref, bias_ref, lam_ref, sub_ref, o_ref, s_ref, *, lambda_init):
    T = ATT_T
    nq = q_ref.shape[0] // T
    r = lax.broadcasted_iota(jnp.int32, (T, T), 0)
    cc = lax.broadcasted_iota(jnp.int32, (T, T), 1)
    causal = r >= cc
    lam = lam_ref[...]
    lam_full = (jnp.exp(jnp.sum(lam[0:1] * lam[1:2], keepdims=True))
                - jnp.exp(jnp.sum(lam[2:3] * lam[3:4], keepdims=True)) + lambda_init)
    def score_steps(qi, st):
        def first():
            st["q01"] = jnp.concatenate(_half_masks(q_ref[qi * T:(qi + 1) * T, :]), axis=0)
            st["mx"] = [None, None]

        def tile(j):
            if j == 0:
                first()
            ss = _qk(st["q01"], k_ref[j * T:(j + 1) * T, :])
            for c in range(2):
                s = ss[c * T:(c + 1) * T]
                if j == qi - 1:
                    s = s + bias_ref[c, :, 0:T]
                if j == qi:
                    s = jnp.where(causal, s + bias_ref[c, :, T:2 * T], NEG)
                s_ref[qi % 2, c, :, j * T:(j + 1) * T] = s
                t = jnp.maximum(s[:, :LANES], s[:, LANES:])
                st["mx"][c] = t if st["mx"][c] is None else jnp.maximum(st["mx"][c], t)
        return [functools.partial(tile, j) for j in range(qi + 1)]

    def value_steps(qi, st):
        def first():
            st["m2"] = []
            for c in range(2):
                m = jnp.broadcast_to(jnp.max(st["mx"][c], axis=-1, keepdims=True), (T, LANES))
                st["m2"].append(jnp.concatenate([m, m], axis=1))
            st["lsum"] = [None, None]
            st["acc"] = None

        def tile(j):
            if j == 0:
                first()
            ps = []
            for c in range(2):
                p = jnp.exp(s_ref[qi % 2, c, :, j * T:(j + 1) * T] - st["m2"][c])
                t = p[:, :LANES] + p[:, LANES:]
                st["lsum"][c] = t if st["lsum"][c] is None else st["lsum"][c] + t
                ps.append(p.astype(BF16))
            pv = jnp.dot(jnp.concatenate(ps, axis=0), v_ref[j * T:(j + 1) * T, :],
                         preferred_element_type=F32)
            st["acc"] = pv if st["acc"] is None else st["acc"] + pv
            if j == qi:
                outs = [st["acc"][c * T:(c + 1) * T] / jnp.sum(st["lsum"][c], axis=-1, keepdims=True)
                        for c in range(2)]
                o = outs[0] - lam_full * outs[1]
                o = _rms(o, sub_ref[...]) * (1.0 - lambda_init)
                o_ref[qi * T:(qi + 1) * T, :] = o.astype(o_ref.dtype)
        return [functools.partial(tile, j) for j in range(qi + 1)]

    states = [dict() for _ in range(nq)]
    pending = []
    for qi in range(nq + 1):
        scores = score_steps(qi, states[qi]) if qi < nq else []
        for t in range(max(len(scores), len(pending))):
            if t < len(scores):
                scores[t]()
            if t < len(pending):
                pending[t]()
        pending = value_steps(qi, states[qi]) if qi < nq else []


def da_call(qkv, bias_near, lam, subln, batch, seq, lambda_init):
    T = ATT_T
    nh = DA_HEADS
    return pl.pallas_call(
        functools.partial(_da_kernel, lambda_init=lambda_init),
        out_shape=jax.ShapeDtypeStruct((batch * seq, nh * DA_V_DIM), BF16),
        grid=(batch, nh),
        in_specs=[pl.BlockSpec((seq, LANES), lambda b, h: (b, h)),
                  pl.BlockSpec((seq, LANES), lambda b, h: (b, nh + h)),
                  pl.BlockSpec((seq, LANES), lambda b, h: (b, 2 * nh + h)),
                  pl.BlockSpec((2, T, 2 * T), lambda b, h: (h, 0, 0)),
                  pl.BlockSpec((4, DA_QK_DIM), lambda b, h: (0, 0)),
                  pl.BlockSpec((1, DA_V_DIM), lambda b, h: (0, 0))],
        out_specs=pl.BlockSpec((seq, LANES), lambda b, h: (b, h)),
        scratch_shapes=[pltpu.VMEM((2, 2, T, seq), F32)],
        compiler_params=_params("parallel", "parallel"),
        name="diff_attention",
    )(qkv, qkv, qkv, bias_near, lam, subln.reshape(1, DA_V_DIM))


def _split2(x):
    hi = x.astype(BF16)
    mid = (x - hi.astype(F32)).astype(BF16)
    return hi, mid


def _sb_kernel(q_ref, k_ref, v_ref, o_ref, lbn_ref, hmn_ref, lbf_ref, hmf_ref, acc_ref):
    T = ATT_T
    nq = q_ref.shape[0] // T
    kr = lax.broadcasted_iota(jnp.int32, (T, T), 0)
    kc = lax.broadcasted_iota(jnp.int32, (T, T), 1)
    suffix = jnp.where(kr > kc, 1.0, 0.0).astype(BF16)
    suffix2 = jnp.concatenate([suffix, suffix], axis=0)
    strict = kc < kr
    lane = lax.broadcasted_iota(jnp.int32, (T, LANES), 1)
    strict2 = jnp.concatenate([strict, strict], axis=0)

    def score_tile(qi, q01, j, blk, carry, lb_view, hm_view):
        zz = _qk(q01, k_ref[j * T:(j + 1) * T, :])
        for c in range(2):
            z = zz[c * T:(c + 1) * T]
            lb = jnp.minimum(z, 0.0) - jnp.log(1.0 + jnp.exp(-jnp.abs(z)))
            log_1m_beta = lb - z
            if j == qi:
                log_1m_beta = jnp.where(strict, log_1m_beta, 0.0)
            hi, mid = _split2(log_1m_beta)
            rows = slice(blk + c * T, blk + (c + 1) * T)
            hm_view[rows, 0:T] = hi
            hm_view[rows, T:2 * T] = mid
            if carry[c] is not None:
                lb = lb + carry[c]
            lb_view[rows, :] = lb
            if j > 0:
                rs = jnp.sum(log_1m_beta, axis=-1, keepdims=True)
                carry[c] = rs if carry[c] is None else carry[c] + rs

    def value_tiles(qi, tiles, lb_view, hm_view):
        between = jnp.dot(hm_view[0:2 * len(tiles) * T, :], suffix2, preferred_element_type=F32)
        acc = None
        for pos, j in enumerate(tiles):
            rows = slice(2 * pos * T, 2 * (pos + 1) * T)
            a = jnp.exp(lb_view[rows, :] + between[rows])
            if j == qi:
                a = jnp.where(strict2, a, 0.0)
            pv = jnp.dot(a.astype(BF16), v_ref[j * T:(j + 1) * T, :], preferred_element_type=F32)
            acc = pv if acc is None else acc + pv
        return acc

    def stacked_q(qi):
        return jnp.concatenate(_half_masks(q_ref[qi * T:(qi + 1) * T, :]), axis=0)

    near = [[j for j in (qi, qi - 1) if j >= 0] for qi in range(nq)]
    carries = [[None, None] for _ in range(nq)]
    for qi in range(nq + 1):
        if qi < nq:
            q01 = stacked_q(qi)
            for pos, j in enumerate(near[qi]):
                score_tile(qi, q01, j, 2 * pos * T, carries[qi], lbn_ref.at[qi % 2], hmn_ref.at[qi % 2])
        if qi > 0:
            acc_ref[qi - 1] = value_tiles(qi - 1, near[qi - 1], lbn_ref.at[(qi - 1) % 2], hmn_ref.at[(qi - 1) % 2])

    for qi in range(2, nq):
        far = list(range(qi - 2, -1, -1))
        carry = carries[qi]
        nearest_sum = jnp.max(jnp.maximum(carry[0], carry[1]))

        @pl.when(nearest_sum > SB_EXP_ZERO)
        def _():
            q01 = stacked_q(qi)
            far_carry = list(carry)
            for pos, j in enumerate(far):
                score_tile(qi, q01, j, 2 * pos * T, far_carry, lbf_ref, hmf_ref)
            acc_ref[qi] += value_tiles(qi, far, lbf_ref, hmf_ref)

    for qi in range(nq):
        acc = acc_ref[qi]
        o_ref[qi * T:(qi + 1) * T, :] = jnp.where(lane < 64, acc[0:T], acc[T:2 * T]).astype(o_ref.dtype)


def sb_call(qkv, batch, seq):
    npair = SB_HEADS // 2
    return pl.pallas_call(
        _sb_kernel,
        out_shape=jax.ShapeDtypeStruct((batch * seq, SB_HEADS * SB_DIM), BF16),
        grid=(batch, npair),
        in_specs=[pl.BlockSpec((seq, LANES), lambda b, p: (b, p)),
                  pl.BlockSpec((seq, LANES), lambda b, p: (b, npair + p)),
                  pl.BlockSpec((seq, LANES), lambda b, p: (b, 2 * npair + p))],
        out_specs=pl.BlockSpec((seq, LANES), lambda b, p: (b, p)),
        scratch_shapes=[pltpu.VMEM((2, 4 * ATT_T, ATT_T), F32), pltpu.VMEM((2, 4 * ATT_T, 2 * ATT_T), BF16),
                        pltpu.VMEM((2 * seq - 4 * ATT_T, ATT_T), F32),
                        pltpu.VMEM((2 * seq - 4 * ATT_T, 2 * ATT_T), BF16),
                        pltpu.VMEM((seq // ATT_T, 2 * ATT_T, LANES), F32)],
        compiler_params=_params("parallel", "parallel"),
        name="stick_breaking_attention",
    )(qkv, qkv, qkv)


def _sw_kernel(sink_ref, q_ref, k_ref, v_ref, bias_ref, o_ref, s_ref):
    W = SW_BLOCK
    nb = q_ref.shape[0] // W
    p_id = pl.program_id(1)
    r = lax.broadcasted_iota(jnp.int32, (2 * W, 2 * W), 0) & (W - 1)
    cidx = lax.broadcasted_iota(jnp.int32, (2 * W, 2 * W), 1)
    valid = ((cidx < W) & (cidx > r)) | ((cidx >= W) & (cidx - W <= r))
    biasm = jnp.where(valid, bias_ref[...], NEG)
    row = lax.broadcasted_iota(jnp.int32, (2 * W, 1), 0)
    sink = jnp.where(row < W, sink_ref[2 * p_id], sink_ref[2 * p_id + 1])
    lane = lax.broadcasted_iota(jnp.int32, (W, LANES), 1)
    for n in range(nb):
        q01 = jnp.concatenate(_half_masks(q_ref[n * W:(n + 1) * W, :]), axis=0)
        if n == 0:
            s_ref[0, :, W:] = _qk(q01, k_ref[0:W, :]) + biasm[:, W:]
        else:
            s_ref[n] = _qk(q01, k_ref[(n - 1) * W:(n + 1) * W, :]) + biasm
    for n in range(nb):
        s = s_ref[0, :, W:] if n == 0 else s_ref[n]
        m = jnp.maximum(jnp.max(s, axis=-1, keepdims=True), sink)
        e = jnp.exp(s - m)
        denom = jnp.sum(e, axis=-1, keepdims=True) + jnp.exp(sink - m)
        v = v_ref[0:W, :] if n == 0 else v_ref[(n - 1) * W:(n + 1) * W, :]
        o = jnp.dot(e.astype(BF16), v, preferred_element_type=F32) / denom
        o_ref[n * W:(n + 1) * W, :] = jnp.where(lane < 64, o[0:W], o[W:]).astype(o_ref.dtype)


def sw_call(qkv, bias_band, sinks, batch, seq):
    npair = SW_Q_HEADS // 2
    q_blocks = SW_Q_HEADS * SW_DIM // LANES
    W = SW_BLOCK
    return pl.pallas_call(
        _sw_kernel,
        out_shape=jax.ShapeDtypeStruct((batch * seq, SW_Q_HEADS * SW_DIM), BF16),
        grid=(batch, npair),
        in_specs=[pl.BlockSpec(memory_space=pltpu.SMEM),
                  pl.BlockSpec((seq, LANES), lambda b, p: (b, p)),
                  pl.BlockSpec((seq, LANES), lambda b, p: (b, q_blocks + p // 2)),
                  pl.BlockSpec((seq, LANES), lambda b, p: (b, q_blocks + SW_KV_HEADS + p // 2)),
                  pl.BlockSpec((None, 2 * W, 2 * W), lambda b, p: (p, 0, 0))],
        out_specs=pl.BlockSpec((seq, LANES), lambda b, p: (b, p)),
        scratch_shapes=[pltpu.VMEM((seq // W, 2 * W, 2 * W), F32)],
        compiler_params=_params("parallel", "parallel"),
        name="sliding_window_attention",
    )(sinks, qkv, qkv, qkv, bias_band.reshape(npair, 2 * W, 2 * W))


def _t5_bucket(dist):
    max_exact = N_BUCKETS // 2
    d = jnp.maximum(dist, 0)
    large = max_exact + (jnp.log(jnp.maximum(d, 1).astype(F32) / max_exact)
                         / math.log(MAX_DISTANCE / max_exact) * (N_BUCKETS - max_exact)).astype(jnp.int32)
    large = jnp.minimum(large, N_BUCKETS - 1)
    return jnp.where(d < max_exact, d, large)


def _band_kernel(row_ref, o_ref):
    rows = jnp.broadcast_to(row_ref[...], (o_ref.shape[0], row_ref.shape[1]))
    o_ref[...] = pltpu.roll(rows, 0, 1, stride=1, stride_axis=0)[:, :o_ref.shape[1]]


def _bias_band(table, T):
    y = np.arange(3 * T)
    dist = np.clip(np.where(y <= 2 * T, T - y, 4 * T - y), 0, 2 * T - 1)
    row = table.astype(F32)[_t5_bucket(jnp.asarray(dist, jnp.int32))].T
    ch = row.shape[0]
    return pl.pallas_call(
        _band_kernel,
        out_shape=jax.ShapeDtypeStruct((ch, T, 2 * T), F32),
        grid=(ch,),
        in_specs=[pl.BlockSpec((None, 1, 3 * T), lambda c: (c, 0, 0))],
        out_specs=pl.BlockSpec((None, T, 2 * T), lambda c: (c, 0, 0)),
        compiler_params=_params("parallel"),
        name="bias_band",
    )(row[:, None, :])


def _lambda_init(layer):
    return 0.8 - 0.6 * math.exp(-0.3 * layer)


def _dup_kv_heads(w):
    q_w = SW_Q_HEADS * SW_DIM
    kv = w[:, q_w:].reshape(w.shape[0], 2 * SW_KV_HEADS, 1, SW_DIM)
    kv = jnp.broadcast_to(kv, (w.shape[0], 2 * SW_KV_HEADS, 2, SW_DIM)).reshape(w.shape[0], -1)
    return jnp.concatenate([w[:, :q_w], kv], axis=1)


def _pad_ff(a, axis):
    pad = [(0, 0)] * a.ndim
    pad[axis] = (0, D_FF_PAD - D_FF)
    return jnp.pad(a, pad)


def kernel(x, rel_bias, attn_norm, ffn_norm, w_o, da_w_qkv, da_lambda, da_subln, sb_w_qkv, sw_w_qkv,
           sw_sinks, ffn_w_up, ffn_conv_w, ffn_conv_b, ffn_w_down, final_norm):
    B, S, D = x.shape
    x2 = x.reshape(B * S, D)
    da_bias_near = _bias_band(rel_bias - rel_bias[N_BUCKETS - 1], ATT_T)
    sw_bias = _bias_band(rel_bias, SW_BLOCK)
    da_w = da_w_qkv.astype(BF16)
    sb_w = sb_w_qkv.astype(BF16)
    sw_w = jax.vmap(_dup_kv_heads)(sw_w_qkv).astype(BF16)
    wo_w = w_o.astype(BF16)
    wg = _pad_ff(ffn_w_up[:, :, :D_FF].astype(BF16), 2)
    wv = _pad_ff(ffn_w_up[:, :, D_FF:].astype(BF16), 2)
    cwg = _pad_ff(ffn_conv_w[:, :, :D_FF], 2)
    cwv = _pad_ff(ffn_conv_w[:, :, D_FF:], 2)
    cbg = _pad_ff(ffn_conv_b[:, None, :D_FF], 2)
    cbv = _pad_ff(ffn_conv_b[:, None, D_FF:], 2)
    wd = _pad_ff(ffn_w_down.astype(BF16), 1)
    attn_gain = attn_norm[:, None, :]
    ffn_gain = ffn_norm[:, None, :]
    final_gain = final_norm[None, None, :]

    h = None
    for layer in range(DEPTH):
        mixer = layer % N_MIXERS
        slot = layer // N_MIXERS
        w_qkv = (da_w, sb_w, sw_w)[mixer]
        qkv = proj_call(x2, w_qkv, slot, attn_gain, 0) if layer == 0 else proj_call(h, w_qkv, slot)
        if mixer == 0:
            m = da_call(qkv, da_bias_near, da_lambda[slot], da_subln[slot], B, S, _lambda_init(layer))
        elif mixer == 1:
            m = sb_call(qkv, B, S)
        else:
            m = sw_call(qkv, sw_bias, sw_sinks[slot], B, S)
        x2, act = wo_ffn_up_call(m, wo_w, x2, ffn_gain, wg, wv, cwg, cwv, cbg, cbv, layer, S)
        if layer < DEPTH - 1:
            x2, h = ffn_down_call(act, wd, layer, x2, attn_gain, layer + 1, BF16, emit_x=True)
        else:
            h = ffn_down_call(act, wd, layer, x2, final_gain, 0, F32, emit_x=False)
    return h.reshape(B, S, D)
```

```python
import functools
import math

import jax
import jax.numpy as jnp
import numpy as np
from jax import lax
from jax.experimental import pallas as pl
from jax.experimental.pallas import tpu as pltpu

D_MODEL = 1024
DEPTH = 4
N_MIXERS = 3
N_BUCKETS = 32
MAX_DISTANCE = 128
DA_HEADS = 8
DA_QK_DIM = 64
DA_V_DIM = 128
SB_HEADS = 16
SB_DIM = 64
SW_Q_HEADS = 16
SW_KV_HEADS = 4
SW_DIM = 64
SW_BLOCK = 128
D_FF = 2752
EPS = 1e-6
NEG = -1e30

LANES = 128
SUBLANES = 8
VMEM_LIMIT = 52 * 1024 * 1024
WO_FFN_VMEM_LIMIT = 57 * 1024 * 1024

ATT_T = 256
SB_EXP_ZERO = -104.0
ROW_TILE = 512
PROJ_TM = 1024
PROJ_CHUNK = 512
PREP_ROWS = 256
FFN_TM = 1024
FFN_TF = 256
D_FF_PAD = 2816

BF16 = jnp.bfloat16
F32 = jnp.float32


def _params(*sem):
    return pltpu.CompilerParams(dimension_semantics=sem, vmem_limit_bytes=VMEM_LIMIT)


def _layer_spec(arr, layer, single_buffer=False):
    shape = (None,) + arr.shape[1:]
    index_map = lambda *_: (layer,) + (0,) * (arr.ndim - 1)
    if single_buffer:
        return pl.BlockSpec(shape, index_map, pipeline_mode=pl.Buffered(1))
    return pl.BlockSpec(shape, index_map)


def _rms(xf, gain):
    return xf * lax.rsqrt(jnp.mean(xf * xf, axis=-1, keepdims=True) + EPS) * gain


def _project(h, w_ref, o_ref):
    for c in range(o_ref.shape[1] // PROJ_CHUNK):
        sl = slice(c * PROJ_CHUNK, (c + 1) * PROJ_CHUNK)
        o_ref[:, sl] = jnp.dot(h, w_ref[:, sl], preferred_element_type=F32).astype(o_ref.dtype)


def _norm_proj_kernel(x_ref, g_ref, w_ref, o_ref):
    _project(_rms(x_ref[...], g_ref[...]).astype(BF16), w_ref, o_ref)


def proj_call(x2, w, layer, gain, gain_layer):
    M, D = x2.shape
    N = w.shape[2]
    row = lambda i: (i, 0)
    return pl.pallas_call(
        _norm_proj_kernel,
        out_shape=jax.ShapeDtypeStruct((M, N), BF16),
        grid=(M // PROJ_TM,),
        in_specs=[pl.BlockSpec((PROJ_TM, D), row), _layer_spec(gain, gain_layer),
                  _layer_spec(w, layer, single_buffer=True)],
        out_specs=pl.BlockSpec((PROJ_TM, N), row),
        compiler_params=_params("parallel"),
        name="qkv_proj",
    )(x2, gain, w)


def _ffn_row_starts(n_groups):
    per = n_groups // SUBLANES
    return [SUBLANES * SUBLANES * (j % per) + j // per for j in range(n_groups)]


def _stage(ref, x):
    for c in range(ref.shape[0]):
        ref[c] = x[:, c * LANES:(c + 1) * LANES]


def _stage_interleaved(ref, x):
    for c in range(ref.shape[0]):
        for j, st in enumerate(_ffn_row_starts(x.shape[0] // SUBLANES)):
            ref[c, pl.ds(st, SUBLANES, stride=SUBLANES), :] = x[j * SUBLANES:(j + 1) * SUBLANES,
                                                                c * LANES:(c + 1) * LANES]


def _deinterleave_rows(ref, c):
    return jnp.concatenate([ref[c, pl.ds(st, SUBLANES, stride=SUBLANES), :]
                            for st in _ffn_row_starts(ref.shape[1] // SUBLANES)], axis=0)


def _causal_conv3(u, cw, cb, prev):
    nblk, g, _, tf = u.shape
    first = lax.broadcasted_iota(jnp.int32, (SUBLANES, tf), 0) == 0
    u1, u2 = [], []
    for b in range(nblk):
        tail = prev if b == 0 else u[b - 1, g - 2:]
        wrap = [jnp.where(first, pltpu.roll(tail[e], 1, 0), pltpu.roll(u[b, g - 2 + e], 1, 0))
                for e in range(2)]
        u1.append(jnp.concatenate([wrap[1][None], u[b, :g - 1]], axis=0))
        u2.append(jnp.concatenate([wrap[0][None], wrap[1][None], u[b, :g - 2]], axis=0))
    u1 = jnp.stack(u1)
    u2 = jnp.stack(u2)
    return u * cw[2:3] + u1 * cw[1:2] + u2 * cw[0:1] + cb


def _wo_ffn_up_kernel(m_ref, wo_ref, x_ref, gain_ref, wg_ref, wv_ref, cwg_ref, cwv_ref, cbg_ref, cbv_ref,
                      xo_ref, g_ref, carry_ref, stage_ref, h_ref, *, tiles_per_seq):
    tm = m_ref.shape[0]
    nblk = tm // ROW_TILE
    g = ROW_TILE // SUBLANES

    @pl.when(pl.program_id(0) % tiles_per_seq == 0)
    def _():
        carry_ref[...] = jnp.zeros(carry_ref.shape, F32)

    for b in range(nblk):
        rows = slice(b * ROW_TILE, (b + 1) * ROW_TILE)
        xn = x_ref[rows, :] + jnp.dot(m_ref[rows, :], wo_ref[...], preferred_element_type=F32)
        xo_ref[rows, :] = xn
        _stage_interleaved(stage_ref, _rms(xn, gain_ref[...]))
        for c in range(stage_ref.shape[0]):
            h_ref[rows, c * LANES:(c + 1) * LANES] = stage_ref[c].astype(h_ref.dtype)

    h = h_ref[...]
    for c in range(g_ref.shape[1] // FFN_TF):
        sl = slice(c * FFN_TF, (c + 1) * FFN_TF)
        ug = jnp.dot(h, wg_ref[:, sl], preferred_element_type=F32).reshape(nblk, g, SUBLANES, FFN_TF)
        uv = jnp.dot(h, wv_ref[:, sl], preferred_element_type=F32).reshape(nblk, g, SUBLANES, FFN_TF)
        prev_g = carry_ref[0, :, :, sl]
        prev_v = carry_ref[1, :, :, sl]
        carry_ref[0, :, :, sl] = ug[nblk - 1, g - 2:]
        carry_ref[1, :, :, sl] = uv[nblk - 1, g - 2:]
        yg = _causal_conv3(ug, cwg_ref[:, sl], cbg_ref[:, sl], prev_g)
        yv = _causal_conv3(uv, cwv_ref[:, sl], cbv_ref[:, sl], prev_v)
        act = yg * (1.0 / (1.0 + jnp.exp(-yg))) * yv
        g_ref[:, sl] = act.reshape(tm, FFN_TF).astype(g_ref.dtype)


def wo_ffn_up_call(m, wo, x2, gain, wg, wv, cwg, cwv, cbg, cbv, layer, seq):
    M, D = x2.shape
    F = wg.shape[2]
    row = lambda i: (i, 0)
    resident = lambda arr: _layer_spec(arr, layer, single_buffer=True)
    return pl.pallas_call(
        functools.partial(_wo_ffn_up_kernel, tiles_per_seq=seq // FFN_TM),
        out_shape=(jax.ShapeDtypeStruct((M, D), F32), jax.ShapeDtypeStruct((M, F), BF16)),
        grid=(M // FFN_TM,),
        in_specs=[pl.BlockSpec((FFN_TM, D), row), resident(wo), pl.BlockSpec((FFN_TM, D), row), resident(gain),
                  resident(wg), resident(wv), resident(cwg), resident(cwv), resident(cbg), resident(cbv)],
        out_specs=(pl.BlockSpec((FFN_TM, D), row), pl.BlockSpec((FFN_TM, F), row)),
        scratch_shapes=[pltpu.VMEM((2, 2, SUBLANES, F), F32),
                        pltpu.VMEM((D // LANES, ROW_TILE, LANES), F32),
                        pltpu.VMEM((FFN_TM, D), BF16)],
        compiler_params=pltpu.CompilerParams(dimension_semantics=("arbitrary",),
                                             vmem_limit_bytes=WO_FFN_VMEM_LIMIT),
        name="wo_ffn_up_conv_gate",
    )(m, wo, x2, gain, wg, wv, cwg, cwv, cbg, cbv)


def _ffn_down_kernel(a_ref, w_ref, x_ref, g_ref, *refs, project):
    if project:
        wq_ref, xo_ref, qkv_ref, stage_ref = refs
    else:
        h_ref, stage_ref, xo_ref = refs
    _stage(stage_ref, jnp.dot(a_ref[...], w_ref[...], preferred_element_type=F32))
    for c in range(stage_ref.shape[0]):
        sl = slice(c * LANES, (c + 1) * LANES)
        xo_ref[:, sl] = x_ref[:, sl] + _deinterleave_rows(stage_ref, c)
    h = _rms(xo_ref[...], g_ref[...])
    if project:
        _project(h.astype(BF16), wq_ref, qkv_ref)
    else:
        h_ref[...] = h.astype(h_ref.dtype)


def ffn_down_call(a, w, layer, x2, gain, gain_layer, w_qkv=None, qkv_layer=0):
    M, D = x2.shape
    F = a.shape[1]
    row = lambda i: (i, 0)
    project = w_qkv is not None
    in_specs = [pl.BlockSpec((ROW_TILE, F), row), _layer_spec(w, layer, single_buffer=True),
                pl.BlockSpec((ROW_TILE, D), row), _layer_spec(gain, gain_layer)]
    stage = pltpu.VMEM((D // LANES, ROW_TILE, LANES), F32)
    if project:
        N = w_qkv.shape[2]
        in_specs.append(_layer_spec(w_qkv, qkv_layer, single_buffer=True))
        out_shape = (jax.ShapeDtypeStruct((M, D), F32), jax.ShapeDtypeStruct((M, N), BF16))
        out_specs = (pl.BlockSpec((ROW_TILE, D), row), pl.BlockSpec((ROW_TILE, N), row))
        scratch, args = [stage], (a, w, x2, gain, w_qkv)
    else:
        out_shape, out_specs = jax.ShapeDtypeStruct((M, D), F32), pl.BlockSpec((ROW_TILE, D), row)
        scratch, args = [stage, pltpu.VMEM((ROW_TILE, D), F32)], (a, w, x2, gain)
    return pl.pallas_call(
        functools.partial(_ffn_down_kernel, project=project),
        out_shape=out_shape,
        grid=(M // ROW_TILE,),
        in_specs=in_specs,
        out_specs=out_specs,
        scratch_shapes=scratch,
        compiler_params=_params("parallel"),
        name="ffn_down_residual_norm",
    )(*args)


def _half_masks(q):
    lane = lax.broadcasted_iota(jnp.int32, q.shape, 1)
    zero = jnp.zeros_like(q)
    scale = jnp.asarray(DA_QK_DIM ** -0.5, q.dtype)
    return (jnp.where(lane < 64, q, zero) * scale, jnp.where(lane >= 64, q, zero) * scale)


def _qk(q, k):
    return lax.dot_general(q, k, (((1,), (1,)), ((), ())), preferred_element_type=F32)


def _da_kernel(q_ref, k_ref, v_ref, bias_ref, lam_ref, sub_ref, o_ref, s_ref, *, lambda_init):
    T = ATT_T
    nq = q_ref.shape[0] // T
    r = lax.broadcasted_iota(jnp.int32, (T, T), 0)
    cc = lax.broadcasted_iota(jnp.int32, (T, T), 1)
    causal = r >= cc
    lam = lam_ref[...]
    lam_full = (jnp.exp(jnp.sum(lam[0:1] * lam[1:2], keepdims=True))
                - jnp.exp(jnp.sum(lam[2:3] * lam[3:4], keepdims=True)) + lambda_init)
    def score_steps(qi, st):
        def first():
            st["q01"] = jnp.concatenate(_half_masks(q_ref[qi * T:(qi + 1) * T, :]), axis=0)
            st["mx"] = [None, None]

        def tile(j):
            if j == 0:
                first()
            ss = _qk(st["q01"], k_ref[j * T:(j + 1) * T, :])
            for c in range(2):
                s = ss[c * T:(c + 1) * T]
                if j == qi - 1:
                    s = s + bias_ref[c, :, 0:T]
                if j == qi:
                    s = jnp.where(causal, s + bias_ref[c, :, T:2 * T], NEG)
                s_ref[qi % 2, c, :, j * T:(j + 1) * T] = s
                t = jnp.maximum(s[:, :LANES], s[:, LANES:])
                st["mx"][c] = t if st["mx"][c] is None else jnp.maximum(st["mx"][c], t)
        return [functools.partial(tile, j) for j in range(qi + 1)]

    def value_steps(qi, st):
        def first():
            st["m2"] = []
            for c in range(2):
                m = jnp.broadcast_to(jnp.max(st["mx"][c], axis=-1, keepdims=True), (T, LANES))
                st["m2"].append(jnp.concatenate([m, m], axis=1))
            st["lsum"] = [None, None]
            st["acc"] = None

        def tile(j):
            if j == 0:
                first()
            ps = []
            for c in range(2):
                p = jnp.exp(s_ref[qi % 2, c, :, j * T:(j + 1) * T] - st["m2"][c])
                t = p[:, :LANES] + p[:, LANES:]
                st["lsum"][c] = t if st["lsum"][c] is None else st["lsum"][c] + t
                ps.append(p.astype(BF16))
            pv = jnp.dot(jnp.concatenate(ps, axis=0), v_ref[j * T:(j + 1) * T, :],
                         preferred_element_type=F32)
            st["acc"] = pv if st["acc"] is None else st["acc"] + pv
            if j == qi:
                outs = [st["acc"][c * T:(c + 1) * T] / jnp.sum(st["lsum"][c], axis=-1, keepdims=True)
                        for c in range(2)]
                o = outs[0] - lam_full * outs[1]
                o = _rms(o, sub_ref[...]) * (1.0 - lambda_init)
                o_ref[qi * T:(qi + 1) * T, :] = o.astype(o_ref.dtype)
        return [functools.partial(tile, j) for j in range(qi + 1)]

    states = [dict() for _ in range(nq)]
    pending = []
    for qi in range(nq + 1):
        scores = score_steps(qi, states[qi]) if qi < nq else []
        for t in range(max(len(scores), len(pending))):
            if t < len(scores):
                scores[t]()
            if t < len(pending):
                pending[t]()
        pending = value_steps(qi, states[qi]) if qi < nq else []


def da_call(qkv, bias_near, lam, subln, batch, seq, lambda_init):
    T = ATT_T
    nh = DA_HEADS
    return pl.pallas_call(
        functools.partial(_da_kernel, lambda_init=lambda_init),
        out_shape=jax.ShapeDtypeStruct((batch * seq, nh * DA_V_DIM), BF16),
        grid=(batch, nh),
        in_specs=[pl.BlockSpec((seq, LANES), lambda b, h: (b, h)),
                  pl.BlockSpec((seq, LANES), lambda b, h: (b, nh + h)),
                  pl.BlockSpec((seq, LANES), lambda b, h: (b, 2 * nh + h)),
                  pl.BlockSpec((2, T, 2 * T), lambda b, h: (h, 0, 0)),
                  pl.BlockSpec((4, DA_QK_DIM), lambda b, h: (0, 0)),
                  pl.BlockSpec((1, DA_V_DIM), lambda b, h: (0, 0))],
        out_specs=pl.BlockSpec((seq, LANES), lambda b, h: (b, h)),
        scratch_shapes=[pltpu.VMEM((2, 2, T, seq), F32)],
        compiler_params=_params("parallel", "parallel"),
        name="diff_attention",
    )(qkv, qkv, qkv, bias_near, lam, subln.reshape(1, DA_V_DIM))


def _split2(x):
    hi = x.astype(BF16)
    mid = (x - hi.astype(F32)).astype(BF16)
    return hi, mid


def _sb_kernel(q_ref, k_ref, v_ref, o_ref, lbn_ref, hmn_ref, lbf_ref, hmf_ref, acc_ref):
    T = ATT_T
    nq = q_ref.shape[0] // T
    kr = lax.broadcasted_iota(jnp.int32, (T, T), 0)
    kc = lax.broadcasted_iota(jnp.int32, (T, T), 1)
    suffix = jnp.where(kr > kc, 1.0, 0.0).astype(BF16)
    suffix2 = jnp.concatenate([suffix, suffix], axis=0)
    strict = kc < kr
    lane = lax.broadcasted_iota(jnp.int32, (T, LANES), 1)
    strict2 = jnp.concatenate([strict, strict], axis=0)

    def score_tile(qi, q01, j, blk, carry, lb_view, hm_view):
        zz = _qk(q01, k_ref[j * T:(j + 1) * T, :])
        for c in range(2):
            z = zz[c * T:(c + 1) * T]
            lb = jnp.minimum(z, 0.0) - jnp.log(1.0 + jnp.exp(-jnp.abs(z)))
            log_1m_beta = lb - z
            if j == qi:
                log_1m_beta = jnp.where(strict, log_1m_beta, 0.0)
            hi, mid = _split2(log_1m_beta)
            rows = slice(blk + c * T, blk + (c + 1) * T)
            hm_view[rows, 0:T] = hi
            hm_view[rows, T:2 * T] = mid
            if carry[c] is not None:
                lb = lb + carry[c]
            lb_view[rows, :] = lb
            if j > 0:
                rs = jnp.sum(log_1m_beta, axis=-1, keepdims=True)
                carry[c] = rs if carry[c] is None else carry[c] + rs

    def value_tiles(qi, tiles, lb_view, hm_view):
        between = jnp.dot(hm_view[0:2 * len(tiles) * T, :], suffix2, preferred_element_type=F32)
        acc = None
        for pos, j in enumerate(tiles):
            rows = slice(2 * pos * T, 2 * (pos + 1) * T)
            a = jnp.exp(lb_view[rows, :] + between[rows])
            if j == qi:
                a = jnp.where(strict2, a, 0.0)
            pv = jnp.dot(a.astype(BF16), v_ref[j * T:(j + 1) * T, :], preferred_element_type=F32)
            acc = pv if acc is None else acc + pv
        return acc

    def stacked_q(qi):
        return jnp.concatenate(_half_masks(q_ref[qi * T:(qi + 1) * T, :]), axis=0)

    near = [[j for j in (qi, qi - 1) if j >= 0] for qi in range(nq)]
    carries = [[None, None] for _ in range(nq)]
    for qi in range(nq + 1):
        if qi < nq:
            q01 = stacked_q(qi)
            for pos, j in enumerate(near[qi]):
                score_tile(qi, q01, j, 2 * pos * T, carries[qi], lbn_ref.at[qi % 2], hmn_ref.at[qi % 2])
        if qi > 0:
            acc_ref[qi - 1] = value_tiles(qi - 1, near[qi - 1], lbn_ref.at[(qi - 1) % 2], hmn_ref.at[(qi - 1) % 2])

    for qi in range(2, nq):
        far = list(range(qi - 2, -1, -1))
        carry = carries[qi]
        nearest_sum = jnp.max(jnp.maximum(carry[0], carry[1]))

        @pl.when(nearest_sum > SB_EXP_ZERO)
        def _():
            q01 = stacked_q(qi)
            far_carry = list(carry)
            for pos, j in enumerate(far):
                score_tile(qi, q01, j, 2 * pos * T, far_carry, lbf_ref, hmf_ref)
            acc_ref[qi] += value_tiles(qi, far, lbf_ref, hmf_ref)

    for qi in range(nq):
        acc = acc_ref[qi]
        o_ref[qi * T:(qi + 1) * T, :] = jnp.where(lane < 64, acc[0:T], acc[T:2 * T]).astype(o_ref.dtype)


def sb_call(qkv, batch, seq):
    npair = SB_HEADS // 2
    return pl.pallas_call(
        _sb_kernel,
        out_shape=jax.ShapeDtypeStruct((batch * seq, SB_HEADS * SB_DIM), BF16),
        grid=(batch, npair),
        in_specs=[pl.BlockSpec((seq, LANES), lambda b, p: (b, p)),
                  pl.BlockSpec((seq, LANES), lambda b, p: (b, npair + p)),
                  pl.BlockSpec((seq, LANES), lambda b, p: (b, 2 * npair + p))],
        out_specs=pl.BlockSpec((seq, LANES), lambda b, p: (b, p)),
        scratch_shapes=[pltpu.VMEM((2, 4 * ATT_T, ATT_T), F32), pltpu.VMEM((2, 4 * ATT_T, 2 * ATT_T), BF16),
                        pltpu.VMEM((2 * seq - 4 * ATT_T, ATT_T), F32),
                        pltpu.VMEM((2 * seq - 4 * ATT_T, 2 * ATT_T), BF16),
                        pltpu.VMEM((seq // ATT_T, 2 * ATT_T, LANES), F32)],
        compiler_params=_params("parallel", "parallel"),
        name="stick_breaking_attention",
    )(qkv, qkv, qkv)


def _sw_kernel(sink_ref, q_ref, k_ref, v_ref, bias_ref, o_ref, s_ref):
    W = SW_BLOCK
    nb = q_ref.shape[0] // W
    p_id = pl.program_id(1)
    r = lax.broadcasted_iota(jnp.int32, (2 * W, 2 * W), 0) & (W - 1)
    cidx = lax.broadcasted_iota(jnp.int32, (2 * W, 2 * W), 1)
    valid = ((cidx < W) & (cidx > r)) | ((cidx >= W) & (cidx - W <= r))
    biasm = jnp.where(valid, bias_ref[...], NEG)
    row = lax.broadcasted_iota(jnp.int32, (2 * W, 1), 0)
    sink = jnp.where(row < W, sink_ref[2 * p_id], sink_ref[2 * p_id + 1])
    lane = lax.broadcasted_iota(jnp.int32, (W, LANES), 1)
    for n in range(nb):
        q01 = jnp.concatenate(_half_masks(q_ref[n * W:(n + 1) * W, :]), axis=0)
        if n == 0:
            s_ref[0, :, W:] = _qk(q01, k_ref[0:W, :]) + biasm[:, W:]
        else:
            s_ref[n] = _qk(q01, k_ref[(n - 1) * W:(n + 1) * W, :]) + biasm
    for n in range(nb):
        s = s_ref[0, :, W:] if n == 0 else s_ref[n]
        m = jnp.maximum(jnp.max(s, axis=-1, keepdims=True), sink)
        e = jnp.exp(s - m)
        denom = jnp.sum(e, axis=-1, keepdims=True) + jnp.exp(sink - m)
        v = v_ref[0:W, :] if n == 0 else v_ref[(n - 1) * W:(n + 1) * W, :]
        o = jnp.dot(e.astype(BF16), v, preferred_element_type=F32) / denom
        o_ref[n * W:(n + 1) * W, :] = jnp.where(lane < 64, o[0:W], o[W:]).astype(o_ref.dtype)


def sw_call(qkv, bias_band, sinks, batch, seq):
    npair = SW_Q_HEADS // 2
    q_blocks = SW_Q_HEADS * SW_DIM // LANES
    W = SW_BLOCK
    return pl.pallas_call(
        _sw_kernel,
        out_shape=jax.ShapeDtypeStruct((batch * seq, SW_Q_HEADS * SW_DIM), BF16),
        grid=(batch, npair),
        in_specs=[pl.BlockSpec(memory_space=pltpu.SMEM),
                  pl.BlockSpec((seq, LANES), lambda b, p: (b, p)),
                  pl.BlockSpec((seq, LANES), lambda b, p: (b, q_blocks + p // 2)),
                  pl.BlockSpec((seq, LANES), lambda b, p: (b, q_blocks + SW_KV_HEADS + p // 2)),
                  pl.BlockSpec((None, 2 * W, 2 * W), lambda b, p: (p, 0, 0))],
        out_specs=pl.BlockSpec((seq, LANES), lambda b, p: (b, p)),
        scratch_shapes=[pltpu.VMEM((seq // W, 2 * W, 2 * W), F32)],
        compiler_params=_params("parallel", "parallel"),
        name="sliding_window_attention",
    )(sinks, qkv, qkv, qkv, bias_band.reshape(npair, 2 * W, 2 * W))


def _t5_bucket(dist):
    max_exact = N_BUCKETS // 2
    d = jnp.maximum(dist, 0)
    large = max_exact + (jnp.log(jnp.maximum(d, 1).astype(F32) / max_exact)
                         / math.log(MAX_DISTANCE / max_exact) * (N_BUCKETS - max_exact)).astype(jnp.int32)
    large = jnp.minimum(large, N_BUCKETS - 1)
    return jnp.where(d < max_exact, d, large)


def _band_kernel(row_ref, o_ref):
    rows = jnp.broadcast_to(row_ref[...], (o_ref.shape[0], row_ref.shape[1]))
    o_ref[...] = pltpu.roll(rows, 0, 1, stride=1, stride_axis=0)[:, :o_ref.shape[1]]


def _bias_band(table, T):
    y = np.arange(3 * T)
    dist = np.clip(np.where(y <= 2 * T, T - y, 4 * T - y), 0, 2 * T - 1)
    row = table.astype(F32)[_t5_bucket(jnp.asarray(dist, jnp.int32))].T
    ch = row.shape[0]
    return pl.pallas_call(
        _band_kernel,
        out_shape=jax.ShapeDtypeStruct((ch, T, 2 * T), F32),
        grid=(ch,),
        in_specs=[pl.BlockSpec((None, 1, 3 * T), lambda c: (c, 0, 0))],
        out_specs=pl.BlockSpec((None, T, 2 * T), lambda c: (c, 0, 0)),
        compiler_params=_params("parallel"),
        name="bias_band",
    )(row[:, None, :])


def _lambda_init(layer):
    return 0.8 - 0.6 * math.exp(-0.3 * layer)


def _dup_kv_heads(w):
    q_w = SW_Q_HEADS * SW_DIM
    kv = w[:, q_w:].reshape(w.shape[0], 2 * SW_KV_HEADS, 1, SW_DIM)
    kv = jnp.broadcast_to(kv, (w.shape[0], 2 * SW_KV_HEADS, 2, SW_DIM)).reshape(w.shape[0], -1)
    return jnp.concatenate([w[:, :q_w], kv], axis=1)


def _pad_ff(a, axis):
    pad = [(0, 0)] * a.ndim
    pad[axis] = (0, D_FF_PAD - D_FF)
    return jnp.pad(a, pad)


def _split_up_kernel(w_ref, gate_ref, value_ref):
    keep = lax.broadcasted_iota(jnp.int32, gate_ref.shape, 1) < D_FF
    gate_ref[...] = jnp.where(keep, w_ref[:, :D_FF_PAD], 0.0).astype(gate_ref.dtype)
    start = 2 * D_FF - D_FF_PAD
    window = pltpu.roll(w_ref[:, start:], D_FF_PAD - (D_FF - start), 1)
    value_ref[...] = jnp.where(keep, window, 0.0).astype(value_ref.dtype)


def split_up_call(w_up):
    L, D, _ = w_up.shape
    out = jax.ShapeDtypeStruct((L, D, D_FF_PAD), BF16)
    out_spec = pl.BlockSpec((None, PREP_ROWS, D_FF_PAD), lambda l, i: (l, i, 0))
    return pl.pallas_call(
        _split_up_kernel,
        out_shape=(out, out),
        grid=(L, D // PREP_ROWS),
        in_specs=[pl.BlockSpec((None, PREP_ROWS, 2 * D_FF), lambda l, i: (l, i, 0))],
        out_specs=(out_spec, out_spec),
        compiler_params=_params("parallel", "parallel"),
        name="ffn_up_weight_prep",
    )(w_up)


def _pad_down_kernel(w_ref, o_ref):
    row = pl.program_id(1) * PREP_ROWS + lax.broadcasted_iota(jnp.int32, o_ref.shape, 0)
    o_ref[...] = jnp.where(row < D_FF, w_ref[...], 0.0).astype(o_ref.dtype)


def pad_down_call(w_down):
    L, _, D = w_down.shape
    spec = pl.BlockSpec((None, PREP_ROWS, D), lambda l, i: (l, i, 0))
    return pl.pallas_call(
        _pad_down_kernel,
        out_shape=jax.ShapeDtypeStruct((L, D_FF_PAD, D), BF16),
        grid=(L, D_FF_PAD // PREP_ROWS),
        in_specs=[spec],
        out_specs=spec,
        compiler_params=_params("parallel", "parallel"),
        name="ffn_down_weight_prep",
    )(w_down)


def kernel(x, rel_bias, attn_norm, ffn_norm, w_o, da_w_qkv, da_lambda, da_subln, sb_w_qkv, sw_w_qkv,
           sw_sinks, ffn_w_up, ffn_conv_w, ffn_conv_b, ffn_w_down, final_norm):
    B, S, D = x.shape
    x2 = x.reshape(B * S, D)
    da_bias_near = _bias_band(rel_bias - rel_bias[N_BUCKETS - 1], ATT_T)
    sw_bias = _bias_band(rel_bias, SW_BLOCK)
    da_w = da_w_qkv.astype(BF16)
    sb_w = sb_w_qkv.astype(BF16)
    sw_w = jax.vmap(_dup_kv_heads)(sw_w_qkv).astype(BF16)
    wo_w = w_o.astype(BF16)
    wg, wv = split_up_call(ffn_w_up)
    cwg = _pad_ff(ffn_conv_w[:, :, :D_FF], 2)
    cwv = _pad_ff(ffn_conv_w[:, :, D_FF:], 2)
    cbg = _pad_ff(ffn_conv_b[:, None, :D_FF], 2)
    cbv = _pad_ff(ffn_conv_b[:, None, D_FF:], 2)
    wd = pad_down_call(ffn_w_down)
    attn_gain = attn_norm[:, None, :]
    ffn_gain = ffn_norm[:, None, :]
    final_gain = final_norm[None, None, :]

    weights_qkv = [((da_w, sb_w, sw_w)[layer % N_MIXERS], layer // N_MIXERS) for layer in range(DEPTH)]
    qkv = proj_call(x2, *weights_qkv[0], attn_gain, 0)
    for layer in range(DEPTH):
        mixer = layer % N_MIXERS
        slot = layer // N_MIXERS
        if mixer == 0:
            m = da_call(qkv, da_bias_near, da_lambda[slot], da_subln[slot], B, S, _lambda_init(layer))
        elif mixer == 1:
            m = sb_call(qkv, B, S)
        else:
            m = sw_call(qkv, sw_bias, sw_sinks[slot], B, S)
        x2, act = wo_ffn_up_call(m, wo_w, x2, ffn_gain, wg, wv, cwg, cwv, cbg, cbv, layer, S)
        if layer < DEPTH - 1:
            x2, qkv = ffn_down_call(act, wd, layer, x2, attn_gain, layer + 1, *weights_qkv[layer + 1])
        else:
            h = ffn_down_call(act, wd, layer, x2, final_gain, 0)
    return h.reshape(B, S, D)
```

```python
import functools
import math

import jax
import jax.numpy as jnp
import numpy as np
from jax import lax
from jax.experimental import pallas as pl
from jax.experimental.pallas import tpu as pltpu

D_MODEL = 1024
DEPTH = 4
N_MIXERS = 3
N_BUCKETS = 32
MAX_DISTANCE = 128
DA_HEADS = 8
DA_QK_DIM = 64
DA_V_DIM = 128
SB_HEADS = 16
SB_DIM = 64
SW_Q_HEADS = 16
SW_KV_HEADS = 4
SW_DIM = 64
SW_BLOCK = 128
D_FF = 2752
EPS = 1e-6
NEG = -1e30

LANES = 128
SUBLANES = 8
VMEM_LIMIT = 52 * 1024 * 1024
WO_FFN_VMEM_LIMIT = 57 * 1024 * 1024

ATT_T = 256
SB_EXP_ZERO = -104.0
ROW_TILE = 512
PROJ_TM = 1024
PROJ_CHUNK = 512
PREP_ROWS = 256
PREP_DOWN_ROWS = 704
SW_LAG = 2
FFN_TM = 1024
FFN_TF = 256
D_FF_PAD = 2816

BF16 = jnp.bfloat16
F32 = jnp.float32


def _params(*sem):
    return pltpu.CompilerParams(dimension_semantics=sem, vmem_limit_bytes=VMEM_LIMIT)


def _layer_spec(arr, layer, single_buffer=False):
    shape = (None,) + arr.shape[1:]
    index_map = lambda *_: (layer,) + (0,) * (arr.ndim - 1)
    if single_buffer:
        return pl.BlockSpec(shape, index_map, pipeline_mode=pl.Buffered(1))
    return pl.BlockSpec(shape, index_map)


def _rms(xf, gain):
    return xf * lax.rsqrt(jnp.mean(xf * xf, axis=-1, keepdims=True) + EPS) * gain


def _project(h, w_ref, o_ref):
    for c in range(o_ref.shape[1] // PROJ_CHUNK):
        sl = slice(c * PROJ_CHUNK, (c + 1) * PROJ_CHUNK)
        o_ref[:, sl] = jnp.dot(h, w_ref[:, sl], preferred_element_type=F32).astype(o_ref.dtype)


def _norm_proj_kernel(x_ref, g_ref, w_ref, o_ref):
    _project(_rms(x_ref[...], g_ref[...]).astype(BF16), w_ref, o_ref)


def proj_call(x2, w, layer, gain, gain_layer):
    M, D = x2.shape
    N = w.shape[2]
    row = lambda i: (i, 0)
    return pl.pallas_call(
        _norm_proj_kernel,
        out_shape=jax.ShapeDtypeStruct((M, N), BF16),
        grid=(M // PROJ_TM,),
        in_specs=[pl.BlockSpec((PROJ_TM, D), row), _layer_spec(gain, gain_layer),
                  _layer_spec(w, layer, single_buffer=True)],
        out_specs=pl.BlockSpec((PROJ_TM, N), row),
        compiler_params=_params("parallel"),
        name="qkv_proj",
    )(x2, gain, w)


def _ffn_row_starts(n_groups):
    per = n_groups // SUBLANES
    return [SUBLANES * SUBLANES * (j % per) + j // per for j in range(n_groups)]


def _stage(ref, x):
    for c in range(ref.shape[0]):
        ref[c] = x[:, c * LANES:(c + 1) * LANES]


def _stage_interleaved(ref, x):
    for c in range(ref.shape[0]):
        for j, st in enumerate(_ffn_row_starts(x.shape[0] // SUBLANES)):
            ref[c, pl.ds(st, SUBLANES, stride=SUBLANES), :] = x[j * SUBLANES:(j + 1) * SUBLANES,
                                                                c * LANES:(c + 1) * LANES]


def _deinterleave_rows(ref, c):
    return jnp.concatenate([ref[c, pl.ds(st, SUBLANES, stride=SUBLANES), :]
                            for st in _ffn_row_starts(ref.shape[1] // SUBLANES)], axis=0)


def _causal_conv3(u, cw, cb, prev):
    nblk, g, _, tf = u.shape
    first = lax.broadcasted_iota(jnp.int32, (SUBLANES, tf), 0) == 0
    u1, u2 = [], []
    for b in range(nblk):
        tail = prev if b == 0 else u[b - 1, g - 2:]
        wrap = [jnp.where(first, pltpu.roll(tail[e], 1, 0), pltpu.roll(u[b, g - 2 + e], 1, 0))
                for e in range(2)]
        u1.append(jnp.concatenate([wrap[1][None], u[b, :g - 1]], axis=0))
        u2.append(jnp.concatenate([wrap[0][None], wrap[1][None], u[b, :g - 2]], axis=0))
    u1 = jnp.stack(u1)
    u2 = jnp.stack(u2)
    return u * cw[2:3] + u1 * cw[1:2] + u2 * cw[0:1] + cb


def _wo_ffn_up_kernel(m_ref, wo_ref, x_ref, gain_ref, wg_ref, wv_ref, cwg_ref, cwv_ref, cbg_ref, cbv_ref,
                      xo_ref, g_ref, carry_ref, stage_ref, h_ref, *, tiles_per_seq):
    tm = m_ref.shape[0]
    nblk = tm // ROW_TILE
    g = ROW_TILE // SUBLANES

    @pl.when(pl.program_id(0) % tiles_per_seq == 0)
    def _():
        carry_ref[...] = jnp.zeros(carry_ref.shape, F32)

    for b in range(nblk):
        rows = slice(b * ROW_TILE, (b + 1) * ROW_TILE)
        xn = x_ref[rows, :] + jnp.dot(m_ref[rows, :], wo_ref[...], preferred_element_type=F32)
        xo_ref[rows, :] = xn
        _stage_interleaved(stage_ref, _rms(xn, gain_ref[...]))
        for c in range(stage_ref.shape[0]):
            h_ref[rows, c * LANES:(c + 1) * LANES] = stage_ref[c].astype(h_ref.dtype)

    h = h_ref[...]
    for c in range(g_ref.shape[1] // FFN_TF):
        sl = slice(c * FFN_TF, (c + 1) * FFN_TF)
        ug = jnp.dot(h, wg_ref[:, sl], preferred_element_type=F32).reshape(nblk, g, SUBLANES, FFN_TF)
        uv = jnp.dot(h, wv_ref[:, sl], preferred_element_type=F32).reshape(nblk, g, SUBLANES, FFN_TF)
        prev_g = carry_ref[0, :, :, sl]
        prev_v = carry_ref[1, :, :, sl]
        carry_ref[0, :, :, sl] = ug[nblk - 1, g - 2:]
        carry_ref[1, :, :, sl] = uv[nblk - 1, g - 2:]
        yg = _causal_conv3(ug, cwg_ref[:, sl], cbg_ref[:, sl], prev_g)
        yv = _causal_conv3(uv, cwv_ref[:, sl], cbv_ref[:, sl], prev_v)
        act = yg * (1.0 / (1.0 + jnp.exp(-yg))) * yv
        g_ref[:, sl] = act.reshape(tm, FFN_TF).astype(g_ref.dtype)


def wo_ffn_up_call(m, wo, x2, gain, wg, wv, cwg, cwv, cbg, cbv, layer, seq):
    M, D = x2.shape
    F = wg.shape[2]
    row = lambda i: (i, 0)
    resident = lambda arr: _layer_spec(arr, layer, single_buffer=True)
    return pl.pallas_call(
        functools.partial(_wo_ffn_up_kernel, tiles_per_seq=seq // FFN_TM),
        out_shape=(jax.ShapeDtypeStruct((M, D), F32), jax.ShapeDtypeStruct((M, F), BF16)),
        grid=(M // FFN_TM,),
        in_specs=[pl.BlockSpec((FFN_TM, D), row), resident(wo), pl.BlockSpec((FFN_TM, D), row), resident(gain),
                  resident(wg), resident(wv), resident(cwg), resident(cwv), resident(cbg), resident(cbv)],
        out_specs=(pl.BlockSpec((FFN_TM, D), row), pl.BlockSpec((FFN_TM, F), row)),
        scratch_shapes=[pltpu.VMEM((2, 2, SUBLANES, F), F32),
                        pltpu.VMEM((D // LANES, ROW_TILE, LANES), F32),
                        pltpu.VMEM((FFN_TM, D), BF16)],
        compiler_params=pltpu.CompilerParams(dimension_semantics=("arbitrary",),
                                             vmem_limit_bytes=WO_FFN_VMEM_LIMIT),
        name="wo_ffn_up_conv_gate",
    )(m, wo, x2, gain, wg, wv, cwg, cwv, cbg, cbv)


def _ffn_down_kernel(a_ref, w_ref, x_ref, g_ref, *refs, project):
    if project:
        wq_ref, xo_ref, qkv_ref, stage_ref = refs
    else:
        h_ref, stage_ref, xo_ref = refs
    _stage(stage_ref, jnp.dot(a_ref[...], w_ref[...], preferred_element_type=F32))
    for c in range(stage_ref.shape[0]):
        sl = slice(c * LANES, (c + 1) * LANES)
        xo_ref[:, sl] = x_ref[:, sl] + _deinterleave_rows(stage_ref, c)
    h = _rms(xo_ref[...], g_ref[...])
    if project:
        _project(h.astype(BF16), wq_ref, qkv_ref)
    else:
        h_ref[...] = h.astype(h_ref.dtype)


def ffn_down_call(a, w, layer, x2, gain, gain_layer, w_qkv=None, qkv_layer=0):
    M, D = x2.shape
    F = a.shape[1]
    row = lambda i: (i, 0)
    project = w_qkv is not None
    in_specs = [pl.BlockSpec((ROW_TILE, F), row), _layer_spec(w, layer, single_buffer=True),
                pl.BlockSpec((ROW_TILE, D), row), _layer_spec(gain, gain_layer)]
    stage = pltpu.VMEM((D // LANES, ROW_TILE, LANES), F32)
    if project:
        N = w_qkv.shape[2]
        in_specs.append(_layer_spec(w_qkv, qkv_layer, single_buffer=True))
        out_shape = (jax.ShapeDtypeStruct((M, D), F32), jax.ShapeDtypeStruct((M, N), BF16))
        out_specs = (pl.BlockSpec((ROW_TILE, D), row), pl.BlockSpec((ROW_TILE, N), row))
        scratch, args = [stage], (a, w, x2, gain, w_qkv)
    else:
        out_shape, out_specs = jax.ShapeDtypeStruct((M, D), F32), pl.BlockSpec((ROW_TILE, D), row)
        scratch, args = [stage, pltpu.VMEM((ROW_TILE, D), F32)], (a, w, x2, gain)
    return pl.pallas_call(
        functools.partial(_ffn_down_kernel, project=project),
        out_shape=out_shape,
        grid=(M // ROW_TILE,),
        in_specs=in_specs,
        out_specs=out_specs,
        scratch_shapes=scratch,
        compiler_params=_params("parallel"),
        name="ffn_down_residual_norm",
    )(*args)


def _half_masks(q):
    lane = lax.broadcasted_iota(jnp.int32, q.shape, 1)
    zero = jnp.zeros_like(q)
    scale = jnp.asarray(DA_QK_DIM ** -0.5, q.dtype)
    return (jnp.where(lane < 64, q, zero) * scale, jnp.where(lane >= 64, q, zero) * scale)


def _qk(q, k):
    return lax.dot_general(q, k, (((1,), (1,)), ((), ())), preferred_element_type=F32)


def _da_kernel(q_ref, k_ref, v_ref, bias_ref, lam_ref, sub_ref, o_ref, s_ref, *, lambda_init):
    T = ATT_T
    nq = q_ref.shape[0] // T
    r = lax.broadcasted_iota(jnp.int32, (T, T), 0)
    cc = lax.broadcasted_iota(jnp.int32, (T, T), 1)
    causal = r >= cc
    lam = lam_ref[...]
    lam_full = (jnp.exp(jnp.sum(lam[0:1] * lam[1:2], keepdims=True))
                - jnp.exp(jnp.sum(lam[2:3] * lam[3:4], keepdims=True)) + lambda_init)
    def score_steps(qi, st):
        def first():
            st["q01"] = jnp.concatenate(_half_masks(q_ref[qi * T:(qi + 1) * T, :]), axis=0)
            st["mx"] = [None, None]

        def tile(j):
            if j == 0:
                first()
            ss = _qk(st["q01"], k_ref[j * T:(j + 1) * T, :])
            for c in range(2):
                s = ss[c * T:(c + 1) * T]
                if j == qi - 1:
                    s = s + bias_ref[c, :, 0:T]
                if j == qi:
                    s = jnp.where(causal, s + bias_ref[c, :, T:2 * T], NEG)
                s_ref[qi % 2, c, :, j * T:(j + 1) * T] = s
                t = jnp.maximum(s[:, :LANES], s[:, LANES:])
                st["mx"][c] = t if st["mx"][c] is None else jnp.maximum(st["mx"][c], t)
        return [functools.partial(tile, j) for j in range(qi + 1)]

    def value_steps(qi, st):
        def first():
            st["m2"] = []
            for c in range(2):
                m = jnp.broadcast_to(jnp.max(st["mx"][c], axis=-1, keepdims=True), (T, LANES))
                st["m2"].append(jnp.concatenate([m, m], axis=1))
            st["lsum"] = [None, None]
            st["acc"] = None

        def tile(j):
            if j == 0:
                first()
            ps = []
            for c in range(2):
                p = jnp.exp(s_ref[qi % 2, c, :, j * T:(j + 1) * T] - st["m2"][c])
                t = p[:, :LANES] + p[:, LANES:]
                st["lsum"][c] = t if st["lsum"][c] is None else st["lsum"][c] + t
                ps.append(p.astype(BF16))
            pv = jnp.dot(jnp.concatenate(ps, axis=0), v_ref[j * T:(j + 1) * T, :],
                         preferred_element_type=F32)
            st["acc"] = pv if st["acc"] is None else st["acc"] + pv
            if j == qi:
                outs = [st["acc"][c * T:(c + 1) * T] / jnp.sum(st["lsum"][c], axis=-1, keepdims=True)
                        for c in range(2)]
                o = outs[0] - lam_full * outs[1]
                o = _rms(o, sub_ref[...]) * (1.0 - lambda_init)
                o_ref[qi * T:(qi + 1) * T, :] = o.astype(o_ref.dtype)
        return [functools.partial(tile, j) for j in range(qi + 1)]

    states = [dict() for _ in range(nq)]
    pending = []
    for qi in range(nq + 1):
        scores = score_steps(qi, states[qi]) if qi < nq else []
        for t in range(max(len(scores), len(pending))):
            if t < len(scores):
                scores[t]()
            if t < len(pending):
                pending[t]()
        pending = value_steps(qi, states[qi]) if qi < nq else []


def da_call(qkv, bias_near, lam, subln, batch, seq, lambda_init):
    T = ATT_T
    nh = DA_HEADS
    return pl.pallas_call(
        functools.partial(_da_kernel, lambda_init=lambda_init),
        out_shape=jax.ShapeDtypeStruct((batch * seq, nh * DA_V_DIM), BF16),
        grid=(batch, nh),
        in_specs=[pl.BlockSpec((seq, LANES), lambda b, h: (b, h)),
                  pl.BlockSpec((seq, LANES), lambda b, h: (b, nh + h)),
                  pl.BlockSpec((seq, LANES), lambda b, h: (b, 2 * nh + h)),
                  pl.BlockSpec((2, T, 2 * T), lambda b, h: (h, 0, 0)),
                  pl.BlockSpec((4, DA_QK_DIM), lambda b, h: (0, 0)),
                  pl.BlockSpec((1, DA_V_DIM), lambda b, h: (0, 0))],
        out_specs=pl.BlockSpec((seq, LANES), lambda b, h: (b, h)),
        scratch_shapes=[pltpu.VMEM((2, 2, T, seq), F32)],
        compiler_params=_params("parallel", "parallel"),
        name="diff_attention",
    )(qkv, qkv, qkv, bias_near, lam, subln.reshape(1, DA_V_DIM))


def _split2(x):
    hi = x.astype(BF16)
    mid = (x - hi.astype(F32)).astype(BF16)
    return hi, mid


def _sb_kernel(q_ref, k_ref, v_ref, o_ref, lbn_ref, hmn_ref, lbf_ref, hmf_ref, acc_ref):
    T = ATT_T
    nq = q_ref.shape[0] // T
    kr = lax.broadcasted_iota(jnp.int32, (T, T), 0)
    kc = lax.broadcasted_iota(jnp.int32, (T, T), 1)
    suffix = jnp.where(kr > kc, 1.0, 0.0).astype(BF16)
    suffix2 = jnp.concatenate([suffix, suffix], axis=0)
    strict = kc < kr
    lane = lax.broadcasted_iota(jnp.int32, (T, LANES), 1)
    strict2 = jnp.concatenate([strict, strict], axis=0)

    def score_tile(qi, q01, j, blk, carry, lb_view, hm_view):
        zz = _qk(q01, k_ref[j * T:(j + 1) * T, :])
        for c in range(2):
            z = zz[c * T:(c + 1) * T]
            lb = jnp.minimum(z, 0.0) - jnp.log(1.0 + jnp.exp(-jnp.abs(z)))
            log_1m_beta = lb - z
            if j == qi:
                log_1m_beta = jnp.where(strict, log_1m_beta, 0.0)
            hi, mid = _split2(log_1m_beta)
            rows = slice(blk + c * T, blk + (c + 1) * T)
            hm_view[rows, 0:T] = hi
            hm_view[rows, T:2 * T] = mid
            if carry[c] is not None:
                lb = lb + carry[c]
            lb_view[rows, :] = lb
            if j > 0:
                rs = jnp.sum(log_1m_beta, axis=-1, keepdims=True)
                carry[c] = rs if carry[c] is None else carry[c] + rs

    def value_tiles(qi, tiles, lb_view, hm_view):
        between = jnp.dot(hm_view[0:2 * len(tiles) * T, :], suffix2, preferred_element_type=F32)
        acc = None
        for pos, j in enumerate(tiles):
            rows = slice(2 * pos * T, 2 * (pos + 1) * T)
            a = jnp.exp(lb_view[rows, :] + between[rows])
            if j == qi:
                a = jnp.where(strict2, a, 0.0)
            pv = jnp.dot(a.astype(BF16), v_ref[j * T:(j + 1) * T, :], preferred_element_type=F32)
            acc = pv if acc is None else acc + pv
        return acc

    def stacked_q(qi):
        return jnp.concatenate(_half_masks(q_ref[qi * T:(qi + 1) * T, :]), axis=0)

    near = [[j for j in (qi, qi - 1) if j >= 0] for qi in range(nq)]
    carries = [[None, None] for _ in range(nq)]
    for qi in range(nq + 1):
        if qi < nq:
            q01 = stacked_q(qi)
            for pos, j in enumerate(near[qi]):
                score_tile(qi, q01, j, 2 * pos * T, carries[qi], lbn_ref.at[qi % 2], hmn_ref.at[qi % 2])
        if qi > 0:
            acc_ref[qi - 1] = value_tiles(qi - 1, near[qi - 1], lbn_ref.at[(qi - 1) % 2], hmn_ref.at[(qi - 1) % 2])

    for qi in range(2, nq):
        far = list(range(qi - 2, -1, -1))
        carry = carries[qi]
        nearest_sum = jnp.max(jnp.maximum(carry[0], carry[1]))

        @pl.when(nearest_sum > SB_EXP_ZERO)
        def _():
            q01 = stacked_q(qi)
            far_carry = list(carry)
            for pos, j in enumerate(far):
                score_tile(qi, q01, j, 2 * pos * T, far_carry, lbf_ref, hmf_ref)
            acc_ref[qi] += value_tiles(qi, far, lbf_ref, hmf_ref)

    for qi in range(nq):
        acc = acc_ref[qi]
        o_ref[qi * T:(qi + 1) * T, :] = jnp.where(lane < 64, acc[0:T], acc[T:2 * T]).astype(o_ref.dtype)


def sb_call(qkv, batch, seq):
    npair = SB_HEADS // 2
    return pl.pallas_call(
        _sb_kernel,
        out_shape=jax.ShapeDtypeStruct((batch * seq, SB_HEADS * SB_DIM), BF16),
        grid=(batch, npair),
        in_specs=[pl.BlockSpec((seq, LANES), lambda b, p: (b, p)),
                  pl.BlockSpec((seq, LANES), lambda b, p: (b, npair + p)),
                  pl.BlockSpec((seq, LANES), lambda b, p: (b, 2 * npair + p))],
        out_specs=pl.BlockSpec((seq, LANES), lambda b, p: (b, p)),
        scratch_shapes=[pltpu.VMEM((2, 4 * ATT_T, ATT_T), F32), pltpu.VMEM((2, 4 * ATT_T, 2 * ATT_T), BF16),
                        pltpu.VMEM((2 * seq - 4 * ATT_T, ATT_T), F32),
                        pltpu.VMEM((2 * seq - 4 * ATT_T, 2 * ATT_T), BF16),
                        pltpu.VMEM((seq // ATT_T, 2 * ATT_T, LANES), F32)],
        compiler_params=_params("parallel", "parallel"),
        name="stick_breaking_attention",
    )(qkv, qkv, qkv)


def _sw_kernel(sink_ref, q_ref, k_ref, v_ref, bias_ref, o_ref, s_ref):
    W = SW_BLOCK
    nb = q_ref.shape[0] // W
    p_id = pl.program_id(1)
    r = lax.broadcasted_iota(jnp.int32, (2 * W, 2 * W), 0) & (W - 1)
    cidx = lax.broadcasted_iota(jnp.int32, (2 * W, 2 * W), 1)
    valid = ((cidx < W) & (cidx > r)) | ((cidx >= W) & (cidx - W <= r))
    biasm = jnp.where(valid, bias_ref[...], NEG)
    row = lax.broadcasted_iota(jnp.int32, (2 * W, 1), 0)
    sink = jnp.where(row < W, sink_ref[2 * p_id], sink_ref[2 * p_id + 1])
    lane = lax.broadcasted_iota(jnp.int32, (W, LANES), 1)
    def scores(n):
        q01 = jnp.concatenate(_half_masks(q_ref[n * W:(n + 1) * W, :]), axis=0)
        if n == 0:
            s_ref[0, :, W:] = _qk(q01, k_ref[0:W, :]) + biasm[:, W:]
        else:
            s_ref[n] = _qk(q01, k_ref[(n - 1) * W:(n + 1) * W, :]) + biasm

    def values(n):
        s = s_ref[0, :, W:] if n == 0 else s_ref[n]
        m = jnp.maximum(jnp.max(s, axis=-1, keepdims=True), sink)
        e = jnp.exp(s - m)
        denom = jnp.sum(e, axis=-1, keepdims=True) + jnp.exp(sink - m)
        v = v_ref[0:W, :] if n == 0 else v_ref[(n - 1) * W:(n + 1) * W, :]
        o = jnp.dot(e.astype(BF16), v, preferred_element_type=F32) / denom
        o_ref[n * W:(n + 1) * W, :] = jnp.where(lane < 64, o[0:W], o[W:]).astype(o_ref.dtype)

    for n in range(nb + SW_LAG):
        if n < nb:
            scores(n)
        if n >= SW_LAG:
            values(n - SW_LAG)


def sw_call(qkv, bias_band, sinks, batch, seq):
    npair = SW_Q_HEADS // 2
    q_blocks = SW_Q_HEADS * SW_DIM // LANES
    W = SW_BLOCK
    return pl.pallas_call(
        _sw_kernel,
        out_shape=jax.ShapeDtypeStruct((batch * seq, SW_Q_HEADS * SW_DIM), BF16),
        grid=(batch, npair),
        in_specs=[pl.BlockSpec(memory_space=pltpu.SMEM),
                  pl.BlockSpec((seq, LANES), lambda b, p: (b, p)),
                  pl.BlockSpec((seq, LANES), lambda b, p: (b, q_blocks + p // 2)),
                  pl.BlockSpec((seq, LANES), lambda b, p: (b, q_blocks + SW_KV_HEADS + p // 2)),
                  pl.BlockSpec((None, 2 * W, 2 * W), lambda b, p: (p, 0, 0))],
        out_specs=pl.BlockSpec((seq, LANES), lambda b, p: (b, p)),
        scratch_shapes=[pltpu.VMEM((seq // W, 2 * W, 2 * W), F32)],
        compiler_params=_params("parallel", "parallel"),
        name="sliding_window_attention",
    )(sinks, qkv, qkv, qkv, bias_band.reshape(npair, 2 * W, 2 * W))


def _t5_bucket(dist):
    max_exact = N_BUCKETS // 2
    d = jnp.maximum(dist, 0)
    large = max_exact + (jnp.log(jnp.maximum(d, 1).astype(F32) / max_exact)
                         / math.log(MAX_DISTANCE / max_exact) * (N_BUCKETS - max_exact)).astype(jnp.int32)
    large = jnp.minimum(large, N_BUCKETS - 1)
    return jnp.where(d < max_exact, d, large)


def _band_kernel(row_ref, o_ref):
    rows = jnp.broadcast_to(row_ref[...], (o_ref.shape[0], row_ref.shape[1]))
    o_ref[...] = pltpu.roll(rows, 0, 1, stride=1, stride_axis=0)[:, :o_ref.shape[1]]


def _bias_band(table, T):
    y = np.arange(3 * T)
    dist = np.clip(np.where(y <= 2 * T, T - y, 4 * T - y), 0, 2 * T - 1)
    row = table.astype(F32)[_t5_bucket(jnp.asarray(dist, jnp.int32))].T
    ch = row.shape[0]
    return pl.pallas_call(
        _band_kernel,
        out_shape=jax.ShapeDtypeStruct((ch, T, 2 * T), F32),
        grid=(ch,),
        in_specs=[pl.BlockSpec((None, 1, 3 * T), lambda c: (c, 0, 0))],
        out_specs=pl.BlockSpec((None, T, 2 * T), lambda c: (c, 0, 0)),
        compiler_params=_params("parallel"),
        name="bias_band",
    )(row[:, None, :])


def _lambda_init(layer):
    return 0.8 - 0.6 * math.exp(-0.3 * layer)


def _dup_kv_heads(w):
    q_w = SW_Q_HEADS * SW_DIM
    kv = w[:, q_w:].reshape(w.shape[0], 2 * SW_KV_HEADS, 1, SW_DIM)
    kv = jnp.broadcast_to(kv, (w.shape[0], 2 * SW_KV_HEADS, 2, SW_DIM)).reshape(w.shape[0], -1)
    return jnp.concatenate([w[:, :q_w], kv], axis=1)


def _pad_ff(a, axis):
    pad = [(0, 0)] * a.ndim
    pad[axis] = (0, D_FF_PAD - D_FF)
    return jnp.pad(a, pad)


def _split_up_kernel(w_ref, gate_ref, value_ref):
    keep = lax.broadcasted_iota(jnp.int32, gate_ref.shape, 1) < D_FF
    gate_ref[...] = jnp.where(keep, w_ref[:, :D_FF_PAD], 0.0).astype(gate_ref.dtype)
    start = 2 * D_FF - D_FF_PAD
    window = pltpu.roll(w_ref[:, start:], D_FF_PAD - (D_FF - start), 1)
    value_ref[...] = jnp.where(keep, window, 0.0).astype(value_ref.dtype)


def split_up_call(w_up):
    L, D, _ = w_up.shape
    out = jax.ShapeDtypeStruct((L, D, D_FF_PAD), BF16)
    out_spec = pl.BlockSpec((None, PREP_ROWS, D_FF_PAD), lambda l, i: (l, i, 0))
    return pl.pallas_call(
        _split_up_kernel,
        out_shape=(out, out),
        grid=(L, D // PREP_ROWS),
        in_specs=[pl.BlockSpec((None, PREP_ROWS, 2 * D_FF), lambda l, i: (l, i, 0))],
        out_specs=(out_spec, out_spec),
        compiler_params=_params("parallel", "parallel"),
        name="ffn_up_weight_prep",
    )(w_up)


def _pad_down_kernel(w_ref, o_ref):
    row = pl.program_id(1) * PREP_DOWN_ROWS + lax.broadcasted_iota(jnp.int32, o_ref.shape, 0)
    o_ref[...] = jnp.where(row < D_FF, w_ref[...], 0.0).astype(o_ref.dtype)


def pad_down_call(w_down):
    L, _, D = w_down.shape
    spec = pl.BlockSpec((None, PREP_DOWN_ROWS, D), lambda l, i: (l, i, 0))
    return pl.pallas_call(
        _pad_down_kernel,
        out_shape=jax.ShapeDtypeStruct((L, D_FF_PAD, D), BF16),
        grid=(L, D_FF_PAD // PREP_DOWN_ROWS),
        in_specs=[spec],
        out_specs=spec,
        compiler_params=_params("parallel", "parallel"),
        name="ffn_down_weight_prep",
    )(w_down)


def kernel(x, rel_bias, attn_norm, ffn_norm, w_o, da_w_qkv, da_lambda, da_subln, sb_w_qkv, sw_w_qkv,
           sw_sinks, ffn_w_up, ffn_conv_w, ffn_conv_b, ffn_w_down, final_norm):
    B, S, D = x.shape
    x2 = x.reshape(B * S, D)
    da_bias_near = _bias_band(rel_bias - rel_bias[N_BUCKETS - 1], ATT_T)
    sw_bias = _bias_band(rel_bias, SW_BLOCK)
    da_w = da_w_qkv.astype(BF16)
    sb_w = sb_w_qkv.astype(BF16)
    sw_w = jax.vmap(_dup_kv_heads)(sw_w_qkv).astype(BF16)
    wo_w = w_o.astype(BF16)
    wg, wv = split_up_call(ffn_w_up)
    cwg = _pad_ff(ffn_conv_w[:, :, :D_FF], 2)
    cwv = _pad_ff(ffn_conv_w[:, :, D_FF:], 2)
    cbg = _pad_ff(ffn_conv_b[:, None, :D_FF], 2)
    cbv = _pad_ff(ffn_conv_b[:, None, D_FF:], 2)
    wd = pad_down_call(ffn_w_down)
    attn_gain = attn_norm[:, None, :]
    ffn_gain = ffn_norm[:, None, :]
    final_gain = final_norm[None, None, :]

    weights_qkv = [((da_w, sb_w, sw_w)[layer % N_MIXERS], layer // N_MIXERS) for layer in range(DEPTH)]
    qkv = proj_call(x2, *weights_qkv[0], attn_gain, 0)
    for layer in range(DEPTH):
        mixer = layer % N_MIXERS
        slot = layer // N_MIXERS
        if mixer == 0:
            m = da_call(qkv, da_bias_near, da_lambda[slot], da_subln[slot], B, S, _lambda_init(layer))
        elif mixer == 1:
            m = sb_call(qkv, B, S)
        else:
            m = sw_call(qkv, sw_bias, sw_sinks[slot], B, S)
        x2, act = wo_ffn_up_call(m, wo_w, x2, ffn_gain, wg, wv, cwg, cwv, cbg, cbv, layer, S)
        if layer < DEPTH - 1:
            x2, qkv = ffn_down_call(act, wd, layer, x2, attn_gain, layer + 1, *weights_qkv[layer + 1])
        else:
            h = ffn_down_call(act, wd, layer, x2, final_gain, 0)
    return h.reshape(B, S, D)
```

```python
import functools
import math

import jax
import jax.numpy as jnp
import numpy as np
from jax import lax
from jax.experimental import pallas as pl
from jax.experimental.pallas import tpu as pltpu

D_MODEL = 1024
DEPTH = 4
N_MIXERS = 3
N_BUCKETS = 32
MAX_DISTANCE = 128
DA_HEADS = 8
DA_QK_DIM = 64
DA_V_DIM = 128
SB_HEADS = 16
SB_DIM = 64
SW_Q_HEADS = 16
SW_KV_HEADS = 4
SW_DIM = 64
SW_BLOCK = 128
D_FF = 2752
EPS = 1e-6
NEG = -1e30

LANES = 128
SUBLANES = 8
VMEM_LIMIT = 52 * 1024 * 1024
WO_FFN_VMEM_LIMIT = 57 * 1024 * 1024

ATT_T = 256
DA_SLOTS = 2
SB_EXP_ZERO = -104.0
ROW_TILE = 512
PROJ_TM = 1024
PROJ_CHUNK = 512
PREP_ROWS = 256
PREP_DOWN_ROWS = 704
SW_LAG = 2
FFN_TM = 1024
FFN_TF = 256
D_FF_PAD = 2816

BF16 = jnp.bfloat16
F32 = jnp.float32


def _params(*sem):
    return pltpu.CompilerParams(dimension_semantics=sem, vmem_limit_bytes=VMEM_LIMIT)


def _layer_spec(arr, layer, single_buffer=False):
    shape = (None,) + arr.shape[1:]
    index_map = lambda *_: (layer,) + (0,) * (arr.ndim - 1)
    if single_buffer:
        return pl.BlockSpec(shape, index_map, pipeline_mode=pl.Buffered(1))
    return pl.BlockSpec(shape, index_map)


def _rms(xf, gain):
    return xf * lax.rsqrt(jnp.mean(xf * xf, axis=-1, keepdims=True) + EPS) * gain


def _project(h, w_ref, o_ref):
    for c in range(o_ref.shape[1] // PROJ_CHUNK):
        sl = slice(c * PROJ_CHUNK, (c + 1) * PROJ_CHUNK)
        o_ref[:, sl] = jnp.dot(h, w_ref[:, sl], preferred_element_type=F32).astype(o_ref.dtype)


def _norm_proj_kernel(x_ref, g_ref, w_ref, o_ref):
    _project(_rms(x_ref[...], g_ref[...]).astype(BF16), w_ref, o_ref)


def proj_call(x2, w, layer, gain, gain_layer):
    M, D = x2.shape
    N = w.shape[2]
    row = lambda i: (i, 0)
    return pl.pallas_call(
        _norm_proj_kernel,
        out_shape=jax.ShapeDtypeStruct((M, N), BF16),
        grid=(M // PROJ_TM,),
        in_specs=[pl.BlockSpec((PROJ_TM, D), row), _layer_spec(gain, gain_layer),
                  _layer_spec(w, layer, single_buffer=True)],
        out_specs=pl.BlockSpec((PROJ_TM, N), row),
        compiler_params=_params("parallel"),
        name="qkv_proj",
    )(x2, gain, w)


def _ffn_row_starts(n_groups):
    per = n_groups // SUBLANES
    return [SUBLANES * SUBLANES * (j % per) + j // per for j in range(n_groups)]


def _stage(ref, x):
    for c in range(ref.shape[0]):
        ref[c] = x[:, c * LANES:(c + 1) * LANES]


def _stage_interleaved(ref, x):
    for c in range(ref.shape[0]):
        for j, st in enumerate(_ffn_row_starts(x.shape[0] // SUBLANES)):
            ref[c, pl.ds(st, SUBLANES, stride=SUBLANES), :] = x[j * SUBLANES:(j + 1) * SUBLANES,
                                                                c * LANES:(c + 1) * LANES]


def _deinterleave_rows(ref, c):
    return jnp.concatenate([ref[c, pl.ds(st, SUBLANES, stride=SUBLANES), :]
                            for st in _ffn_row_starts(ref.shape[1] // SUBLANES)], axis=0)


def _causal_conv3(u, cw, cb, prev):
    nblk, g, _, tf = u.shape
    first = lax.broadcasted_iota(jnp.int32, (SUBLANES, tf), 0) == 0
    u1, u2 = [], []
    for b in range(nblk):
        tail = prev if b == 0 else u[b - 1, g - 2:]
        wrap = [jnp.where(first, pltpu.roll(tail[e], 1, 0), pltpu.roll(u[b, g - 2 + e], 1, 0))
                for e in range(2)]
        u1.append(jnp.concatenate([wrap[1][None], u[b, :g - 1]], axis=0))
        u2.append(jnp.concatenate([wrap[0][None], wrap[1][None], u[b, :g - 2]], axis=0))
    u1 = jnp.stack(u1)
    u2 = jnp.stack(u2)
    return u * cw[2:3] + u1 * cw[1:2] + u2 * cw[0:1] + cb


def _wo_ffn_up_kernel(m_ref, wo_ref, x_ref, gain_ref, wg_ref, wv_ref, cwg_ref, cwv_ref, cbg_ref, cbv_ref,
                      xo_ref, g_ref, carry_ref, stage_ref, h_ref, *, tiles_per_seq):
    tm = m_ref.shape[0]
    nblk = tm // ROW_TILE
    g = ROW_TILE // SUBLANES

    @pl.when(pl.program_id(0) % tiles_per_seq == 0)
    def _():
        carry_ref[...] = jnp.zeros(carry_ref.shape, F32)

    for b in range(nblk):
        rows = slice(b * ROW_TILE, (b + 1) * ROW_TILE)
        xn = x_ref[rows, :] + jnp.dot(m_ref[rows, :], wo_ref[...], preferred_element_type=F32)
        xo_ref[rows, :] = xn
        _stage_interleaved(stage_ref, _rms(xn, gain_ref[...]))
        for c in range(stage_ref.shape[0]):
            h_ref[rows, c * LANES:(c + 1) * LANES] = stage_ref[c].astype(h_ref.dtype)

    h = h_ref[...]
    for c in range(g_ref.shape[1] // FFN_TF):
        sl = slice(c * FFN_TF, (c + 1) * FFN_TF)
        ug = jnp.dot(h, wg_ref[:, sl], preferred_element_type=F32).reshape(nblk, g, SUBLANES, FFN_TF)
        uv = jnp.dot(h, wv_ref[:, sl], preferred_element_type=F32).reshape(nblk, g, SUBLANES, FFN_TF)
        prev_g = carry_ref[0, :, :, sl]
        prev_v = carry_ref[1, :, :, sl]
        carry_ref[0, :, :, sl] = ug[nblk - 1, g - 2:]
        carry_ref[1, :, :, sl] = uv[nblk - 1, g - 2:]
        yg = _causal_conv3(ug, cwg_ref[:, sl], cbg_ref[:, sl], prev_g)
        yv = _causal_conv3(uv, cwv_ref[:, sl], cbv_ref[:, sl], prev_v)
        act = yg * (1.0 / (1.0 + jnp.exp(-yg))) * yv
        g_ref[:, sl] = act.reshape(tm, FFN_TF).astype(g_ref.dtype)


def wo_ffn_up_call(m, wo, x2, gain, wg, wv, cwg, cwv, cbg, cbv, layer, seq):
    M, D = x2.shape
    F = wg.shape[2]
    row = lambda i: (i, 0)
    resident = lambda arr: _layer_spec(arr, layer, single_buffer=True)
    return pl.pallas_call(
        functools.partial(_wo_ffn_up_kernel, tiles_per_seq=seq // FFN_TM),
        out_shape=(jax.ShapeDtypeStruct((M, D), F32), jax.ShapeDtypeStruct((M, F), BF16)),
        grid=(M // FFN_TM,),
        in_specs=[pl.BlockSpec((FFN_TM, D), row), resident(wo), pl.BlockSpec((FFN_TM, D), row), resident(gain),
                  resident(wg), resident(wv), resident(cwg), resident(cwv), resident(cbg), resident(cbv)],
        out_specs=(pl.BlockSpec((FFN_TM, D), row), pl.BlockSpec((FFN_TM, F), row)),
        scratch_shapes=[pltpu.VMEM((2, 2, SUBLANES, F), F32),
                        pltpu.VMEM((D // LANES, ROW_TILE, LANES), F32),
                        pltpu.VMEM((FFN_TM, D), BF16)],
        compiler_params=pltpu.CompilerParams(dimension_semantics=("arbitrary",),
                                             vmem_limit_bytes=WO_FFN_VMEM_LIMIT),
        name="wo_ffn_up_conv_gate",
    )(m, wo, x2, gain, wg, wv, cwg, cwv, cbg, cbv)


def _ffn_down_kernel(a_ref, w_ref, x_ref, g_ref, *refs, project):
    if project:
        wq_ref, xo_ref, qkv_ref, stage_ref = refs
    else:
        h_ref, stage_ref, xo_ref = refs
    _stage(stage_ref, jnp.dot(a_ref[...], w_ref[...], preferred_element_type=F32))
    for c in range(stage_ref.shape[0]):
        sl = slice(c * LANES, (c + 1) * LANES)
        xo_ref[:, sl] = x_ref[:, sl] + _deinterleave_rows(stage_ref, c)
    h = _rms(xo_ref[...], g_ref[...])
    if project:
        _project(h.astype(BF16), wq_ref, qkv_ref)
    else:
        h_ref[...] = h.astype(h_ref.dtype)


def ffn_down_call(a, w, layer, x2, gain, gain_layer, w_qkv=None, qkv_layer=0):
    M, D = x2.shape
    F = a.shape[1]
    row = lambda i: (i, 0)
    project = w_qkv is not None
    in_specs = [pl.BlockSpec((ROW_TILE, F), row), _layer_spec(w, layer, single_buffer=True),
                pl.BlockSpec((ROW_TILE, D), row), _layer_spec(gain, gain_layer)]
    stage = pltpu.VMEM((D // LANES, ROW_TILE, LANES), F32)
    if project:
        N = w_qkv.shape[2]
        in_specs.append(_layer_spec(w_qkv, qkv_layer, single_buffer=True))
        out_shape = (jax.ShapeDtypeStruct((M, D), F32), jax.ShapeDtypeStruct((M, N), BF16))
        out_specs = (pl.BlockSpec((ROW_TILE, D), row), pl.BlockSpec((ROW_TILE, N), row))
        scratch, args = [stage], (a, w, x2, gain, w_qkv)
    else:
        out_shape, out_specs = jax.ShapeDtypeStruct((M, D), F32), pl.BlockSpec((ROW_TILE, D), row)
        scratch, args = [stage, pltpu.VMEM((ROW_TILE, D), F32)], (a, w, x2, gain)
    return pl.pallas_call(
        functools.partial(_ffn_down_kernel, project=project),
        out_shape=out_shape,
        grid=(M // ROW_TILE,),
        in_specs=in_specs,
        out_specs=out_specs,
        scratch_shapes=scratch,
        compiler_params=_params("parallel"),
        name="ffn_down_residual_norm",
    )(*args)


def _half_masks(q):
    lane = lax.broadcasted_iota(jnp.int32, q.shape, 1)
    zero = jnp.zeros_like(q)
    scale = jnp.asarray(DA_QK_DIM ** -0.5, q.dtype)
    return (jnp.where(lane < 64, q, zero) * scale, jnp.where(lane >= 64, q, zero) * scale)


def _qk(q, k):
    return lax.dot_general(q, k, (((1,), (1,)), ((), ())), preferred_element_type=F32)


def _da_kernel(q_ref, k_ref, v_ref, bias_ref, lam_ref, sub_ref, o_ref, s_ref, *, lambda_init):
    T = ATT_T
    nq = q_ref.shape[0] // T
    r = lax.broadcasted_iota(jnp.int32, (T, T), 0)
    cc = lax.broadcasted_iota(jnp.int32, (T, T), 1)
    causal = r >= cc
    lam = lam_ref[...]
    lam_full = (jnp.exp(jnp.sum(lam[0:1] * lam[1:2], keepdims=True))
                - jnp.exp(jnp.sum(lam[2:3] * lam[3:4], keepdims=True)) + lambda_init)
    def score_steps(qi, st):
        def first():
            st["q01"] = jnp.concatenate(_half_masks(q_ref[qi * T:(qi + 1) * T, :]), axis=0)
            st["mx"] = [None, None]

        def tile(j):
            if j == 0:
                first()
            ss = _qk(st["q01"], k_ref[j * T:(j + 1) * T, :])
            for c in range(2):
                s = ss[c * T:(c + 1) * T]
                if j == qi - 1:
                    s = s + bias_ref[c, :, 0:T]
                if j == qi:
                    s = jnp.where(causal, s + bias_ref[c, :, T:2 * T], NEG)
                s_ref[qi % DA_SLOTS, c, :, j * T:(j + 1) * T] = s
                t = jnp.maximum(s[:, :LANES], s[:, LANES:])
                st["mx"][c] = t if st["mx"][c] is None else jnp.maximum(st["mx"][c], t)
        return [functools.partial(tile, j) for j in range(qi + 1)]

    def value_steps(qi, st):
        def exp_tile(j):
            if j == 0:
                st["m2"] = []
                for c in range(2):
                    m = jnp.broadcast_to(jnp.max(st["mx"][c], axis=-1, keepdims=True), (T, LANES))
                    st["m2"].append(jnp.concatenate([m, m], axis=1))
                st["lsum"] = [None, None]
            for c in range(2):
                p = jnp.exp(s_ref[qi % DA_SLOTS, c, :, j * T:(j + 1) * T] - st["m2"][c])
                s_ref[qi % DA_SLOTS, c, :, j * T:(j + 1) * T] = p
                t = p[:, :LANES] + p[:, LANES:]
                st["lsum"][c] = t if st["lsum"][c] is None else st["lsum"][c] + t

        def value_tile(j):
            if j == 0:
                l0 = jnp.sum(st["lsum"][0], axis=-1, keepdims=True)
                l1 = jnp.sum(st["lsum"][1], axis=-1, keepdims=True)
                st["ratio"] = lam_full * l0 / l1
                st["inv_l0"] = 1.0 / l0
                st["acc"] = None
            a = (s_ref[qi % DA_SLOTS, 0, :, j * T:(j + 1) * T]
                 - s_ref[qi % DA_SLOTS, 1, :, j * T:(j + 1) * T] * st["ratio"])
            pv = jnp.dot(a.astype(BF16), v_ref[j * T:(j + 1) * T, :], preferred_element_type=F32)
            st["acc"] = pv if st["acc"] is None else st["acc"] + pv
            if j == qi:
                o = _rms(st["acc"] * st["inv_l0"], sub_ref[...]) * (1.0 - lambda_init)
                o_ref[qi * T:(qi + 1) * T, :] = o.astype(o_ref.dtype)
        return ([functools.partial(exp_tile, j) for j in range(qi + 1)],
                [functools.partial(value_tile, j) for j in range(qi + 1)])

    states = [dict() for _ in range(nq)]
    stages = [score_steps(qi, states[qi]) for qi in range(nq)]
    later = [value_steps(qi, states[qi]) for qi in range(nq)]
    for r in range(nq + 1):
        lists = [stages[r] if r < nq else [],
                 later[r - 1][0] + later[r - 1][1] if r >= 1 else []]
        for t in range(max(len(steps) for steps in lists)):
            for steps in lists:
                if t < len(steps):
                    steps[t]()


def da_call(qkv, bias_near, lam, subln, batch, seq, lambda_init):
    T = ATT_T
    nh = DA_HEADS
    return pl.pallas_call(
        functools.partial(_da_kernel, lambda_init=lambda_init),
        out_shape=jax.ShapeDtypeStruct((batch * seq, nh * DA_V_DIM), BF16),
        grid=(batch, nh),
        in_specs=[pl.BlockSpec((seq, LANES), lambda b, h: (b, h)),
                  pl.BlockSpec((seq, LANES), lambda b, h: (b, nh + h)),
                  pl.BlockSpec((seq, LANES), lambda b, h: (b, 2 * nh + h)),
                  pl.BlockSpec((2, T, 2 * T), lambda b, h: (h, 0, 0)),
                  pl.BlockSpec((4, DA_QK_DIM), lambda b, h: (0, 0)),
                  pl.BlockSpec((1, DA_V_DIM), lambda b, h: (0, 0))],
        out_specs=pl.BlockSpec((seq, LANES), lambda b, h: (b, h)),
        scratch_shapes=[pltpu.VMEM((DA_SLOTS, 2, T, seq), F32)],
        compiler_params=_params("parallel", "parallel"),
        name="diff_attention",
    )(qkv, qkv, qkv, bias_near, lam, subln.reshape(1, DA_V_DIM))


def _split2(x):
    hi = x.astype(BF16)
    mid = (x - hi.astype(F32)).astype(BF16)
    return hi, mid


def _sb_kernel(q_ref, k_ref, v_ref, o_ref, lbn_ref, hmn_ref, lbf_ref, hmf_ref, acc_ref):
    T = ATT_T
    nq = q_ref.shape[0] // T
    kr = lax.broadcasted_iota(jnp.int32, (T, T), 0)
    kc = lax.broadcasted_iota(jnp.int32, (T, T), 1)
    suffix = jnp.where(kr > kc, 1.0, 0.0).astype(BF16)
    suffix2 = jnp.concatenate([suffix, suffix], axis=0)
    strict = kc < kr
    lane = lax.broadcasted_iota(jnp.int32, (T, LANES), 1)
    strict2 = jnp.concatenate([strict, strict], axis=0)

    def score_tile(qi, q01, j, blk, carry, lb_view, hm_view):
        zz = _qk(q01, k_ref[j * T:(j + 1) * T, :])
        for c in range(2):
            z = zz[c * T:(c + 1) * T]
            lb = jnp.minimum(z, 0.0) - jnp.log(1.0 + jnp.exp(-jnp.abs(z)))
            log_1m_beta = lb - z
            if j == qi:
                log_1m_beta = jnp.where(strict, log_1m_beta, 0.0)
            hi, mid = _split2(log_1m_beta)
            rows = slice(blk + c * T, blk + (c + 1) * T)
            hm_view[rows, 0:T] = hi
            hm_view[rows, T:2 * T] = mid
            if carry[c] is not None:
                lb = lb + carry[c]
            lb_view[rows, :] = lb
            if j > 0:
                rs = jnp.sum(log_1m_beta, axis=-1, keepdims=True)
                carry[c] = rs if carry[c] is None else carry[c] + rs

    def value_tiles(qi, tiles, lb_view, hm_view):
        between = jnp.dot(hm_view[0:2 * len(tiles) * T, :], suffix2, preferred_element_type=F32)
        acc = None
        for pos, j in enumerate(tiles):
            rows = slice(2 * pos * T, 2 * (pos + 1) * T)
            a = jnp.exp(lb_view[rows, :] + between[rows])
            if j == qi:
                a = jnp.where(strict2, a, 0.0)
            pv = jnp.dot(a.astype(BF16), v_ref[j * T:(j + 1) * T, :], preferred_element_type=F32)
            acc = pv if acc is None else acc + pv
        return acc

    def stacked_q(qi):
        return jnp.concatenate(_half_masks(q_ref[qi * T:(qi + 1) * T, :]), axis=0)

    near = [[j for j in (qi, qi - 1) if j >= 0] for qi in range(nq)]
    carries = [[None, None] for _ in range(nq)]
    for qi in range(nq + 1):
        if qi < nq:
            q01 = stacked_q(qi)
            for pos, j in enumerate(near[qi]):
                score_tile(qi, q01, j, 2 * pos * T, carries[qi], lbn_ref.at[qi % 2], hmn_ref.at[qi % 2])
        if qi > 0:
            acc_ref[qi - 1] = value_tiles(qi - 1, near[qi - 1], lbn_ref.at[(qi - 1) % 2], hmn_ref.at[(qi - 1) % 2])

    for qi in range(2, nq):
        far = list(range(qi - 2, -1, -1))
        carry = carries[qi]
        nearest_sum = jnp.max(jnp.maximum(carry[0], carry[1]))

        @pl.when(nearest_sum > SB_EXP_ZERO)
        def _():
            q01 = stacked_q(qi)
            far_carry = list(carry)
            for pos, j in enumerate(far):
                score_tile(qi, q01, j, 2 * pos * T, far_carry, lbf_ref, hmf_ref)
            acc_ref[qi] += value_tiles(qi, far, lbf_ref, hmf_ref)

    for qi in range(nq):
        acc = acc_ref[qi]
        o_ref[qi * T:(qi + 1) * T, :] = jnp.where(lane < 64, acc[0:T], acc[T:2 * T]).astype(o_ref.dtype)


def sb_call(qkv, batch, seq):
    npair = SB_HEADS // 2
    return pl.pallas_call(
        _sb_kernel,
        out_shape=jax.ShapeDtypeStruct((batch * seq, SB_HEADS * SB_DIM), BF16),
        grid=(batch, npair),
        in_specs=[pl.BlockSpec((seq, LANES), lambda b, p: (b, p)),
                  pl.BlockSpec((seq, LANES), lambda b, p: (b, npair + p)),
                  pl.BlockSpec((seq, LANES), lambda b, p: (b, 2 * npair + p))],
        out_specs=pl.BlockSpec((seq, LANES), lambda b, p: (b, p)),
        scratch_shapes=[pltpu.VMEM((2, 4 * ATT_T, ATT_T), F32), pltpu.VMEM((2, 4 * ATT_T, 2 * ATT_T), BF16),
                        pltpu.VMEM((2 * seq - 4 * ATT_T, ATT_T), F32),
                        pltpu.VMEM((2 * seq - 4 * ATT_T, 2 * ATT_T), BF16),
                        pltpu.VMEM((seq // ATT_T, 2 * ATT_T, LANES), F32)],
        compiler_params=_params("parallel", "parallel"),
        name="stick_breaking_attention",
    )(qkv, qkv, qkv)


def _sw_kernel(sink_ref, q_ref, k_ref, v_ref, bias_ref, o_ref, s_ref):
    W = SW_BLOCK
    nb = q_ref.shape[0] // W
    p_id = pl.program_id(1)
    r = lax.broadcasted_iota(jnp.int32, (2 * W, 2 * W), 0) & (W - 1)
    cidx = lax.broadcasted_iota(jnp.int32, (2 * W, 2 * W), 1)
    valid = ((cidx < W) & (cidx > r)) | ((cidx >= W) & (cidx - W <= r))
    biasm = jnp.where(valid, bias_ref[...], NEG)
    row = lax.broadcasted_iota(jnp.int32, (2 * W, 1), 0)
    sink = jnp.where(row < W, sink_ref[2 * p_id], sink_ref[2 * p_id + 1])
    lane = lax.broadcasted_iota(jnp.int32, (W, LANES), 1)
    def scores(n):
        q01 = jnp.concatenate(_half_masks(q_ref[n * W:(n + 1) * W, :]), axis=0)
        if n == 0:
            s_ref[0, :, W:] = _qk(q01, k_ref[0:W, :]) + biasm[:, W:]
        else:
            s_ref[n] = _qk(q01, k_ref[(n - 1) * W:(n + 1) * W, :]) + biasm

    def values(n):
        s = s_ref[0, :, W:] if n == 0 else s_ref[n]
        m = jnp.maximum(jnp.max(s, axis=-1, keepdims=True), sink)
        e = jnp.exp(s - m)
        denom = jnp.sum(e, axis=-1, keepdims=True) + jnp.exp(sink - m)
        v = v_ref[0:W, :] if n == 0 else v_ref[(n - 1) * W:(n + 1) * W, :]
        o = jnp.dot(e.astype(BF16), v, preferred_element_type=F32) / denom
        o_ref[n * W:(n + 1) * W, :] = jnp.where(lane < 64, o[0:W], o[W:]).astype(o_ref.dtype)

    for n in range(nb + SW_LAG):
        if n < nb:
            scores(n)
        if n >= SW_LAG:
            values(n - SW_LAG)


def sw_call(qkv, bias_band, sinks, batch, seq):
    npair = SW_Q_HEADS // 2
    q_blocks = SW_Q_HEADS * SW_DIM // LANES
    W = SW_BLOCK
    return pl.pallas_call(
        _sw_kernel,
        out_shape=jax.ShapeDtypeStruct((batch * seq, SW_Q_HEADS * SW_DIM), BF16),
        grid=(batch, npair),
        in_specs=[pl.BlockSpec(memory_space=pltpu.SMEM),
                  pl.BlockSpec((seq, LANES), lambda b, p: (b, p)),
                  pl.BlockSpec((seq, LANES), lambda b, p: (b, q_blocks + p // 2)),
                  pl.BlockSpec((seq, LANES), lambda b, p: (b, q_blocks + SW_KV_HEADS + p // 2)),
                  pl.BlockSpec((None, 2 * W, 2 * W), lambda b, p: (p, 0, 0))],
        out_specs=pl.BlockSpec((seq, LANES), lambda b, p: (b, p)),
        scratch_shapes=[pltpu.VMEM((seq // W, 2 * W, 2 * W), F32)],
        compiler_params=_params("parallel", "parallel"),
        name="sliding_window_attention",
    )(sinks, qkv, qkv, qkv, bias_band.reshape(npair, 2 * W, 2 * W))


def _t5_bucket(dist):
    max_exact = N_BUCKETS // 2
    d = jnp.maximum(dist, 0)
    large = max_exact + (jnp.log(jnp.maximum(d, 1).astype(F32) / max_exact)
                         / math.log(MAX_DISTANCE / max_exact) * (N_BUCKETS - max_exact)).astype(jnp.int32)
    large = jnp.minimum(large, N_BUCKETS - 1)
    return jnp.where(d < max_exact, d, large)


def _band_kernel(row_ref, o_ref):
    rows = jnp.broadcast_to(row_ref[...], (o_ref.shape[0], row_ref.shape[1]))
    o_ref[...] = pltpu.roll(rows, 0, 1, stride=1, stride_axis=0)[:, :o_ref.shape[1]]


def _bias_band(table, T):
    y = np.arange(3 * T)
    dist = np.clip(np.where(y <= 2 * T, T - y, 4 * T - y), 0, 2 * T - 1)
    row = table.astype(F32)[_t5_bucket(jnp.asarray(dist, jnp.int32))].T
    ch = row.shape[0]
    return pl.pallas_call(
        _band_kernel,
        out_shape=jax.ShapeDtypeStruct((ch, T, 2 * T), F32),
        grid=(ch,),
        in_specs=[pl.BlockSpec((None, 1, 3 * T), lambda c: (c, 0, 0))],
        out_specs=pl.BlockSpec((None, T, 2 * T), lambda c: (c, 0, 0)),
        compiler_params=_params("parallel"),
        name="bias_band",
    )(row[:, None, :])


def _lambda_init(layer):
    return 0.8 - 0.6 * math.exp(-0.3 * layer)


def _dup_kv_heads(w):
    q_w = SW_Q_HEADS * SW_DIM
    kv = w[:, q_w:].reshape(w.shape[0], 2 * SW_KV_HEADS, 1, SW_DIM)
    kv = jnp.broadcast_to(kv, (w.shape[0], 2 * SW_KV_HEADS, 2, SW_DIM)).reshape(w.shape[0], -1)
    return jnp.concatenate([w[:, :q_w], kv], axis=1)


def _pad_ff(a, axis):
    pad = [(0, 0)] * a.ndim
    pad[axis] = (0, D_FF_PAD - D_FF)
    return jnp.pad(a, pad)


def _split_up_kernel(w_ref, gate_ref, value_ref):
    keep = lax.broadcasted_iota(jnp.int32, gate_ref.shape, 1) < D_FF
    gate_ref[...] = jnp.where(keep, w_ref[:, :D_FF_PAD], 0.0).astype(gate_ref.dtype)
    start = 2 * D_FF - D_FF_PAD
    window = pltpu.roll(w_ref[:, start:], D_FF_PAD - (D_FF - start), 1)
    value_ref[...] = jnp.where(keep, window, 0.0).astype(value_ref.dtype)


def split_up_call(w_up):
    L, D, _ = w_up.shape
    out = jax.ShapeDtypeStruct((L, D, D_FF_PAD), BF16)
    out_spec = pl.BlockSpec((None, PREP_ROWS, D_FF_PAD), lambda l, i: (l, i, 0))
    return pl.pallas_call(
        _split_up_kernel,
        out_shape=(out, out),
        grid=(L, D // PREP_ROWS),
        in_specs=[pl.BlockSpec((None, PREP_ROWS, 2 * D_FF), lambda l, i: (l, i, 0))],
        out_specs=(out_spec, out_spec),
        compiler_params=_params("parallel", "parallel"),
        name="ffn_up_weight_prep",
    )(w_up)


def _pad_down_kernel(w_ref, o_ref):
    row = pl.program_id(1) * PREP_DOWN_ROWS + lax.broadcasted_iota(jnp.int32, o_ref.shape, 0)
    o_ref[...] = jnp.where(row < D_FF, w_ref[...], 0.0).astype(o_ref.dtype)


def pad_down_call(w_down):
    L, _, D = w_down.shape
    spec = pl.BlockSpec((None, PREP_DOWN_ROWS, D), lambda l, i: (l, i, 0))
    return pl.pallas_call(
        _pad_down_kernel,
        out_shape=jax.ShapeDtypeStruct((L, D_FF_PAD, D), BF16),
        grid=(L, D_FF_PAD // PREP_DOWN_ROWS),
        in_specs=[spec],
        out_specs=spec,
        compiler_params=_params("parallel", "parallel"),
        name="ffn_down_weight_prep",
    )(w_down)


def kernel(x, rel_bias, attn_norm, ffn_norm, w_o, da_w_qkv, da_lambda, da_subln, sb_w_qkv, sw_w_qkv,
           sw_sinks, ffn_w_up, ffn_conv_w, ffn_conv_b, ffn_w_down, final_norm):
    B, S, D = x.shape
    x2 = x.reshape(B * S, D)
    da_bias_near = _bias_band(rel_bias - rel_bias[N_BUCKETS - 1], ATT_T)
    sw_bias = _bias_band(rel_bias, SW_BLOCK)
    da_w = da_w_qkv.astype(BF16)
    sb_w = sb_w_qkv.astype(BF16)
    sw_w = jax.vmap(_dup_kv_heads)(sw_w_qkv).astype(BF16)
    wo_w = w_o.astype(BF16)
    wg, wv = split_up_call(ffn_w_up)
    cwg = _pad_ff(ffn_conv_w[:, :, :D_FF], 2)
    cwv = _pad_ff(ffn_conv_w[:, :, D_FF:], 2)
    cbg = _pad_ff(ffn_conv_b[:, None, :D_FF], 2)
    cbv = _pad_ff(ffn_conv_b[:, None, D_FF:], 2)
    wd = pad_down_call(ffn_w_down)
    attn_gain = attn_norm[:, None, :]
    ffn_gain = ffn_norm[:, None, :]
    final_gain = final_norm[None, None, :]

    weights_qkv = [((da_w, sb_w, sw_w)[layer % N_MIXERS], layer // N_MIXERS) for layer in range(DEPTH)]
    qkv = proj_call(x2, *weights_qkv[0], attn_gain, 0)
    for layer in range(DEPTH):
        mixer = layer % N_MIXERS
        slot = layer // N_MIXERS
        if mixer == 0:
            m = da_call(qkv, da_bias_near, da_lambda[slot], da_subln[slot], B, S, _lambda_init(layer))
        elif mixer == 1:
            m = sb_call(qkv, B, S)
        else:
            m = sw_call(qkv, sw_bias, sw_sinks[slot], B, S)
        x2, act = wo_ffn_up_call(m, wo_w, x2, ffn_gain, wg, wv, cwg, cwv, cbg, cbv, layer, S)
        if layer < DEPTH - 1:
            x2, qkv = ffn_down_call(act, wd, layer, x2, attn_gain, layer + 1, *weights_qkv[layer + 1])
        else:
            h = ffn_down_call(act, wd, layer, x2, final_gain, 0)
    return h.reshape(B, S, D)
```

```python
import functools
import math

import jax
import jax.numpy as jnp
import numpy as np
from jax import lax
from jax.experimental import pallas as pl
from jax.experimental.pallas import tpu as pltpu

D_MODEL = 1024
DEPTH = 4
N_MIXERS = 3
N_BUCKETS = 32
MAX_DISTANCE = 128
DA_HEADS = 8
DA_QK_DIM = 64
DA_V_DIM = 128
SB_HEADS = 16
SB_DIM = 64
SW_Q_HEADS = 16
SW_KV_HEADS = 4
SW_DIM = 64
SW_BLOCK = 128
D_FF = 2752
EPS = 1e-6
NEG = -1e30

LANES = 128
SUBLANES = 8
VMEM_LIMIT = 52 * 1024 * 1024
WO_FFN_VMEM_LIMIT = 57 * 1024 * 1024

ATT_T = 256
DA_SLOTS = 2
SB_EXP_ZERO = -104.0
ROW_TILE = 512
PROJ_TM = 1024
PROJ_CHUNK = 512
PREP_ROWS = 256
PREP_DOWN_ROWS = 704
SW_LAG = 2
FFN_TM = 1024
FFN_TF = 256
D_FF_PAD = 2816

BF16 = jnp.bfloat16
F32 = jnp.float32


def _params(*sem):
    return pltpu.CompilerParams(dimension_semantics=sem, vmem_limit_bytes=VMEM_LIMIT)


def _layer_spec(arr, layer, single_buffer=False):
    shape = (None,) + arr.shape[1:]
    index_map = lambda *_: (layer,) + (0,) * (arr.ndim - 1)
    if single_buffer:
        return pl.BlockSpec(shape, index_map, pipeline_mode=pl.Buffered(1))
    return pl.BlockSpec(shape, index_map)


def _rms(xf, gain):
    return xf * lax.rsqrt(jnp.mean(xf * xf, axis=-1, keepdims=True) + EPS) * gain


def _project(h, w_ref, o_ref):
    for c in range(o_ref.shape[1] // PROJ_CHUNK):
        sl = slice(c * PROJ_CHUNK, (c + 1) * PROJ_CHUNK)
        o_ref[:, sl] = jnp.dot(h, w_ref[:, sl], preferred_element_type=F32).astype(o_ref.dtype)


def _norm_proj_kernel(x_ref, g_ref, w_ref, o_ref):
    _project(_rms(x_ref[...], g_ref[...]).astype(BF16), w_ref, o_ref)


def proj_call(x2, w, layer, gain, gain_layer):
    M, D = x2.shape
    N = w.shape[2]
    row = lambda i: (i, 0)
    return pl.pallas_call(
        _norm_proj_kernel,
        out_shape=jax.ShapeDtypeStruct((M, N), BF16),
        grid=(M // PROJ_TM,),
        in_specs=[pl.BlockSpec((PROJ_TM, D), row), _layer_spec(gain, gain_layer),
                  _layer_spec(w, layer, single_buffer=True)],
        out_specs=pl.BlockSpec((PROJ_TM, N), row),
        compiler_params=_params("parallel"),
        name="qkv_proj",
    )(x2, gain, w)


def _ffn_row_starts(n_groups):
    per = n_groups // SUBLANES
    return [SUBLANES * SUBLANES * (j % per) + j // per for j in range(n_groups)]


def _stage(ref, x):
    for c in range(ref.shape[0]):
        ref[c] = x[:, c * LANES:(c + 1) * LANES]


def _stage_interleaved(ref, x):
    for c in range(ref.shape[0]):
        for j, st in enumerate(_ffn_row_starts(x.shape[0] // SUBLANES)):
            ref[c, pl.ds(st, SUBLANES, stride=SUBLANES), :] = x[j * SUBLANES:(j + 1) * SUBLANES,
                                                                c * LANES:(c + 1) * LANES]


def _deinterleave_rows(ref, c):
    return jnp.concatenate([ref[c, pl.ds(st, SUBLANES, stride=SUBLANES), :]
                            for st in _ffn_row_starts(ref.shape[1] // SUBLANES)], axis=0)


def _causal_conv3(u, cw, cb, prev):
    nblk, g, _, tf = u.shape
    first = lax.broadcasted_iota(jnp.int32, (SUBLANES, tf), 0) == 0
    u1, u2 = [], []
    for b in range(nblk):
        tail = prev if b == 0 else u[b - 1, g - 2:]
        wrap = [jnp.where(first, pltpu.roll(tail[e], 1, 0), pltpu.roll(u[b, g - 2 + e], 1, 0))
                for e in range(2)]
        u1.append(jnp.concatenate([wrap[1][None], u[b, :g - 1]], axis=0))
        u2.append(jnp.concatenate([wrap[0][None], wrap[1][None], u[b, :g - 2]], axis=0))
    u1 = jnp.stack(u1)
    u2 = jnp.stack(u2)
    return u * cw[2:3] + u1 * cw[1:2] + u2 * cw[0:1] + cb


def _wo_ffn_up_kernel(m_ref, wo_ref, x_ref, gain_ref, wg_ref, wv_ref, cwg_ref, cwv_ref, cbg_ref, cbv_ref,
                      xo_ref, g_ref, carry_ref, stage_ref, h_ref, *, tiles_per_seq):
    tm = m_ref.shape[0]
    nblk = tm // ROW_TILE
    g = ROW_TILE // SUBLANES

    @pl.when(pl.program_id(0) % tiles_per_seq == 0)
    def _():
        carry_ref[...] = jnp.zeros(carry_ref.shape, F32)

    for b in range(nblk):
        rows = slice(b * ROW_TILE, (b + 1) * ROW_TILE)
        xn = x_ref[rows, :] + jnp.dot(m_ref[rows, :], wo_ref[...], preferred_element_type=F32)
        xo_ref[rows, :] = xn
        _stage_interleaved(stage_ref, _rms(xn, gain_ref[...]))
        for c in range(stage_ref.shape[0]):
            h_ref[rows, c * LANES:(c + 1) * LANES] = stage_ref[c].astype(h_ref.dtype)

    h = h_ref[...]
    for c in range(g_ref.shape[1] // FFN_TF):
        sl = slice(c * FFN_TF, (c + 1) * FFN_TF)
        ug = jnp.dot(h, wg_ref[:, sl], preferred_element_type=F32).reshape(nblk, g, SUBLANES, FFN_TF)
        uv = jnp.dot(h, wv_ref[:, sl], preferred_element_type=F32).reshape(nblk, g, SUBLANES, FFN_TF)
        prev_g = carry_ref[0, :, :, sl]
        prev_v = carry_ref[1, :, :, sl]
        carry_ref[0, :, :, sl] = ug[nblk - 1, g - 2:]
        carry_ref[1, :, :, sl] = uv[nblk - 1, g - 2:]
        yg = _causal_conv3(ug, cwg_ref[:, sl], cbg_ref[:, sl], prev_g)
        yv = _causal_conv3(uv, cwv_ref[:, sl], cbv_ref[:, sl], prev_v)
        act = yg * (1.0 / (1.0 + jnp.exp(-yg))) * yv
        g_ref[:, sl] = act.reshape(tm, FFN_TF).astype(g_ref.dtype)


def wo_ffn_up_call(m, wo, x2, gain, wg, wv, cwg, cwv, cbg, cbv, layer, seq):
    M, D = x2.shape
    F = wg.shape[2]
    row = lambda i: (i, 0)
    resident = lambda arr: _layer_spec(arr, layer, single_buffer=True)
    return pl.pallas_call(
        functools.partial(_wo_ffn_up_kernel, tiles_per_seq=seq // FFN_TM),
        out_shape=(jax.ShapeDtypeStruct((M, D), F32), jax.ShapeDtypeStruct((M, F), BF16)),
        grid=(M // FFN_TM,),
        in_specs=[pl.BlockSpec((FFN_TM, D), row), resident(wo), pl.BlockSpec((FFN_TM, D), row), resident(gain),
                  resident(wg), resident(wv), resident(cwg), resident(cwv), resident(cbg), resident(cbv)],
        out_specs=(pl.BlockSpec((FFN_TM, D), row), pl.BlockSpec((FFN_TM, F), row)),
        scratch_shapes=[pltpu.VMEM((2, 2, SUBLANES, F), F32),
                        pltpu.VMEM((D // LANES, ROW_TILE, LANES), F32),
                        pltpu.VMEM((FFN_TM, D), BF16)],
        compiler_params=pltpu.CompilerParams(dimension_semantics=("arbitrary",),
                                             vmem_limit_bytes=WO_FFN_VMEM_LIMIT),
        name="wo_ffn_up_conv_gate",
    )(m, wo, x2, gain, wg, wv, cwg, cwv, cbg, cbv)


def _ffn_down_kernel(a_ref, w_ref, x_ref, g_ref, *refs, project):
    if project:
        wq_ref, xo_ref, qkv_ref, stage_ref = refs
    else:
        h_ref, stage_ref, xo_ref = refs
    _stage(stage_ref, jnp.dot(a_ref[...], w_ref[...], preferred_element_type=F32))
    for c in range(stage_ref.shape[0]):
        sl = slice(c * LANES, (c + 1) * LANES)
        xo_ref[:, sl] = x_ref[:, sl] + _deinterleave_rows(stage_ref, c)
    h = _rms(xo_ref[...], g_ref[...])
    if project:
        _project(h.astype(BF16), wq_ref, qkv_ref)
    else:
        h_ref[...] = h.astype(h_ref.dtype)


def ffn_down_call(a, w, layer, x2, gain, gain_layer, w_qkv=None, qkv_layer=0):
    M, D = x2.shape
    F = a.shape[1]
    row = lambda i: (i, 0)
    project = w_qkv is not None
    in_specs = [pl.BlockSpec((ROW_TILE, F), row), _layer_spec(w, layer, single_buffer=True),
                pl.BlockSpec((ROW_TILE, D), row), _layer_spec(gain, gain_layer)]
    stage = pltpu.VMEM((D // LANES, ROW_TILE, LANES), F32)
    if project:
        N = w_qkv.shape[2]
        in_specs.append(_layer_spec(w_qkv, qkv_layer, single_buffer=True))
        out_shape = (jax.ShapeDtypeStruct((M, D), F32), jax.ShapeDtypeStruct((M, N), BF16))
        out_specs = (pl.BlockSpec((ROW_TILE, D), row), pl.BlockSpec((ROW_TILE, N), row))
        scratch, args = [stage], (a, w, x2, gain, w_qkv)
    else:
        out_shape, out_specs = jax.ShapeDtypeStruct((M, D), F32), pl.BlockSpec((ROW_TILE, D), row)
        scratch, args = [stage, pltpu.VMEM((ROW_TILE, D), F32)], (a, w, x2, gain)
    return pl.pallas_call(
        functools.partial(_ffn_down_kernel, project=project),
        out_shape=out_shape,
        grid=(M // ROW_TILE,),
        in_specs=in_specs,
        out_specs=out_specs,
        scratch_shapes=scratch,
        compiler_params=_params("parallel"),
        name="ffn_down_residual_norm",
    )(*args)


def _half_masks(q):
    lane = lax.broadcasted_iota(jnp.int32, q.shape, 1)
    zero = jnp.zeros_like(q)
    scale = jnp.asarray(DA_QK_DIM ** -0.5, q.dtype)
    return (jnp.where(lane < 64, q, zero) * scale, jnp.where(lane >= 64, q, zero) * scale)


def _qk(q, k):
    return lax.dot_general(q, k, (((1,), (1,)), ((), ())), preferred_element_type=F32)


def _da_kernel(q_ref, k_ref, v_ref, bias_ref, lam_ref, sub_ref, o_ref, s_ref, *, lambda_init):
    T = ATT_T
    nq = q_ref.shape[0] // T
    r = lax.broadcasted_iota(jnp.int32, (T, T), 0)
    cc = lax.broadcasted_iota(jnp.int32, (T, T), 1)
    causal = r >= cc
    lam = lam_ref[...]
    lam_full = (jnp.exp(jnp.sum(lam[0:1] * lam[1:2], keepdims=True))
                - jnp.exp(jnp.sum(lam[2:3] * lam[3:4], keepdims=True)) + lambda_init)
    def score_steps(qi, st):
        def first():
            st["q01"] = jnp.concatenate(_half_masks(q_ref[qi * T:(qi + 1) * T, :]), axis=0)
            st["mx"] = [None, None]

        def tile(j):
            if j == 0:
                first()
            ss = _qk(st["q01"], k_ref[j * T:(j + 1) * T, :])
            for c in range(2):
                s = ss[c * T:(c + 1) * T]
                if j == qi - 1:
                    s = s + bias_ref[c, :, 0:T]
                if j == qi:
                    s = jnp.where(causal, s + bias_ref[c, :, T:2 * T], NEG)
                s_ref[qi % DA_SLOTS, c, :, j * T:(j + 1) * T] = s
                t = jnp.maximum(s[:, :LANES], s[:, LANES:])
                st["mx"][c] = t if st["mx"][c] is None else jnp.maximum(st["mx"][c], t)
        return [functools.partial(tile, j) for j in range(qi + 1)]

    def value_steps(qi, st):
        def exp_tile(j):
            if j == 0:
                st["m2"] = []
                for c in range(2):
                    m = jnp.broadcast_to(jnp.max(st["mx"][c], axis=-1, keepdims=True), (T, LANES))
                    st["m2"].append(jnp.concatenate([m, m], axis=1))
                st["lsum"] = [None, None]
            for c in range(2):
                p = jnp.exp(s_ref[qi % DA_SLOTS, c, :, j * T:(j + 1) * T] - st["m2"][c])
                s_ref[qi % DA_SLOTS, c, :, j * T:(j + 1) * T] = p
                t = p[:, :LANES] + p[:, LANES:]
                st["lsum"][c] = t if st["lsum"][c] is None else st["lsum"][c] + t

        def value_tile(j):
            if j == 0:
                l0 = jnp.sum(st["lsum"][0], axis=-1, keepdims=True)
                l1 = jnp.sum(st["lsum"][1], axis=-1, keepdims=True)
                st["ratio"] = lam_full * l0 / l1
                st["inv_l0"] = 1.0 / l0
                st["acc"] = None
            a = (s_ref[qi % DA_SLOTS, 0, :, j * T:(j + 1) * T]
                 - s_ref[qi % DA_SLOTS, 1, :, j * T:(j + 1) * T] * st["ratio"])
            pv = jnp.dot(a.astype(BF16), v_ref[j * T:(j + 1) * T, :], preferred_element_type=F32)
            st["acc"] = pv if st["acc"] is None else st["acc"] + pv
            if j == qi:
                o = _rms(st["acc"] * st["inv_l0"], sub_ref[...]) * (1.0 - lambda_init)
                o_ref[qi * T:(qi + 1) * T, :] = o.astype(o_ref.dtype)
        return ([functools.partial(exp_tile, j) for j in range(qi + 1)],
                [functools.partial(value_tile, j) for j in range(qi + 1)])

    states = [dict() for _ in range(nq)]
    stages = [score_steps(qi, states[qi]) for qi in range(nq)]
    later = [value_steps(qi, states[qi]) for qi in range(nq)]
    for r in range(nq + 1):
        lists = [stages[r] if r < nq else [],
                 later[r - 1][0] + later[r - 1][1] if r >= 1 else []]
        for t in range(max(len(steps) for steps in lists)):
            for steps in lists:
                if t < len(steps):
                    steps[t]()


def da_call(qkv, bias_near, lam, subln, batch, seq, lambda_init):
    T = ATT_T
    nh = DA_HEADS
    return pl.pallas_call(
        functools.partial(_da_kernel, lambda_init=lambda_init),
        out_shape=jax.ShapeDtypeStruct((batch * seq, nh * DA_V_DIM), BF16),
        grid=(nh, batch),
        in_specs=[pl.BlockSpec((seq, LANES), lambda h, b: (b, h)),
                  pl.BlockSpec((seq, LANES), lambda h, b: (b, nh + h)),
                  pl.BlockSpec((seq, LANES), lambda h, b: (b, 2 * nh + h)),
                  pl.BlockSpec((2, T, 2 * T), lambda h, b: (h, 0, 0)),
                  pl.BlockSpec((4, DA_QK_DIM), lambda h, b: (0, 0)),
                  pl.BlockSpec((1, DA_V_DIM), lambda h, b: (0, 0))],
        out_specs=pl.BlockSpec((seq, LANES), lambda h, b: (b, h)),
        scratch_shapes=[pltpu.VMEM((DA_SLOTS, 2, T, seq), F32)],
        compiler_params=_params("parallel", "parallel"),
        name="diff_attention",
    )(qkv, qkv, qkv, bias_near, lam, subln.reshape(1, DA_V_DIM))


def _split2(x):
    hi = x.astype(BF16)
    mid = (x - hi.astype(F32)).astype(BF16)
    return hi, mid


def _sb_kernel(q_ref, k_ref, v_ref, o_ref, lbn_ref, hmn_ref, lbf_ref, hmf_ref, acc_ref):
    T = ATT_T
    nq = q_ref.shape[0] // T
    kr = lax.broadcasted_iota(jnp.int32, (T, T), 0)
    kc = lax.broadcasted_iota(jnp.int32, (T, T), 1)
    suffix = jnp.where(kr > kc, 1.0, 0.0).astype(BF16)
    suffix2 = jnp.concatenate([suffix, suffix], axis=0)
    strict = kc < kr
    lane = lax.broadcasted_iota(jnp.int32, (T, LANES), 1)
    strict2 = jnp.concatenate([strict, strict], axis=0)

    def score_tile(qi, q01, j, blk, carry, lb_view, hm_view):
        zz = _qk(q01, k_ref[j * T:(j + 1) * T, :])
        for c in range(2):
            z = zz[c * T:(c + 1) * T]
            lb = jnp.minimum(z, 0.0) - jnp.log(1.0 + jnp.exp(-jnp.abs(z)))
            log_1m_beta = lb - z
            if j == qi:
                log_1m_beta = jnp.where(strict, log_1m_beta, 0.0)
            hi, mid = _split2(log_1m_beta)
            rows = slice(blk + c * T, blk + (c + 1) * T)
            hm_view[rows, 0:T] = hi
            hm_view[rows, T:2 * T] = mid
            if carry[c] is not None:
                lb = lb + carry[c]
            lb_view[rows, :] = lb
            if j > 0:
                rs = jnp.sum(log_1m_beta, axis=-1, keepdims=True)
                carry[c] = rs if carry[c] is None else carry[c] + rs

    def value_tiles(qi, tiles, lb_view, hm_view):
        between = jnp.dot(hm_view[0:2 * len(tiles) * T, :], suffix2, preferred_element_type=F32)
        acc = None
        for pos, j in enumerate(tiles):
            rows = slice(2 * pos * T, 2 * (pos + 1) * T)
            a = jnp.exp(lb_view[rows, :] + between[rows])
            if j == qi:
                a = jnp.where(strict2, a, 0.0)
            pv = jnp.dot(a.astype(BF16), v_ref[j * T:(j + 1) * T, :], preferred_element_type=F32)
            acc = pv if acc is None else acc + pv
        return acc

    def stacked_q(qi):
        return jnp.concatenate(_half_masks(q_ref[qi * T:(qi + 1) * T, :]), axis=0)

    near = [[j for j in (qi, qi - 1) if j >= 0] for qi in range(nq)]
    carries = [[None, None] for _ in range(nq)]
    for qi in range(nq + 1):
        if qi < nq:
            q01 = stacked_q(qi)
            for pos, j in enumerate(near[qi]):
                score_tile(qi, q01, j, 2 * pos * T, carries[qi], lbn_ref.at[qi % 2], hmn_ref.at[qi % 2])
        if qi > 0:
            acc_ref[qi - 1] = value_tiles(qi - 1, near[qi - 1], lbn_ref.at[(qi - 1) % 2], hmn_ref.at[(qi - 1) % 2])

    for qi in range(2, nq):
        far = list(range(qi - 2, -1, -1))
        carry = carries[qi]
        nearest_sum = jnp.max(jnp.maximum(carry[0], carry[1]))

        @pl.when(nearest_sum > SB_EXP_ZERO)
        def _():
            q01 = stacked_q(qi)
            far_carry = list(carry)
            for pos, j in enumerate(far):
                score_tile(qi, q01, j, 2 * pos * T, far_carry, lbf_ref, hmf_ref)
            acc_ref[qi] += value_tiles(qi, far, lbf_ref, hmf_ref)

    for qi in range(nq):
        acc = acc_ref[qi]
        o_ref[qi * T:(qi + 1) * T, :] = jnp.where(lane < 64, acc[0:T], acc[T:2 * T]).astype(o_ref.dtype)


def sb_call(qkv, batch, seq):
    npair = SB_HEADS // 2
    return pl.pallas_call(
        _sb_kernel,
        out_shape=jax.ShapeDtypeStruct((batch * seq, SB_HEADS * SB_DIM), BF16),
        grid=(batch, npair),
        in_specs=[pl.BlockSpec((seq, LANES), lambda b, p: (b, p)),
                  pl.BlockSpec((seq, LANES), lambda b, p: (b, npair + p)),
                  pl.BlockSpec((seq, LANES), lambda b, p: (b, 2 * npair + p))],
        out_specs=pl.BlockSpec((seq, LANES), lambda b, p: (b, p)),
        scratch_shapes=[pltpu.VMEM((2, 4 * ATT_T, ATT_T), F32), pltpu.VMEM((2, 4 * ATT_T, 2 * ATT_T), BF16),
                        pltpu.VMEM((2 * seq - 4 * ATT_T, ATT_T), F32),
                        pltpu.VMEM((2 * seq - 4 * ATT_T, 2 * ATT_T), BF16),
                        pltpu.VMEM((seq // ATT_T, 2 * ATT_T, LANES), F32)],
        compiler_params=_params("parallel", "parallel"),
        name="stick_breaking_attention",
    )(qkv, qkv, qkv)


def _sw_kernel(sink_ref, q_ref, k_ref, v_ref, bias_ref, o_ref, s_ref):
    W = SW_BLOCK
    nb = q_ref.shape[0] // W
    p_id = pl.program_id(1)
    r = lax.broadcasted_iota(jnp.int32, (2 * W, 2 * W), 0) & (W - 1)
    cidx = lax.broadcasted_iota(jnp.int32, (2 * W, 2 * W), 1)
    valid = ((cidx < W) & (cidx > r)) | ((cidx >= W) & (cidx - W <= r))
    biasm = jnp.where(valid, bias_ref[...], NEG)
    row = lax.broadcasted_iota(jnp.int32, (2 * W, 1), 0)
    sink = jnp.where(row < W, sink_ref[2 * p_id], sink_ref[2 * p_id + 1])
    lane = lax.broadcasted_iota(jnp.int32, (W, LANES), 1)
    def scores(n):
        q01 = jnp.concatenate(_half_masks(q_ref[n * W:(n + 1) * W, :]), axis=0)
        if n == 0:
            s_ref[0, :, W:] = _qk(q01, k_ref[0:W, :]) + biasm[:, W:]
        else:
            s_ref[n] = _qk(q01, k_ref[(n - 1) * W:(n + 1) * W, :]) + biasm

    def values(n):
        s = s_ref[0, :, W:] if n == 0 else s_ref[n]
        m = jnp.maximum(jnp.max(s, axis=-1, keepdims=True), sink)
        e = jnp.exp(s - m)
        denom = jnp.sum(e, axis=-1, keepdims=True) + jnp.exp(sink - m)
        v = v_ref[0:W, :] if n == 0 else v_ref[(n - 1) * W:(n + 1) * W, :]
        o = jnp.dot(e.astype(BF16), v, preferred_element_type=F32) / denom
        o_ref[n * W:(n + 1) * W, :] = jnp.where(lane < 64, o[0:W], o[W:]).astype(o_ref.dtype)

    for n in range(nb + SW_LAG):
        if n < nb:
            scores(n)
        if n >= SW_LAG:
            values(n - SW_LAG)


def sw_call(qkv, bias_band, sinks, batch, seq):
    npair = SW_Q_HEADS // 2
    q_blocks = SW_Q_HEADS * SW_DIM // LANES
    W = SW_BLOCK
    return pl.pallas_call(
        _sw_kernel,
        out_shape=jax.ShapeDtypeStruct((batch * seq, SW_Q_HEADS * SW_DIM), BF16),
        grid=(batch, npair),
        in_specs=[pl.BlockSpec(memory_space=pltpu.SMEM),
                  pl.BlockSpec((seq, LANES), lambda b, p: (b, p)),
                  pl.BlockSpec((seq, LANES), lambda b, p: (b, q_blocks + p // 2)),
                  pl.BlockSpec((seq, LANES), lambda b, p: (b, q_blocks + SW_KV_HEADS + p // 2)),
                  pl.BlockSpec((None, 2 * W, 2 * W), lambda b, p: (p, 0, 0))],
        out_specs=pl.BlockSpec((seq, LANES), lambda b, p: (b, p)),
        scratch_shapes=[pltpu.VMEM((seq // W, 2 * W, 2 * W), F32)],
        compiler_params=_params("parallel", "parallel"),
        name="sliding_window_attention",
    )(sinks, qkv, qkv, qkv, bias_band.reshape(npair, 2 * W, 2 * W))


def _t5_bucket(dist):
    max_exact = N_BUCKETS // 2
    d = jnp.maximum(dist, 0)
    large = max_exact + (jnp.log(jnp.maximum(d, 1).astype(F32) / max_exact)
                         / math.log(MAX_DISTANCE / max_exact) * (N_BUCKETS - max_exact)).astype(jnp.int32)
    large = jnp.minimum(large, N_BUCKETS - 1)
    return jnp.where(d < max_exact, d, large)


def _band_kernel(row_ref, o_ref):
    rows = jnp.broadcast_to(row_ref[...], (o_ref.shape[0], row_ref.shape[1]))
    o_ref[...] = pltpu.roll(rows, 0, 1, stride=1, stride_axis=0)[:, :o_ref.shape[1]]


def _bias_band(table, T):
    y = np.arange(3 * T)
    dist = np.clip(np.where(y <= 2 * T, T - y, 4 * T - y), 0, 2 * T - 1)
    row = table.astype(F32)[_t5_bucket(jnp.asarray(dist, jnp.int32))].T
    ch = row.shape[0]
    return pl.pallas_call(
        _band_kernel,
        out_shape=jax.ShapeDtypeStruct((ch, T, 2 * T), F32),
        grid=(ch,),
        in_specs=[pl.BlockSpec((None, 1, 3 * T), lambda c: (c, 0, 0))],
        out_specs=pl.BlockSpec((None, T, 2 * T), lambda c: (c, 0, 0)),
        compiler_params=_params("parallel"),
        name="bias_band",
    )(row[:, None, :])


def _lambda_init(layer):
    return 0.8 - 0.6 * math.exp(-0.3 * layer)


def _dup_kv_heads(w):
    q_w = SW_Q_HEADS * SW_DIM
    kv = w[:, q_w:].reshape(w.shape[0], 2 * SW_KV_HEADS, 1, SW_DIM)
    kv = jnp.broadcast_to(kv, (w.shape[0], 2 * SW_KV_HEADS, 2, SW_DIM)).reshape(w.shape[0], -1)
    return jnp.concatenate([w[:, :q_w], kv], axis=1)


def _pad_ff(a, axis):
    pad = [(0, 0)] * a.ndim
    pad[axis] = (0, D_FF_PAD - D_FF)
    return jnp.pad(a, pad)


def _split_up_kernel(w_ref, gate_ref, value_ref):
    keep = lax.broadcasted_iota(jnp.int32, gate_ref.shape, 1) < D_FF
    gate_ref[...] = jnp.where(keep, w_ref[:, :D_FF_PAD], 0.0).astype(gate_ref.dtype)
    start = 2 * D_FF - D_FF_PAD
    window = pltpu.roll(w_ref[:, start:], D_FF_PAD - (D_FF - start), 1)
    value_ref[...] = jnp.where(keep, window, 0.0).astype(value_ref.dtype)


def split_up_call(w_up):
    L, D, _ = w_up.shape
    out = jax.ShapeDtypeStruct((L, D, D_FF_PAD), BF16)
    out_spec = pl.BlockSpec((None, PREP_ROWS, D_FF_PAD), lambda l, i: (l, i, 0))
    return pl.pallas_call(
        _split_up_kernel,
        out_shape=(out, out),
        grid=(L, D // PREP_ROWS),
        in_specs=[pl.BlockSpec((None, PREP_ROWS, 2 * D_FF), lambda l, i: (l, i, 0))],
        out_specs=(out_spec, out_spec),
        compiler_params=_params("parallel", "parallel"),
        name="ffn_up_weight_prep",
    )(w_up)


def _pad_down_kernel(w_ref, o_ref):
    row = pl.program_id(1) * PREP_DOWN_ROWS + lax.broadcasted_iota(jnp.int32, o_ref.shape, 0)
    o_ref[...] = jnp.where(row < D_FF, w_ref[...], 0.0).astype(o_ref.dtype)


def pad_down_call(w_down):
    L, _, D = w_down.shape
    spec = pl.BlockSpec((None, PREP_DOWN_ROWS, D), lambda l, i: (l, i, 0))
    return pl.pallas_call(
        _pad_down_kernel,
        out_shape=jax.ShapeDtypeStruct((L, D_FF_PAD, D), BF16),
        grid=(L, D_FF_PAD // PREP_DOWN_ROWS),
        in_specs=[spec],
        out_specs=spec,
        compiler_params=_params("parallel", "parallel"),
        name="ffn_down_weight_prep",
    )(w_down)


def kernel(x, rel_bias, attn_norm, ffn_norm, w_o, da_w_qkv, da_lambda, da_subln, sb_w_qkv, sw_w_qkv,
           sw_sinks, ffn_w_up, ffn_conv_w, ffn_conv_b, ffn_w_down, final_norm):
    B, S, D = x.shape
    x2 = x.reshape(B * S, D)
    da_bias_near = _bias_band(rel_bias - rel_bias[N_BUCKETS - 1], ATT_T)
    sw_bias = _bias_band(rel_bias, SW_BLOCK)
    da_w = da_w_qkv.astype(BF16)
    sb_w = sb_w_qkv.astype(BF16)
    sw_w = jax.vmap(_dup_kv_heads)(sw_w_qkv).astype(BF16)
    wo_w = w_o.astype(BF16)
    wg, wv = split_up_call(ffn_w_up)
    cwg = _pad_ff(ffn_conv_w[:, :, :D_FF], 2)
    cwv = _pad_ff(ffn_conv_w[:, :, D_FF:], 2)
    cbg = _pad_ff(ffn_conv_b[:, None, :D_FF], 2)
    cbv = _pad_ff(ffn_conv_b[:, None, D_FF:], 2)
    wd = pad_down_call(ffn_w_down)
    attn_gain = attn_norm[:, None, :]
    ffn_gain = ffn_norm[:, None, :]
    final_gain = final_norm[None, None, :]

    weights_qkv = [((da_w, sb_w, sw_w)[layer % N_MIXERS], layer // N_MIXERS) for layer in range(DEPTH)]
    qkv = proj_call(x2, *weights_qkv[0], attn_gain, 0)
    for layer in range(DEPTH):
        mixer = layer % N_MIXERS
        slot = layer // N_MIXERS
        if mixer == 0:
            m = da_call(qkv, da_bias_near, da_lambda[slot], da_subln[slot], B, S, _lambda_init(layer))
        elif mixer == 1:
            m = sb_call(qkv, B, S)
        else:
            m = sw_call(qkv, sw_bias, sw_sinks[slot], B, S)
        x2, act = wo_ffn_up_call(m, wo_w, x2, ffn_gain, wg, wv, cwg, cwv, cbg, cbv, layer, S)
        if layer < DEPTH - 1:
            x2, qkv = ffn_down_call(act, wd, layer, x2, attn_gain, layer + 1, *weights_qkv[layer + 1])
        else:
            h = ffn_down_call(act, wd, layer, x2, final_gain, 0)
    return h.reshape(B, S, D)
```

```python
import functools
import math

import jax
import jax.numpy as jnp
import numpy as np
from jax import lax
from jax.experimental import pallas as pl
from jax.experimental.pallas import tpu as pltpu

D_MODEL = 1024
DEPTH = 4
N_MIXERS = 3
N_BUCKETS = 32
MAX_DISTANCE = 128
DA_HEADS = 8
DA_QK_DIM = 64
DA_V_DIM = 128
SB_HEADS = 16
SB_DIM = 64
SW_Q_HEADS = 16
SW_KV_HEADS = 4
SW_DIM = 64
SW_BLOCK = 128
D_FF = 2752
EPS = 1e-6
NEG = -1e30

LANES = 128
SUBLANES = 8
VMEM_LIMIT = 52 * 1024 * 1024
WO_FFN_VMEM_LIMIT = 57 * 1024 * 1024

ATT_T = 256
DA_SLOTS = 2
SB_EXP_ZERO = -104.0
ROW_TILE = 512
PROJ_TM = 1024
PROJ_CHUNK = 512
PREP_ROWS = 256
PREP_DOWN_ROWS = 704
SW_LAG = 2
FFN_TM = 1024
FFN_TF = 256
D_FF_PAD = 2816

BF16 = jnp.bfloat16
F32 = jnp.float32


def _params(*sem):
    return pltpu.CompilerParams(dimension_semantics=sem, vmem_limit_bytes=VMEM_LIMIT)


def _layer_spec(arr, layer, single_buffer=False):
    shape = (None,) + arr.shape[1:]
    index_map = lambda *_: (layer,) + (0,) * (arr.ndim - 1)
    if single_buffer:
        return pl.BlockSpec(shape, index_map, pipeline_mode=pl.Buffered(1))
    return pl.BlockSpec(shape, index_map)


def _rms(xf, gain):
    return xf * lax.rsqrt(jnp.mean(xf * xf, axis=-1, keepdims=True) + EPS) * gain


def _project(h, w_ref, o_ref):
    for c in range(o_ref.shape[1] // PROJ_CHUNK):
        sl = slice(c * PROJ_CHUNK, (c + 1) * PROJ_CHUNK)
        o_ref[:, sl] = jnp.dot(h, w_ref[:, sl], preferred_element_type=F32).astype(o_ref.dtype)


def _norm_proj_kernel(x_ref, g_ref, w_ref, o_ref):
    _project(_rms(x_ref[...], g_ref[...]).astype(BF16), w_ref, o_ref)


def proj_call(x2, w, layer, gain, gain_layer):
    M, D = x2.shape
    N = w.shape[2]
    row = lambda i: (i, 0)
    return pl.pallas_call(
        _norm_proj_kernel,
        out_shape=jax.ShapeDtypeStruct((M, N), BF16),
        grid=(M // PROJ_TM,),
        in_specs=[pl.BlockSpec((PROJ_TM, D), row), _layer_spec(gain, gain_layer),
                  _layer_spec(w, layer, single_buffer=True)],
        out_specs=pl.BlockSpec((PROJ_TM, N), row),
        compiler_params=_params("parallel"),
        name="qkv_proj",
    )(x2, gain, w)


def _ffn_row_starts(n_groups):
    per = n_groups // SUBLANES
    return [SUBLANES * SUBLANES * (j % per) + j // per for j in range(n_groups)]


def _stage(ref, x):
    for c in range(ref.shape[0]):
        ref[c] = x[:, c * LANES:(c + 1) * LANES]


def _stage_interleaved(ref, x):
    for c in range(ref.shape[0]):
        for j, st in enumerate(_ffn_row_starts(x.shape[0] // SUBLANES)):
            ref[c, pl.ds(st, SUBLANES, stride=SUBLANES), :] = x[j * SUBLANES:(j + 1) * SUBLANES,
                                                                c * LANES:(c + 1) * LANES]


def _deinterleave_rows(ref, c):
    return jnp.concatenate([ref[c, pl.ds(st, SUBLANES, stride=SUBLANES), :]
                            for st in _ffn_row_starts(ref.shape[1] // SUBLANES)], axis=0)


def _causal_conv3(u, cw, cb, prev):
    nblk, g, _, tf = u.shape
    first = lax.broadcasted_iota(jnp.int32, (SUBLANES, tf), 0) == 0
    u1, u2 = [], []
    for b in range(nblk):
        tail = prev if b == 0 else u[b - 1, g - 2:]
        wrap = [jnp.where(first, pltpu.roll(tail[e], 1, 0), pltpu.roll(u[b, g - 2 + e], 1, 0))
                for e in range(2)]
        u1.append(jnp.concatenate([wrap[1][None], u[b, :g - 1]], axis=0))
        u2.append(jnp.concatenate([wrap[0][None], wrap[1][None], u[b, :g - 2]], axis=0))
    u1 = jnp.stack(u1)
    u2 = jnp.stack(u2)
    return u * cw[2:3] + u1 * cw[1:2] + u2 * cw[0:1] + cb


def _wo_ffn_up_kernel(m_ref, wo_ref, x_ref, gain_ref, wg_ref, wv_ref, cwg_ref, cwv_ref, cbg_ref, cbv_ref,
                      xo_ref, g_ref, carry_ref, stage_ref, h_ref, *, tiles_per_seq):
    tm = m_ref.shape[0]
    nblk = tm // ROW_TILE
    g = ROW_TILE // SUBLANES

    @pl.when(pl.program_id(0) % tiles_per_seq == 0)
    def _():
        carry_ref[...] = jnp.zeros(carry_ref.shape, F32)

    for b in range(nblk):
        rows = slice(b * ROW_TILE, (b + 1) * ROW_TILE)
        xn = x_ref[rows, :] + jnp.dot(m_ref[rows, :], wo_ref[...], preferred_element_type=F32)
        xo_ref[rows, :] = xn
        _stage_interleaved(stage_ref, _rms(xn, gain_ref[...]))
        for c in range(stage_ref.shape[0]):
            h_ref[rows, c * LANES:(c + 1) * LANES] = stage_ref[c].astype(h_ref.dtype)

    h = h_ref[...]
    for c in range(g_ref.shape[1] // FFN_TF):
        sl = slice(c * FFN_TF, (c + 1) * FFN_TF)
        ug = jnp.dot(h, wg_ref[:, sl], preferred_element_type=F32).reshape(nblk, g, SUBLANES, FFN_TF)
        uv = jnp.dot(h, wv_ref[:, sl], preferred_element_type=F32).reshape(nblk, g, SUBLANES, FFN_TF)
        prev_g = carry_ref[0, :, :, sl]
        prev_v = carry_ref[1, :, :, sl]
        carry_ref[0, :, :, sl] = ug[nblk - 1, g - 2:]
        carry_ref[1, :, :, sl] = uv[nblk - 1, g - 2:]
        yg = _causal_conv3(ug, cwg_ref[:, sl], cbg_ref[:, sl], prev_g)
        yv = _causal_conv3(uv, cwv_ref[:, sl], cbv_ref[:, sl], prev_v)
        act = yg * (1.0 / (1.0 + jnp.exp(-yg))) * yv
        g_ref[:, sl] = act.reshape(tm, FFN_TF).astype(g_ref.dtype)


def wo_ffn_up_call(m, wo, x2, gain, wg, wv, cwg, cwv, cbg, cbv, layer, seq):
    M, D = x2.shape
    F = wg.shape[2]
    row = lambda i: (i, 0)
    resident = lambda arr: _layer_spec(arr, layer, single_buffer=True)
    return pl.pallas_call(
        functools.partial(_wo_ffn_up_kernel, tiles_per_seq=seq // FFN_TM),
        out_shape=(jax.ShapeDtypeStruct((M, D), F32), jax.ShapeDtypeStruct((M, F), BF16)),
        grid=(M // FFN_TM,),
        in_specs=[pl.BlockSpec((FFN_TM, D), row), resident(wo), pl.BlockSpec((FFN_TM, D), row), resident(gain),
                  resident(wg), resident(wv), resident(cwg), resident(cwv), resident(cbg), resident(cbv)],
        out_specs=(pl.BlockSpec((FFN_TM, D), row), pl.BlockSpec((FFN_TM, F), row)),
        scratch_shapes=[pltpu.VMEM((2, 2, SUBLANES, F), F32),
                        pltpu.VMEM((D // LANES, ROW_TILE, LANES), F32),
                        pltpu.VMEM((FFN_TM, D), BF16)],
        compiler_params=pltpu.CompilerParams(dimension_semantics=("arbitrary",),
                                             vmem_limit_bytes=WO_FFN_VMEM_LIMIT),
        name="wo_ffn_up_conv_gate",
    )(m, wo, x2, gain, wg, wv, cwg, cwv, cbg, cbv)


def _ffn_down_kernel(a_ref, w_ref, x_ref, g_ref, *refs, project):
    if project:
        wq_ref, xo_ref, qkv_ref, stage_ref = refs
    else:
        h_ref, stage_ref, xo_ref = refs
    _stage(stage_ref, jnp.dot(a_ref[...], w_ref[...], preferred_element_type=F32))
    for c in range(stage_ref.shape[0]):
        sl = slice(c * LANES, (c + 1) * LANES)
        xo_ref[:, sl] = x_ref[:, sl] + _deinterleave_rows(stage_ref, c)
    h = _rms(xo_ref[...], g_ref[...])
    if project:
        _project(h.astype(BF16), wq_ref, qkv_ref)
    else:
        h_ref[...] = h.astype(h_ref.dtype)


def ffn_down_call(a, w, layer, x2, gain, gain_layer, w_qkv=None, qkv_layer=0):
    M, D = x2.shape
    F = a.shape[1]
    row = lambda i: (i, 0)
    project = w_qkv is not None
    in_specs = [pl.BlockSpec((ROW_TILE, F), row), _layer_spec(w, layer, single_buffer=True),
                pl.BlockSpec((ROW_TILE, D), row), _layer_spec(gain, gain_layer)]
    stage = pltpu.VMEM((D // LANES, ROW_TILE, LANES), F32)
    if project:
        N = w_qkv.shape[2]
        in_specs.append(_layer_spec(w_qkv, qkv_layer, single_buffer=True))
        out_shape = (jax.ShapeDtypeStruct((M, D), F32), jax.ShapeDtypeStruct((M, N), BF16))
        out_specs = (pl.BlockSpec((ROW_TILE, D), row), pl.BlockSpec((ROW_TILE, N), row))
        scratch, args = [stage], (a, w, x2, gain, w_qkv)
    else:
        out_shape, out_specs = jax.ShapeDtypeStruct((M, D), F32), pl.BlockSpec((ROW_TILE, D), row)
        scratch, args = [stage, pltpu.VMEM((ROW_TILE, D), F32)], (a, w, x2, gain)
    return pl.pallas_call(
        functools.partial(_ffn_down_kernel, project=project),
        out_shape=out_shape,
        grid=(M // ROW_TILE,),
        in_specs=in_specs,
        out_specs=out_specs,
        scratch_shapes=scratch,
        compiler_params=_params("parallel"),
        name="ffn_down_residual_norm",
    )(*args)


def _half_masks(q):
    lane = lax.broadcasted_iota(jnp.int32, q.shape, 1)
    zero = jnp.zeros_like(q)
    scale = jnp.asarray(DA_QK_DIM ** -0.5, q.dtype)
    return (jnp.where(lane < 64, q, zero) * scale, jnp.where(lane >= 64, q, zero) * scale)


def _qk(q, k):
    return lax.dot_general(q, k, (((1,), (1,)), ((), ())), preferred_element_type=F32)


def _da_kernel(q_ref, k_ref, v_ref, bias_ref, lam_ref, sub_ref, o_ref, s_ref, *, lambda_init):
    T = ATT_T
    nq = q_ref.shape[0] // T
    r = lax.broadcasted_iota(jnp.int32, (T, T), 0)
    cc = lax.broadcasted_iota(jnp.int32, (T, T), 1)
    causal = r >= cc
    lam = lam_ref[...]
    lam_full = (jnp.exp(jnp.sum(lam[0:1] * lam[1:2], keepdims=True))
                - jnp.exp(jnp.sum(lam[2:3] * lam[3:4], keepdims=True)) + lambda_init)
    def score_steps(qi, st):
        def first():
            st["q01"] = jnp.concatenate(_half_masks(q_ref[qi * T:(qi + 1) * T, :]), axis=0)
            st["mx"] = [None, None]

        def tile(j):
            if j == 0:
                first()
            ss = _qk(st["q01"], k_ref[j * T:(j + 1) * T, :])
            for c in range(2):
                s = ss[c * T:(c + 1) * T]
                if j == qi - 1:
                    s = s + bias_ref[c, :, 0:T]
                if j == qi:
                    s = jnp.where(causal, s + bias_ref[c, :, T:2 * T], NEG)
                s_ref[qi % DA_SLOTS, c, :, j * T:(j + 1) * T] = s
                t = jnp.maximum(s[:, :LANES], s[:, LANES:])
                st["mx"][c] = t if st["mx"][c] is None else jnp.maximum(st["mx"][c], t)
        return [functools.partial(tile, j) for j in range(qi + 1)]

    def value_steps(qi, st):
        def exp_tile(j):
            if j == 0:
                st["m2"] = []
                for c in range(2):
                    m = jnp.broadcast_to(jnp.max(st["mx"][c], axis=-1, keepdims=True), (T, LANES))
                    st["m2"].append(jnp.concatenate([m, m], axis=1))
                st["lsum"] = [None, None]
            for c in range(2):
                p = jnp.exp(s_ref[qi % DA_SLOTS, c, :, j * T:(j + 1) * T] - st["m2"][c])
                s_ref[qi % DA_SLOTS, c, :, j * T:(j + 1) * T] = p
                t = p[:, :LANES] + p[:, LANES:]
                st["lsum"][c] = t if st["lsum"][c] is None else st["lsum"][c] + t

        def value_tile(j):
            if j == 0:
                l0 = jnp.sum(st["lsum"][0], axis=-1, keepdims=True)
                l1 = jnp.sum(st["lsum"][1], axis=-1, keepdims=True)
                st["ratio"] = lam_full * l0 / l1
                st["inv_l0"] = 1.0 / l0
                st["acc"] = None
            a = (s_ref[qi % DA_SLOTS, 0, :, j * T:(j + 1) * T]
                 - s_ref[qi % DA_SLOTS, 1, :, j * T:(j + 1) * T] * st["ratio"])
            pv = jnp.dot(a.astype(BF16), v_ref[j * T:(j + 1) * T, :], preferred_element_type=F32)
            st["acc"] = pv if st["acc"] is None else st["acc"] + pv
            if j == qi:
                o = _rms(st["acc"] * st["inv_l0"], sub_ref[...]) * (1.0 - lambda_init)
                o_ref[qi * T:(qi + 1) * T, :] = o.astype(o_ref.dtype)
        return ([functools.partial(exp_tile, j) for j in range(qi + 1)],
                [functools.partial(value_tile, j) for j in range(qi + 1)])

    states = [dict() for _ in range(nq)]
    stages = [score_steps(qi, states[qi]) for qi in range(nq)]
    later = [value_steps(qi, states[qi]) for qi in range(nq)]
    for r in range(nq + 1):
        lists = [stages[r] if r < nq else [],
                 later[r - 1][0] + later[r - 1][1] if r >= 1 else []]
        for t in range(max(len(steps) for steps in lists)):
            for steps in lists:
                if t < len(steps):
                    steps[t]()


def da_call(qkv, bias_near, lam, subln, batch, seq, lambda_init):
    T = ATT_T
    nh = DA_HEADS
    return pl.pallas_call(
        functools.partial(_da_kernel, lambda_init=lambda_init),
        out_shape=jax.ShapeDtypeStruct((batch * seq, nh * DA_V_DIM), BF16),
        grid=(nh, batch),
        in_specs=[pl.BlockSpec((seq, LANES), lambda h, b: (b, h)),
                  pl.BlockSpec((seq, LANES), lambda h, b: (b, nh + h)),
                  pl.BlockSpec((seq, LANES), lambda h, b: (b, 2 * nh + h)),
                  pl.BlockSpec((2, T, 2 * T), lambda h, b: (h, 0, 0)),
                  pl.BlockSpec((4, DA_QK_DIM), lambda h, b: (0, 0)),
                  pl.BlockSpec((1, DA_V_DIM), lambda h, b: (0, 0))],
        out_specs=pl.BlockSpec((seq, LANES), lambda h, b: (b, h)),
        scratch_shapes=[pltpu.VMEM((DA_SLOTS, 2, T, seq), F32)],
        compiler_params=_params("parallel", "parallel"),
        name="diff_attention",
    )(qkv, qkv, qkv, bias_near, lam, subln.reshape(1, DA_V_DIM))


def _sb_kernel(q_ref, k_ref, v_ref, o_ref, lbn_ref, hmn_ref, lbf_ref, hmf_ref, acc_ref):
    T = ATT_T
    nq = q_ref.shape[0] // T
    kr = lax.broadcasted_iota(jnp.int32, (T, T), 0)
    kc = lax.broadcasted_iota(jnp.int32, (T, T), 1)
    suffix = jnp.where(kr > kc, 1.0, 0.0).astype(BF16)
    strict = kc < kr
    lane = lax.broadcasted_iota(jnp.int32, (T, LANES), 1)
    strict2 = jnp.concatenate([strict, strict], axis=0)

    def score_tile(qi, q01, j, blk, carry, lb_view, hm_view):
        zz = _qk(q01, k_ref[j * T:(j + 1) * T, :])
        for c in range(2):
            z = zz[c * T:(c + 1) * T]
            lb = jnp.minimum(z, 0.0) - jnp.log(1.0 + jnp.exp(-jnp.abs(z)))
            log_1m_beta = lb - z
            if j == qi:
                log_1m_beta = jnp.where(strict, log_1m_beta, 0.0)
            rows = slice(blk + c * T, blk + (c + 1) * T)
            hm_view[rows, :] = log_1m_beta.astype(BF16)
            if carry[c] is not None:
                lb = lb + carry[c]
            lb_view[rows, :] = lb
            if j > 0:
                rs = jnp.sum(log_1m_beta, axis=-1, keepdims=True)
                carry[c] = rs if carry[c] is None else carry[c] + rs

    def value_tiles(qi, tiles, lb_view, hm_view):
        between = jnp.dot(hm_view[0:2 * len(tiles) * T, :], suffix, preferred_element_type=F32)
        acc = None
        for pos, j in enumerate(tiles):
            rows = slice(2 * pos * T, 2 * (pos + 1) * T)
            a = jnp.exp(lb_view[rows, :] + between[rows])
            if j == qi:
                a = jnp.where(strict2, a, 0.0)
            pv = jnp.dot(a.astype(BF16), v_ref[j * T:(j + 1) * T, :], preferred_element_type=F32)
            acc = pv if acc is None else acc + pv
        return acc

    def stacked_q(qi):
        return jnp.concatenate(_half_masks(q_ref[qi * T:(qi + 1) * T, :]), axis=0)

    near = [[j for j in (qi, qi - 1) if j >= 0] for qi in range(nq)]
    carries = [[None, None] for _ in range(nq)]
    for qi in range(nq + 1):
        if qi < nq:
            q01 = stacked_q(qi)
            for pos, j in enumerate(near[qi]):
                score_tile(qi, q01, j, 2 * pos * T, carries[qi], lbn_ref.at[qi % 2], hmn_ref.at[qi % 2])
        if qi > 0:
            acc_ref[qi - 1] = value_tiles(qi - 1, near[qi - 1], lbn_ref.at[(qi - 1) % 2], hmn_ref.at[(qi - 1) % 2])

    for qi in range(2, nq):
        far = list(range(qi - 2, -1, -1))
        carry = carries[qi]
        nearest_sum = jnp.max(jnp.maximum(carry[0], carry[1]))

        @pl.when(nearest_sum > SB_EXP_ZERO)
        def _():
            q01 = stacked_q(qi)
            far_carry = list(carry)
            for pos, j in enumerate(far):
                score_tile(qi, q01, j, 2 * pos * T, far_carry, lbf_ref, hmf_ref)
            acc_ref[qi] += value_tiles(qi, far, lbf_ref, hmf_ref)

    for qi in range(nq):
        acc = acc_ref[qi]
        o_ref[qi * T:(qi + 1) * T, :] = jnp.where(lane < 64, acc[0:T], acc[T:2 * T]).astype(o_ref.dtype)


def sb_call(qkv, batch, seq):
    npair = SB_HEADS // 2
    return pl.pallas_call(
        _sb_kernel,
        out_shape=jax.ShapeDtypeStruct((batch * seq, SB_HEADS * SB_DIM), BF16),
        grid=(batch, npair),
        in_specs=[pl.BlockSpec((seq, LANES), lambda b, p: (b, p)),
                  pl.BlockSpec((seq, LANES), lambda b, p: (b, npair + p)),
                  pl.BlockSpec((seq, LANES), lambda b, p: (b, 2 * npair + p))],
        out_specs=pl.BlockSpec((seq, LANES), lambda b, p: (b, p)),
        scratch_shapes=[pltpu.VMEM((2, 4 * ATT_T, ATT_T), F32), pltpu.VMEM((2, 4 * ATT_T, ATT_T), BF16),
                        pltpu.VMEM((2 * seq - 4 * ATT_T, ATT_T), F32),
                        pltpu.VMEM((2 * seq - 4 * ATT_T, ATT_T), BF16),
                        pltpu.VMEM((seq // ATT_T, 2 * ATT_T, LANES), F32)],
        compiler_params=_params("parallel", "parallel"),
        name="stick_breaking_attention",
    )(qkv, qkv, qkv)


def _sw_kernel(sink_ref, q_ref, k_ref, v_ref, bias_ref, o_ref, s_ref):
    W = SW_BLOCK
    nb = q_ref.shape[0] // W
    p_id = pl.program_id(1)
    r = lax.broadcasted_iota(jnp.int32, (2 * W, 2 * W), 0) & (W - 1)
    cidx = lax.broadcasted_iota(jnp.int32, (2 * W, 2 * W), 1)
    valid = ((cidx < W) & (cidx > r)) | ((cidx >= W) & (cidx - W <= r))
    biasm = jnp.where(valid, bias_ref[...], NEG)
    row = lax.broadcasted_iota(jnp.int32, (2 * W, 1), 0)
    sink = jnp.where(row < W, sink_ref[2 * p_id], sink_ref[2 * p_id + 1])
    lane = lax.broadcasted_iota(jnp.int32, (W, LANES), 1)
    def scores(n):
        q01 = jnp.concatenate(_half_masks(q_ref[n * W:(n + 1) * W, :]), axis=0)
        if n == 0:
            s_ref[0, :, W:] = _qk(q01, k_ref[0:W, :]) + biasm[:, W:]
        else:
            s_ref[n] = _qk(q01, k_ref[(n - 1) * W:(n + 1) * W, :]) + biasm

    def values(n):
        s = s_ref[0, :, W:] if n == 0 else s_ref[n]
        m = jnp.maximum(jnp.max(s, axis=-1, keepdims=True), sink)
        e = jnp.exp(s - m)
        denom = jnp.sum(e, axis=-1, keepdims=True) + jnp.exp(sink - m)
        v = v_ref[0:W, :] if n == 0 else v_ref[(n - 1) * W:(n + 1) * W, :]
        o = jnp.dot(e.astype(BF16), v, preferred_element_type=F32) / denom
        o_ref[n * W:(n + 1) * W, :] = jnp.where(lane < 64, o[0:W], o[W:]).astype(o_ref.dtype)

    for n in range(nb + SW_LAG):
        if n < nb:
            scores(n)
        if n >= SW_LAG:
            values(n - SW_LAG)


def sw_call(qkv, bias_band, sinks, batch, seq):
    npair = SW_Q_HEADS // 2
    q_blocks = SW_Q_HEADS * SW_DIM // LANES
    W = SW_BLOCK
    return pl.pallas_call(
        _sw_kernel,
        out_shape=jax.ShapeDtypeStruct((batch * seq, SW_Q_HEADS * SW_DIM), BF16),
        grid=(batch, npair),
        in_specs=[pl.BlockSpec(memory_space=pltpu.SMEM),
                  pl.BlockSpec((seq, LANES), lambda b, p: (b, p)),
                  pl.BlockSpec((seq, LANES), lambda b, p: (b, q_blocks + p // 2)),
                  pl.BlockSpec((seq, LANES), lambda b, p: (b, q_blocks + SW_KV_HEADS + p // 2)),
                  pl.BlockSpec((None, 2 * W, 2 * W), lambda b, p: (p, 0, 0))],
        out_specs=pl.BlockSpec((seq, LANES), lambda b, p: (b, p)),
        scratch_shapes=[pltpu.VMEM((seq // W, 2 * W, 2 * W), F32)],
        compiler_params=_params("parallel", "parallel"),
        name="sliding_window_attention",
    )(sinks, qkv, qkv, qkv, bias_band.reshape(npair, 2 * W, 2 * W))


def _t5_bucket(dist):
    max_exact = N_BUCKETS // 2
    d = jnp.maximum(dist, 0)
    large = max_exact + (jnp.log(jnp.maximum(d, 1).astype(F32) / max_exact)
                         / math.log(MAX_DISTANCE / max_exact) * (N_BUCKETS - max_exact)).astype(jnp.int32)
    large = jnp.minimum(large, N_BUCKETS - 1)
    return jnp.where(d < max_exact, d, large)


def _band_kernel(row_ref, o_ref):
    rows = jnp.broadcast_to(row_ref[...], (o_ref.shape[0], row_ref.shape[1]))
    o_ref[...] = pltpu.roll(rows, 0, 1, stride=1, stride_axis=0)[:, :o_ref.shape[1]]


def _bias_band(table, T):
    y = np.arange(3 * T)
    dist = np.clip(np.where(y <= 2 * T, T - y, 4 * T - y), 0, 2 * T - 1)
    row = table.astype(F32)[_t5_bucket(jnp.asarray(dist, jnp.int32))].T
    ch = row.shape[0]
    return pl.pallas_call(
        _band_kernel,
        out_shape=jax.ShapeDtypeStruct((ch, T, 2 * T), F32),
        grid=(ch,),
        in_specs=[pl.BlockSpec((None, 1, 3 * T), lambda c: (c, 0, 0))],
        out_specs=pl.BlockSpec((None, T, 2 * T), lambda c: (c, 0, 0)),
        compiler_params=_params("parallel"),
        name="bias_band",
    )(row[:, None, :])


def _lambda_init(layer):
    return 0.8 - 0.6 * math.exp(-0.3 * layer)


def _dup_kv_heads(w):
    q_w = SW_Q_HEADS * SW_DIM
    kv = w[:, q_w:].reshape(w.shape[0], 2 * SW_KV_HEADS, 1, SW_DIM)
    kv = jnp.broadcast_to(kv, (w.shape[0], 2 * SW_KV_HEADS, 2, SW_DIM)).reshape(w.shape[0], -1)
    return jnp.concatenate([w[:, :q_w], kv], axis=1)


def _pad_ff(a, axis):
    pad = [(0, 0)] * a.ndim
    pad[axis] = (0, D_FF_PAD - D_FF)
    return jnp.pad(a, pad)


def _split_up_kernel(w_ref, gate_ref, value_ref):
    keep = lax.broadcasted_iota(jnp.int32, gate_ref.shape, 1) < D_FF
    gate_ref[...] = jnp.where(keep, w_ref[:, :D_FF_PAD], 0.0).astype(gate_ref.dtype)
    start = 2 * D_FF - D_FF_PAD
    window = pltpu.roll(w_ref[:, start:], D_FF_PAD - (D_FF - start), 1)
    value_ref[...] = jnp.where(keep, window, 0.0).astype(value_ref.dtype)


def split_up_call(w_up):
    L, D, _ = w_up.shape
    out = jax.ShapeDtypeStruct((L, D, D_FF_PAD), BF16)
    out_spec = pl.BlockSpec((None, PREP_ROWS, D_FF_PAD), lambda l, i: (l, i, 0))
    return pl.pallas_call(
        _split_up_kernel,
        out_shape=(out, out),
        grid=(L, D // PREP_ROWS),
        in_specs=[pl.BlockSpec((None, PREP_ROWS, 2 * D_FF), lambda l, i: (l, i, 0))],
        out_specs=(out_spec, out_spec),
        compiler_params=_params("parallel", "parallel"),
        name="ffn_up_weight_prep",
    )(w_up)


def _pad_down_kernel(w_ref, o_ref):
    row = pl.program_id(1) * PREP_DOWN_ROWS + lax.broadcasted_iota(jnp.int32, o_ref.shape, 0)
    o_ref[...] = jnp.where(row < D_FF, w_ref[...], 0.0).astype(o_ref.dtype)


def pad_down_call(w_down):
    L, _, D = w_down.shape
    spec = pl.BlockSpec((None, PREP_DOWN_ROWS, D), lambda l, i: (l, i, 0))
    return pl.pallas_call(
        _pad_down_kernel,
        out_shape=jax.ShapeDtypeStruct((L, D_FF_PAD, D), BF16),
        grid=(L, D_FF_PAD // PREP_DOWN_ROWS),
        in_specs=[spec],
        out_specs=spec,
        compiler_params=_params("parallel", "parallel"),
        name="ffn_down_weight_prep",
    )(w_down)


def kernel(x, rel_bias, attn_norm, ffn_norm, w_o, da_w_qkv, da_lambda, da_subln, sb_w_qkv, sw_w_qkv,
           sw_sinks, ffn_w_up, ffn_conv_w, ffn_conv_b, ffn_w_down, final_norm):
    B, S, D = x.shape
    x2 = x.reshape(B * S, D)
    da_bias_near = _bias_band(rel_bias - rel_bias[N_BUCKETS - 1], ATT_T)
    sw_bias = _bias_band(rel_bias, SW_BLOCK)
    da_w = da_w_qkv.astype(BF16)
    sb_w = sb_w_qkv.astype(BF16)
    sw_w = jax.vmap(_dup_kv_heads)(sw_w_qkv).astype(BF16)
    wo_w = w_o.astype(BF16)
    wg, wv = split_up_call(ffn_w_up)
    cwg = _pad_ff(ffn_conv_w[:, :, :D_FF], 2)
    cwv = _pad_ff(ffn_conv_w[:, :, D_FF:], 2)
    cbg = _pad_ff(ffn_conv_b[:, None, :D_FF], 2)
    cbv = _pad_ff(ffn_conv_b[:, None, D_FF:], 2)
    wd = pad_down_call(ffn_w_down)
    attn_gain = attn_norm[:, None, :]
    ffn_gain = ffn_norm[:, None, :]
    final_gain = final_norm[None, None, :]

    weights_qkv = [((da_w, sb_w, sw_w)[layer % N_MIXERS], layer // N_MIXERS) for layer in range(DEPTH)]
    qkv = proj_call(x2, *weights_qkv[0], attn_gain, 0)
    for layer in range(DEPTH):
        mixer = layer % N_MIXERS
        slot = layer // N_MIXERS
        if mixer == 0:
            m = da_call(qkv, da_bias_near, da_lambda[slot], da_subln[slot], B, S, _lambda_init(layer))
        elif mixer == 1:
            m = sb_call(qkv, B, S)
        else:
            m = sw_call(qkv, sw_bias, sw_sinks[slot], B, S)
        x2, act = wo_ffn_up_call(m, wo_w, x2, ffn_gain, wg, wv, cwg, cwv, cbg, cbv, layer, S)
        if layer < DEPTH - 1:
            x2, qkv = ffn_down_call(act, wd, layer, x2, attn_gain, layer + 1, *weights_qkv[layer + 1])
        else:
            h = ffn_down_call(act, wd, layer, x2, final_gain, 0)
    return h.reshape(B, S, D)
```

```python
import functools
import math

import jax
import jax.numpy as jnp
import numpy as np
from jax import lax
from jax.experimental import pallas as pl
from jax.experimental.pallas import tpu as pltpu

D_MODEL = 1024
DEPTH = 4
N_MIXERS = 3
N_BUCKETS = 32
MAX_DISTANCE = 128
DA_HEADS = 8
DA_QK_DIM = 64
DA_V_DIM = 128
SB_HEADS = 16
SB_DIM = 64
SW_Q_HEADS = 16
SW_KV_HEADS = 4
SW_DIM = 64
SW_BLOCK = 128
D_FF = 2752
EPS = 1e-6
NEG = -1e30
LOG2E = math.log2(math.e)

LANES = 128
SUBLANES = 8
VMEM_LIMIT = 52 * 1024 * 1024
WO_FFN_VMEM_LIMIT = 57 * 1024 * 1024

ATT_T = 256
DA_SLOTS = 2
SB_EXP_ZERO = -104.0
ROW_TILE = 512
PROJ_TM = 1024
PROJ_CHUNK = 512
PREP_ROWS = 256
PREP_DOWN_ROWS = 704
SW_LAG = 2
FFN_TM = 1024
FFN_TF = 256
D_FF_PAD = 2816

BF16 = jnp.bfloat16
F32 = jnp.float32


def _params(*sem):
    return pltpu.CompilerParams(dimension_semantics=sem, vmem_limit_bytes=VMEM_LIMIT)


def _layer_spec(arr, layer, single_buffer=False):
    shape = (None,) + arr.shape[1:]
    index_map = lambda *_: (layer,) + (0,) * (arr.ndim - 1)
    if single_buffer:
        return pl.BlockSpec(shape, index_map, pipeline_mode=pl.Buffered(1))
    return pl.BlockSpec(shape, index_map)


def _rms(xf, gain):
    return xf * lax.rsqrt(jnp.mean(xf * xf, axis=-1, keepdims=True) + EPS) * gain


def _project(h, w_ref, o_ref):
    for c in range(o_ref.shape[1] // PROJ_CHUNK):
        sl = slice(c * PROJ_CHUNK, (c + 1) * PROJ_CHUNK)
        o_ref[:, sl] = jnp.dot(h, w_ref[:, sl], preferred_element_type=F32).astype(o_ref.dtype)


def _norm_proj_kernel(x_ref, g_ref, w_ref, o_ref):
    _project(_rms(x_ref[...], g_ref[...]).astype(BF16), w_ref, o_ref)


def proj_call(x2, w, layer, gain, gain_layer):
    M, D = x2.shape
    N = w.shape[2]
    row = lambda i: (i, 0)
    return pl.pallas_call(
        _norm_proj_kernel,
        out_shape=jax.ShapeDtypeStruct((M, N), BF16),
        grid=(M // PROJ_TM,),
        in_specs=[pl.BlockSpec((PROJ_TM, D), row), _layer_spec(gain, gain_layer),
                  _layer_spec(w, layer, single_buffer=True)],
        out_specs=pl.BlockSpec((PROJ_TM, N), row),
        compiler_params=_params("parallel"),
        name="qkv_proj",
    )(x2, gain, w)


def _ffn_row_starts(n_groups):
    per = n_groups // SUBLANES
    return [SUBLANES * SUBLANES * (j % per) + j // per for j in range(n_groups)]


def _stage(ref, x):
    for c in range(ref.shape[0]):
        ref[c] = x[:, c * LANES:(c + 1) * LANES]


def _stage_interleaved(ref, x):
    for c in range(ref.shape[0]):
        for j, st in enumerate(_ffn_row_starts(x.shape[0] // SUBLANES)):
            ref[c, pl.ds(st, SUBLANES, stride=SUBLANES), :] = x[j * SUBLANES:(j + 1) * SUBLANES,
                                                                c * LANES:(c + 1) * LANES]


def _deinterleave_rows(ref, c):
    return jnp.concatenate([ref[c, pl.ds(st, SUBLANES, stride=SUBLANES), :]
                            for st in _ffn_row_starts(ref.shape[1] // SUBLANES)], axis=0)


def _causal_conv3(u, cw, cb, prev):
    nblk, g, _, tf = u.shape
    first = lax.broadcasted_iota(jnp.int32, (SUBLANES, tf), 0) == 0
    u1, u2 = [], []
    for b in range(nblk):
        tail = prev if b == 0 else u[b - 1, g - 2:]
        wrap = [jnp.where(first, pltpu.roll(tail[e], 1, 0), pltpu.roll(u[b, g - 2 + e], 1, 0))
                for e in range(2)]
        u1.append(jnp.concatenate([wrap[1][None], u[b, :g - 1]], axis=0))
        u2.append(jnp.concatenate([wrap[0][None], wrap[1][None], u[b, :g - 2]], axis=0))
    u1 = jnp.stack(u1)
    u2 = jnp.stack(u2)
    return u * cw[2:3] + u1 * cw[1:2] + u2 * cw[0:1] + cb


def _wo_ffn_up_kernel(m_ref, wo_ref, x_ref, gain_ref, wg_ref, wv_ref, cwg_ref, cwv_ref, cbg_ref, cbv_ref,
                      xo_ref, g_ref, carry_ref, stage_ref, h_ref, *, tiles_per_seq):
    tm = m_ref.shape[0]
    nblk = tm // ROW_TILE
    g = ROW_TILE // SUBLANES

    @pl.when(pl.program_id(0) % tiles_per_seq == 0)
    def _():
        carry_ref[...] = jnp.zeros(carry_ref.shape, F32)

    for b in range(nblk):
        rows = slice(b * ROW_TILE, (b + 1) * ROW_TILE)
        xn = x_ref[rows, :] + jnp.dot(m_ref[rows, :], wo_ref[...], preferred_element_type=F32)
        xo_ref[rows, :] = xn
        _stage_interleaved(stage_ref, _rms(xn, gain_ref[...]))
        for c in range(stage_ref.shape[0]):
            h_ref[rows, c * LANES:(c + 1) * LANES] = stage_ref[c].astype(h_ref.dtype)

    h = h_ref[...]
    for c in range(g_ref.shape[1] // FFN_TF):
        sl = slice(c * FFN_TF, (c + 1) * FFN_TF)
        ug = jnp.dot(h, wg_ref[:, sl], preferred_element_type=F32).reshape(nblk, g, SUBLANES, FFN_TF)
        uv = jnp.dot(h, wv_ref[:, sl], preferred_element_type=F32).reshape(nblk, g, SUBLANES, FFN_TF)
        prev_g = carry_ref[0, :, :, sl]
        prev_v = carry_ref[1, :, :, sl]
        carry_ref[0, :, :, sl] = ug[nblk - 1, g - 2:]
        carry_ref[1, :, :, sl] = uv[nblk - 1, g - 2:]
        yg = _causal_conv3(ug, cwg_ref[:, sl], cbg_ref[:, sl], prev_g)
        yv = _causal_conv3(uv, cwv_ref[:, sl], cbv_ref[:, sl], prev_v)
        act = yg * (1.0 / (1.0 + jnp.exp(-yg))) * yv
        g_ref[:, sl] = act.reshape(tm, FFN_TF).astype(g_ref.dtype)


def wo_ffn_up_call(m, wo, x2, gain, wg, wv, cwg, cwv, cbg, cbv, layer, seq):
    M, D = x2.shape
    F = wg.shape[2]
    row = lambda i: (i, 0)
    resident = lambda arr: _layer_spec(arr, layer, single_buffer=True)
    return pl.pallas_call(
        functools.partial(_wo_ffn_up_kernel, tiles_per_seq=seq // FFN_TM),
        out_shape=(jax.ShapeDtypeStruct((M, D), F32), jax.ShapeDtypeStruct((M, F), BF16)),
        grid=(M // FFN_TM,),
        in_specs=[pl.BlockSpec((FFN_TM, D), row), resident(wo), pl.BlockSpec((FFN_TM, D), row), resident(gain),
                  resident(wg), resident(wv), resident(cwg), resident(cwv), resident(cbg), resident(cbv)],
        out_specs=(pl.BlockSpec((FFN_TM, D), row), pl.BlockSpec((FFN_TM, F), row)),
        scratch_shapes=[pltpu.VMEM((2, 2, SUBLANES, F), F32),
                        pltpu.VMEM((D // LANES, ROW_TILE, LANES), F32),
                        pltpu.VMEM((FFN_TM, D), BF16)],
        compiler_params=pltpu.CompilerParams(dimension_semantics=("arbitrary",),
                                             vmem_limit_bytes=WO_FFN_VMEM_LIMIT),
        name="wo_ffn_up_conv_gate",
    )(m, wo, x2, gain, wg, wv, cwg, cwv, cbg, cbv)


def _ffn_down_kernel(a_ref, w_ref, x_ref, g_ref, *refs, project):
    if project:
        wq_ref, xo_ref, qkv_ref, stage_ref = refs
    else:
        h_ref, stage_ref, xo_ref = refs
    _stage(stage_ref, jnp.dot(a_ref[...], w_ref[...], preferred_element_type=F32))
    for c in range(stage_ref.shape[0]):
        sl = slice(c * LANES, (c + 1) * LANES)
        xo_ref[:, sl] = x_ref[:, sl] + _deinterleave_rows(stage_ref, c)
    h = _rms(xo_ref[...], g_ref[...])
    if project:
        _project(h.astype(BF16), wq_ref, qkv_ref)
    else:
        h_ref[...] = h.astype(h_ref.dtype)


def ffn_down_call(a, w, layer, x2, gain, gain_layer, w_qkv=None, qkv_layer=0):
    M, D = x2.shape
    F = a.shape[1]
    row = lambda i: (i, 0)
    project = w_qkv is not None
    in_specs = [pl.BlockSpec((ROW_TILE, F), row), _layer_spec(w, layer, single_buffer=True),
                pl.BlockSpec((ROW_TILE, D), row), _layer_spec(gain, gain_layer)]
    stage = pltpu.VMEM((D // LANES, ROW_TILE, LANES), F32)
    if project:
        N = w_qkv.shape[2]
        in_specs.append(_layer_spec(w_qkv, qkv_layer, single_buffer=True))
        out_shape = (jax.ShapeDtypeStruct((M, D), F32), jax.ShapeDtypeStruct((M, N), BF16))
        out_specs = (pl.BlockSpec((ROW_TILE, D), row), pl.BlockSpec((ROW_TILE, N), row))
        scratch, args = [stage], (a, w, x2, gain, w_qkv)
    else:
        out_shape, out_specs = jax.ShapeDtypeStruct((M, D), F32), pl.BlockSpec((ROW_TILE, D), row)
        scratch, args = [stage, pltpu.VMEM((ROW_TILE, D), F32)], (a, w, x2, gain)
    return pl.pallas_call(
        functools.partial(_ffn_down_kernel, project=project),
        out_shape=out_shape,
        grid=(M // ROW_TILE,),
        in_specs=in_specs,
        out_specs=out_specs,
        scratch_shapes=scratch,
        compiler_params=_params("parallel"),
        name="ffn_down_residual_norm",
    )(*args)


def _half_masks(q, prescaled=False):
    lane = lax.broadcasted_iota(jnp.int32, q.shape, 1)
    zero = jnp.zeros_like(q)
    if not prescaled:
        q = q * jnp.asarray(DA_QK_DIM ** -0.5, q.dtype)
    return jnp.where(lane < 64, q, zero), jnp.where(lane >= 64, q, zero)


def _qk(q, k):
    return lax.dot_general(q, k, (((1,), (1,)), ((), ())), preferred_element_type=F32)


def _da_kernel(q_ref, k_ref, v_ref, bias_ref, lam_ref, sub_ref, o_ref, s_ref, *, lambda_init):
    T = ATT_T
    nq = q_ref.shape[0] // T
    r = lax.broadcasted_iota(jnp.int32, (T, T), 0)
    cc = lax.broadcasted_iota(jnp.int32, (T, T), 1)
    causal = r >= cc
    lam = lam_ref[...]
    lam_full = (jnp.exp(jnp.sum(lam[0:1] * lam[1:2], keepdims=True))
                - jnp.exp(jnp.sum(lam[2:3] * lam[3:4], keepdims=True)) + lambda_init)
    def score_steps(qi, st):
        def first():
            st["q01"] = jnp.concatenate(_half_masks(q_ref[qi * T:(qi + 1) * T, :], True), axis=0)
            st["mx"] = [None, None]

        def tile(j):
            if j == 0:
                first()
            ss = _qk(st["q01"], k_ref[j * T:(j + 1) * T, :])
            for c in range(2):
                s = ss[c * T:(c + 1) * T]
                if j == qi - 1:
                    s = s + bias_ref[c, :, 0:T]
                if j == qi:
                    s = jnp.where(causal, s + bias_ref[c, :, T:2 * T], NEG)
                s_ref[qi % DA_SLOTS, c, :, j * T:(j + 1) * T] = s
                t = jnp.maximum(s[:, :LANES], s[:, LANES:])
                st["mx"][c] = t if st["mx"][c] is None else jnp.maximum(st["mx"][c], t)
        return [functools.partial(tile, j) for j in range(qi + 1)]

    def value_steps(qi, st):
        def exp_tile(j):
            if j == 0:
                st["m2"] = []
                for c in range(2):
                    m = jnp.broadcast_to(jnp.max(st["mx"][c], axis=-1, keepdims=True), (T, LANES))
                    st["m2"].append(jnp.concatenate([m, m], axis=1))
                st["lsum"] = [None, None]
            for c in range(2):
                p = jnp.exp2(s_ref[qi % DA_SLOTS, c, :, j * T:(j + 1) * T] - st["m2"][c])
                s_ref[qi % DA_SLOTS, c, :, j * T:(j + 1) * T] = p
                t = p[:, :LANES] + p[:, LANES:]
                st["lsum"][c] = t if st["lsum"][c] is None else st["lsum"][c] + t

        def value_tile(j):
            if j == 0:
                l0 = jnp.sum(st["lsum"][0], axis=-1, keepdims=True)
                l1 = jnp.sum(st["lsum"][1], axis=-1, keepdims=True)
                st["ratio"] = lam_full * l0 / l1
                st["inv_l0"] = 1.0 / l0
                st["acc"] = None
            a = (s_ref[qi % DA_SLOTS, 0, :, j * T:(j + 1) * T]
                 - s_ref[qi % DA_SLOTS, 1, :, j * T:(j + 1) * T] * st["ratio"])
            pv = jnp.dot(a.astype(BF16), v_ref[j * T:(j + 1) * T, :], preferred_element_type=F32)
            st["acc"] = pv if st["acc"] is None else st["acc"] + pv
            if j == qi:
                o = _rms(st["acc"] * st["inv_l0"], sub_ref[...]) * (1.0 - lambda_init)
                o_ref[qi * T:(qi + 1) * T, :] = o.astype(o_ref.dtype)
        return ([functools.partial(exp_tile, j) for j in range(qi + 1)],
                [functools.partial(value_tile, j) for j in range(qi + 1)])

    states = [dict() for _ in range(nq)]
    stages = [score_steps(qi, states[qi]) for qi in range(nq)]
    later = [value_steps(qi, states[qi]) for qi in range(nq)]
    for r in range(nq + 1):
        lists = [stages[r] if r < nq else [],
                 later[r - 1][0] + later[r - 1][1] if r >= 1 else []]
        for t in range(max(len(steps) for steps in lists)):
            for steps in lists:
                if t < len(steps):
                    steps[t]()


def da_call(qkv, bias_near, lam, subln, batch, seq, lambda_init):
    T = ATT_T
    nh = DA_HEADS
    return pl.pallas_call(
        functools.partial(_da_kernel, lambda_init=lambda_init),
        out_shape=jax.ShapeDtypeStruct((batch * seq, nh * DA_V_DIM), BF16),
        grid=(nh, batch),
        in_specs=[pl.BlockSpec((seq, LANES), lambda h, b: (b, h)),
                  pl.BlockSpec((seq, LANES), lambda h, b: (b, nh + h)),
                  pl.BlockSpec((seq, LANES), lambda h, b: (b, 2 * nh + h)),
                  pl.BlockSpec((2, T, 2 * T), lambda h, b: (h, 0, 0)),
                  pl.BlockSpec((4, DA_QK_DIM), lambda h, b: (0, 0)),
                  pl.BlockSpec((1, DA_V_DIM), lambda h, b: (0, 0))],
        out_specs=pl.BlockSpec((seq, LANES), lambda h, b: (b, h)),
        scratch_shapes=[pltpu.VMEM((DA_SLOTS, 2, T, seq), F32)],
        compiler_params=_params("parallel", "parallel"),
        name="diff_attention",
    )(qkv, qkv, qkv, bias_near, lam, subln.reshape(1, DA_V_DIM))


def _sb_kernel(q_ref, k_ref, v_ref, o_ref, lbn_ref, hmn_ref, lbf_ref, hmf_ref, acc_ref):
    T = ATT_T
    nq = q_ref.shape[0] // T
    kr = lax.broadcasted_iota(jnp.int32, (T, T), 0)
    kc = lax.broadcasted_iota(jnp.int32, (T, T), 1)
    suffix = jnp.where(kr > kc, 1.0, 0.0).astype(BF16)
    strict = kc < kr
    lane = lax.broadcasted_iota(jnp.int32, (T, LANES), 1)
    strict2 = jnp.concatenate([strict, strict], axis=0)

    def score_tile(qi, q01, j, blk, carry, lb_view, hm_view):
        zz = _qk(q01, k_ref[j * T:(j + 1) * T, :])
        for c in range(2):
            z = zz[c * T:(c + 1) * T]
            lb = jnp.minimum(z, 0.0) - jnp.log(1.0 + jnp.exp(-jnp.abs(z)))
            log_1m_beta = lb - z
            if j == qi:
                log_1m_beta = jnp.where(strict, log_1m_beta, 0.0)
            rows = slice(blk + c * T, blk + (c + 1) * T)
            hm_view[rows, :] = log_1m_beta.astype(BF16)
            if carry[c] is not None:
                lb = lb + carry[c]
            lb_view[rows, :] = lb
            if j > 0:
                rs = jnp.sum(log_1m_beta, axis=-1, keepdims=True)
                carry[c] = rs if carry[c] is None else carry[c] + rs

    def value_tiles(qi, tiles, lb_view, hm_view):
        between = jnp.dot(hm_view[0:2 * len(tiles) * T, :], suffix, preferred_element_type=F32)
        acc = None
        for pos, j in enumerate(tiles):
            rows = slice(2 * pos * T, 2 * (pos + 1) * T)
            a = jnp.exp(lb_view[rows, :] + between[rows])
            if j == qi:
                a = jnp.where(strict2, a, 0.0)
            pv = jnp.dot(a.astype(BF16), v_ref[j * T:(j + 1) * T, :], preferred_element_type=F32)
            acc = pv if acc is None else acc + pv
        return acc

    def stacked_q(qi):
        return jnp.concatenate(_half_masks(q_ref[qi * T:(qi + 1) * T, :]), axis=0)

    near = [[j for j in (qi, qi - 1) if j >= 0] for qi in range(nq)]
    carries = [[None, None] for _ in range(nq)]
    for qi in range(nq + 1):
        if qi < nq:
            q01 = stacked_q(qi)
            for pos, j in enumerate(near[qi]):
                score_tile(qi, q01, j, 2 * pos * T, carries[qi], lbn_ref.at[qi % 2], hmn_ref.at[qi % 2])
        if qi > 0:
            acc_ref[qi - 1] = value_tiles(qi - 1, near[qi - 1], lbn_ref.at[(qi - 1) % 2], hmn_ref.at[(qi - 1) % 2])

    for qi in range(2, nq):
        far = list(range(qi - 2, -1, -1))
        carry = carries[qi]
        nearest_sum = jnp.max(jnp.maximum(carry[0], carry[1]))

        @pl.when(nearest_sum > SB_EXP_ZERO)
        def _():
            q01 = stacked_q(qi)
            far_carry = list(carry)
            for pos, j in enumerate(far):
                score_tile(qi, q01, j, 2 * pos * T, far_carry, lbf_ref, hmf_ref)
            acc_ref[qi] += value_tiles(qi, far, lbf_ref, hmf_ref)

    for qi in range(nq):
        acc = acc_ref[qi]
        o_ref[qi * T:(qi + 1) * T, :] = jnp.where(lane < 64, acc[0:T], acc[T:2 * T]).astype(o_ref.dtype)


def sb_call(qkv, batch, seq):
    npair = SB_HEADS // 2
    return pl.pallas_call(
        _sb_kernel,
        out_shape=jax.ShapeDtypeStruct((batch * seq, SB_HEADS * SB_DIM), BF16),
        grid=(batch, npair),
        in_specs=[pl.BlockSpec((seq, LANES), lambda b, p: (b, p)),
                  pl.BlockSpec((seq, LANES), lambda b, p: (b, npair + p)),
                  pl.BlockSpec((seq, LANES), lambda b, p: (b, 2 * npair + p))],
        out_specs=pl.BlockSpec((seq, LANES), lambda b, p: (b, p)),
        scratch_shapes=[pltpu.VMEM((2, 4 * ATT_T, ATT_T), F32), pltpu.VMEM((2, 4 * ATT_T, ATT_T), BF16),
                        pltpu.VMEM((2 * seq - 4 * ATT_T, ATT_T), F32),
                        pltpu.VMEM((2 * seq - 4 * ATT_T, ATT_T), BF16),
                        pltpu.VMEM((seq // ATT_T, 2 * ATT_T, LANES), F32)],
        compiler_params=_params("parallel", "parallel"),
        name="stick_breaking_attention",
    )(qkv, qkv, qkv)


def _sw_kernel(sink_ref, q_ref, k_ref, v_ref, bias_ref, o_ref, s_ref):
    W = SW_BLOCK
    nb = q_ref.shape[0] // W
    p_id = pl.program_id(1)
    r = lax.broadcasted_iota(jnp.int32, (2 * W, 2 * W), 0) & (W - 1)
    cidx = lax.broadcasted_iota(jnp.int32, (2 * W, 2 * W), 1)
    valid = ((cidx < W) & (cidx > r)) | ((cidx >= W) & (cidx - W <= r))
    biasm = jnp.where(valid, bias_ref[...], NEG)
    row = lax.broadcasted_iota(jnp.int32, (2 * W, 1), 0)
    sink = jnp.where(row < W, sink_ref[2 * p_id], sink_ref[2 * p_id + 1])
    lane = lax.broadcasted_iota(jnp.int32, (W, LANES), 1)
    def scores(n):
        q01 = jnp.concatenate(_half_masks(q_ref[n * W:(n + 1) * W, :]), axis=0)
        if n == 0:
            s_ref[0, :, W:] = _qk(q01, k_ref[0:W, :]) + biasm[:, W:]
        else:
            s_ref[n] = _qk(q01, k_ref[(n - 1) * W:(n + 1) * W, :]) + biasm

    def values(n):
        s = s_ref[0, :, W:] if n == 0 else s_ref[n]
        m = jnp.maximum(jnp.max(s, axis=-1, keepdims=True), sink)
        e = jnp.exp(s - m)
        denom = jnp.sum(e, axis=-1, keepdims=True) + jnp.exp(sink - m)
        v = v_ref[0:W, :] if n == 0 else v_ref[(n - 1) * W:(n + 1) * W, :]
        o = jnp.dot(e.astype(BF16), v, preferred_element_type=F32) / denom
        o_ref[n * W:(n + 1) * W, :] = jnp.where(lane < 64, o[0:W], o[W:]).astype(o_ref.dtype)

    for n in range(nb + SW_LAG):
        if n < nb:
            scores(n)
        if n >= SW_LAG:
            values(n - SW_LAG)


def sw_call(qkv, bias_band, sinks, batch, seq):
    npair = SW_Q_HEADS // 2
    q_blocks = SW_Q_HEADS * SW_DIM // LANES
    W = SW_BLOCK
    return pl.pallas_call(
        _sw_kernel,
        out_shape=jax.ShapeDtypeStruct((batch * seq, SW_Q_HEADS * SW_DIM), BF16),
        grid=(batch, npair),
        in_specs=[pl.BlockSpec(memory_space=pltpu.SMEM),
                  pl.BlockSpec((seq, LANES), lambda b, p: (b, p)),
                  pl.BlockSpec((seq, LANES), lambda b, p: (b, q_blocks + p // 2)),
                  pl.BlockSpec((seq, LANES), lambda b, p: (b, q_blocks + SW_KV_HEADS + p // 2)),
                  pl.BlockSpec((None, 2 * W, 2 * W), lambda b, p: (p, 0, 0))],
        out_specs=pl.BlockSpec((seq, LANES), lambda b, p: (b, p)),
        scratch_shapes=[pltpu.VMEM((seq // W, 2 * W, 2 * W), F32)],
        compiler_params=_params("parallel", "parallel"),
        name="sliding_window_attention",
    )(sinks, qkv, qkv, qkv, bias_band.reshape(npair, 2 * W, 2 * W))


def _t5_bucket(dist):
    max_exact = N_BUCKETS // 2
    d = jnp.maximum(dist, 0)
    large = max_exact + (jnp.log(jnp.maximum(d, 1).astype(F32) / max_exact)
                         / math.log(MAX_DISTANCE / max_exact) * (N_BUCKETS - max_exact)).astype(jnp.int32)
    large = jnp.minimum(large, N_BUCKETS - 1)
    return jnp.where(d < max_exact, d, large)


def _band_kernel(row_ref, o_ref):
    rows = jnp.broadcast_to(row_ref[...], (o_ref.shape[0], row_ref.shape[1]))
    o_ref[...] = pltpu.roll(rows, 0, 1, stride=1, stride_axis=0)[:, :o_ref.shape[1]]


def _bias_band(table, T):
    y = np.arange(3 * T)
    dist = np.clip(np.where(y <= 2 * T, T - y, 4 * T - y), 0, 2 * T - 1)
    row = table.astype(F32)[_t5_bucket(jnp.asarray(dist, jnp.int32))].T
    ch = row.shape[0]
    return pl.pallas_call(
        _band_kernel,
        out_shape=jax.ShapeDtypeStruct((ch, T, 2 * T), F32),
        grid=(ch,),
        in_specs=[pl.BlockSpec((None, 1, 3 * T), lambda c: (c, 0, 0))],
        out_specs=pl.BlockSpec((None, T, 2 * T), lambda c: (c, 0, 0)),
        compiler_params=_params("parallel"),
        name="bias_band",
    )(row[:, None, :])


def _lambda_init(layer):
    return 0.8 - 0.6 * math.exp(-0.3 * layer)


def _dup_kv_heads(w):
    q_w = SW_Q_HEADS * SW_DIM
    kv = w[:, q_w:].reshape(w.shape[0], 2 * SW_KV_HEADS, 1, SW_DIM)
    kv = jnp.broadcast_to(kv, (w.shape[0], 2 * SW_KV_HEADS, 2, SW_DIM)).reshape(w.shape[0], -1)
    return jnp.concatenate([w[:, :q_w], kv], axis=1)


def _pad_ff(a, axis):
    pad = [(0, 0)] * a.ndim
    pad[axis] = (0, D_FF_PAD - D_FF)
    return jnp.pad(a, pad)


def _split_up_kernel(w_ref, gate_ref, value_ref):
    keep = lax.broadcasted_iota(jnp.int32, gate_ref.shape, 1) < D_FF
    gate_ref[...] = jnp.where(keep, w_ref[:, :D_FF_PAD], 0.0).astype(gate_ref.dtype)
    start = 2 * D_FF - D_FF_PAD
    window = pltpu.roll(w_ref[:, start:], D_FF_PAD - (D_FF - start), 1)
    value_ref[...] = jnp.where(keep, window, 0.0).astype(value_ref.dtype)


def split_up_call(w_up):
    L, D, _ = w_up.shape
    out = jax.ShapeDtypeStruct((L, D, D_FF_PAD), BF16)
    out_spec = pl.BlockSpec((None, PREP_ROWS, D_FF_PAD), lambda l, i: (l, i, 0))
    return pl.pallas_call(
        _split_up_kernel,
        out_shape=(out, out),
        grid=(L, D // PREP_ROWS),
        in_specs=[pl.BlockSpec((None, PREP_ROWS, 2 * D_FF), lambda l, i: (l, i, 0))],
        out_specs=(out_spec, out_spec),
        compiler_params=_params("parallel", "parallel"),
        name="ffn_up_weight_prep",
    )(w_up)


def _pad_down_kernel(w_ref, o_ref):
    row = pl.program_id(1) * PREP_DOWN_ROWS + lax.broadcasted_iota(jnp.int32, o_ref.shape, 0)
    o_ref[...] = jnp.where(row < D_FF, w_ref[...], 0.0).astype(o_ref.dtype)


def pad_down_call(w_down):
    L, _, D = w_down.shape
    spec = pl.BlockSpec((None, PREP_DOWN_ROWS, D), lambda l, i: (l, i, 0))
    return pl.pallas_call(
        _pad_down_kernel,
        out_shape=jax.ShapeDtypeStruct((L, D_FF_PAD, D), BF16),
        grid=(L, D_FF_PAD // PREP_DOWN_ROWS),
        in_specs=[spec],
        out_specs=spec,
        compiler_params=_params("parallel", "parallel"),
        name="ffn_down_weight_prep",
    )(w_down)


def kernel(x, rel_bias, attn_norm, ffn_norm, w_o, da_w_qkv, da_lambda, da_subln, sb_w_qkv, sw_w_qkv,
           sw_sinks, ffn_w_up, ffn_conv_w, ffn_conv_b, ffn_w_down, final_norm):
    B, S, D = x.shape
    x2 = x.reshape(B * S, D)
    da_bias_near = _bias_band((rel_bias - rel_bias[N_BUCKETS - 1]) * LOG2E, ATT_T)
    sw_bias = _bias_band(rel_bias, SW_BLOCK)
    q_cols = jnp.arange(da_w_qkv.shape[2]) < DA_HEADS * 2 * DA_QK_DIM
    da_w = (da_w_qkv * jnp.where(q_cols, DA_QK_DIM ** -0.5 * LOG2E, 1.0)).astype(BF16)
    sb_w = sb_w_qkv.astype(BF16)
    sw_w = jax.vmap(_dup_kv_heads)(sw_w_qkv).astype(BF16)
    wo_w = w_o.astype(BF16)
    wg, wv = split_up_call(ffn_w_up)
    cwg = _pad_ff(ffn_conv_w[:, :, :D_FF], 2)
    cwv = _pad_ff(ffn_conv_w[:, :, D_FF:], 2)
    cbg = _pad_ff(ffn_conv_b[:, None, :D_FF], 2)
    cbv = _pad_ff(ffn_conv_b[:, None, D_FF:], 2)
    wd = pad_down_call(ffn_w_down)
    attn_gain = attn_norm[:, None, :]
    ffn_gain = ffn_norm[:, None, :]
    final_gain = final_norm[None, None, :]

    weights_qkv = [((da_w, sb_w, sw_w)[layer % N_MIXERS], layer // N_MIXERS) for layer in range(DEPTH)]
    qkv = proj_call(x2, *weights_qkv[0], attn_gain, 0)
    for layer in range(DEPTH):
        mixer = layer % N_MIXERS
        slot = layer // N_MIXERS
        if mixer == 0:
            m = da_call(qkv, da_bias_near, da_lambda[slot], da_subln[slot], B, S, _lambda_init(layer))
        elif mixer == 1:
            m = sb_call(qkv, B, S)
        else:
            m = sw_call(qkv, sw_bias, sw_sinks[slot], B, S)
        x2, act = wo_ffn_up_call(m, wo_w, x2, ffn_gain, wg, wv, cwg, cwv, cbg, cbv, layer, S)
        if layer < DEPTH - 1:
            x2, qkv = ffn_down_call(act, wd, layer, x2, attn_gain, layer + 1, *weights_qkv[layer + 1])
        else:
            h = ffn_down_call(act, wd, layer, x2, final_gain, 0)
    return h.reshape(B, S, D)
```

```python
import functools
import math

import jax
import jax.numpy as jnp
import numpy as np
from jax import lax
from jax.experimental import pallas as pl
from jax.experimental.pallas import tpu as pltpu

D_MODEL = 1024
DEPTH = 4
N_MIXERS = 3
N_BUCKETS = 32
MAX_DISTANCE = 128
DA_HEADS = 8
DA_QK_DIM = 64
DA_V_DIM = 128
SB_HEADS = 16
SB_DIM = 64
SW_Q_HEADS = 16
SW_KV_HEADS = 4
SW_DIM = 64
SW_BLOCK = 128
D_FF = 2752
EPS = 1e-6
NEG = -1e30
LOG2E = math.log2(math.e)

LANES = 128
SUBLANES = 8
VMEM_LIMIT = 52 * 1024 * 1024
WO_FFN_VMEM_LIMIT = 57 * 1024 * 1024

ATT_T = 256
DA_SLOTS = 2
SB_EXP_ZERO = -104.0
ROW_TILE = 512
PROJ_TM = 1024
PROJ_CHUNK = 512
PREP_ROWS = 256
PREP_DOWN_ROWS = 704
SW_LAG = 2
FFN_TM = 1024
FFN_TF = 256
D_FF_PAD = 2816

BF16 = jnp.bfloat16
F32 = jnp.float32


def _params(*sem):
    return pltpu.CompilerParams(dimension_semantics=sem, vmem_limit_bytes=VMEM_LIMIT)


def _layer_spec(arr, layer, single_buffer=False):
    shape = (None,) + arr.shape[1:]
    index_map = lambda *_: (layer,) + (0,) * (arr.ndim - 1)
    if single_buffer:
        return pl.BlockSpec(shape, index_map, pipeline_mode=pl.Buffered(1))
    return pl.BlockSpec(shape, index_map)


def _rms(xf, gain):
    return xf * lax.rsqrt(jnp.mean(xf * xf, axis=-1, keepdims=True) + EPS) * gain


def _project(h, w_ref, o_ref):
    for c in range(o_ref.shape[1] // PROJ_CHUNK):
        sl = slice(c * PROJ_CHUNK, (c + 1) * PROJ_CHUNK)
        o_ref[:, sl] = jnp.dot(h, w_ref[:, sl], preferred_element_type=F32).astype(o_ref.dtype)


def _norm_proj_kernel(x_ref, g_ref, w_ref, o_ref):
    _project(_rms(x_ref[...], g_ref[...]).astype(BF16), w_ref, o_ref)


def proj_call(x2, w, layer, gain, gain_layer):
    M, D = x2.shape
    N = w.shape[2]
    row = lambda i: (i, 0)
    return pl.pallas_call(
        _norm_proj_kernel,
        out_shape=jax.ShapeDtypeStruct((M, N), BF16),
        grid=(M // PROJ_TM,),
        in_specs=[pl.BlockSpec((PROJ_TM, D), row), _layer_spec(gain, gain_layer),
                  _layer_spec(w, layer, single_buffer=True)],
        out_specs=pl.BlockSpec((PROJ_TM, N), row),
        compiler_params=_params("parallel"),
        name="qkv_proj",
    )(x2, gain, w)


def _ffn_row_starts(n_groups):
    per = n_groups // SUBLANES
    return [SUBLANES * SUBLANES * (j % per) + j // per for j in range(n_groups)]


def _stage(ref, x):
    for c in range(ref.shape[0]):
        ref[c] = x[:, c * LANES:(c + 1) * LANES]


def _stage_interleaved(ref, x):
    for c in range(ref.shape[0]):
        for j, st in enumerate(_ffn_row_starts(x.shape[0] // SUBLANES)):
            ref[c, pl.ds(st, SUBLANES, stride=SUBLANES), :] = x[j * SUBLANES:(j + 1) * SUBLANES,
                                                                c * LANES:(c + 1) * LANES]


def _deinterleave_rows(ref, c):
    return jnp.concatenate([ref[c, pl.ds(st, SUBLANES, stride=SUBLANES), :]
                            for st in _ffn_row_starts(ref.shape[1] // SUBLANES)], axis=0)


def _causal_conv3(u, cw, cb, prev):
    nblk, g, _, tf = u.shape
    first = lax.broadcasted_iota(jnp.int32, (SUBLANES, tf), 0) == 0
    u1, u2 = [], []
    for b in range(nblk):
        tail = prev if b == 0 else u[b - 1, g - 2:]
        wrap = [jnp.where(first, pltpu.roll(tail[e], 1, 0), pltpu.roll(u[b, g - 2 + e], 1, 0))
                for e in range(2)]
        u1.append(jnp.concatenate([wrap[1][None], u[b, :g - 1]], axis=0))
        u2.append(jnp.concatenate([wrap[0][None], wrap[1][None], u[b, :g - 2]], axis=0))
    u1 = jnp.stack(u1)
    u2 = jnp.stack(u2)
    return u * cw[2:3] + u1 * cw[1:2] + u2 * cw[0:1] + cb


def _wo_ffn_up_kernel(m_ref, wo_ref, x_ref, gain_ref, wg_ref, wv_ref, cwg_ref, cwv_ref, cbg_ref, cbv_ref,
                      xo_ref, g_ref, carry_ref, stage_ref, h_ref, *, tiles_per_seq):
    tm = m_ref.shape[0]
    nblk = tm // ROW_TILE
    g = ROW_TILE // SUBLANES

    @pl.when(pl.program_id(0) % tiles_per_seq == 0)
    def _():
        carry_ref[...] = jnp.zeros(carry_ref.shape, F32)

    for b in range(nblk):
        rows = slice(b * ROW_TILE, (b + 1) * ROW_TILE)
        xn = x_ref[rows, :] + jnp.dot(m_ref[rows, :], wo_ref[...], preferred_element_type=F32)
        xo_ref[rows, :] = xn
        _stage_interleaved(stage_ref, _rms(xn, gain_ref[...]))
        for c in range(stage_ref.shape[0]):
            h_ref[rows, c * LANES:(c + 1) * LANES] = stage_ref[c].astype(h_ref.dtype)

    h = h_ref[...]
    for c in range(g_ref.shape[1] // FFN_TF):
        sl = slice(c * FFN_TF, (c + 1) * FFN_TF)
        ug = jnp.dot(h, wg_ref[:, sl], preferred_element_type=F32).reshape(nblk, g, SUBLANES, FFN_TF)
        uv = jnp.dot(h, wv_ref[:, sl], preferred_element_type=F32).reshape(nblk, g, SUBLANES, FFN_TF)
        prev_g = carry_ref[0, :, :, sl]
        prev_v = carry_ref[1, :, :, sl]
        carry_ref[0, :, :, sl] = ug[nblk - 1, g - 2:]
        carry_ref[1, :, :, sl] = uv[nblk - 1, g - 2:]
        yg = _causal_conv3(ug, cwg_ref[:, sl], cbg_ref[:, sl], prev_g)
        yv = _causal_conv3(uv, cwv_ref[:, sl], cbv_ref[:, sl], prev_v)
        act = yg * (1.0 / (1.0 + jnp.exp(-yg))) * yv
        g_ref[:, sl] = act.reshape(tm, FFN_TF).astype(g_ref.dtype)


def wo_ffn_up_call(m, wo, x2, gain, wg, wv, cwg, cwv, cbg, cbv, layer, seq):
    M, D = x2.shape
    F = wg.shape[2]
    row = lambda i: (i, 0)
    resident = lambda arr: _layer_spec(arr, layer, single_buffer=True)
    return pl.pallas_call(
        functools.partial(_wo_ffn_up_kernel, tiles_per_seq=seq // FFN_TM),
        out_shape=(jax.ShapeDtypeStruct((M, D), F32), jax.ShapeDtypeStruct((M, F), BF16)),
        grid=(M // FFN_TM,),
        in_specs=[pl.BlockSpec((FFN_TM, D), row), resident(wo), pl.BlockSpec((FFN_TM, D), row), resident(gain),
                  resident(wg), resident(wv), resident(cwg), resident(cwv), resident(cbg), resident(cbv)],
        out_specs=(pl.BlockSpec((FFN_TM, D), row), pl.BlockSpec((FFN_TM, F), row)),
        scratch_shapes=[pltpu.VMEM((2, 2, SUBLANES, F), F32),
                        pltpu.VMEM((D // LANES, ROW_TILE, LANES), F32),
                        pltpu.VMEM((FFN_TM, D), BF16)],
        compiler_params=pltpu.CompilerParams(dimension_semantics=("arbitrary",),
                                             vmem_limit_bytes=WO_FFN_VMEM_LIMIT),
        name="wo_ffn_up_conv_gate",
    )(m, wo, x2, gain, wg, wv, cwg, cwv, cbg, cbv)


def _ffn_down_kernel(a_ref, w_ref, x_ref, g_ref, *refs, project):
    if project:
        wq_ref, xo_ref, qkv_ref, stage_ref = refs
    else:
        h_ref, stage_ref, xo_ref = refs
    _stage(stage_ref, jnp.dot(a_ref[...], w_ref[...], preferred_element_type=F32))
    for c in range(stage_ref.shape[0]):
        sl = slice(c * LANES, (c + 1) * LANES)
        xo_ref[:, sl] = x_ref[:, sl] + _deinterleave_rows(stage_ref, c)
    h = _rms(xo_ref[...], g_ref[...])
    if project:
        _project(h.astype(BF16), wq_ref, qkv_ref)
    else:
        h_ref[...] = h.astype(h_ref.dtype)


def ffn_down_call(a, w, layer, x2, gain, gain_layer, w_qkv=None, qkv_layer=0):
    M, D = x2.shape
    F = a.shape[1]
    row = lambda i: (i, 0)
    project = w_qkv is not None
    in_specs = [pl.BlockSpec((ROW_TILE, F), row), _layer_spec(w, layer, single_buffer=True),
                pl.BlockSpec((ROW_TILE, D), row), _layer_spec(gain, gain_layer)]
    stage = pltpu.VMEM((D // LANES, ROW_TILE, LANES), F32)
    if project:
        N = w_qkv.shape[2]
        in_specs.append(_layer_spec(w_qkv, qkv_layer, single_buffer=True))
        out_shape = (jax.ShapeDtypeStruct((M, D), F32), jax.ShapeDtypeStruct((M, N), BF16))
        out_specs = (pl.BlockSpec((ROW_TILE, D), row), pl.BlockSpec((ROW_TILE, N), row))
        scratch, args = [stage], (a, w, x2, gain, w_qkv)
    else:
        out_shape, out_specs = jax.ShapeDtypeStruct((M, D), F32), pl.BlockSpec((ROW_TILE, D), row)
        scratch, args = [stage, pltpu.VMEM((ROW_TILE, D), F32)], (a, w, x2, gain)
    return pl.pallas_call(
        functools.partial(_ffn_down_kernel, project=project),
        out_shape=out_shape,
        grid=(M // ROW_TILE,),
        in_specs=in_specs,
        out_specs=out_specs,
        scratch_shapes=scratch,
        compiler_params=_params("parallel"),
        name="ffn_down_residual_norm",
    )(*args)


def _half_masks(q):
    lane = lax.broadcasted_iota(jnp.int32, q.shape, 1)
    zero = jnp.zeros_like(q)
    return jnp.where(lane < 64, q, zero), jnp.where(lane >= 64, q, zero)


def _qk(q, k):
    return lax.dot_general(q, k, (((1,), (1,)), ((), ())), preferred_element_type=F32)


def _da_kernel(q_ref, k_ref, v_ref, bias_ref, lam_ref, sub_ref, o_ref, s_ref, *, lambda_init):
    T = ATT_T
    nq = q_ref.shape[0] // T
    r = lax.broadcasted_iota(jnp.int32, (T, T), 0)
    cc = lax.broadcasted_iota(jnp.int32, (T, T), 1)
    causal = r >= cc
    lam = lam_ref[...]
    lam_full = (jnp.exp(jnp.sum(lam[0:1] * lam[1:2], keepdims=True))
                - jnp.exp(jnp.sum(lam[2:3] * lam[3:4], keepdims=True)) + lambda_init)
    def score_steps(qi, st):
        def first():
            st["q01"] = jnp.concatenate(_half_masks(q_ref[qi * T:(qi + 1) * T, :]), axis=0)
            st["mx"] = [None, None]

        def tile(j):
            if j == 0:
                first()
            ss = _qk(st["q01"], k_ref[j * T:(j + 1) * T, :])
            for c in range(2):
                s = ss[c * T:(c + 1) * T]
                if j == qi - 1:
                    s = s + bias_ref[c, :, 0:T]
                if j == qi:
                    s = jnp.where(causal, s + bias_ref[c, :, T:2 * T], NEG)
                s_ref[qi % DA_SLOTS, c, :, j * T:(j + 1) * T] = s
                t = jnp.maximum(s[:, :LANES], s[:, LANES:])
                st["mx"][c] = t if st["mx"][c] is None else jnp.maximum(st["mx"][c], t)
        return [functools.partial(tile, j) for j in range(qi + 1)]

    def value_steps(qi, st):
        def exp_tile(j):
            if j == 0:
                st["m2"] = []
                for c in range(2):
                    m = jnp.broadcast_to(jnp.max(st["mx"][c], axis=-1, keepdims=True), (T, LANES))
                    st["m2"].append(jnp.concatenate([m, m], axis=1))
                st["lsum"] = [None, None]
            for c in range(2):
                p = jnp.exp2(s_ref[qi % DA_SLOTS, c, :, j * T:(j + 1) * T] - st["m2"][c])
                s_ref[qi % DA_SLOTS, c, :, j * T:(j + 1) * T] = p
                t = p[:, :LANES] + p[:, LANES:]
                st["lsum"][c] = t if st["lsum"][c] is None else st["lsum"][c] + t

        def value_tile(j):
            if j == 0:
                l0 = jnp.sum(st["lsum"][0], axis=-1, keepdims=True)
                l1 = jnp.sum(st["lsum"][1], axis=-1, keepdims=True)
                st["ratio"] = lam_full * l0 / l1
                st["inv_l0"] = 1.0 / l0
                st["acc"] = None
            a = (s_ref[qi % DA_SLOTS, 0, :, j * T:(j + 1) * T]
                 - s_ref[qi % DA_SLOTS, 1, :, j * T:(j + 1) * T] * st["ratio"])
            pv = jnp.dot(a.astype(BF16), v_ref[j * T:(j + 1) * T, :], preferred_element_type=F32)
            st["acc"] = pv if st["acc"] is None else st["acc"] + pv
            if j == qi:
                o = _rms(st["acc"] * st["inv_l0"], sub_ref[...]) * (1.0 - lambda_init)
                o_ref[qi * T:(qi + 1) * T, :] = o.astype(o_ref.dtype)
        return ([functools.partial(exp_tile, j) for j in range(qi + 1)],
                [functools.partial(value_tile, j) for j in range(qi + 1)])

    states = [dict() for _ in range(nq)]
    stages = [score_steps(qi, states[qi]) for qi in range(nq)]
    later = [value_steps(qi, states[qi]) for qi in range(nq)]
    for r in range(nq + 1):
        lists = [stages[r] if r < nq else [],
                 later[r - 1][0] + later[r - 1][1] if r >= 1 else []]
        for t in range(max(len(steps) for steps in lists)):
            for steps in lists:
                if t < len(steps):
                    steps[t]()


def da_call(qkv, bias_near, lam, subln, batch, seq, lambda_init):
    T = ATT_T
    nh = DA_HEADS
    return pl.pallas_call(
        functools.partial(_da_kernel, lambda_init=lambda_init),
        out_shape=jax.ShapeDtypeStruct((batch * seq, nh * DA_V_DIM), BF16),
        grid=(nh, batch),
        in_specs=[pl.BlockSpec((seq, LANES), lambda h, b: (b, h)),
                  pl.BlockSpec((seq, LANES), lambda h, b: (b, nh + h)),
                  pl.BlockSpec((seq, LANES), lambda h, b: (b, 2 * nh + h)),
                  pl.BlockSpec((2, T, 2 * T), lambda h, b: (h, 0, 0)),
                  pl.BlockSpec((4, DA_QK_DIM), lambda h, b: (0, 0)),
                  pl.BlockSpec((1, DA_V_DIM), lambda h, b: (0, 0))],
        out_specs=pl.BlockSpec((seq, LANES), lambda h, b: (b, h)),
        scratch_shapes=[pltpu.VMEM((DA_SLOTS, 2, T, seq), F32)],
        compiler_params=_params("parallel", "parallel"),
        name="diff_attention",
    )(qkv, qkv, qkv, bias_near, lam, subln.reshape(1, DA_V_DIM))


def _sb_kernel(q_ref, k_ref, v_ref, o_ref, lbn_ref, hmn_ref, lbf_ref, hmf_ref, acc_ref):
    T = ATT_T
    nq = q_ref.shape[0] // T
    kr = lax.broadcasted_iota(jnp.int32, (T, T), 0)
    kc = lax.broadcasted_iota(jnp.int32, (T, T), 1)
    suffix = jnp.where(kr > kc, 1.0, 0.0).astype(BF16)
    strict = kc < kr
    lane = lax.broadcasted_iota(jnp.int32, (T, LANES), 1)
    strict2 = jnp.concatenate([strict, strict], axis=0)

    def score_tile(qi, q01, j, blk, carry, lb_view, hm_view):
        zz = _qk(q01, k_ref[j * T:(j + 1) * T, :])
        for c in range(2):
            z = zz[c * T:(c + 1) * T]
            lb = jnp.minimum(z, 0.0) - jnp.log(1.0 + jnp.exp(-jnp.abs(z)))
            log_1m_beta = lb - z
            if j == qi:
                log_1m_beta = jnp.where(strict, log_1m_beta, 0.0)
            rows = slice(blk + c * T, blk + (c + 1) * T)
            hm_view[rows, :] = log_1m_beta.astype(BF16)
            if carry[c] is not None:
                lb = lb + carry[c]
            lb_view[rows, :] = lb
            if j > 0:
                rs = jnp.sum(log_1m_beta, axis=-1, keepdims=True)
                carry[c] = rs if carry[c] is None else carry[c] + rs

    def value_tiles(qi, tiles, lb_view, hm_view):
        between = jnp.dot(hm_view[0:2 * len(tiles) * T, :], suffix, preferred_element_type=F32)
        acc = None
        for pos, j in enumerate(tiles):
            rows = slice(2 * pos * T, 2 * (pos + 1) * T)
            a = jnp.exp(lb_view[rows, :] + between[rows])
            if j == qi:
                a = jnp.where(strict2, a, 0.0)
            pv = jnp.dot(a.astype(BF16), v_ref[j * T:(j + 1) * T, :], preferred_element_type=F32)
            acc = pv if acc is None else acc + pv
        return acc

    def stacked_q(qi):
        return jnp.concatenate(_half_masks(q_ref[qi * T:(qi + 1) * T, :]), axis=0)

    near = [[j for j in (qi, qi - 1) if j >= 0] for qi in range(nq)]
    carries = [[None, None] for _ in range(nq)]
    for qi in range(nq + 1):
        if qi < nq:
            q01 = stacked_q(qi)
            for pos, j in enumerate(near[qi]):
                score_tile(qi, q01, j, 2 * pos * T, carries[qi], lbn_ref.at[qi % 2], hmn_ref.at[qi % 2])
        if qi > 0:
            acc_ref[qi - 1] = value_tiles(qi - 1, near[qi - 1], lbn_ref.at[(qi - 1) % 2], hmn_ref.at[(qi - 1) % 2])

    for qi in range(2, nq):
        far = list(range(qi - 2, -1, -1))
        carry = carries[qi]
        nearest_sum = jnp.max(jnp.maximum(carry[0], carry[1]))

        @pl.when(nearest_sum > SB_EXP_ZERO)
        def _():
            q01 = stacked_q(qi)
            far_carry = list(carry)
            for pos, j in enumerate(far):
                score_tile(qi, q01, j, 2 * pos * T, far_carry, lbf_ref, hmf_ref)
            acc_ref[qi] += value_tiles(qi, far, lbf_ref, hmf_ref)

    for qi in range(nq):
        acc = acc_ref[qi]
        o_ref[qi * T:(qi + 1) * T, :] = jnp.where(lane < 64, acc[0:T], acc[T:2 * T]).astype(o_ref.dtype)


def sb_call(qkv, batch, seq):
    npair = SB_HEADS // 2
    return pl.pallas_call(
        _sb_kernel,
        out_shape=jax.ShapeDtypeStruct((batch * seq, SB_HEADS * SB_DIM), BF16),
        grid=(batch, npair),
        in_specs=[pl.BlockSpec((seq, LANES), lambda b, p: (b, p)),
                  pl.BlockSpec((seq, LANES), lambda b, p: (b, npair + p)),
                  pl.BlockSpec((seq, LANES), lambda b, p: (b, 2 * npair + p))],
        out_specs=pl.BlockSpec((seq, LANES), lambda b, p: (b, p)),
        scratch_shapes=[pltpu.VMEM((2, 4 * ATT_T, ATT_T), F32), pltpu.VMEM((2, 4 * ATT_T, ATT_T), BF16),
                        pltpu.VMEM((2 * seq - 4 * ATT_T, ATT_T), F32),
                        pltpu.VMEM((2 * seq - 4 * ATT_T, ATT_T), BF16),
                        pltpu.VMEM((seq // ATT_T, 2 * ATT_T, LANES), F32)],
        compiler_params=_params("parallel", "parallel"),
        name="stick_breaking_attention",
    )(qkv, qkv, qkv)


def _sw_kernel(sink_ref, q_ref, k_ref, v_ref, bias_ref, o_ref, s_ref):
    W = SW_BLOCK
    nb = q_ref.shape[0] // W
    p_id = pl.program_id(1)
    r = lax.broadcasted_iota(jnp.int32, (2 * W, 2 * W), 0) & (W - 1)
    cidx = lax.broadcasted_iota(jnp.int32, (2 * W, 2 * W), 1)
    valid = ((cidx < W) & (cidx > r)) | ((cidx >= W) & (cidx - W <= r))
    biasm = jnp.where(valid, bias_ref[...], NEG)
    row = lax.broadcasted_iota(jnp.int32, (2 * W, 1), 0)
    sink = jnp.where(row < W, sink_ref[2 * p_id], sink_ref[2 * p_id + 1])
    lane = lax.broadcasted_iota(jnp.int32, (W, LANES), 1)
    def scores(n):
        q01 = jnp.concatenate(_half_masks(q_ref[n * W:(n + 1) * W, :]), axis=0)
        if n == 0:
            s_ref[0, :, W:] = _qk(q01, k_ref[0:W, :]) + biasm[:, W:]
        else:
            s_ref[n] = _qk(q01, k_ref[(n - 1) * W:(n + 1) * W, :]) + biasm

    def values(n):
        s = s_ref[0, :, W:] if n == 0 else s_ref[n]
        m = jnp.maximum(jnp.max(s, axis=-1, keepdims=True), sink)
        e = jnp.exp2(s - m)
        denom = jnp.sum(e, axis=-1, keepdims=True) + jnp.exp2(sink - m)
        v = v_ref[0:W, :] if n == 0 else v_ref[(n - 1) * W:(n + 1) * W, :]
        o = jnp.dot(e.astype(BF16), v, preferred_element_type=F32) / denom
        o_ref[n * W:(n + 1) * W, :] = jnp.where(lane < 64, o[0:W], o[W:]).astype(o_ref.dtype)

    for n in range(nb + SW_LAG):
        if n < nb:
            scores(n)
        if n >= SW_LAG:
            values(n - SW_LAG)


def sw_call(qkv, bias_band, sinks, batch, seq):
    npair = SW_Q_HEADS // 2
    q_blocks = SW_Q_HEADS * SW_DIM // LANES
    W = SW_BLOCK
    return pl.pallas_call(
        _sw_kernel,
        out_shape=jax.ShapeDtypeStruct((batch * seq, SW_Q_HEADS * SW_DIM), BF16),
        grid=(batch, npair),
        in_specs=[pl.BlockSpec(memory_space=pltpu.SMEM),
                  pl.BlockSpec((seq, LANES), lambda b, p: (b, p)),
                  pl.BlockSpec((seq, LANES), lambda b, p: (b, q_blocks + p // 2)),
                  pl.BlockSpec((seq, LANES), lambda b, p: (b, q_blocks + SW_KV_HEADS + p // 2)),
                  pl.BlockSpec((None, 2 * W, 2 * W), lambda b, p: (p, 0, 0))],
        out_specs=pl.BlockSpec((seq, LANES), lambda b, p: (b, p)),
        scratch_shapes=[pltpu.VMEM((seq // W, 2 * W, 2 * W), F32)],
        compiler_params=_params("parallel", "parallel"),
        name="sliding_window_attention",
    )(sinks, qkv, qkv, qkv, bias_band.reshape(npair, 2 * W, 2 * W))


def _t5_bucket(dist):
    max_exact = N_BUCKETS // 2
    d = jnp.maximum(dist, 0)
    large = max_exact + (jnp.log(jnp.maximum(d, 1).astype(F32) / max_exact)
                         / math.log(MAX_DISTANCE / max_exact) * (N_BUCKETS - max_exact)).astype(jnp.int32)
    large = jnp.minimum(large, N_BUCKETS - 1)
    return jnp.where(d < max_exact, d, large)


def _band_kernel(row_ref, o_ref):
    rows = jnp.broadcast_to(row_ref[...], (o_ref.shape[0], row_ref.shape[1]))
    o_ref[...] = pltpu.roll(rows, 0, 1, stride=1, stride_axis=0)[:, :o_ref.shape[1]]


def _bias_band(table, T):
    y = np.arange(3 * T)
    dist = np.clip(np.where(y <= 2 * T, T - y, 4 * T - y), 0, 2 * T - 1)
    row = table.astype(F32)[_t5_bucket(jnp.asarray(dist, jnp.int32))].T
    ch = row.shape[0]
    return pl.pallas_call(
        _band_kernel,
        out_shape=jax.ShapeDtypeStruct((ch, T, 2 * T), F32),
        grid=(ch,),
        in_specs=[pl.BlockSpec((None, 1, 3 * T), lambda c: (c, 0, 0))],
        out_specs=pl.BlockSpec((None, T, 2 * T), lambda c: (c, 0, 0)),
        compiler_params=_params("parallel"),
        name="bias_band",
    )(row[:, None, :])


def _lambda_init(layer):
    return 0.8 - 0.6 * math.exp(-0.3 * layer)


def _dup_kv_heads(w):
    q_w = SW_Q_HEADS * SW_DIM
    kv = w[:, q_w:].reshape(w.shape[0], 2 * SW_KV_HEADS, 1, SW_DIM)
    kv = jnp.broadcast_to(kv, (w.shape[0], 2 * SW_KV_HEADS, 2, SW_DIM)).reshape(w.shape[0], -1)
    return jnp.concatenate([w[:, :q_w], kv], axis=1)


def _scale_q_columns(w, n_q, scale):
    return w * jnp.where(jnp.arange(w.shape[-1]) < n_q, scale, 1.0)


def _pad_ff(a, axis):
    pad = [(0, 0)] * a.ndim
    pad[axis] = (0, D_FF_PAD - D_FF)
    return jnp.pad(a, pad)


def _split_up_kernel(w_ref, gate_ref, value_ref):
    keep = lax.broadcasted_iota(jnp.int32, gate_ref.shape, 1) < D_FF
    gate_ref[...] = jnp.where(keep, w_ref[:, :D_FF_PAD], 0.0).astype(gate_ref.dtype)
    start = 2 * D_FF - D_FF_PAD
    window = pltpu.roll(w_ref[:, start:], D_FF_PAD - (D_FF - start), 1)
    value_ref[...] = jnp.where(keep, window, 0.0).astype(value_ref.dtype)


def split_up_call(w_up):
    L, D, _ = w_up.shape
    out = jax.ShapeDtypeStruct((L, D, D_FF_PAD), BF16)
    out_spec = pl.BlockSpec((None, PREP_ROWS, D_FF_PAD), lambda l, i: (l, i, 0))
    return pl.pallas_call(
        _split_up_kernel,
        out_shape=(out, out),
        grid=(L, D // PREP_ROWS),
        in_specs=[pl.BlockSpec((None, PREP_ROWS, 2 * D_FF), lambda l, i: (l, i, 0))],
        out_specs=(out_spec, out_spec),
        compiler_params=_params("parallel", "parallel"),
        name="ffn_up_weight_prep",
    )(w_up)


def _pad_down_kernel(w_ref, o_ref):
    row = pl.program_id(1) * PREP_DOWN_ROWS + lax.broadcasted_iota(jnp.int32, o_ref.shape, 0)
    o_ref[...] = jnp.where(row < D_FF, w_ref[...], 0.0).astype(o_ref.dtype)


def pad_down_call(w_down):
    L, _, D = w_down.shape
    spec = pl.BlockSpec((None, PREP_DOWN_ROWS, D), lambda l, i: (l, i, 0))
    return pl.pallas_call(
        _pad_down_kernel,
        out_shape=jax.ShapeDtypeStruct((L, D_FF_PAD, D), BF16),
        grid=(L, D_FF_PAD // PREP_DOWN_ROWS),
        in_specs=[spec],
        out_specs=spec,
        compiler_params=_params("parallel", "parallel"),
        name="ffn_down_weight_prep",
    )(w_down)


def kernel(x, rel_bias, attn_norm, ffn_norm, w_o, da_w_qkv, da_lambda, da_subln, sb_w_qkv, sw_w_qkv,
           sw_sinks, ffn_w_up, ffn_conv_w, ffn_conv_b, ffn_w_down, final_norm):
    B, S, D = x.shape
    x2 = x.reshape(B * S, D)
    da_bias_near = _bias_band((rel_bias - rel_bias[N_BUCKETS - 1]) * LOG2E, ATT_T)
    sw_bias = _bias_band(rel_bias * LOG2E, SW_BLOCK)
    da_w = _scale_q_columns(da_w_qkv, DA_HEADS * 2 * DA_QK_DIM, DA_QK_DIM ** -0.5 * LOG2E).astype(BF16)
    sb_w = _scale_q_columns(sb_w_qkv, SB_HEADS * SB_DIM, SB_DIM ** -0.5).astype(BF16)
    sw_w = jax.vmap(_dup_kv_heads)(
        _scale_q_columns(sw_w_qkv, SW_Q_HEADS * SW_DIM, SW_DIM ** -0.5 * LOG2E)).astype(BF16)
    wo_w = w_o.astype(BF16)
    wg, wv = split_up_call(ffn_w_up)
    cwg = _pad_ff(ffn_conv_w[:, :, :D_FF], 2)
    cwv = _pad_ff(ffn_conv_w[:, :, D_FF:], 2)
    cbg = _pad_ff(ffn_conv_b[:, None, :D_FF], 2)
    cbv = _pad_ff(ffn_conv_b[:, None, D_FF:], 2)
    wd = pad_down_call(ffn_w_down)
    attn_gain = attn_norm[:, None, :]
    ffn_gain = ffn_norm[:, None, :]
    final_gain = final_norm[None, None, :]

    weights_qkv = [((da_w, sb_w, sw_w)[layer % N_MIXERS], layer // N_MIXERS) for layer in range(DEPTH)]
    qkv = proj_call(x2, *weights_qkv[0], attn_gain, 0)
    for layer in range(DEPTH):
        mixer = layer % N_MIXERS
        slot = layer // N_MIXERS
        if mixer == 0:
            m = da_call(qkv, da_bias_near, da_lambda[slot], da_subln[slot], B, S, _lambda_init(layer))
        elif mixer == 1:
            m = sb_call(qkv, B, S)
        else:
            m = sw_call(qkv, sw_bias, sw_sinks[slot] * LOG2E, B, S)
        x2, act = wo_ffn_up_call(m, wo_w, x2, ffn_gain, wg, wv, cwg, cwv, cbg, cbv, layer, S)
        if layer < DEPTH - 1:
            x2, qkv = ffn_down_call(act, wd, layer, x2, attn_gain, layer + 1, *weights_qkv[layer + 1])
        else:
            h = ffn_down_call(act, wd, layer, x2, final_gain, 0)
    return h.reshape(B, S, D)
```

```python
import functools
import math

import jax
import jax.numpy as jnp
import numpy as np
from jax import lax
from jax.experimental import pallas as pl
from jax.experimental.pallas import tpu as pltpu

D_MODEL = 1024
DEPTH = 4
N_MIXERS = 3
N_BUCKETS = 32
MAX_DISTANCE = 128
DA_HEADS = 8
DA_QK_DIM = 64
DA_V_DIM = 128
SB_HEADS = 16
SB_DIM = 64
SW_Q_HEADS = 16
SW_KV_HEADS = 4
SW_DIM = 64
SW_BLOCK = 128
D_FF = 2752
EPS = 1e-6
NEG = -1e30
LOG2E = math.log2(math.e)

LANES = 128
SUBLANES = 8
VMEM_LIMIT = 52 * 1024 * 1024
WO_FFN_VMEM_LIMIT = 57 * 1024 * 1024

ATT_T = 256
DA_SLOTS = 2
SB_EXP_ZERO = -104.0
ROW_TILE = 512
PROJ_TM = 1024
PROJ_CHUNK = 512
PREP_ROWS = 256
PREP_DOWN_ROWS = 704
SW_LAG = 2
FFN_TM = 1024
FFN_TF = 256
D_FF_PAD = 2816

BF16 = jnp.bfloat16
F32 = jnp.float32


def _params(*sem):
    return pltpu.CompilerParams(dimension_semantics=sem, vmem_limit_bytes=VMEM_LIMIT)


def _layer_spec(arr, layer, single_buffer=False):
    shape = (None,) + arr.shape[1:]
    index_map = lambda *_: (layer,) + (0,) * (arr.ndim - 1)
    if single_buffer:
        return pl.BlockSpec(shape, index_map, pipeline_mode=pl.Buffered(1))
    return pl.BlockSpec(shape, index_map)


def _rms(xf, gain):
    return xf * lax.rsqrt(jnp.mean(xf * xf, axis=-1, keepdims=True) + EPS) * gain


def _project(h, w_ref, o_ref):
    for c in range(o_ref.shape[1] // PROJ_CHUNK):
        sl = slice(c * PROJ_CHUNK, (c + 1) * PROJ_CHUNK)
        o_ref[:, sl] = jnp.dot(h, w_ref[:, sl], preferred_element_type=F32).astype(o_ref.dtype)


def _norm_proj_kernel(x_ref, g_ref, w_ref, wup_ref, o_ref, gate_ref, value_ref):
    _project(_rms(x_ref[...], g_ref[...]).astype(BF16), w_ref, o_ref)
    _split_up(wup_ref, gate_ref, value_ref)


def proj_call(x2, w, layer, gain, gain_layer, w_up):
    M, D = x2.shape
    N = w.shape[2]
    L = w_up.shape[0]
    blocks = D // PREP_ROWS
    assert M // PROJ_TM == L * blocks, "one up-projection weight row block per grid step"
    row = lambda i: (i, 0)
    prep = lambda i: (i // blocks, i % blocks, 0)
    w_shape = jax.ShapeDtypeStruct((L, D, D_FF_PAD), BF16)
    w_spec = pl.BlockSpec((None, PREP_ROWS, D_FF_PAD), prep)
    return pl.pallas_call(
        _norm_proj_kernel,
        out_shape=(jax.ShapeDtypeStruct((M, N), BF16), w_shape, w_shape),
        grid=(M // PROJ_TM,),
        in_specs=[pl.BlockSpec((PROJ_TM, D), row), _layer_spec(gain, gain_layer),
                  _layer_spec(w, layer, single_buffer=True),
                  pl.BlockSpec((None, PREP_ROWS, 2 * D_FF), prep)],
        out_specs=(pl.BlockSpec((PROJ_TM, N), row), w_spec, w_spec),
        compiler_params=_params("parallel"),
        name="qkv_proj",
    )(x2, gain, w, w_up)


def _ffn_row_starts(n_groups):
    per = n_groups // SUBLANES
    return [SUBLANES * SUBLANES * (j % per) + j // per for j in range(n_groups)]


def _stage(ref, x):
    for c in range(ref.shape[0]):
        ref[c] = x[:, c * LANES:(c + 1) * LANES]


def _stage_interleaved(ref, x):
    for c in range(ref.shape[0]):
        for j, st in enumerate(_ffn_row_starts(x.shape[0] // SUBLANES)):
            ref[c, pl.ds(st, SUBLANES, stride=SUBLANES), :] = x[j * SUBLANES:(j + 1) * SUBLANES,
                                                                c * LANES:(c + 1) * LANES]


def _deinterleave_rows(ref, c):
    return jnp.concatenate([ref[c, pl.ds(st, SUBLANES, stride=SUBLANES), :]
                            for st in _ffn_row_starts(ref.shape[1] // SUBLANES)], axis=0)


def _causal_conv3(u, cw, cb, prev):
    nblk, g, _, tf = u.shape
    first = lax.broadcasted_iota(jnp.int32, (SUBLANES, tf), 0) == 0
    u1, u2 = [], []
    for b in range(nblk):
        tail = prev if b == 0 else u[b - 1, g - 2:]
        wrap = [jnp.where(first, pltpu.roll(tail[e], 1, 0), pltpu.roll(u[b, g - 2 + e], 1, 0))
                for e in range(2)]
        u1.append(jnp.concatenate([wrap[1][None], u[b, :g - 1]], axis=0))
        u2.append(jnp.concatenate([wrap[0][None], wrap[1][None], u[b, :g - 2]], axis=0))
    u1 = jnp.stack(u1)
    u2 = jnp.stack(u2)
    return u * cw[2:3] + u1 * cw[1:2] + u2 * cw[0:1] + cb


def _wo_ffn_up_kernel(m_ref, wo_ref, x_ref, gain_ref, wg_ref, wv_ref, cwg_ref, cwv_ref, cbg_ref, cbv_ref,
                      xo_ref, g_ref, carry_ref, stage_ref, h_ref, *, tiles_per_seq):
    tm = m_ref.shape[0]
    nblk = tm // ROW_TILE
    g = ROW_TILE // SUBLANES

    @pl.when(pl.program_id(0) % tiles_per_seq == 0)
    def _():
        carry_ref[...] = jnp.zeros(carry_ref.shape, F32)

    for b in range(nblk):
        rows = slice(b * ROW_TILE, (b + 1) * ROW_TILE)
        xn = x_ref[rows, :] + jnp.dot(m_ref[rows, :], wo_ref[...], preferred_element_type=F32)
        xo_ref[rows, :] = xn
        _stage_interleaved(stage_ref, _rms(xn, gain_ref[...]))
        for c in range(stage_ref.shape[0]):
            h_ref[rows, c * LANES:(c + 1) * LANES] = stage_ref[c].astype(h_ref.dtype)

    h = h_ref[...]
    for c in range(g_ref.shape[1] // FFN_TF):
        sl = slice(c * FFN_TF, (c + 1) * FFN_TF)
        ug = jnp.dot(h, wg_ref[:, sl], preferred_element_type=F32).reshape(nblk, g, SUBLANES, FFN_TF)
        uv = jnp.dot(h, wv_ref[:, sl], preferred_element_type=F32).reshape(nblk, g, SUBLANES, FFN_TF)
        prev_g = carry_ref[0, :, :, sl]
        prev_v = carry_ref[1, :, :, sl]
        carry_ref[0, :, :, sl] = ug[nblk - 1, g - 2:]
        carry_ref[1, :, :, sl] = uv[nblk - 1, g - 2:]
        yg = _causal_conv3(ug, cwg_ref[:, sl], cbg_ref[:, sl], prev_g)
        yv = _causal_conv3(uv, cwv_ref[:, sl], cbv_ref[:, sl], prev_v)
        act = yg * (1.0 / (1.0 + jnp.exp(-yg))) * yv
        g_ref[:, sl] = act.reshape(tm, FFN_TF).astype(g_ref.dtype)


def wo_ffn_up_call(m, wo, x2, gain, wg, wv, cwg, cwv, cbg, cbv, layer, seq):
    M, D = x2.shape
    F = wg.shape[2]
    row = lambda i: (i, 0)
    resident = lambda arr: _layer_spec(arr, layer, single_buffer=True)
    return pl.pallas_call(
        functools.partial(_wo_ffn_up_kernel, tiles_per_seq=seq // FFN_TM),
        out_shape=(jax.ShapeDtypeStruct((M, D), F32), jax.ShapeDtypeStruct((M, F), BF16)),
        grid=(M // FFN_TM,),
        in_specs=[pl.BlockSpec((FFN_TM, D), row), resident(wo), pl.BlockSpec((FFN_TM, D), row), resident(gain),
                  resident(wg), resident(wv), resident(cwg), resident(cwv), resident(cbg), resident(cbv)],
        out_specs=(pl.BlockSpec((FFN_TM, D), row), pl.BlockSpec((FFN_TM, F), row)),
        scratch_shapes=[pltpu.VMEM((2, 2, SUBLANES, F), F32),
                        pltpu.VMEM((D // LANES, ROW_TILE, LANES), F32),
                        pltpu.VMEM((FFN_TM, D), BF16)],
        compiler_params=pltpu.CompilerParams(dimension_semantics=("arbitrary",),
                                             vmem_limit_bytes=WO_FFN_VMEM_LIMIT),
        name="wo_ffn_up_conv_gate",
    )(m, wo, x2, gain, wg, wv, cwg, cwv, cbg, cbv)


def _ffn_down_kernel(a_ref, w_ref, x_ref, g_ref, *refs, project):
    if project:
        wq_ref, xo_ref, qkv_ref, stage_ref = refs
    else:
        h_ref, stage_ref, xo_ref = refs
    _stage(stage_ref, jnp.dot(a_ref[...], w_ref[...], preferred_element_type=F32))
    for c in range(stage_ref.shape[0]):
        sl = slice(c * LANES, (c + 1) * LANES)
        xo_ref[:, sl] = x_ref[:, sl] + _deinterleave_rows(stage_ref, c)
    h = _rms(xo_ref[...], g_ref[...])
    if project:
        _project(h.astype(BF16), wq_ref, qkv_ref)
    else:
        h_ref[...] = h.astype(h_ref.dtype)


def ffn_down_call(a, w, layer, x2, gain, gain_layer, w_qkv=None, qkv_layer=0):
    M, D = x2.shape
    F = a.shape[1]
    row = lambda i: (i, 0)
    project = w_qkv is not None
    in_specs = [pl.BlockSpec((ROW_TILE, F), row), _layer_spec(w, layer, single_buffer=True),
                pl.BlockSpec((ROW_TILE, D), row), _layer_spec(gain, gain_layer)]
    stage = pltpu.VMEM((D // LANES, ROW_TILE, LANES), F32)
    if project:
        N = w_qkv.shape[2]
        in_specs.append(_layer_spec(w_qkv, qkv_layer, single_buffer=True))
        out_shape = (jax.ShapeDtypeStruct((M, D), F32), jax.ShapeDtypeStruct((M, N), BF16))
        out_specs = (pl.BlockSpec((ROW_TILE, D), row), pl.BlockSpec((ROW_TILE, N), row))
        scratch, args = [stage], (a, w, x2, gain, w_qkv)
    else:
        out_shape, out_specs = jax.ShapeDtypeStruct((M, D), F32), pl.BlockSpec((ROW_TILE, D), row)
        scratch, args = [stage, pltpu.VMEM((ROW_TILE, D), F32)], (a, w, x2, gain)
    return pl.pallas_call(
        functools.partial(_ffn_down_kernel, project=project),
        out_shape=out_shape,
        grid=(M // ROW_TILE,),
        in_specs=in_specs,
        out_specs=out_specs,
        scratch_shapes=scratch,
        compiler_params=_params("parallel"),
        name="ffn_down_residual_norm",
    )(*args)


def _half_masks(q):
    lane = lax.broadcasted_iota(jnp.int32, q.shape, 1)
    zero = jnp.zeros_like(q)
    return jnp.where(lane < 64, q, zero), jnp.where(lane >= 64, q, zero)


def _qk(q, k):
    return lax.dot_general(q, k, (((1,), (1,)), ((), ())), preferred_element_type=F32)


def _da_kernel(q_ref, k_ref, v_ref, bias_ref, lam_ref, sub_ref, o_ref, s_ref, *, lambda_init):
    T = ATT_T
    nq = q_ref.shape[0] // T
    r = lax.broadcasted_iota(jnp.int32, (T, T), 0)
    cc = lax.broadcasted_iota(jnp.int32, (T, T), 1)
    causal = r >= cc
    lam = lam_ref[...]
    lam_full = (jnp.exp(jnp.sum(lam[0:1] * lam[1:2], keepdims=True))
                - jnp.exp(jnp.sum(lam[2:3] * lam[3:4], keepdims=True)) + lambda_init)
    def score_steps(qi, st):
        def first():
            st["q01"] = jnp.concatenate(_half_masks(q_ref[qi * T:(qi + 1) * T, :]), axis=0)
            st["mx"] = [None, None]

        def tile(j):
            if j == 0:
                first()
            ss = _qk(st["q01"], k_ref[j * T:(j + 1) * T, :])
            for c in range(2):
                s = ss[c * T:(c + 1) * T]
                if j == qi - 1:
                    s = s + bias_ref[c, :, 0:T]
                if j == qi:
                    s = jnp.where(causal, s + bias_ref[c, :, T:2 * T], NEG)
                s_ref[qi % DA_SLOTS, c, :, j * T:(j + 1) * T] = s
                t = jnp.maximum(s[:, :LANES], s[:, LANES:])
                st["mx"][c] = t if st["mx"][c] is None else jnp.maximum(st["mx"][c], t)
        return [functools.partial(tile, j) for j in range(qi + 1)]

    def value_steps(qi, st):
        def exp_tile(j):
            if j == 0:
                st["m2"] = []
                for c in range(2):
                    m = jnp.broadcast_to(jnp.max(st["mx"][c], axis=-1, keepdims=True), (T, LANES))
                    st["m2"].append(jnp.concatenate([m, m], axis=1))
                st["lsum"] = [None, None]
            for c in range(2):
                p = jnp.exp2(s_ref[qi % DA_SLOTS, c, :, j * T:(j + 1) * T] - st["m2"][c])
                s_ref[qi % DA_SLOTS, c, :, j * T:(j + 1) * T] = p
                t = p[:, :LANES] + p[:, LANES:]
                st["lsum"][c] = t if st["lsum"][c] is None else st["lsum"][c] + t

        def value_tile(j):
            if j == 0:
                l0 = jnp.sum(st["lsum"][0], axis=-1, keepdims=True)
                l1 = jnp.sum(st["lsum"][1], axis=-1, keepdims=True)
                st["ratio"] = lam_full * l0 / l1
                st["inv_l0"] = 1.0 / l0
                st["acc"] = None
            a = (s_ref[qi % DA_SLOTS, 0, :, j * T:(j + 1) * T]
                 - s_ref[qi % DA_SLOTS, 1, :, j * T:(j + 1) * T] * st["ratio"])
            pv = jnp.dot(a.astype(BF16), v_ref[j * T:(j + 1) * T, :], preferred_element_type=F32)
            st["acc"] = pv if st["acc"] is None else st["acc"] + pv
            if j == qi:
                o = _rms(st["acc"] * st["inv_l0"], sub_ref[...]) * (1.0 - lambda_init)
                o_ref[qi * T:(qi + 1) * T, :] = o.astype(o_ref.dtype)
        return ([functools.partial(exp_tile, j) for j in range(qi + 1)],
                [functools.partial(value_tile, j) for j in range(qi + 1)])

    states = [dict() for _ in range(nq)]
    stages = [score_steps(qi, states[qi]) for qi in range(nq)]
    later = [value_steps(qi, states[qi]) for qi in range(nq)]
    for r in range(nq + 1):
        lists = [stages[r] if r < nq else [],
                 later[r - 1][0] + later[r - 1][1] if r >= 1 else []]
        for t in range(max(len(steps) for steps in lists)):
            for steps in lists:
                if t < len(steps):
                    steps[t]()


def da_call(qkv, bias_near, lam, subln, batch, seq, lambda_init):
    T = ATT_T
    nh = DA_HEADS
    return pl.pallas_call(
        functools.partial(_da_kernel, lambda_init=lambda_init),
        out_shape=jax.ShapeDtypeStruct((batch * seq, nh * DA_V_DIM), BF16),
        grid=(nh, batch),
        in_specs=[pl.BlockSpec((seq, LANES), lambda h, b: (b, h)),
                  pl.BlockSpec((seq, LANES), lambda h, b: (b, nh + h)),
                  pl.BlockSpec((seq, LANES), lambda h, b: (b, 2 * nh + h)),
                  pl.BlockSpec((2, T, 2 * T), lambda h, b: (h, 0, 0)),
                  pl.BlockSpec((4, DA_QK_DIM), lambda h, b: (0, 0)),
                  pl.BlockSpec((1, DA_V_DIM), lambda h, b: (0, 0))],
        out_specs=pl.BlockSpec((seq, LANES), lambda h, b: (b, h)),
        scratch_shapes=[pltpu.VMEM((DA_SLOTS, 2, T, seq), F32)],
        compiler_params=_params("parallel", "parallel"),
        name="diff_attention",
    )(qkv, qkv, qkv, bias_near, lam, subln.reshape(1, DA_V_DIM))


def _sb_kernel(q_ref, k_ref, v_ref, o_ref, lbn_ref, hmn_ref, lbf_ref, hmf_ref, acc_ref):
    T = ATT_T
    nq = q_ref.shape[0] // T
    kr = lax.broadcasted_iota(jnp.int32, (T, T), 0)
    kc = lax.broadcasted_iota(jnp.int32, (T, T), 1)
    suffix = jnp.where(kr > kc, 1.0, 0.0).astype(BF16)
    strict = kc < kr
    lane = lax.broadcasted_iota(jnp.int32, (T, LANES), 1)
    strict2 = jnp.concatenate([strict, strict], axis=0)

    def score_tile(qi, q01, j, blk, carry, lb_view, hm_view):
        zz = _qk(q01, k_ref[j * T:(j + 1) * T, :])
        for c in range(2):
            z = zz[c * T:(c + 1) * T]
            lb = jnp.minimum(z, 0.0) - jnp.log(1.0 + jnp.exp(-jnp.abs(z)))
            log_1m_beta = lb - z
            if j == qi:
                log_1m_beta = jnp.where(strict, log_1m_beta, 0.0)
            rows = slice(blk + c * T, blk + (c + 1) * T)
            hm_view[rows, :] = log_1m_beta.astype(BF16)
            if carry[c] is not None:
                lb = lb + carry[c]
            lb_view[rows, :] = lb
            if j > 0:
                rs = jnp.sum(log_1m_beta, axis=-1, keepdims=True)
                carry[c] = rs if carry[c] is None else carry[c] + rs

    def value_tiles(qi, tiles, lb_view, hm_view):
        between = jnp.dot(hm_view[0:2 * len(tiles) * T, :], suffix, preferred_element_type=F32)
        acc = None
        for pos, j in enumerate(tiles):
            rows = slice(2 * pos * T, 2 * (pos + 1) * T)
            a = jnp.exp(lb_view[rows, :] + between[rows])
            if j == qi:
                a = jnp.where(strict2, a, 0.0)
            pv = jnp.dot(a.astype(BF16), v_ref[j * T:(j + 1) * T, :], preferred_element_type=F32)
            acc = pv if acc is None else acc + pv
        return acc

    def stacked_q(qi):
        return jnp.concatenate(_half_masks(q_ref[qi * T:(qi + 1) * T, :]), axis=0)

    near = [[j for j in (qi, qi - 1) if j >= 0] for qi in range(nq)]
    carries = [[None, None] for _ in range(nq)]
    for qi in range(nq + 1):
        if qi < nq:
            q01 = stacked_q(qi)
            for pos, j in enumerate(near[qi]):
                score_tile(qi, q01, j, 2 * pos * T, carries[qi], lbn_ref.at[qi % 2], hmn_ref.at[qi % 2])
        if qi > 0:
            acc_ref[qi - 1] = value_tiles(qi - 1, near[qi - 1], lbn_ref.at[(qi - 1) % 2], hmn_ref.at[(qi - 1) % 2])

    for qi in range(2, nq):
        far = list(range(qi - 2, -1, -1))
        carry = carries[qi]
        nearest_sum = jnp.max(jnp.maximum(carry[0], carry[1]))

        @pl.when(nearest_sum > SB_EXP_ZERO)
        def _():
            q01 = stacked_q(qi)
            far_carry = list(carry)
            for pos, j in enumerate(far):
                score_tile(qi, q01, j, 2 * pos * T, far_carry, lbf_ref, hmf_ref)
            acc_ref[qi] += value_tiles(qi, far, lbf_ref, hmf_ref)

    for qi in range(nq):
        acc = acc_ref[qi]
        o_ref[qi * T:(qi + 1) * T, :] = jnp.where(lane < 64, acc[0:T], acc[T:2 * T]).astype(o_ref.dtype)


def sb_call(qkv, batch, seq):
    npair = SB_HEADS // 2
    return pl.pallas_call(
        _sb_kernel,
        out_shape=jax.ShapeDtypeStruct((batch * seq, SB_HEADS * SB_DIM), BF16),
        grid=(batch, npair),
        in_specs=[pl.BlockSpec((seq, LANES), lambda b, p: (b, p)),
                  pl.BlockSpec((seq, LANES), lambda b, p: (b, npair + p)),
                  pl.BlockSpec((seq, LANES), lambda b, p: (b, 2 * npair + p))],
        out_specs=pl.BlockSpec((seq, LANES), lambda b, p: (b, p)),
        scratch_shapes=[pltpu.VMEM((2, 4 * ATT_T, ATT_T), F32), pltpu.VMEM((2, 4 * ATT_T, ATT_T), BF16),
                        pltpu.VMEM((2 * seq - 4 * ATT_T, ATT_T), F32),
                        pltpu.VMEM((2 * seq - 4 * ATT_T, ATT_T), BF16),
                        pltpu.VMEM((seq // ATT_T, 2 * ATT_T, LANES), F32)],
        compiler_params=_params("parallel", "parallel"),
        name="stick_breaking_attention",
    )(qkv, qkv, qkv)


def _sw_kernel(sink_ref, q_ref, k_ref, v_ref, bias_ref, o_ref, s_ref):
    W = SW_BLOCK
    nb = q_ref.shape[0] // W
    p_id = pl.program_id(1)
    r = lax.broadcasted_iota(jnp.int32, (2 * W, 2 * W), 0) & (W - 1)
    cidx = lax.broadcasted_iota(jnp.int32, (2 * W, 2 * W), 1)
    valid = ((cidx < W) & (cidx > r)) | ((cidx >= W) & (cidx - W <= r))
    biasm = jnp.where(valid, bias_ref[...], NEG)
    row = lax.broadcasted_iota(jnp.int32, (2 * W, 1), 0)
    sink = jnp.where(row < W, sink_ref[2 * p_id], sink_ref[2 * p_id + 1])
    lane = lax.broadcasted_iota(jnp.int32, (W, LANES), 1)
    def scores(n):
        q01 = jnp.concatenate(_half_masks(q_ref[n * W:(n + 1) * W, :]), axis=0)
        if n == 0:
            s_ref[0, :, W:] = _qk(q01, k_ref[0:W, :]) + biasm[:, W:]
        else:
            s_ref[n] = _qk(q01, k_ref[(n - 1) * W:(n + 1) * W, :]) + biasm

    def values(n):
        s = s_ref[0, :, W:] if n == 0 else s_ref[n]
        m = jnp.maximum(jnp.max(s, axis=-1, keepdims=True), sink)
        e = jnp.exp2(s - m)
        denom = jnp.sum(e, axis=-1, keepdims=True) + jnp.exp2(sink - m)
        v = v_ref[0:W, :] if n == 0 else v_ref[(n - 1) * W:(n + 1) * W, :]
        o = jnp.dot(e.astype(BF16), v, preferred_element_type=F32) / denom
        o_ref[n * W:(n + 1) * W, :] = jnp.where(lane < 64, o[0:W], o[W:]).astype(o_ref.dtype)

    for n in range(nb + SW_LAG):
        if n < nb:
            scores(n)
        if n >= SW_LAG:
            values(n - SW_LAG)


def sw_call(qkv, bias_band, sinks, batch, seq):
    npair = SW_Q_HEADS // 2
    q_blocks = SW_Q_HEADS * SW_DIM // LANES
    W = SW_BLOCK
    return pl.pallas_call(
        _sw_kernel,
        out_shape=jax.ShapeDtypeStruct((batch * seq, SW_Q_HEADS * SW_DIM), BF16),
        grid=(batch, npair),
        in_specs=[pl.BlockSpec(memory_space=pltpu.SMEM),
                  pl.BlockSpec((seq, LANES), lambda b, p: (b, p)),
                  pl.BlockSpec((seq, LANES), lambda b, p: (b, q_blocks + p // 2)),
                  pl.BlockSpec((seq, LANES), lambda b, p: (b, q_blocks + SW_KV_HEADS + p // 2)),
                  pl.BlockSpec((None, 2 * W, 2 * W), lambda b, p: (p, 0, 0))],
        out_specs=pl.BlockSpec((seq, LANES), lambda b, p: (b, p)),
        scratch_shapes=[pltpu.VMEM((seq // W, 2 * W, 2 * W), F32)],
        compiler_params=_params("parallel", "parallel"),
        name="sliding_window_attention",
    )(sinks, qkv, qkv, qkv, bias_band.reshape(npair, 2 * W, 2 * W))


def _t5_bucket(dist):
    max_exact = N_BUCKETS // 2
    d = jnp.maximum(dist, 0)
    large = max_exact + (jnp.log(jnp.maximum(d, 1).astype(F32) / max_exact)
                         / math.log(MAX_DISTANCE / max_exact) * (N_BUCKETS - max_exact)).astype(jnp.int32)
    large = jnp.minimum(large, N_BUCKETS - 1)
    return jnp.where(d < max_exact, d, large)


def _band_kernel(row_ref, o_ref):
    rows = jnp.broadcast_to(row_ref[...], (o_ref.shape[0], row_ref.shape[1]))
    o_ref[...] = pltpu.roll(rows, 0, 1, stride=1, stride_axis=0)[:, :o_ref.shape[1]]


def _bias_band(table, T):
    y = np.arange(3 * T)
    dist = np.clip(np.where(y <= 2 * T, T - y, 4 * T - y), 0, 2 * T - 1)
    row = table.astype(F32)[_t5_bucket(jnp.asarray(dist, jnp.int32))].T
    ch = row.shape[0]
    return pl.pallas_call(
        _band_kernel,
        out_shape=jax.ShapeDtypeStruct((ch, T, 2 * T), F32),
        grid=(ch,),
        in_specs=[pl.BlockSpec((None, 1, 3 * T), lambda c: (c, 0, 0))],
        out_specs=pl.BlockSpec((None, T, 2 * T), lambda c: (c, 0, 0)),
        compiler_params=_params("parallel"),
        name="bias_band",
    )(row[:, None, :])


def _lambda_init(layer):
    return 0.8 - 0.6 * math.exp(-0.3 * layer)


def _dup_kv_heads(w):
    q_w = SW_Q_HEADS * SW_DIM
    kv = w[:, q_w:].reshape(w.shape[0], 2 * SW_KV_HEADS, 1, SW_DIM)
    kv = jnp.broadcast_to(kv, (w.shape[0], 2 * SW_KV_HEADS, 2, SW_DIM)).reshape(w.shape[0], -1)
    return jnp.concatenate([w[:, :q_w], kv], axis=1)


def _scale_q_columns(w, n_q, scale):
    return w * jnp.where(jnp.arange(w.shape[-1]) < n_q, scale, 1.0)


def _pad_ff(a, axis):
    pad = [(0, 0)] * a.ndim
    pad[axis] = (0, D_FF_PAD - D_FF)
    return jnp.pad(a, pad)


def _split_up(w_ref, gate_ref, value_ref):
    keep = lax.broadcasted_iota(jnp.int32, gate_ref.shape, 1) < D_FF
    gate_ref[...] = jnp.where(keep, w_ref[:, :D_FF_PAD], 0.0).astype(gate_ref.dtype)
    start = 2 * D_FF - D_FF_PAD
    window = pltpu.roll(w_ref[:, start:], D_FF_PAD - (D_FF - start), 1)
    value_ref[...] = jnp.where(keep, window, 0.0).astype(value_ref.dtype)


def _pad_down_kernel(w_ref, o_ref):
    row = pl.program_id(1) * PREP_DOWN_ROWS + lax.broadcasted_iota(jnp.int32, o_ref.shape, 0)
    o_ref[...] = jnp.where(row < D_FF, w_ref[...], 0.0).astype(o_ref.dtype)


def pad_down_call(w_down):
    L, _, D = w_down.shape
    spec = pl.BlockSpec((None, PREP_DOWN_ROWS, D), lambda l, i: (l, i, 0))
    return pl.pallas_call(
        _pad_down_kernel,
        out_shape=jax.ShapeDtypeStruct((L, D_FF_PAD, D), BF16),
        grid=(L, D_FF_PAD // PREP_DOWN_ROWS),
        in_specs=[spec],
        out_specs=spec,
        compiler_params=_params("parallel", "parallel"),
        name="ffn_down_weight_prep",
    )(w_down)


def kernel(x, rel_bias, attn_norm, ffn_norm, w_o, da_w_qkv, da_lambda, da_subln, sb_w_qkv, sw_w_qkv,
           sw_sinks, ffn_w_up, ffn_conv_w, ffn_conv_b, ffn_w_down, final_norm):
    B, S, D = x.shape
    x2 = x.reshape(B * S, D)
    da_bias_near = _bias_band((rel_bias - rel_bias[N_BUCKETS - 1]) * LOG2E, ATT_T)
    sw_bias = _bias_band(rel_bias * LOG2E, SW_BLOCK)
    da_w = _scale_q_columns(da_w_qkv, DA_HEADS * 2 * DA_QK_DIM, DA_QK_DIM ** -0.5 * LOG2E).astype(BF16)
    sb_w = _scale_q_columns(sb_w_qkv, SB_HEADS * SB_DIM, SB_DIM ** -0.5).astype(BF16)
    sw_w = jax.vmap(_dup_kv_heads)(
        _scale_q_columns(sw_w_qkv, SW_Q_HEADS * SW_DIM, SW_DIM ** -0.5 * LOG2E)).astype(BF16)
    wo_w = w_o.astype(BF16)
    cwg = _pad_ff(ffn_conv_w[:, :, :D_FF], 2)
    cwv = _pad_ff(ffn_conv_w[:, :, D_FF:], 2)
    cbg = _pad_ff(ffn_conv_b[:, None, :D_FF], 2)
    cbv = _pad_ff(ffn_conv_b[:, None, D_FF:], 2)
    wd = pad_down_call(ffn_w_down)
    attn_gain = attn_norm[:, None, :]
    ffn_gain = ffn_norm[:, None, :]
    final_gain = final_norm[None, None, :]

    weights_qkv = [((da_w, sb_w, sw_w)[layer % N_MIXERS], layer // N_MIXERS) for layer in range(DEPTH)]
    qkv, wg, wv = proj_call(x2, *weights_qkv[0], attn_gain, 0, ffn_w_up)
    for layer in range(DEPTH):
        mixer = layer % N_MIXERS
        slot = layer // N_MIXERS
        if mixer == 0:
            m = da_call(qkv, da_bias_near, da_lambda[slot], da_subln[slot], B, S, _lambda_init(layer))
        elif mixer == 1:
            m = sb_call(qkv, B, S)
        else:
            m = sw_call(qkv, sw_bias, sw_sinks[slot] * LOG2E, B, S)
        x2, act = wo_ffn_up_call(m, wo_w, x2, ffn_gain, wg, wv, cwg, cwv, cbg, cbv, layer, S)
        if layer < DEPTH - 1:
            x2, qkv = ffn_down_call(act, wd, layer, x2, attn_gain, layer + 1, *weights_qkv[layer + 1])
        else:
            h = ffn_down_call(act, wd, layer, x2, final_gain, 0)
    return h.reshape(B, S, D)
```

```python
import functools
import math

import jax
import jax.numpy as jnp
import numpy as np
from jax import lax
from jax.experimental import pallas as pl
from jax.experimental.pallas import tpu as pltpu

D_MODEL = 1024
DEPTH = 4
N_MIXERS = 3
N_BUCKETS = 32
MAX_DISTANCE = 128
DA_HEADS = 8
DA_QK_DIM = 64
DA_V_DIM = 128
SB_HEADS = 16
SB_DIM = 64
SW_Q_HEADS = 16
SW_KV_HEADS = 4
SW_DIM = 64
SW_BLOCK = 128
D_FF = 2752
EPS = 1e-6
NEG = -1e30
LOG2E = math.log2(math.e)

LANES = 128
SUBLANES = 8
VMEM_LIMIT = 52 * 1024 * 1024
WO_FFN_VMEM_LIMIT = 57 * 1024 * 1024

ATT_T = 256
DA_SLOTS = 2
SB_EXP_ZERO = -104.0
ROW_TILE = 512
PROJ_TM = 1024
PROJ_CHUNK = 512
PREP_ROWS = 256
PREP_DOWN_ROWS = 176
SW_LAG = 2
FFN_TM = 1024
FFN_TF = 256
D_FF_PAD = 2816

BF16 = jnp.bfloat16
F32 = jnp.float32


def _params(*sem):
    return pltpu.CompilerParams(dimension_semantics=sem, vmem_limit_bytes=VMEM_LIMIT)


def _layer_spec(arr, layer, single_buffer=False):
    shape = (None,) + arr.shape[1:]
    index_map = lambda *_: (layer,) + (0,) * (arr.ndim - 1)
    if single_buffer:
        return pl.BlockSpec(shape, index_map, pipeline_mode=pl.Buffered(1))
    return pl.BlockSpec(shape, index_map)


def _rms(xf, gain):
    return xf * lax.rsqrt(jnp.mean(xf * xf, axis=-1, keepdims=True) + EPS) * gain


def _project(h, w_ref, o_ref):
    for c in range(o_ref.shape[1] // PROJ_CHUNK):
        sl = slice(c * PROJ_CHUNK, (c + 1) * PROJ_CHUNK)
        o_ref[:, sl] = jnp.dot(h, w_ref[:, sl], preferred_element_type=F32).astype(o_ref.dtype)


def _norm_proj_kernel(x_ref, g_ref, w_ref, wup_ref, o_ref, gate_ref, value_ref):
    _project(_rms(x_ref[...], g_ref[...]).astype(BF16), w_ref, o_ref)
    _split_up(wup_ref, gate_ref, value_ref)


def proj_call(x2, w, layer, gain, gain_layer, w_up):
    M, D = x2.shape
    N = w.shape[2]
    L = w_up.shape[0]
    blocks = D // PREP_ROWS
    assert M // PROJ_TM == L * blocks, "one up-projection weight row block per grid step"
    row = lambda i: (i, 0)
    prep = lambda i: (i // blocks, i % blocks, 0)
    w_shape = jax.ShapeDtypeStruct((L, D, D_FF_PAD), BF16)
    w_spec = pl.BlockSpec((None, PREP_ROWS, D_FF_PAD), prep)
    return pl.pallas_call(
        _norm_proj_kernel,
        out_shape=(jax.ShapeDtypeStruct((M, N), BF16), w_shape, w_shape),
        grid=(M // PROJ_TM,),
        in_specs=[pl.BlockSpec((PROJ_TM, D), row), _layer_spec(gain, gain_layer),
                  _layer_spec(w, layer, single_buffer=True),
                  pl.BlockSpec((None, PREP_ROWS, 2 * D_FF), prep)],
        out_specs=(pl.BlockSpec((PROJ_TM, N), row), w_spec, w_spec),
        compiler_params=_params("parallel"),
        name="qkv_proj",
    )(x2, gain, w, w_up)


def _ffn_row_starts(n_groups):
    per = n_groups // SUBLANES
    return [SUBLANES * SUBLANES * (j % per) + j // per for j in range(n_groups)]


def _stage(ref, x):
    for c in range(ref.shape[0]):
        ref[c] = x[:, c * LANES:(c + 1) * LANES]


def _stage_interleaved(ref, x):
    for c in range(ref.shape[0]):
        for j, st in enumerate(_ffn_row_starts(x.shape[0] // SUBLANES)):
            ref[c, pl.ds(st, SUBLANES, stride=SUBLANES), :] = x[j * SUBLANES:(j + 1) * SUBLANES,
                                                                c * LANES:(c + 1) * LANES]


def _deinterleave_rows(ref, c):
    return jnp.concatenate([ref[c, pl.ds(st, SUBLANES, stride=SUBLANES), :]
                            for st in _ffn_row_starts(ref.shape[1] // SUBLANES)], axis=0)


def _causal_conv3(u, cw, cb, prev):
    nblk, g, _, tf = u.shape
    first = lax.broadcasted_iota(jnp.int32, (SUBLANES, tf), 0) == 0
    u1, u2 = [], []
    for b in range(nblk):
        tail = prev if b == 0 else u[b - 1, g - 2:]
        wrap = [jnp.where(first, pltpu.roll(tail[e], 1, 0), pltpu.roll(u[b, g - 2 + e], 1, 0))
                for e in range(2)]
        u1.append(jnp.concatenate([wrap[1][None], u[b, :g - 1]], axis=0))
        u2.append(jnp.concatenate([wrap[0][None], wrap[1][None], u[b, :g - 2]], axis=0))
    u1 = jnp.stack(u1)
    u2 = jnp.stack(u2)
    return u * cw[2:3] + u1 * cw[1:2] + u2 * cw[0:1] + cb


def _wo_ffn_up_kernel(m_ref, wo_ref, x_ref, gain_ref, wg_ref, wv_ref, cwg_ref, cwv_ref, cbg_ref, cbv_ref,
                      xo_ref, g_ref, carry_ref, stage_ref, h_ref, *, tiles_per_seq):
    tm = m_ref.shape[0]
    nblk = tm // ROW_TILE
    g = ROW_TILE // SUBLANES

    @pl.when(pl.program_id(0) % tiles_per_seq == 0)
    def _():
        carry_ref[...] = jnp.zeros(carry_ref.shape, F32)

    for b in range(nblk):
        rows = slice(b * ROW_TILE, (b + 1) * ROW_TILE)
        xn = x_ref[rows, :] + jnp.dot(m_ref[rows, :], wo_ref[...], preferred_element_type=F32)
        xo_ref[rows, :] = xn
        _stage_interleaved(stage_ref, _rms(xn, gain_ref[...]))
        for c in range(stage_ref.shape[0]):
            h_ref[rows, c * LANES:(c + 1) * LANES] = stage_ref[c].astype(h_ref.dtype)

    h = h_ref[...]
    for c in range(g_ref.shape[1] // FFN_TF):
        sl = slice(c * FFN_TF, (c + 1) * FFN_TF)
        ug = jnp.dot(h, wg_ref[:, sl], preferred_element_type=F32).reshape(nblk, g, SUBLANES, FFN_TF)
        uv = jnp.dot(h, wv_ref[:, sl], preferred_element_type=F32).reshape(nblk, g, SUBLANES, FFN_TF)
        prev_g = carry_ref[0, :, :, sl]
        prev_v = carry_ref[1, :, :, sl]
        carry_ref[0, :, :, sl] = ug[nblk - 1, g - 2:]
        carry_ref[1, :, :, sl] = uv[nblk - 1, g - 2:]
        yg = _causal_conv3(ug, cwg_ref[:, sl], cbg_ref[:, sl], prev_g)
        yv = _causal_conv3(uv, cwv_ref[:, sl], cbv_ref[:, sl], prev_v)
        act = yg * (1.0 / (1.0 + jnp.exp(-yg))) * yv
        g_ref[:, sl] = act.reshape(tm, FFN_TF).astype(g_ref.dtype)


def wo_ffn_up_call(m, wo, x2, gain, wg, wv, cwg, cwv, cbg, cbv, layer, seq):
    M, D = x2.shape
    F = wg.shape[2]
    row = lambda i: (i, 0)
    resident = lambda arr: _layer_spec(arr, layer, single_buffer=True)
    return pl.pallas_call(
        functools.partial(_wo_ffn_up_kernel, tiles_per_seq=seq // FFN_TM),
        out_shape=(jax.ShapeDtypeStruct((M, D), F32), jax.ShapeDtypeStruct((M, F), BF16)),
        grid=(M // FFN_TM,),
        in_specs=[pl.BlockSpec((FFN_TM, D), row), resident(wo), pl.BlockSpec((FFN_TM, D), row), resident(gain),
                  resident(wg), resident(wv), resident(cwg), resident(cwv), resident(cbg), resident(cbv)],
        out_specs=(pl.BlockSpec((FFN_TM, D), row), pl.BlockSpec((FFN_TM, F), row)),
        scratch_shapes=[pltpu.VMEM((2, 2, SUBLANES, F), F32),
                        pltpu.VMEM((D // LANES, ROW_TILE, LANES), F32),
                        pltpu.VMEM((FFN_TM, D), BF16)],
        compiler_params=pltpu.CompilerParams(dimension_semantics=("arbitrary",),
                                             vmem_limit_bytes=WO_FFN_VMEM_LIMIT),
        name="wo_ffn_up_conv_gate",
    )(m, wo, x2, gain, wg, wv, cwg, cwv, cbg, cbv)


def _ffn_down_kernel(a_ref, w_ref, x_ref, g_ref, *refs, project):
    if project:
        wq_ref, xo_ref, qkv_ref, stage_ref = refs
    else:
        h_ref, stage_ref, xo_ref = refs
    _stage(stage_ref, jnp.dot(a_ref[...], w_ref[...], preferred_element_type=F32))
    for c in range(stage_ref.shape[0]):
        sl = slice(c * LANES, (c + 1) * LANES)
        xo_ref[:, sl] = x_ref[:, sl] + _deinterleave_rows(stage_ref, c)
    h = _rms(xo_ref[...], g_ref[...])
    if project:
        _project(h.astype(BF16), wq_ref, qkv_ref)
    else:
        h_ref[...] = h.astype(h_ref.dtype)


def ffn_down_call(a, w, layer, x2, gain, gain_layer, w_qkv=None, qkv_layer=0):
    M, D = x2.shape
    F = a.shape[1]
    row = lambda i: (i, 0)
    project = w_qkv is not None
    in_specs = [pl.BlockSpec((ROW_TILE, F), row), _layer_spec(w, layer, single_buffer=True),
                pl.BlockSpec((ROW_TILE, D), row), _layer_spec(gain, gain_layer)]
    stage = pltpu.VMEM((D // LANES, ROW_TILE, LANES), F32)
    if project:
        N = w_qkv.shape[2]
        in_specs.append(_layer_spec(w_qkv, qkv_layer, single_buffer=True))
        out_shape = (jax.ShapeDtypeStruct((M, D), F32), jax.ShapeDtypeStruct((M, N), BF16))
        out_specs = (pl.BlockSpec((ROW_TILE, D), row), pl.BlockSpec((ROW_TILE, N), row))
        scratch, args = [stage], (a, w, x2, gain, w_qkv)
    else:
        out_shape, out_specs = jax.ShapeDtypeStruct((M, D), F32), pl.BlockSpec((ROW_TILE, D), row)
        scratch, args = [stage, pltpu.VMEM((ROW_TILE, D), F32)], (a, w, x2, gain)
    return pl.pallas_call(
        functools.partial(_ffn_down_kernel, project=project),
        out_shape=out_shape,
        grid=(M // ROW_TILE,),
        in_specs=in_specs,
        out_specs=out_specs,
        scratch_shapes=scratch,
        compiler_params=_params("parallel"),
        name="ffn_down_residual_norm",
    )(*args)


def _half_masks(q):
    lane = lax.broadcasted_iota(jnp.int32, q.shape, 1)
    zero = jnp.zeros_like(q)
    return jnp.where(lane < 64, q, zero), jnp.where(lane >= 64, q, zero)


def _qk(q, k):
    return lax.dot_general(q, k, (((1,), (1,)), ((), ())), preferred_element_type=F32)


def _da_kernel(q_ref, k_ref, v_ref, bias_ref, lam_ref, sub_ref, *refs, lambda_init, prep_down):
    if prep_down:
        wd_ref, o_ref, wdo_ref, s_ref = refs
        step = pl.program_id(0) * pl.num_programs(1) + pl.program_id(1)
        _pad_down(wd_ref, wdo_ref, step % (D_FF_PAD // PREP_DOWN_ROWS))
    else:
        o_ref, s_ref = refs
    T = ATT_T
    nq = q_ref.shape[0] // T
    r = lax.broadcasted_iota(jnp.int32, (T, T), 0)
    cc = lax.broadcasted_iota(jnp.int32, (T, T), 1)
    causal = r >= cc
    lam = lam_ref[...]
    lam_full = (jnp.exp(jnp.sum(lam[0:1] * lam[1:2], keepdims=True))
                - jnp.exp(jnp.sum(lam[2:3] * lam[3:4], keepdims=True)) + lambda_init)
    def score_steps(qi, st):
        def first():
            st["q01"] = jnp.concatenate(_half_masks(q_ref[qi * T:(qi + 1) * T, :]), axis=0)
            st["mx"] = [None, None]

        def tile(j):
            if j == 0:
                first()
            ss = _qk(st["q01"], k_ref[j * T:(j + 1) * T, :])
            for c in range(2):
                s = ss[c * T:(c + 1) * T]
                if j == qi - 1:
                    s = s + bias_ref[c, :, 0:T]
                if j == qi:
                    s = jnp.where(causal, s + bias_ref[c, :, T:2 * T], NEG)
                s_ref[qi % DA_SLOTS, c, :, j * T:(j + 1) * T] = s
                t = jnp.maximum(s[:, :LANES], s[:, LANES:])
                st["mx"][c] = t if st["mx"][c] is None else jnp.maximum(st["mx"][c], t)
        return [functools.partial(tile, j) for j in range(qi + 1)]

    def value_steps(qi, st):
        def exp_tile(j):
            if j == 0:
                st["m2"] = []
                for c in range(2):
                    m = jnp.broadcast_to(jnp.max(st["mx"][c], axis=-1, keepdims=True), (T, LANES))
                    st["m2"].append(jnp.concatenate([m, m], axis=1))
                st["lsum"] = [None, None]
            for c in range(2):
                p = jnp.exp2(s_ref[qi % DA_SLOTS, c, :, j * T:(j + 1) * T] - st["m2"][c])
                s_ref[qi % DA_SLOTS, c, :, j * T:(j + 1) * T] = p
                t = p[:, :LANES] + p[:, LANES:]
                st["lsum"][c] = t if st["lsum"][c] is None else st["lsum"][c] + t

        def value_tile(j):
            if j == 0:
                l0 = jnp.sum(st["lsum"][0], axis=-1, keepdims=True)
                l1 = jnp.sum(st["lsum"][1], axis=-1, keepdims=True)
                st["ratio"] = lam_full * l0 / l1
                st["inv_l0"] = 1.0 / l0
                st["acc"] = None
            a = (s_ref[qi % DA_SLOTS, 0, :, j * T:(j + 1) * T]
                 - s_ref[qi % DA_SLOTS, 1, :, j * T:(j + 1) * T] * st["ratio"])
            pv = jnp.dot(a.astype(BF16), v_ref[j * T:(j + 1) * T, :], preferred_element_type=F32)
            st["acc"] = pv if st["acc"] is None else st["acc"] + pv
            if j == qi:
                o = _rms(st["acc"] * st["inv_l0"], sub_ref[...]) * (1.0 - lambda_init)
                o_ref[qi * T:(qi + 1) * T, :] = o.astype(o_ref.dtype)
        return ([functools.partial(exp_tile, j) for j in range(qi + 1)],
                [functools.partial(value_tile, j) for j in range(qi + 1)])

    states = [dict() for _ in range(nq)]
    stages = [score_steps(qi, states[qi]) for qi in range(nq)]
    later = [value_steps(qi, states[qi]) for qi in range(nq)]
    for r in range(nq + 1):
        lists = [stages[r] if r < nq else [],
                 later[r - 1][0] + later[r - 1][1] if r >= 1 else []]
        for t in range(max(len(steps) for steps in lists)):
            for steps in lists:
                if t < len(steps):
                    steps[t]()


def da_call(qkv, bias_near, lam, subln, batch, seq, lambda_init, w_down=None):
    T = ATT_T
    nh = DA_HEADS
    in_specs = [pl.BlockSpec((seq, LANES), lambda h, b: (b, h)),
                pl.BlockSpec((seq, LANES), lambda h, b: (b, nh + h)),
                pl.BlockSpec((seq, LANES), lambda h, b: (b, 2 * nh + h)),
                pl.BlockSpec((2, T, 2 * T), lambda h, b: (h, 0, 0)),
                pl.BlockSpec((4, DA_QK_DIM), lambda h, b: (0, 0)),
                pl.BlockSpec((1, DA_V_DIM), lambda h, b: (0, 0))]
    args = [qkv, qkv, qkv, bias_near, lam, subln.reshape(1, DA_V_DIM)]
    out_shape = jax.ShapeDtypeStruct((batch * seq, nh * DA_V_DIM), BF16)
    out_specs = pl.BlockSpec((seq, LANES), lambda h, b: (b, h))
    if w_down is not None:
        L, _, D = w_down.shape
        blocks = D_FF_PAD // PREP_DOWN_ROWS
        assert nh * batch == L * blocks, "one down-projection weight row block per grid step"
        prep = lambda h, b: ((h * batch + b) // blocks, (h * batch + b) % blocks, 0)
        in_specs.append(pl.BlockSpec((None, PREP_DOWN_ROWS, D), prep))
        args.append(w_down)
        out_shape = (out_shape, jax.ShapeDtypeStruct((L, D_FF_PAD, D), BF16))
        out_specs = (out_specs, pl.BlockSpec((None, PREP_DOWN_ROWS, D), prep))
    return pl.pallas_call(
        functools.partial(_da_kernel, lambda_init=lambda_init, prep_down=w_down is not None),
        out_shape=out_shape,
        grid=(nh, batch),
        in_specs=in_specs,
        out_specs=out_specs,
        scratch_shapes=[pltpu.VMEM((DA_SLOTS, 2, T, seq), F32)],
        compiler_params=_params("parallel", "parallel"),
        name="diff_attention",
    )(*args)


def _sb_kernel(q_ref, k_ref, v_ref, o_ref, lbn_ref, hmn_ref, lbf_ref, hmf_ref, acc_ref):
    T = ATT_T
    nq = q_ref.shape[0] // T
    kr = lax.broadcasted_iota(jnp.int32, (T, T), 0)
    kc = lax.broadcasted_iota(jnp.int32, (T, T), 1)
    suffix = jnp.where(kr > kc, 1.0, 0.0).astype(BF16)
    strict = kc < kr
    lane = lax.broadcasted_iota(jnp.int32, (T, LANES), 1)
    strict2 = jnp.concatenate([strict, strict], axis=0)

    def score_tile(qi, q01, j, blk, carry, lb_view, hm_view):
        zz = _qk(q01, k_ref[j * T:(j + 1) * T, :])
        for c in range(2):
            z = zz[c * T:(c + 1) * T]
            lb = jnp.minimum(z, 0.0) - jnp.log(1.0 + jnp.exp(-jnp.abs(z)))
            log_1m_beta = lb - z
            if j == qi:
                log_1m_beta = jnp.where(strict, log_1m_beta, 0.0)
            rows = slice(blk + c * T, blk + (c + 1) * T)
            hm_view[rows, :] = log_1m_beta.astype(BF16)
            if carry[c] is not None:
                lb = lb + carry[c]
            lb_view[rows, :] = lb
            if j > 0:
                rs = jnp.sum(log_1m_beta, axis=-1, keepdims=True)
                carry[c] = rs if carry[c] is None else carry[c] + rs

    def value_tiles(qi, tiles, lb_view, hm_view):
        between = jnp.dot(hm_view[0:2 * len(tiles) * T, :], suffix, preferred_element_type=F32)
        acc = None
        for pos, j in enumerate(tiles):
            rows = slice(2 * pos * T, 2 * (pos + 1) * T)
            a = jnp.exp(lb_view[rows, :] + between[rows])
            if j == qi:
                a = jnp.where(strict2, a, 0.0)
            pv = jnp.dot(a.astype(BF16), v_ref[j * T:(j + 1) * T, :], preferred_element_type=F32)
            acc = pv if acc is None else acc + pv
        return acc

    def stacked_q(qi):
        return jnp.concatenate(_half_masks(q_ref[qi * T:(qi + 1) * T, :]), axis=0)

    near = [[j for j in (qi, qi - 1) if j >= 0] for qi in range(nq)]
    carries = [[None, None] for _ in range(nq)]
    for qi in range(nq + 1):
        if qi < nq:
            q01 = stacked_q(qi)
            for pos, j in enumerate(near[qi]):
                score_tile(qi, q01, j, 2 * pos * T, carries[qi], lbn_ref.at[qi % 2], hmn_ref.at[qi % 2])
        if qi > 0:
            acc_ref[qi - 1] = value_tiles(qi - 1, near[qi - 1], lbn_ref.at[(qi - 1) % 2], hmn_ref.at[(qi - 1) % 2])

    for qi in range(2, nq):
        far = list(range(qi - 2, -1, -1))
        carry = carries[qi]
        nearest_sum = jnp.max(jnp.maximum(carry[0], carry[1]))

        @pl.when(nearest_sum > SB_EXP_ZERO)
        def _():
            q01 = stacked_q(qi)
            far_carry = list(carry)
            for pos, j in enumerate(far):
                score_tile(qi, q01, j, 2 * pos * T, far_carry, lbf_ref, hmf_ref)
            acc_ref[qi] += value_tiles(qi, far, lbf_ref, hmf_ref)

    for qi in range(nq):
        acc = acc_ref[qi]
        o_ref[qi * T:(qi + 1) * T, :] = jnp.where(lane < 64, acc[0:T], acc[T:2 * T]).astype(o_ref.dtype)


def sb_call(qkv, batch, seq):
    npair = SB_HEADS // 2
    return pl.pallas_call(
        _sb_kernel,
        out_shape=jax.ShapeDtypeStruct((batch * seq, SB_HEADS * SB_DIM), BF16),
        grid=(batch, npair),
        in_specs=[pl.BlockSpec((seq, LANES), lambda b, p: (b, p)),
                  pl.BlockSpec((seq, LANES), lambda b, p: (b, npair + p)),
                  pl.BlockSpec((seq, LANES), lambda b, p: (b, 2 * npair + p))],
        out_specs=pl.BlockSpec((seq, LANES), lambda b, p: (b, p)),
        scratch_shapes=[pltpu.VMEM((2, 4 * ATT_T, ATT_T), F32), pltpu.VMEM((2, 4 * ATT_T, ATT_T), BF16),
                        pltpu.VMEM((2 * seq - 4 * ATT_T, ATT_T), F32),
                        pltpu.VMEM((2 * seq - 4 * ATT_T, ATT_T), BF16),
                        pltpu.VMEM((seq // ATT_T, 2 * ATT_T, LANES), F32)],
        compiler_params=_params("parallel", "parallel"),
        name="stick_breaking_attention",
    )(qkv, qkv, qkv)


def _sw_kernel(sink_ref, q_ref, k_ref, v_ref, bias_ref, o_ref, s_ref):
    W = SW_BLOCK
    nb = q_ref.shape[0] // W
    p_id = pl.program_id(1)
    r = lax.broadcasted_iota(jnp.int32, (2 * W, 2 * W), 0) & (W - 1)
    cidx = lax.broadcasted_iota(jnp.int32, (2 * W, 2 * W), 1)
    valid = ((cidx < W) & (cidx > r)) | ((cidx >= W) & (cidx - W <= r))
    biasm = jnp.where(valid, bias_ref[...], NEG)
    row = lax.broadcasted_iota(jnp.int32, (2 * W, 1), 0)
    sink = jnp.where(row < W, sink_ref[2 * p_id], sink_ref[2 * p_id + 1])
    lane = lax.broadcasted_iota(jnp.int32, (W, LANES), 1)
    def scores(n):
        q01 = jnp.concatenate(_half_masks(q_ref[n * W:(n + 1) * W, :]), axis=0)
        if n == 0:
            s_ref[0, :, W:] = _qk(q01, k_ref[0:W, :]) + biasm[:, W:]
        else:
            s_ref[n] = _qk(q01, k_ref[(n - 1) * W:(n + 1) * W, :]) + biasm

    def values(n):
        s = s_ref[0, :, W:] if n == 0 else s_ref[n]
        m = jnp.maximum(jnp.max(s, axis=-1, keepdims=True), sink)
        e = jnp.exp2(s - m)
        denom = jnp.sum(e, axis=-1, keepdims=True) + jnp.exp2(sink - m)
        v = v_ref[0:W, :] if n == 0 else v_ref[(n - 1) * W:(n + 1) * W, :]
        o = jnp.dot(e.astype(BF16), v, preferred_element_type=F32) / denom
        o_ref[n * W:(n + 1) * W, :] = jnp.where(lane < 64, o[0:W], o[W:]).astype(o_ref.dtype)

    for n in range(nb + SW_LAG):
        if n < nb:
            scores(n)
        if n >= SW_LAG:
            values(n - SW_LAG)


def sw_call(qkv, bias_band, sinks, batch, seq):
    npair = SW_Q_HEADS // 2
    q_blocks = SW_Q_HEADS * SW_DIM // LANES
    W = SW_BLOCK
    return pl.pallas_call(
        _sw_kernel,
        out_shape=jax.ShapeDtypeStruct((batch * seq, SW_Q_HEADS * SW_DIM), BF16),
        grid=(batch, npair),
        in_specs=[pl.BlockSpec(memory_space=pltpu.SMEM),
                  pl.BlockSpec((seq, LANES), lambda b, p: (b, p)),
                  pl.BlockSpec((seq, LANES), lambda b, p: (b, q_blocks + p // 2)),
                  pl.BlockSpec((seq, LANES), lambda b, p: (b, q_blocks + SW_KV_HEADS + p // 2)),
                  pl.BlockSpec((None, 2 * W, 2 * W), lambda b, p: (p, 0, 0))],
        out_specs=pl.BlockSpec((seq, LANES), lambda b, p: (b, p)),
        scratch_shapes=[pltpu.VMEM((seq // W, 2 * W, 2 * W), F32)],
        compiler_params=_params("parallel", "parallel"),
        name="sliding_window_attention",
    )(sinks, qkv, qkv, qkv, bias_band.reshape(npair, 2 * W, 2 * W))


def _t5_bucket(dist):
    max_exact = N_BUCKETS // 2
    d = jnp.maximum(dist, 0)
    large = max_exact + (jnp.log(jnp.maximum(d, 1).astype(F32) / max_exact)
                         / math.log(MAX_DISTANCE / max_exact) * (N_BUCKETS - max_exact)).astype(jnp.int32)
    large = jnp.minimum(large, N_BUCKETS - 1)
    return jnp.where(d < max_exact, d, large)


def _band_kernel(row_ref, o_ref):
    rows = jnp.broadcast_to(row_ref[...], (o_ref.shape[0], row_ref.shape[1]))
    o_ref[...] = pltpu.roll(rows, 0, 1, stride=1, stride_axis=0)[:, :o_ref.shape[1]]


def _bias_band(table, T):
    y = np.arange(3 * T)
    dist = np.clip(np.where(y <= 2 * T, T - y, 4 * T - y), 0, 2 * T - 1)
    row = table.astype(F32)[_t5_bucket(jnp.asarray(dist, jnp.int32))].T
    ch = row.shape[0]
    return pl.pallas_call(
        _band_kernel,
        out_shape=jax.ShapeDtypeStruct((ch, T, 2 * T), F32),
        grid=(ch,),
        in_specs=[pl.BlockSpec((None, 1, 3 * T), lambda c: (c, 0, 0))],
        out_specs=pl.BlockSpec((None, T, 2 * T), lambda c: (c, 0, 0)),
        compiler_params=_params("parallel"),
        name="bias_band",
    )(row[:, None, :])


def _lambda_init(layer):
    return 0.8 - 0.6 * math.exp(-0.3 * layer)


def _dup_kv_heads(w):
    q_w = SW_Q_HEADS * SW_DIM
    kv = w[:, q_w:].reshape(w.shape[0], 2 * SW_KV_HEADS, 1, SW_DIM)
    kv = jnp.broadcast_to(kv, (w.shape[0], 2 * SW_KV_HEADS, 2, SW_DIM)).reshape(w.shape[0], -1)
    return jnp.concatenate([w[:, :q_w], kv], axis=1)


def _scale_q_columns(w, n_q, scale):
    return w * jnp.where(jnp.arange(w.shape[-1]) < n_q, scale, 1.0)


def _pad_ff(a, axis):
    pad = [(0, 0)] * a.ndim
    pad[axis] = (0, D_FF_PAD - D_FF)
    return jnp.pad(a, pad)


def _split_up(w_ref, gate_ref, value_ref):
    keep = lax.broadcasted_iota(jnp.int32, gate_ref.shape, 1) < D_FF
    gate_ref[...] = jnp.where(keep, w_ref[:, :D_FF_PAD], 0.0).astype(gate_ref.dtype)
    start = 2 * D_FF - D_FF_PAD
    window = pltpu.roll(w_ref[:, start:], D_FF_PAD - (D_FF - start), 1)
    value_ref[...] = jnp.where(keep, window, 0.0).astype(value_ref.dtype)


def _pad_down(w_ref, o_ref, block):
    row = block * PREP_DOWN_ROWS + lax.broadcasted_iota(jnp.int32, o_ref.shape, 0)
    o_ref[...] = jnp.where(row < D_FF, w_ref[...], 0.0).astype(o_ref.dtype)


def kernel(x, rel_bias, attn_norm, ffn_norm, w_o, da_w_qkv, da_lambda, da_subln, sb_w_qkv, sw_w_qkv,
           sw_sinks, ffn_w_up, ffn_conv_w, ffn_conv_b, ffn_w_down, final_norm):
    B, S, D = x.shape
    x2 = x.reshape(B * S, D)
    da_bias_near = _bias_band((rel_bias - rel_bias[N_BUCKETS - 1]) * LOG2E, ATT_T)
    sw_bias = _bias_band(rel_bias * LOG2E, SW_BLOCK)
    da_w = _scale_q_columns(da_w_qkv, DA_HEADS * 2 * DA_QK_DIM, DA_QK_DIM ** -0.5 * LOG2E).astype(BF16)
    sb_w = _scale_q_columns(sb_w_qkv, SB_HEADS * SB_DIM, SB_DIM ** -0.5).astype(BF16)
    sw_w = jax.vmap(_dup_kv_heads)(
        _scale_q_columns(sw_w_qkv, SW_Q_HEADS * SW_DIM, SW_DIM ** -0.5 * LOG2E)).astype(BF16)
    wo_w = w_o.astype(BF16)
    cwg = _pad_ff(ffn_conv_w[:, :, :D_FF], 2)
    cwv = _pad_ff(ffn_conv_w[:, :, D_FF:], 2)
    cbg = _pad_ff(ffn_conv_b[:, None, :D_FF], 2)
    cbv = _pad_ff(ffn_conv_b[:, None, D_FF:], 2)
    attn_gain = attn_norm[:, None, :]
    ffn_gain = ffn_norm[:, None, :]
    final_gain = final_norm[None, None, :]

    weights_qkv = [((da_w, sb_w, sw_w)[layer % N_MIXERS], layer // N_MIXERS) for layer in range(DEPTH)]
    qkv, wg, wv = proj_call(x2, *weights_qkv[0], attn_gain, 0, ffn_w_up)
    for layer in range(DEPTH):
        mixer = layer % N_MIXERS
        slot = layer // N_MIXERS
        if mixer == 0:
            if layer == 0:
                m, wd = da_call(qkv, da_bias_near, da_lambda[slot], da_subln[slot], B, S, _lambda_init(layer),
                                ffn_w_down)
            else:
                m = da_call(qkv, da_bias_near, da_lambda[slot], da_subln[slot], B, S, _lambda_init(layer))
        elif mixer == 1:
            m = sb_call(qkv, B, S)
        else:
            m = sw_call(qkv, sw_bias, sw_sinks[slot] * LOG2E, B, S)
        x2, act = wo_ffn_up_call(m, wo_w, x2, ffn_gain, wg, wv, cwg, cwv, cbg, cbv, layer, S)
        if layer < DEPTH - 1:
            x2, qkv = ffn_down_call(act, wd, layer, x2, attn_gain, layer + 1, *weights_qkv[layer + 1])
        else:
            h = ffn_down_call(act, wd, layer, x2, final_gain, 0)
    return h.reshape(B, S, D)
```

```python
import functools
import math

import jax
import jax.numpy as jnp
import numpy as np
from jax import lax
from jax.experimental import pallas as pl
from jax.experimental.pallas import tpu as pltpu

D_MODEL = 1024
DEPTH = 4
N_MIXERS = 3
N_BUCKETS = 32
MAX_DISTANCE = 128
DA_HEADS = 8
DA_QK_DIM = 64
DA_V_DIM = 128
SB_HEADS = 16
SB_DIM = 64
SW_Q_HEADS = 16
SW_KV_HEADS = 4
SW_DIM = 64
SW_BLOCK = 128
D_FF = 2752
EPS = 1e-6
NEG = -1e30
LOG2E = math.log2(math.e)

LANES = 128
SUBLANES = 8
VMEM_LIMIT = 52 * 1024 * 1024
WO_FFN_VMEM_LIMIT = 57 * 1024 * 1024

ATT_T = 256
DA_SLOTS = 2
SB_EXP_ZERO = -104.0
ROW_TILE = 512
PROJ_TM = 1024
PROJ_CHUNK = 512
PREP_ROWS = 256
PREP_DOWN_ROWS = 176
SW_LAG = 2
FFN_TM = 1024
FFN_TF = 256
D_FF_PAD = 2816

BF16 = jnp.bfloat16
F32 = jnp.float32


def _params(*sem):
    return pltpu.CompilerParams(dimension_semantics=sem, vmem_limit_bytes=VMEM_LIMIT)


def _layer_spec(arr, layer, single_buffer=False):
    shape = (None,) + arr.shape[1:]
    index_map = lambda *_: (layer,) + (0,) * (arr.ndim - 1)
    if single_buffer:
        return pl.BlockSpec(shape, index_map, pipeline_mode=pl.Buffered(1))
    return pl.BlockSpec(shape, index_map)


def _rms(xf, gain):
    return xf * lax.rsqrt(jnp.mean(xf * xf, axis=-1, keepdims=True) + EPS) * gain


def _project(h, w_ref, o_ref):
    for c in range(o_ref.shape[1] // PROJ_CHUNK):
        sl = slice(c * PROJ_CHUNK, (c + 1) * PROJ_CHUNK)
        o_ref[:, sl] = jnp.dot(h, w_ref[:, sl], preferred_element_type=F32).astype(o_ref.dtype)


def _norm_proj_kernel(x_ref, g_ref, w_ref, wup_ref, o_ref, gate_ref, value_ref):
    _project(_rms(x_ref[...], g_ref[...]).astype(BF16), w_ref, o_ref)
    _split_up(wup_ref, gate_ref, value_ref)


def proj_call(x2, w, layer, gain, gain_layer, w_up):
    M, D = x2.shape
    N = w.shape[2]
    L = w_up.shape[0]
    blocks = D // PREP_ROWS
    assert M // PROJ_TM == L * blocks, "one up-projection weight row block per grid step"
    row = lambda i: (i, 0)
    prep = lambda i: (i // blocks, i % blocks, 0)
    w_shape = jax.ShapeDtypeStruct((L, D, D_FF_PAD), BF16)
    w_spec = pl.BlockSpec((None, PREP_ROWS, D_FF_PAD), prep)
    return pl.pallas_call(
        _norm_proj_kernel,
        out_shape=(jax.ShapeDtypeStruct((M, N), BF16), w_shape, w_shape),
        grid=(M // PROJ_TM,),
        in_specs=[pl.BlockSpec((PROJ_TM, D), row), _layer_spec(gain, gain_layer),
                  _layer_spec(w, layer, single_buffer=True),
                  pl.BlockSpec((None, PREP_ROWS, 2 * D_FF), prep)],
        out_specs=(pl.BlockSpec((PROJ_TM, N), row), w_spec, w_spec),
        compiler_params=_params("parallel"),
        name="qkv_proj",
    )(x2, gain, w, w_up)


def _ffn_row_starts(n_groups):
    per = n_groups // SUBLANES
    return [SUBLANES * SUBLANES * (j % per) + j // per for j in range(n_groups)]


def _stage(ref, x):
    for c in range(ref.shape[0]):
        ref[c] = x[:, c * LANES:(c + 1) * LANES]


def _stage_interleaved(ref, x):
    for c in range(ref.shape[0]):
        for j, st in enumerate(_ffn_row_starts(x.shape[0] // SUBLANES)):
            ref[c, pl.ds(st, SUBLANES, stride=SUBLANES), :] = x[j * SUBLANES:(j + 1) * SUBLANES,
                                                                c * LANES:(c + 1) * LANES]


def _deinterleave_rows(ref, c):
    return jnp.concatenate([ref[c, pl.ds(st, SUBLANES, stride=SUBLANES), :]
                            for st in _ffn_row_starts(ref.shape[1] // SUBLANES)], axis=0)


def _causal_conv3(u, cw, cb, prev):
    nblk, g, _, tf = u.shape
    first = lax.broadcasted_iota(jnp.int32, (SUBLANES, tf), 0) == 0
    u1, u2 = [], []
    for b in range(nblk):
        tail = prev if b == 0 else u[b - 1, g - 2:]
        wrap = [jnp.where(first, pltpu.roll(tail[e], 1, 0), pltpu.roll(u[b, g - 2 + e], 1, 0))
                for e in range(2)]
        u1.append(jnp.concatenate([wrap[1][None], u[b, :g - 1]], axis=0))
        u2.append(jnp.concatenate([wrap[0][None], wrap[1][None], u[b, :g - 2]], axis=0))
    u1 = jnp.stack(u1)
    u2 = jnp.stack(u2)
    return u * cw[2:3] + u1 * cw[1:2] + u2 * cw[0:1] + cb


def _wo_ffn_up_kernel(m_ref, wo_ref, x_ref, gain_ref, wg_ref, wv_ref, cwg_ref, cwv_ref, cbg_ref, cbv_ref,
                      xo_ref, g_ref, carry_ref, stage_ref, h_ref, *, tiles_per_seq):
    tm = m_ref.shape[0]
    nblk = tm // ROW_TILE
    g = ROW_TILE // SUBLANES

    @pl.when(pl.program_id(0) % tiles_per_seq == 0)
    def _():
        carry_ref[...] = jnp.zeros(carry_ref.shape, F32)

    for b in range(nblk):
        rows = slice(b * ROW_TILE, (b + 1) * ROW_TILE)
        xn = x_ref[rows, :] + jnp.dot(m_ref[rows, :], wo_ref[...], preferred_element_type=F32)
        xo_ref[rows, :] = xn
        _stage_interleaved(stage_ref, _rms(xn, gain_ref[...]))
        for c in range(stage_ref.shape[0]):
            h_ref[rows, c * LANES:(c + 1) * LANES] = stage_ref[c].astype(h_ref.dtype)

    h = h_ref[...]
    for c in range(g_ref.shape[1] // FFN_TF):
        sl = slice(c * FFN_TF, (c + 1) * FFN_TF)
        ug = jnp.dot(h, wg_ref[:, sl], preferred_element_type=F32).reshape(nblk, g, SUBLANES, FFN_TF)
        uv = jnp.dot(h, wv_ref[:, sl], preferred_element_type=F32).reshape(nblk, g, SUBLANES, FFN_TF)
        prev_g = carry_ref[0, :, :, sl]
        prev_v = carry_ref[1, :, :, sl]
        carry_ref[0, :, :, sl] = ug[nblk - 1, g - 2:]
        carry_ref[1, :, :, sl] = uv[nblk - 1, g - 2:]
        yg = _causal_conv3(ug, cwg_ref[:, sl], cbg_ref[:, sl], prev_g)
        yv = _causal_conv3(uv, cwv_ref[:, sl], cbv_ref[:, sl], prev_v)
        act = yg * (1.0 / (1.0 + jnp.exp(-yg))) * yv
        g_ref[:, sl] = act.reshape(tm, FFN_TF).astype(g_ref.dtype)


def wo_ffn_up_call(m, wo, x2, gain, wg, wv, cwg, cwv, cbg, cbv, layer, seq):
    M, D = x2.shape
    F = wg.shape[2]
    row = lambda i: (i, 0)
    resident = lambda arr: _layer_spec(arr, layer, single_buffer=True)
    return pl.pallas_call(
        functools.partial(_wo_ffn_up_kernel, tiles_per_seq=seq // FFN_TM),
        out_shape=(jax.ShapeDtypeStruct((M, D), F32), jax.ShapeDtypeStruct((M, F), BF16)),
        grid=(M // FFN_TM,),
        in_specs=[pl.BlockSpec((FFN_TM, D), row), resident(wo), pl.BlockSpec((FFN_TM, D), row), resident(gain),
                  resident(wg), resident(wv), resident(cwg), resident(cwv), resident(cbg), resident(cbv)],
        out_specs=(pl.BlockSpec((FFN_TM, D), row), pl.BlockSpec((FFN_TM, F), row)),
        scratch_shapes=[pltpu.VMEM((2, 2, SUBLANES, F), F32),
                        pltpu.VMEM((D // LANES, ROW_TILE, LANES), F32),
                        pltpu.VMEM((FFN_TM, D), BF16)],
        compiler_params=pltpu.CompilerParams(dimension_semantics=("arbitrary",),
                                             vmem_limit_bytes=WO_FFN_VMEM_LIMIT),
        name="wo_ffn_up_conv_gate",
    )(m, wo, x2, gain, wg, wv, cwg, cwv, cbg, cbv)


def _ffn_down_kernel(a_ref, w_ref, x_ref, g_ref, *refs, project):
    if project:
        wq_ref, xo_ref, qkv_ref, stage_ref = refs
    else:
        h_ref, stage_ref, xo_ref = refs
    _stage(stage_ref, jnp.dot(a_ref[...], w_ref[...], preferred_element_type=F32))
    for c in range(stage_ref.shape[0]):
        sl = slice(c * LANES, (c + 1) * LANES)
        xo_ref[:, sl] = x_ref[:, sl] + _deinterleave_rows(stage_ref, c)
    h = _rms(xo_ref[...], g_ref[...])
    if project:
        _project(h.astype(BF16), wq_ref, qkv_ref)
    else:
        h_ref[...] = h.astype(h_ref.dtype)


def ffn_down_call(a, w, layer, x2, gain, gain_layer, w_qkv=None, qkv_layer=0):
    M, D = x2.shape
    F = a.shape[1]
    row = lambda i: (i, 0)
    project = w_qkv is not None
    in_specs = [pl.BlockSpec((ROW_TILE, F), row), _layer_spec(w, layer, single_buffer=True),
                pl.BlockSpec((ROW_TILE, D), row), _layer_spec(gain, gain_layer)]
    stage = pltpu.VMEM((D // LANES, ROW_TILE, LANES), F32)
    if project:
        N = w_qkv.shape[2]
        in_specs.append(_layer_spec(w_qkv, qkv_layer, single_buffer=True))
        out_shape = (jax.ShapeDtypeStruct((M, D), F32), jax.ShapeDtypeStruct((M, N), BF16))
        out_specs = (pl.BlockSpec((ROW_TILE, D), row), pl.BlockSpec((ROW_TILE, N), row))
        scratch, args = [stage], (a, w, x2, gain, w_qkv)
    else:
        out_shape, out_specs = jax.ShapeDtypeStruct((M, D), F32), pl.BlockSpec((ROW_TILE, D), row)
        scratch, args = [stage, pltpu.VMEM((ROW_TILE, D), F32)], (a, w, x2, gain)
    return pl.pallas_call(
        functools.partial(_ffn_down_kernel, project=project),
        out_shape=out_shape,
        grid=(M // ROW_TILE,),
        in_specs=in_specs,
        out_specs=out_specs,
        scratch_shapes=scratch,
        compiler_params=_params("parallel"),
        name="ffn_down_residual_norm",
    )(*args)


def _half_masks(q):
    lane = lax.broadcasted_iota(jnp.int32, q.shape, 1)
    zero = jnp.zeros_like(q)
    return jnp.where(lane < 64, q, zero), jnp.where(lane >= 64, q, zero)


def _qk(q, k):
    return lax.dot_general(q, k, (((1,), (1,)), ((), ())), preferred_element_type=F32)


def _da_kernel(q_ref, k_ref, v_ref, bias_ref, lam_ref, sub_ref, *refs, lambda_init, prep_down):
    if prep_down:
        wd_ref, o_ref, wdo_ref, s_ref = refs
        step = pl.program_id(0) * pl.num_programs(1) + pl.program_id(1)
        _pad_down(wd_ref, wdo_ref, step % (D_FF_PAD // PREP_DOWN_ROWS))
    else:
        o_ref, s_ref = refs
    T = ATT_T
    nq = q_ref.shape[0] // T
    r = lax.broadcasted_iota(jnp.int32, (T, T), 0)
    cc = lax.broadcasted_iota(jnp.int32, (T, T), 1)
    causal = r >= cc
    lam = lam_ref[...]
    lam_full = (jnp.exp(jnp.sum(lam[0:1] * lam[1:2], keepdims=True))
                - jnp.exp(jnp.sum(lam[2:3] * lam[3:4], keepdims=True)) + lambda_init)
    def score_steps(qi, st):
        def first():
            st["q01"] = jnp.concatenate(_half_masks(q_ref[qi * T:(qi + 1) * T, :]), axis=0)
            st["mx"] = [None, None]

        def tile(j):
            if j == 0:
                first()
            ss = _qk(st["q01"], k_ref[j * T:(j + 1) * T, :])
            for c in range(2):
                s = ss[c * T:(c + 1) * T]
                if j == qi - 1:
                    s = s + bias_ref[c, :, 0:T]
                if j == qi:
                    s = jnp.where(causal, s + bias_ref[c, :, T:2 * T], NEG)
                s_ref[qi % DA_SLOTS, c, :, j * T:(j + 1) * T] = s
                t = jnp.maximum(s[:, :LANES], s[:, LANES:])
                st["mx"][c] = t if st["mx"][c] is None else jnp.maximum(st["mx"][c], t)
        return [functools.partial(tile, j) for j in range(qi + 1)]

    def value_steps(qi, st):
        def exp_tile(j):
            if j == 0:
                st["m2"] = []
                for c in range(2):
                    m = jnp.broadcast_to(jnp.max(st["mx"][c], axis=-1, keepdims=True), (T, LANES))
                    st["m2"].append(jnp.concatenate([m, m], axis=1))
                st["lsum"] = [None, None]
            for c in range(2):
                p = jnp.exp2(s_ref[qi % DA_SLOTS, c, :, j * T:(j + 1) * T] - st["m2"][c])
                s_ref[qi % DA_SLOTS, c, :, j * T:(j + 1) * T] = p
                t = p[:, :LANES] + p[:, LANES:]
                st["lsum"][c] = t if st["lsum"][c] is None else st["lsum"][c] + t

        def value_tile(j):
            if j == 0:
                l0 = jnp.sum(st["lsum"][0], axis=-1, keepdims=True)
                l1 = jnp.sum(st["lsum"][1], axis=-1, keepdims=True)
                st["ratio"] = lam_full * l0 / l1
                st["inv_l0"] = 1.0 / l0
                st["acc"] = None
            a = (s_ref[qi % DA_SLOTS, 0, :, j * T:(j + 1) * T]
                 - s_ref[qi % DA_SLOTS, 1, :, j * T:(j + 1) * T] * st["ratio"])
            pv = jnp.dot(a.astype(BF16), v_ref[j * T:(j + 1) * T, :], preferred_element_type=F32)
            st["acc"] = pv if st["acc"] is None else st["acc"] + pv
            if j == qi:
                o = _rms(st["acc"] * st["inv_l0"], sub_ref[...]) * (1.0 - lambda_init)
                o_ref[qi * T:(qi + 1) * T, :] = o.astype(o_ref.dtype)
        return ([functools.partial(exp_tile, j) for j in range(qi + 1)],
                [functools.partial(value_tile, j) for j in range(qi + 1)])

    states = [dict() for _ in range(nq)]
    stages = [score_steps(qi, states[qi]) for qi in range(nq)]
    later = [value_steps(qi, states[qi]) for qi in range(nq)]
    for r in range(nq + 1):
        lists = [stages[r] if r < nq else [],
                 later[r - 1][0] + later[r - 1][1] if r >= 1 else []]
        for t in range(max(len(steps) for steps in lists)):
            for steps in lists:
                if t < len(steps):
                    steps[t]()


def da_call(qkv, bias_near, lam, subln, batch, seq, lambda_init, w_down=None):
    T = ATT_T
    nh = DA_HEADS
    in_specs = [pl.BlockSpec((seq, LANES), lambda h, b: (b, h)),
                pl.BlockSpec((seq, LANES), lambda h, b: (b, nh + h)),
                pl.BlockSpec((seq, LANES), lambda h, b: (b, 2 * nh + h)),
                pl.BlockSpec((2, T, 2 * T), lambda h, b: (h, 0, 0)),
                pl.BlockSpec((4, DA_QK_DIM), lambda h, b: (0, 0)),
                pl.BlockSpec((1, DA_V_DIM), lambda h, b: (0, 0))]
    args = [qkv, qkv, qkv, bias_near, lam, subln.reshape(1, DA_V_DIM)]
    out_shape = jax.ShapeDtypeStruct((batch * seq, nh * DA_V_DIM), BF16)
    out_specs = pl.BlockSpec((seq, LANES), lambda h, b: (b, h))
    if w_down is not None:
        L, _, D = w_down.shape
        blocks = D_FF_PAD // PREP_DOWN_ROWS
        assert nh * batch == L * blocks, "one down-projection weight row block per grid step"
        prep = lambda h, b: ((h * batch + b) // blocks, (h * batch + b) % blocks, 0)
        in_specs.append(pl.BlockSpec((None, PREP_DOWN_ROWS, D), prep))
        args.append(w_down)
        out_shape = (out_shape, jax.ShapeDtypeStruct((L, D_FF_PAD, D), BF16))
        out_specs = (out_specs, pl.BlockSpec((None, PREP_DOWN_ROWS, D), prep))
    return pl.pallas_call(
        functools.partial(_da_kernel, lambda_init=lambda_init, prep_down=w_down is not None),
        out_shape=out_shape,
        grid=(nh, batch),
        in_specs=in_specs,
        out_specs=out_specs,
        scratch_shapes=[pltpu.VMEM((DA_SLOTS, 2, T, seq), F32)],
        compiler_params=_params("parallel", "parallel"),
        name="diff_attention",
    )(*args)


def _sb_kernel(q_ref, k_ref, v_ref, o_ref, lbn_ref, hmn_ref, lbf_ref, hmf_ref, acc_ref):
    T = ATT_T
    nq = q_ref.shape[0] // T
    kr = lax.broadcasted_iota(jnp.int32, (T, T), 0)
    kc = lax.broadcasted_iota(jnp.int32, (T, T), 1)
    suffix = jnp.where(kr > kc, 1.0, 0.0).astype(BF16)
    strict = kc < kr
    lane = lax.broadcasted_iota(jnp.int32, (T, LANES), 1)
    strict2 = jnp.concatenate([strict, strict], axis=0)

    def score_tile(qi, q01, j, blk, carry, lb_view, hm_view):
        zz = _qk(q01, k_ref[j * T:(j + 1) * T, :])
        for c in range(2):
            z = zz[c * T:(c + 1) * T]
            lb = jnp.minimum(z, 0.0) - jnp.log(1.0 + jnp.exp(-jnp.abs(z)))
            log_1m_beta = lb - z
            if j == qi:
                log_1m_beta = jnp.where(strict, log_1m_beta, 0.0)
            rows = slice(blk + c * T, blk + (c + 1) * T)
            hm_view[rows, :] = log_1m_beta.astype(BF16)
            if carry[c] is not None:
                lb = lb + carry[c]
            lb_view[rows, :] = lb
            if j > 0:
                rs = jnp.sum(log_1m_beta, axis=-1, keepdims=True)
                carry[c] = rs if carry[c] is None else carry[c] + rs

    def value_tiles(qi, tiles, lb_view, hm_view):
        between = jnp.dot(hm_view[0:2 * len(tiles) * T, :], suffix, preferred_element_type=F32)
        acc = None
        for pos, j in enumerate(tiles):
            rows = slice(2 * pos * T, 2 * (pos + 1) * T)
            a = jnp.exp(lb_view[rows, :] + between[rows])
            if j == qi:
                a = jnp.where(strict2, a, 0.0)
            pv = jnp.dot(a.astype(BF16), v_ref[j * T:(j + 1) * T, :], preferred_element_type=F32)
            acc = pv if acc is None else acc + pv
        return acc

    def stacked_q(qi):
        return jnp.concatenate(_half_masks(q_ref[qi * T:(qi + 1) * T, :]), axis=0)

    near = [[j for j in (qi, qi - 1) if j >= 0] for qi in range(nq)]
    carries = [[None, None] for _ in range(nq)]
    for qi in range(nq + 1):
        p = qi - 1
        if qi < nq:
            q01 = stacked_q(qi)
            score_tile(qi, q01, near[qi][0], 0, carries[qi], lbn_ref.at[qi % 2], hmn_ref.at[qi % 2])
        if p >= 0:
            between = jnp.dot(hmn_ref[p % 2, 0:2 * len(near[p]) * T, :], suffix, preferred_element_type=F32)
        if qi < nq and len(near[qi]) > 1:
            score_tile(qi, q01, near[qi][1], 2 * T, carries[qi], lbn_ref.at[qi % 2], hmn_ref.at[qi % 2])
        if p >= 0:
            acc = None
            for pos, j in enumerate(near[p]):
                rows = slice(2 * pos * T, 2 * (pos + 1) * T)
                a = jnp.exp(lbn_ref[p % 2, rows, :] + between[rows])
                if j == p:
                    a = jnp.where(strict2, a, 0.0)
                pv = jnp.dot(a.astype(BF16), v_ref[j * T:(j + 1) * T, :], preferred_element_type=F32)
                acc = pv if acc is None else acc + pv
            acc_ref[p] = acc

    for qi in range(2, nq):
        far = list(range(qi - 2, -1, -1))
        carry = carries[qi]
        nearest_sum = jnp.max(jnp.maximum(carry[0], carry[1]))

        @pl.when(nearest_sum > SB_EXP_ZERO)
        def _():
            q01 = stacked_q(qi)
            far_carry = list(carry)
            for pos, j in enumerate(far):
                score_tile(qi, q01, j, 2 * pos * T, far_carry, lbf_ref, hmf_ref)
            acc_ref[qi] += value_tiles(qi, far, lbf_ref, hmf_ref)

    for qi in range(nq):
        acc = acc_ref[qi]
        o_ref[qi * T:(qi + 1) * T, :] = jnp.where(lane < 64, acc[0:T], acc[T:2 * T]).astype(o_ref.dtype)


def sb_call(qkv, batch, seq):
    npair = SB_HEADS // 2
    return pl.pallas_call(
        _sb_kernel,
        out_shape=jax.ShapeDtypeStruct((batch * seq, SB_HEADS * SB_DIM), BF16),
        grid=(batch, npair),
        in_specs=[pl.BlockSpec((seq, LANES), lambda b, p: (b, p)),
                  pl.BlockSpec((seq, LANES), lambda b, p: (b, npair + p)),
                  pl.BlockSpec((seq, LANES), lambda b, p: (b, 2 * npair + p))],
        out_specs=pl.BlockSpec((seq, LANES), lambda b, p: (b, p)),
        scratch_shapes=[pltpu.VMEM((2, 4 * ATT_T, ATT_T), F32), pltpu.VMEM((2, 4 * ATT_T, ATT_T), BF16),
                        pltpu.VMEM((2 * seq - 4 * ATT_T, ATT_T), F32),
                        pltpu.VMEM((2 * seq - 4 * ATT_T, ATT_T), BF16),
                        pltpu.VMEM((seq // ATT_T, 2 * ATT_T, LANES), F32)],
        compiler_params=_params("parallel", "parallel"),
        name="stick_breaking_attention",
    )(qkv, qkv, qkv)


def _sw_kernel(sink_ref, q_ref, k_ref, v_ref, bias_ref, o_ref, s_ref):
    W = SW_BLOCK
    nb = q_ref.shape[0] // W
    p_id = pl.program_id(1)
    r = lax.broadcasted_iota(jnp.int32, (2 * W, 2 * W), 0) & (W - 1)
    cidx = lax.broadcasted_iota(jnp.int32, (2 * W, 2 * W), 1)
    valid = ((cidx < W) & (cidx > r)) | ((cidx >= W) & (cidx - W <= r))
    biasm = jnp.where(valid, bias_ref[...], NEG)
    row = lax.broadcasted_iota(jnp.int32, (2 * W, 1), 0)
    sink = jnp.where(row < W, sink_ref[2 * p_id], sink_ref[2 * p_id + 1])
    lane = lax.broadcasted_iota(jnp.int32, (W, LANES), 1)
    def scores(n):
        q01 = jnp.concatenate(_half_masks(q_ref[n * W:(n + 1) * W, :]), axis=0)
        if n == 0:
            s_ref[0, :, W:] = _qk(q01, k_ref[0:W, :]) + biasm[:, W:]
        else:
            s_ref[n] = _qk(q01, k_ref[(n - 1) * W:(n + 1) * W, :]) + biasm

    def values(n):
        s = s_ref[0, :, W:] if n == 0 else s_ref[n]
        m = jnp.maximum(jnp.max(s, axis=-1, keepdims=True), sink)
        e = jnp.exp2(s - m)
        denom = jnp.sum(e, axis=-1, keepdims=True) + jnp.exp2(sink - m)
        v = v_ref[0:W, :] if n == 0 else v_ref[(n - 1) * W:(n + 1) * W, :]
        o = jnp.dot(e.astype(BF16), v, preferred_element_type=F32) / denom
        o_ref[n * W:(n + 1) * W, :] = jnp.where(lane < 64, o[0:W], o[W:]).astype(o_ref.dtype)

    for n in range(nb + SW_LAG):
        if n < nb:
            scores(n)
        if n >= SW_LAG:
            values(n - SW_LAG)


def sw_call(qkv, bias_band, sinks, batch, seq):
    npair = SW_Q_HEADS // 2
    q_blocks = SW_Q_HEADS * SW_DIM // LANES
    W = SW_BLOCK
    return pl.pallas_call(
        _sw_kernel,
        out_shape=jax.ShapeDtypeStruct((batch * seq, SW_Q_HEADS * SW_DIM), BF16),
        grid=(batch, npair),
        in_specs=[pl.BlockSpec(memory_space=pltpu.SMEM),
                  pl.BlockSpec((seq, LANES), lambda b, p: (b, p)),
                  pl.BlockSpec((seq, LANES), lambda b, p: (b, q_blocks + p // 2)),
                  pl.BlockSpec((seq, LANES), lambda b, p: (b, q_blocks + SW_KV_HEADS + p // 2)),
                  pl.BlockSpec((None, 2 * W, 2 * W), lambda b, p: (p, 0, 0))],
        out_specs=pl.BlockSpec((seq, LANES), lambda b, p: (b, p)),
        scratch_shapes=[pltpu.VMEM((seq // W, 2 * W, 2 * W), F32)],
        compiler_params=_params("parallel", "parallel"),
        name="sliding_window_attention",
    )(sinks, qkv, qkv, qkv, bias_band.reshape(npair, 2 * W, 2 * W))


def _t5_bucket(dist):
    max_exact = N_BUCKETS // 2
    d = jnp.maximum(dist, 0)
    large = max_exact + (jnp.log(jnp.maximum(d, 1).astype(F32) / max_exact)
                         / math.log(MAX_DISTANCE / max_exact) * (N_BUCKETS - max_exact)).astype(jnp.int32)
    large = jnp.minimum(large, N_BUCKETS - 1)
    return jnp.where(d < max_exact, d, large)


def _band_kernel(row_ref, o_ref):
    rows = jnp.broadcast_to(row_ref[...], (o_ref.shape[0], row_ref.shape[1]))
    o_ref[...] = pltpu.roll(rows, 0, 1, stride=1, stride_axis=0)[:, :o_ref.shape[1]]


def _bias_band(table, T):
    y = np.arange(3 * T)
    dist = np.clip(np.where(y <= 2 * T, T - y, 4 * T - y), 0, 2 * T - 1)
    row = table.astype(F32)[_t5_bucket(jnp.asarray(dist, jnp.int32))].T
    ch = row.shape[0]
    return pl.pallas_call(
        _band_kernel,
        out_shape=jax.ShapeDtypeStruct((ch, T, 2 * T), F32),
        grid=(ch,),
        in_specs=[pl.BlockSpec((None, 1, 3 * T), lambda c: (c, 0, 0))],
        out_specs=pl.BlockSpec((None, T, 2 * T), lambda c: (c, 0, 0)),
        compiler_params=_params("parallel"),
        name="bias_band",
    )(row[:, None, :])


def _lambda_init(layer):
    return 0.8 - 0.6 * math.exp(-0.3 * layer)


def _dup_kv_heads(w):
    q_w = SW_Q_HEADS * SW_DIM
    kv = w[:, q_w:].reshape(w.shape[0], 2 * SW_KV_HEADS, 1, SW_DIM)
    kv = jnp.broadcast_to(kv, (w.shape[0], 2 * SW_KV_HEADS, 2, SW_DIM)).reshape(w.shape[0], -1)
    return jnp.concatenate([w[:, :q_w], kv], axis=1)


def _scale_q_columns(w, n_q, scale):
    return w * jnp.where(jnp.arange(w.shape[-1]) < n_q, scale, 1.0)


def _pad_ff(a, axis):
    pad = [(0, 0)] * a.ndim
    pad[axis] = (0, D_FF_PAD - D_FF)
    return jnp.pad(a, pad)


def _split_up(w_ref, gate_ref, value_ref):
    keep = lax.broadcasted_iota(jnp.int32, gate_ref.shape, 1) < D_FF
    gate_ref[...] = jnp.where(keep, w_ref[:, :D_FF_PAD], 0.0).astype(gate_ref.dtype)
    start = 2 * D_FF - D_FF_PAD
    window = pltpu.roll(w_ref[:, start:], D_FF_PAD - (D_FF - start), 1)
    value_ref[...] = jnp.where(keep, window, 0.0).astype(value_ref.dtype)


def _pad_down(w_ref, o_ref, block):
    row = block * PREP_DOWN_ROWS + lax.broadcasted_iota(jnp.int32, o_ref.shape, 0)
    o_ref[...] = jnp.where(row < D_FF, w_ref[...], 0.0).astype(o_ref.dtype)


def kernel(x, rel_bias, attn_norm, ffn_norm, w_o, da_w_qkv, da_lambda, da_subln, sb_w_qkv, sw_w_qkv,
           sw_sinks, ffn_w_up, ffn_conv_w, ffn_conv_b, ffn_w_down, final_norm):
    B, S, D = x.shape
    x2 = x.reshape(B * S, D)
    da_bias_near = _bias_band((rel_bias - rel_bias[N_BUCKETS - 1]) * LOG2E, ATT_T)
    sw_bias = _bias_band(rel_bias * LOG2E, SW_BLOCK)
    da_w = _scale_q_columns(da_w_qkv, DA_HEADS * 2 * DA_QK_DIM, DA_QK_DIM ** -0.5 * LOG2E).astype(BF16)
    sb_w = _scale_q_columns(sb_w_qkv, SB_HEADS * SB_DIM, SB_DIM ** -0.5).astype(BF16)
    sw_w = jax.vmap(_dup_kv_heads)(
        _scale_q_columns(sw_w_qkv, SW_Q_HEADS * SW_DIM, SW_DIM ** -0.5 * LOG2E)).astype(BF16)
    wo_w = w_o.astype(BF16)
    cwg = _pad_ff(ffn_conv_w[:, :, :D_FF], 2)
    cwv = _pad_ff(ffn_conv_w[:, :, D_FF:], 2)
    cbg = _pad_ff(ffn_conv_b[:, None, :D_FF], 2)
    cbv = _pad_ff(ffn_conv_b[:, None, D_FF:], 2)
    attn_gain = attn_norm[:, None, :]
    ffn_gain = ffn_norm[:, None, :]
    final_gain = final_norm[None, None, :]

    weights_qkv = [((da_w, sb_w, sw_w)[layer % N_MIXERS], layer // N_MIXERS) for layer in range(DEPTH)]
    qkv, wg, wv = proj_call(x2, *weights_qkv[0], attn_gain, 0, ffn_w_up)
    for layer in range(DEPTH):
        mixer = layer % N_MIXERS
        slot = layer // N_MIXERS
        if mixer == 0:
            if layer == 0:
                m, wd = da_call(qkv, da_bias_near, da_lambda[slot], da_subln[slot], B, S, _lambda_init(layer),
                                ffn_w_down)
            else:
                m = da_call(qkv, da_bias_near, da_lambda[slot], da_subln[slot], B, S, _lambda_init(layer))
        elif mixer == 1:
            m = sb_call(qkv, B, S)
        else:
            m = sw_call(qkv, sw_bias, sw_sinks[slot] * LOG2E, B, S)
        x2, act = wo_ffn_up_call(m, wo_w, x2, ffn_gain, wg, wv, cwg, cwv, cbg, cbv, layer, S)
        if layer < DEPTH - 1:
            x2, qkv = ffn_down_call(act, wd, layer, x2, attn_gain, layer + 1, *weights_qkv[layer + 1])
        else:
            h = ffn_down_call(act, wd, layer, x2, final_gain, 0)
    return h.reshape(B, S, D)
```
